```python
import math
import jax, jax.numpy as jnp
from jax import lax
import numpy as np

D_MODEL = 4096
BATCH = 1
SEQ = 8192
DEPTH = 1
DEC_BATCH = 8
DEC_SEQ = 64
PAST_LEN = 1024

CHUNK = 64
D_MIX = D_MODEL
D_RET = D_MIX // 2
D_SB = D_MIX - D_RET
RET_HEAD_DIM = 256
RET_HEADS = D_RET // RET_HEAD_DIM
SB_HEAD_DIM = 128
SB_HEADS = D_SB // SB_HEAD_DIM
SB_BLOCK = 128
N_MEM = 256
MEM_HEADS = 4
MEM_HEAD_DIM = D_MODEL // MEM_HEADS
N_GROUPS = 4
EXPERTS_PER_GROUP = 8
N_EXPERTS = N_GROUPS * EXPERTS_PER_GROUP
TOP_K = 2
D_EXPERT = D_MODEL // 4
MOE_BLOCK = 128
ROPE_BASE = 10000.0
EPS = 1e-6
D_IN = 4 * D_RET + 3 * D_SB
SPLITS = (D_RET, 2 * D_RET, 3 * D_RET, 4 * D_RET, 4 * D_RET + D_SB, 4 * D_RET + 2 * D_SB)

kernel_name = "chunk_causal_retention_stickbreaking_hmoe_step"


def rmsnorm(x, g):
    xf = x.astype(jnp.float32)
    y = xf * lax.rsqrt(jnp.mean(xf * xf, axis=-1, keepdims=True) + EPS)
    return (y * g.astype(jnp.float32)).astype(x.dtype)


def rotary(x, pos):
    half = x.shape[-1] // 2
    inv = ROPE_BASE ** (-jnp.arange(half, dtype=jnp.float32) / half)
    ang = pos.astype(jnp.float32)[:, None] * inv[None, :]
    cos = jnp.cos(ang)[None, :, None, :]
    sin = jnp.sin(ang)[None, :, None, :]
    x1, x2 = x[..., :half], x[..., half:]
    return jnp.concatenate([x1 * cos - x2 * sin, x1 * sin + x2 * cos], axis=-1)


def retention_log_gamma():
    return jnp.log1p(-jnp.exp2(-5.0 - jnp.arange(RET_HEADS, dtype=jnp.float32)))


def project_heads(h, w_in, pos):
    B, T, _ = h.shape
    p = jnp.einsum('btd,de->bte', h, w_in).astype(jnp.float32)
    rq, rk, rv, rg, sq, sk, sv = jnp.split(p, SPLITS, axis=-1)
    rh = lambda a: a.reshape(B, T, RET_HEADS, RET_HEAD_DIM)
    sh = lambda a: a.reshape(B, T, SB_HEADS, SB_HEAD_DIM)
    rq = rotary(rh(rq), pos)
    rk = rotary(rh(rk), pos) * (RET_HEAD_DIM ** -0.5)
    return rq, rk, rh(rv), rg, sh(sq), sh(sk), sh(sv)


def retention_chunk(q, k, v, S, log_gamma):
    L = q.shape[1]
    idx = jnp.arange(L, dtype=jnp.float32)
    dist = jnp.abs(idx[:, None] - idx[None, :])
    dmat = jnp.exp(log_gamma[:, None, None] * dist[None])
    scores = jnp.einsum('bihk,bjhk->bhij', q, k) * dmat
    out = jnp.einsum('bhij,bjhv->bihv', scores, v)
    dec_in = jnp.exp(log_gamma[:, None] * (idx + 1.0)[None, :])
    out = out + jnp.einsum('bihk,hi,bhkv->bihv', q, dec_in, S)
    dec_out = jnp.exp(log_gamma[:, None] * (L - 1.0 - idx)[None, :])
    S_new = jnp.exp(log_gamma * L)[None, :, None, None] * S + jnp.einsum('bjhk,hj,bjhv->bhkv', k, dec_out, v)
    return out, S_new


def retention_prompt(q, k, v, log_gamma):
    B, T, H, dk = q.shape
    nc = T // CHUNK
    to_chunks = lambda a: a.reshape(B, nc, CHUNK, H, a.shape[-1]).swapaxes(0, 1)

    def step(S, qkv):
        o, S = retention_chunk(qkv[0], qkv[1], qkv[2], S, log_gamma)
        return S, o

    S0 = jnp.zeros((B, H, dk, v.shape[-1]), jnp.float32)
    S, o = lax.scan(step, S0, (to_chunks(q), to_chunks(k), to_chunks(v)))
    return o.swapaxes(0, 1).reshape(B, T, H, v.shape[-1]), S


def stick_breaking(q, q_pos, k, v, k_pos):
    z = jnp.einsum('bqhd,bshd->bhqs', q, k).astype(jnp.float32) * (SB_HEAD_DIM ** -0.5)
    mask = k_pos[None, :] < q_pos[:, None]
    log_fail = jnp.where(mask, jax.nn.log_sigmoid(-z), 0.0)
    between = lax.cumsum(log_fail, axis=3, reverse=True) - log_fail
    a = jnp.where(mask, jnp.exp(jax.nn.log_sigmoid(z) + between), 0.0)
    return jnp.einsum('bhqs,bshd->bqhd', a, v.astype(jnp.float32))


def stick_breaking_prompt(q, k, v, pos):
    B, T, H, d = q.shape
    nb = T // SB_BLOCK
    qb = q.reshape(B, nb, SB_BLOCK, H, d).swapaxes(0, 1)
    pb = pos.reshape(nb, SB_BLOCK)
    o = lax.map(lambda a: stick_breaking(a[0], a[1], k, v, pos), (qb, pb))
    return o.swapaxes(0, 1).reshape(B, T, H, d)


def mixer_out(ret_o, rg, sb_o, ret_gn, w_out, dtype):
    B, T = ret_o.shape[:2]
    mu = jnp.mean(ret_o, axis=-1, keepdims=True)
    var = jnp.mean(jnp.square(ret_o - mu), axis=-1, keepdims=True)
    yr = (ret_o - mu) * lax.rsqrt(var + EPS) * ret_gn.astype(jnp.float32)
    yr = jax.nn.silu(rg) * yr.reshape(B, T, D_RET)
    y = jnp.concatenate([yr, sb_o.reshape(B, T, D_SB)], axis=-1).astype(dtype)
    return jnp.einsum('bte,ed->btd', y, w_out)


def memory_kv(mem, g, w_ck, w_cv):
    B = mem.shape[0]
    m = rmsnorm(mem, g)
    k = jnp.einsum('bmd,de->bme', m, w_ck).reshape(B, N_MEM, MEM_HEADS, MEM_HEAD_DIM)
    v = jnp.einsum('bmd,de->bme', m, w_cv).reshape(B, N_MEM, MEM_HEADS, MEM_HEAD_DIM)
    return k, v


def cross_attend(h, mk, mv, w_cq, w_co):
    B, T, _ = h.shape
    q = jnp.einsum('btd,de->bte', h, w_cq).reshape(B, T, MEM_HEADS, MEM_HEAD_DIM)
    s = jnp.einsum('bthd,bmhd->bhtm', q, mk).astype(jnp.float32) * (MEM_HEAD_DIM ** -0.5)
    p = jax.nn.softmax(s, axis=-1)
    o = jnp.einsum('bhtm,bmhd->bthd', p, mv.astype(jnp.float32)).astype(h.dtype).reshape(B, T, D_MODEL)
    return jnp.einsum('btd,de->bte', o, w_co)


def hier_moe(h, wg, bg, we, be, w1, w3, w2):
    B, T, D = h.shape
    n = B * T
    x = h.reshape(n, D)
    g_logits = (x @ wg + bg).astype(jnp.float32)
    g_prob = jax.nn.softmax(g_logits, axis=-1)
    g_top = jnp.argmax(g_logits, axis=-1)
    g_w = jnp.take_along_axis(g_prob, g_top[:, None], axis=1)[:, 0]
    e_logits = (x @ we + be).astype(jnp.float32).reshape(n, N_GROUPS, EXPERTS_PER_GROUP)
    e_logits = jnp.take_along_axis(e_logits, g_top[:, None, None], axis=1)[:, 0]
    top_v, top_i = lax.top_k(e_logits, TOP_K)
    e_w = jax.nn.softmax(top_v, axis=-1) * g_w[:, None]
    expert = g_top[:, None].astype(jnp.int32) * EXPERTS_PER_GROUP + top_i.astype(jnp.int32)
    flat_e = expert.reshape(-1)
    flat_w = e_w.reshape(-1)
    flat_t = jnp.repeat(jnp.arange(n, dtype=jnp.int32), TOP_K)
    order = jnp.argsort(flat_e)
    se, st, sw = flat_e[order], flat_t[order], flat_w[order]
    counts = jnp.zeros((N_EXPERTS,), jnp.int32).at[flat_e].add(1)
    padded = (counts + MOE_BLOCK - 1) // MOE_BLOCK * MOE_BLOCK
    pad_end = jnp.cumsum(padded)
    pad_start = pad_end - padded
    start = jnp.cumsum(counts) - counts
    dest = pad_start[se] + jnp.arange(n * TOP_K, dtype=jnp.int32) - start[se]
    n_blocks = -(-(n * TOP_K) // MOE_BLOCK) + N_EXPERTS
    n_rows = n_blocks * MOE_BLOCK
    row_tok = jnp.full((n_rows,), n, jnp.int32).at[dest].set(st)
    row_w = jnp.zeros((n_rows,), jnp.float32).at[dest].set(sw)
    blk_exp = jnp.minimum(jnp.searchsorted(pad_end, jnp.arange(n_blocks, dtype=jnp.int32) * MOE_BLOCK, side='right'), N_EXPERTS - 1)
    x_pad = jnp.concatenate([x, jnp.zeros((1, D), x.dtype)], axis=0)

    def run_block(args):
        toks, e = args
        xb = x_pad[toks]
        hb = jax.nn.silu(xb @ w1[e]) * (xb @ w3[e])
        return hb @ w2[e]

    yb = lax.map(run_block, (row_tok.reshape(n_blocks, MOE_BLOCK), blk_exp))
    yb = yb.reshape(n_rows, D) * row_w[:, None].astype(h.dtype)
    y = jnp.zeros((n + 1, D), h.dtype).at[row_tok].add(yb)[:n]
    return y.reshape(B, T, D)


def setup_inputs(seed: int = 0) -> dict:
    key = jax.random.key(seed)
    ks = jax.random.split(key, 32)
    f32 = jnp.float32
    nrm = lambda k, shape, scale: jax.random.normal(k, shape, f32) * scale
    gain = lambda k, shape: 1.0 + 0.01 * jax.random.normal(k, shape, f32)
    return {
        "x_prompt": nrm(ks[0], (BATCH, SEQ, D_MODEL), 1.0),
        "x_sample": nrm(ks[1], (DEC_BATCH, DEC_SEQ, D_MODEL), 1.0),
        "cache_sb_k": nrm(ks[2], (DEPTH, DEC_BATCH, PAST_LEN, SB_HEADS, SB_HEAD_DIM), 1.0),
        "cache_sb_v": nrm(ks[3], (DEPTH, DEC_BATCH, PAST_LEN, SB_HEADS, SB_HEAD_DIM), 1.0),
        "state_ret": nrm(ks[4], (DEPTH, DEC_BATCH, RET_HEADS, RET_HEAD_DIM, RET_HEAD_DIM), 0.5),
        "cache_mem_k": nrm(ks[5], (DEPTH, DEC_BATCH, N_MEM, MEM_HEADS, MEM_HEAD_DIM), 1.0),
        "cache_mem_v": nrm(ks[6], (DEPTH, DEC_BATCH, N_MEM, MEM_HEADS, MEM_HEAD_DIM), 1.0),
        "mem_prompt": nrm(ks[7], (BATCH, N_MEM, D_MODEL), 1.0),
        "norm_mix": gain(ks[8], (DEPTH, D_MODEL)),
        "w_in": nrm(ks[9], (DEPTH, D_MODEL, D_IN), D_MODEL ** -0.5),
        "ret_gn": gain(ks[10], (DEPTH, RET_HEADS, RET_HEAD_DIM)),
        "w_out": nrm(ks[11], (DEPTH, D_MIX, D_MODEL), D_MIX ** -0.5),
        "norm_cross": gain(ks[12], (DEPTH, D_MODEL)),
        "norm_mem": gain(ks[13], (DEPTH, D_MODEL)),
        "w_cq": nrm(ks[14], (DEPTH, D_MODEL, D_MODEL), D_MODEL ** -0.5),
        "w_ck": nrm(ks[15], (DEPTH, D_MODEL, D_MODEL), D_MODEL ** -0.5),
        "w_cv": nrm(ks[16], (DEPTH, D_MODEL, D_MODEL), D_MODEL ** -0.5),
        "w_co": nrm(ks[17], (DEPTH, D_MODEL, D_MODEL), D_MODEL ** -0.5),
        "norm_ffn": gain(ks[18], (DEPTH, D_MODEL)),
        "router_group_w": nrm(ks[19], (DEPTH, D_MODEL, N_GROUPS), D_MODEL ** -0.5),
        "router_group_b": nrm(ks[20], (DEPTH, N_GROUPS), 0.01),
        "router_expert_w": nrm(ks[21], (DEPTH, D_MODEL, N_EXPERTS), D_MODEL ** -0.5),
        "router_expert_b": nrm(ks[22], (DEPTH, N_EXPERTS), 0.01),
        "expert_w1": nrm(ks[23], (DEPTH, N_EXPERTS, D_MODEL, D_EXPERT), D_MODEL ** -0.5),
        "expert_w3": nrm(ks[24], (DEPTH, N_EXPERTS, D_MODEL, D_EXPERT), D_MODEL ** -0.5),
        "expert_w2": nrm(ks[25], (DEPTH, N_EXPERTS, D_EXPERT, D_MODEL), D_EXPERT ** -0.5),
        "norm_final": gain(ks[26], (D_MODEL,)),
    }


def reference(x_prompt, x_sample, cache_sb_k, cache_sb_v, state_ret, cache_mem_k, cache_mem_v, mem_prompt,
              norm_mix, w_in, ret_gn, w_out, norm_cross, norm_mem, w_cq, w_ck, w_cv, w_co, norm_ffn,
              router_group_w, router_group_b, router_expert_w, router_expert_b,
              expert_w1, expert_w3, expert_w2, norm_final):
    log_gamma = retention_log_gamma()
    T = x_prompt.shape[1]
    L = x_sample.shape[1]
    P = cache_sb_k.shape[2]
    pos_p = jnp.arange(T, dtype=jnp.int32)
    pos_s = P + jnp.arange(L, dtype=jnp.int32)
    pos_all = jnp.arange(P + L, dtype=jnp.int32)
    dt_p, dt_s = x_prompt.dtype, x_sample.dtype
    xp, xs = x_prompt, x_sample
    sbk_p, sbv_p, ret_p, memk_p, memv_p, sbk_s, sbv_s, ret_s = [], [], [], [], [], [], [], []
    for l in range(DEPTH):
        rq, rk, rv, rg, sq, sk, sv = project_heads(rmsnorm(xp, norm_mix[l]), w_in[l], pos_p)
        ret_o, S_p = retention_prompt(rq, rk, rv, log_gamma)
        sb_o = stick_breaking_prompt(sq, sk, sv, pos_p)
        xp = xp + mixer_out(ret_o, rg, sb_o, ret_gn[l], w_out[l], dt_p)
        sbk_p.append(sk.astype(dt_p)); sbv_p.append(sv.astype(dt_p)); ret_p.append(S_p.astype(dt_p))

        rq, rk, rv, rg, sq, sk, sv = project_heads(rmsnorm(xs, norm_mix[l]), w_in[l], pos_s)
        ret_o, S_s = retention_chunk(rq, rk, rv, state_ret[l].astype(jnp.float32), log_gamma)
        k_all = jnp.concatenate([cache_sb_k[l].astype(jnp.float32), sk], axis=1)
        v_all = jnp.concatenate([cache_sb_v[l].astype(jnp.float32), sv], axis=1)
        sb_o = stick_breaking(sq, pos_s, k_all, v_all, pos_all)
        xs = xs + mixer_out(ret_o, rg, sb_o, ret_gn[l], w_out[l], dt_s)
        sbk_s.append(sk.astype(dt_s)); sbv_s.append(sv.astype(dt_s)); ret_s.append(S_s.astype(dt_s))

        mk, mv = memory_kv(mem_prompt, norm_mem[l], w_ck[l], w_cv[l])
        xp = xp + cross_attend(rmsnorm(xp, norm_cross[l]), mk, mv, w_cq[l], w_co[l])
        xs = xs + cross_attend(rmsnorm(xs, norm_cross[l]), cache_mem_k[l], cache_mem_v[l], w_cq[l], w_co[l])
        memk_p.append(mk); memv_p.append(mv)

        xp = xp + hier_moe(rmsnorm(xp, norm_ffn[l]), router_group_w[l], router_group_b[l], router_expert_w[l],
                           router_expert_b[l], expert_w1[l], expert_w3[l], expert_w2[l])
        xs = xs + hier_moe(rmsnorm(xs, norm_ffn[l]), router_group_w[l], router_group_b[l], router_expert_w[l],
                           router_expert_b[l], expert_w1[l], expert_w3[l], expert_w2[l])
    y_prompt = rmsnorm(xp, norm_final)
    y_sample = rmsnorm(xs, norm_final)
    return (y_prompt, y_sample, jnp.stack(sbk_p), jnp.stack(sbv_p), jnp.stack(ret_p), jnp.stack(memk_p),
            jnp.stack(memv_p), jnp.stack(sbk_s), jnp.stack(sbv_s), jnp.stack(ret_s))
```

```python
import functools

import jax
import jax.numpy as jnp
from jax import lax
from jax.experimental import pallas as pl
from jax.experimental.pallas import tpu as pltpu

BF16 = jnp.bfloat16
F32 = jnp.float32

EPS = 1e-6
CHUNK = 64
RET_HEAD_DIM = 256
SB_HEAD_DIM = 128
MEM_HEADS = 4
N_GROUPS = 4
EXPERTS_PER_GROUP = 8
N_EXPERTS = N_GROUPS * EXPERTS_PER_GROUP
TOP_K = 2
ROPE_BASE = 10000.0

LANES = 128
VMEM_LIMIT_BYTES = 56 * 1024 * 1024
ROW_TILE = 512
COL_TILE = 512
RET_BLOCK = 256
SB_BLOCK = 256
MOE_ROWS = 256
GATHER_WINDOW = 32

_NT = (((1,), (1,)), ((), ()))
_TN = (((0,), (0,)), ((), ()))


def _tile(n, pref):
    t = min(pref, n)
    while n % t:
        t //= 2
    return t


def _params(*sem):
    return pltpu.CompilerParams(dimension_semantics=sem, vmem_limit_bytes=VMEM_LIMIT_BYTES)


def _sigmoid(x):
    return 1.0 / (1.0 + jnp.exp(-x))


def _rmsnorm_rows(x, g):
    ms = jnp.mean(x * x, axis=-1, keepdims=True)
    return x * lax.rsqrt(ms + EPS) * g


def _rmsnorm_body(x_ref, g_ref, o_ref):
    o_ref[...] = _rmsnorm_rows(x_ref[...], g_ref[...]).astype(o_ref.dtype)


def _rmsnorm(x, g, out_dtype):
    m, d = x.shape
    tm = _tile(m, 256)
    return pl.pallas_call(
        _rmsnorm_body,
        out_shape=jax.ShapeDtypeStruct((m, d), out_dtype),
        grid=(m // tm,),
        in_specs=[pl.BlockSpec((tm, d), lambda i: (i, 0)), pl.BlockSpec((1, d), lambda i: (0, 0))],
        out_specs=pl.BlockSpec((tm, d), lambda i: (i, 0)),
        compiler_params=_params("parallel"),
        name="rmsnorm",
    )(x, g.reshape(1, d))


def _matmul_body(*refs, n_lhs, n_extra, n_out, epilogue):
    x_refs = refs[:n_lhs]
    w_refs = refs[n_lhs:2 * n_lhs]
    extra = refs[2 * n_lhs:2 * n_lhs + n_extra]
    outs = refs[2 * n_lhs + n_extra:2 * n_lhs + n_extra + n_out]
    wb = refs[2 * n_lhs + n_extra + n_out:]

    @pl.when(pl.program_id(1) == 0)
    def _():
        for p in range(n_lhs):
            wb[p][...] = w_refs[p][...].astype(BF16)

    acc = jnp.dot(x_refs[0][...], wb[0][...], preferred_element_type=F32)
    for p in range(1, n_lhs):
        acc += jnp.dot(x_refs[p][...], wb[p][...], preferred_element_type=F32)
    epilogue(acc, extra, outs)


def _matmul(xs, w, col_off, n_cols, epilogue, extras, outs, name):
    m = xs[0].shape[0]
    tm = _tile(m, ROW_TILE)
    tn = _tile(n_cols, COL_TILE)
    assert col_off % tn == 0
    jo = col_off // tn
    in_specs, args = [], []
    for x in xs:
        in_specs.append(pl.BlockSpec((tm, x.shape[1]), lambda j, i: (i, 0)))
        args.append(x)
    row = 0
    for x in xs:
        kp = x.shape[1]
        assert row % kp == 0
        in_specs.append(pl.BlockSpec((kp, tn), lambda j, i, r=row // kp: (r, j + jo)))
        args.append(w)
        row += kp
    assert row == w.shape[0]

    def spec(cols):
        if cols is None:
            return pl.BlockSpec((tm, tn), lambda j, i: (i, j))
        return pl.BlockSpec((tm, cols), lambda j, i: (i, 0))

    for a, cols in extras:
        in_specs.append(spec(cols))
        args.append(a)
    body = functools.partial(_matmul_body, n_lhs=len(xs), n_extra=len(extras), n_out=len(outs),
                             epilogue=epilogue)
    res = pl.pallas_call(
        body,
        out_shape=[o for o, _ in outs],
        grid=(n_cols // tn, m // tm),
        in_specs=in_specs,
        out_specs=[spec(cols) for _, cols in outs],
        scratch_shapes=[pltpu.VMEM((x.shape[1], tn), BF16) for x in xs],
        compiler_params=_params("arbitrary", "arbitrary"),
        name=name,
    )(*args)
    return res


def _ep_store(scale):
    def ep(acc, extra, outs):
        for o in outs:
            o[...] = (acc * scale).astype(o.dtype) if scale != 1.0 else acc.astype(o.dtype)
    return ep


def _ep_residual(acc, extra, outs):
    outs[0][...] = extra[0][...] + acc


def _ep_rotary(scale):
    half = RET_HEAD_DIM // 2

    def ep(acc, extra, outs):
        cos = extra[0][...]
        sin = extra[1][...]
        for h in range(acc.shape[1] // RET_HEAD_DIM):
            lo = h * RET_HEAD_DIM
            x1 = acc[:, lo:lo + half]
            x2 = acc[:, lo + half:lo + RET_HEAD_DIM]
            outs[0][:, lo:lo + half] = ((x1 * cos - x2 * sin) * scale).astype(outs[0].dtype)
            outs[0][:, lo + half:lo + RET_HEAD_DIM] = ((x1 * sin + x2 * cos) * scale).astype(outs[0].dtype)
    return ep


def _retention_body(lg_ref, q_ref, k_ref, v_ref, g_ref, gn_ref, s0_ref, o_ref, sout_ref, s_scr, *, rows):
    h = pl.program_id(1)

    @pl.when(pl.program_id(2) == 0)
    def _():
        s_scr[...] = s0_ref[...]

    lg = lg_ref[h]
    q = q_ref[...]
    k = k_ref[...]
    v = v_ref[...]
    ii = lax.broadcasted_iota(jnp.int32, (rows, rows), 0)
    jj = lax.broadcasted_iota(jnp.int32, (rows, rows), 1)
    shift = CHUNK.bit_length() - 1
    dist = jnp.abs(ii - jj).astype(F32)
    decay = jnp.where((jj >> shift) <= (ii >> shift), jnp.exp(lg * dist), 0.0)
    scores = lax.dot_general(q, k, _NT, preferred_element_type=F32) * decay
    out = jnp.dot(scores.astype(BF16), v, preferred_element_type=F32)
    s = s_scr[...]
    idx = lax.broadcasted_iota(jnp.int32, (rows, 1), 0).astype(F32)
    out += jnp.exp(lg * (idx + 1.0)) * jnp.dot(q, s.astype(BF16), preferred_element_type=F32)
    kd = (k.astype(F32) * jnp.exp(lg * (rows - 1.0 - idx))).astype(BF16)
    s_new = jnp.exp(lg * rows) * s + lax.dot_general(kd, v, _TN, preferred_element_type=F32)
    s_scr[...] = s_new
    sout_ref[...] = s_new
    mu = jnp.mean(out, axis=-1, keepdims=True)
    cen = out - mu
    var = jnp.mean(cen * cen, axis=-1, keepdims=True)
    yr = cen * lax.rsqrt(var + EPS) * gn_ref[...]
    g = g_ref[...]
    o_ref[...] = (g * _sigmoid(g) * yr).astype(o_ref.dtype)


def _retention(log_gamma, q, k, v, gate, gn, s0, row_off, n_batch, n_steps, rows):
    heads = q.shape[1] // RET_HEAD_DIM
    assert row_off % rows == 0 and rows % CHUNK == 0
    ro = row_off // rows
    tok = lambda b, h, c, lg: (ro + b * n_steps + c, h)
    blk = pl.BlockSpec((rows, RET_HEAD_DIM), tok)
    st = pl.BlockSpec((None, None, RET_HEAD_DIM, RET_HEAD_DIM), lambda b, h, c, lg: (b, h, 0, 0))
    grid_spec = pltpu.PrefetchScalarGridSpec(
        num_scalar_prefetch=1,
        grid=(n_batch, heads, n_steps),
        in_specs=[blk, blk, blk, blk,
                  pl.BlockSpec((None, 1, RET_HEAD_DIM), lambda b, h, c, lg: (h, 0, 0)), st],
        out_specs=[pl.BlockSpec((rows, RET_HEAD_DIM), lambda b, h, c, lg: (b * n_steps + c, h)), st],
        scratch_shapes=[pltpu.VMEM((RET_HEAD_DIM, RET_HEAD_DIM), F32)],
    )
    return pl.pallas_call(
        functools.partial(_retention_body, rows=rows),
        out_shape=[jax.ShapeDtypeStruct((n_batch * n_steps * rows, heads * RET_HEAD_DIM), BF16),
                   jax.ShapeDtypeStruct((n_batch, heads, RET_HEAD_DIM, RET_HEAD_DIM), F32)],
        grid_spec=grid_spec,
        compiler_params=_params("arbitrary", "arbitrary", "arbitrary"),
        name="retention",
    )(log_gamma, q, k, v, gate, gn.reshape(heads, 1, RET_HEAD_DIM), s0)


def _suffix_ones(n):
    j = lax.broadcasted_iota(jnp.int32, (n, n), 0)
    s = lax.broadcasted_iota(jnp.int32, (n, n), 1)
    return jnp.where(j >= s, 1.0, 0.0).astype(BF16)


def _sb_tile(q, kb, vb, ones, carry, acc, masked):
    z = lax.dot_general(q, kb, _NT, preferred_element_type=F32)
    log_fail = -(jnp.maximum(z, 0.0) + jnp.log(1.0 + jnp.exp(-jnp.abs(z))))
    if masked:
        t = lax.broadcasted_iota(jnp.int32, z.shape, 0)
        s = lax.broadcasted_iota(jnp.int32, z.shape, 1)
        valid = s < t
        log_fail = jnp.where(valid, log_fail, 0.0)
    hi = log_fail.astype(BF16)
    lo = (log_fail - hi.astype(F32)).astype(BF16)
    suffix = (jnp.dot(hi, ones, preferred_element_type=F32)
              + jnp.dot(lo, ones, preferred_element_type=F32))
    a = jnp.exp(z + suffix + carry)
    if masked:
        a = jnp.where(valid, a, 0.0)
    acc = acc + jnp.dot(a.astype(BF16), vb, preferred_element_type=F32)
    return carry + suffix[:, :1], acc


def _sb_body(q_ref, kd_ref, vd_ref, kp_ref, vp_ref, o_ref, *, tq, tk, past_per_step, past_static):
    q = q_ref[...]
    carry = jnp.zeros((tq, 1), F32)
    acc = jnp.zeros((tq, SB_HEAD_DIM), F32)
    carry, acc = _sb_tile(q, kd_ref[...].astype(BF16), vd_ref[...].astype(BF16), _suffix_ones(tq),
                          carry, acc, True)
    n_past = past_static if past_static is not None else pl.program_id(1) * past_per_step
    ones = _suffix_ones(tk)

    def step(it, state):
        start = pl.multiple_of((n_past - 1 - it) * tk, tk)
        kb = kp_ref[pl.ds(start, tk), :].astype(BF16)
        vb = vp_ref[pl.ds(start, tk), :].astype(BF16)
        return _sb_tile(q, kb, vb, ones, state[0], state[1], False)

    carry, acc = lax.fori_loop(0, n_past, step, (carry, acc))
    o_ref[...] = acc.astype(o_ref.dtype)


def _sb_prompt(q, k, v, n_tok):
    heads = q.shape[1] // SB_HEAD_DIM
    tq = _tile(n_tok, SB_BLOCK)
    blk = pl.BlockSpec((tq, SB_HEAD_DIM), lambda h, i: (i, h))
    full = pl.BlockSpec((n_tok, SB_HEAD_DIM), lambda h, i: (0, h))
    return pl.pallas_call(
        functools.partial(_sb_body, tq=tq, tk=tq, past_per_step=1, past_static=None),
        out_shape=jax.ShapeDtypeStruct((n_tok, heads * SB_HEAD_DIM), BF16),
        grid=(heads, n_tok // tq),
        in_specs=[blk, blk, blk, full, full],
        out_specs=blk,
        compiler_params=_params("parallel", "arbitrary"),
        name="stick_breaking_prompt",
    )(q, k, v, k, v)


def _sb_sample(q, k, v, cache_k, cache_v, row_off, n_batch, n_new):
    heads = q.shape[1] // SB_HEAD_DIM
    past = cache_k.shape[1]
    tk = _tile(past, SB_BLOCK)
    assert row_off % n_new == 0
    ro = row_off // n_new
    blk = pl.BlockSpec((n_new, SB_HEAD_DIM), lambda b, h: (ro + b, h))
    full = pl.BlockSpec((None, past, SB_HEAD_DIM), lambda b, h: (b, 0, h))
    return pl.pallas_call(
        functools.partial(_sb_body, tq=n_new, tk=tk, past_per_step=0, past_static=past // tk),
        out_shape=jax.ShapeDtypeStruct((n_batch * n_new, heads * SB_HEAD_DIM), BF16),
        grid=(n_batch, heads),
        in_specs=[blk, blk, blk, full, full],
        out_specs=pl.BlockSpec((n_new, SB_HEAD_DIM), lambda b, h: (b, h)),
        compiler_params=_params("parallel", "arbitrary"),
        name="stick_breaking_sample",
    )(q, k, v, cache_k, cache_v)


def _cross_body(q_ref, k_ref, v_ref, o_ref, *, scale):
    s = lax.dot_general(q_ref[...], k_ref[...].astype(BF16), _NT, preferred_element_type=F32) * scale
    e = jnp.exp(s - jnp.max(s, axis=-1, keepdims=True))
    p = e / jnp.sum(e, axis=-1, keepdims=True)
    o_ref[...] = jnp.dot(p.astype(BF16), v_ref[...].astype(BF16),
                         preferred_element_type=F32).astype(o_ref.dtype)


def _cross_attend(q, mem_k, mem_v, row_off, rows_per_seq):
    n_seq, n_mem, d = mem_k.shape
    dh = d // MEM_HEADS
    tm = _tile(rows_per_seq, ROW_TILE)
    steps = rows_per_seq // tm
    assert row_off % tm == 0
    ro = row_off // tm
    mem = pl.BlockSpec((None, n_mem, dh), lambda i, h: (i // steps, 0, h))
    return pl.pallas_call(
        functools.partial(_cross_body, scale=dh ** -0.5),
        out_shape=jax.ShapeDtypeStruct((n_seq * rows_per_seq, d), BF16),
        grid=(n_seq * steps, MEM_HEADS),
        in_specs=[pl.BlockSpec((tm, dh), lambda i, h: (ro + i, h)), mem, mem],
        out_specs=pl.BlockSpec((tm, dh), lambda i, h: (i, h)),
        compiler_params=_params("parallel", "arbitrary"),
        name="cross_attention",
    )(q, mem_k, mem_v)


def _split_bf16(x):
    hi = x.astype(BF16)
    return hi, (x - hi.astype(F32)).astype(BF16)


def _router_body(x_ref, g_ref, wr_ref, br_ref, h_ref, ids_ref, wts_ref):
    h = _rmsnorm_rows(x_ref[...], g_ref[...])
    h_ref[...] = h
    h_hi, h_lo = _split_bf16(h)
    w_hi, w_lo = _split_bf16(wr_ref[...])
    logits = (jnp.dot(h_hi, w_hi, preferred_element_type=F32)
              + jnp.dot(h_hi, w_lo, preferred_element_type=F32)
              + jnp.dot(h_lo, w_hi, preferred_element_type=F32)) + br_ref[...]
    lane = lax.broadcasted_iota(jnp.int32, logits.shape, 1).astype(F32)
    big = float(LANES)
    neg = -jnp.inf

    def first_max(vals):
        top = jnp.max(vals, axis=-1, keepdims=True)
        return top, jnp.min(jnp.where(vals == top, lane, big), axis=-1, keepdims=True)

    is_group = lane < N_GROUPS
    g_max, g_top = first_max(jnp.where(is_group, logits, neg))
    g_w = 1.0 / jnp.sum(jnp.where(is_group, jnp.exp(logits - g_max), 0.0), axis=-1, keepdims=True)
    first = N_GROUPS + EXPERTS_PER_GROUP * g_top
    cand = jnp.where((lane >= first) & (lane < first + EXPERTS_PER_GROUP), logits, neg)
    v1, i1 = first_max(cand)
    v2, i2 = first_max(jnp.where(lane == i1, neg, cand))
    t = jnp.exp(v2 - v1)
    w1 = g_w / (1.0 + t)
    w2 = g_w * t / (1.0 + t)
    ids = jnp.where(lane == 0.0, i1 - N_GROUPS, jnp.where(lane == 1.0, i2 - N_GROUPS, 0.0))
    ids_ref[...] = ids.astype(jnp.int32)
    wts_ref[...] = jnp.where(lane == 0.0, w1, jnp.where(lane == 1.0, w2, 0.0))


def _norm_and_route(x, g, wg, bg, we, be):
    m, d = x.shape
    tm = _tile(m, 256)
    pad = LANES - N_GROUPS - N_EXPERTS
    wr = jnp.concatenate([wg, we, jnp.zeros((d, pad), F32)], axis=1)
    br = jnp.concatenate([bg, be, jnp.zeros((pad,), F32)]).reshape(1, LANES)
    row = lambda c: pl.BlockSpec((tm, c), lambda i: (i, 0))
    return pl.pallas_call(
        _router_body,
        out_shape=[jax.ShapeDtypeStruct((m, d), F32),
                   jax.ShapeDtypeStruct((m, LANES), jnp.int32),
                   jax.ShapeDtypeStruct((m, LANES), F32)],
        grid=(m // tm,),
        in_specs=[row(d), pl.BlockSpec((1, d), lambda i: (0, 0)),
                  pl.BlockSpec((d, LANES), lambda i: (0, 0)), pl.BlockSpec((1, LANES), lambda i: (0, 0))],
        out_specs=[row(d), row(LANES), row(LANES)],
        compiler_params=_params("parallel"),
        name="norm_route",
    )(x, g.reshape(1, d), wr, br)


def _gather_body(idx_ref, cnt_ref, src_ref, zero_ref, out_ref, sem, *, n_src):
    n = cnt_ref[0]

    def row_copy(src, t, r):
        return pltpu.make_async_copy(src.at[pl.ds(t, 1)], out_ref.at[pl.ds(r, 1)], sem)

    def wait_one():
        row_copy(zero_ref, 0, 0).wait()

    def issue(r, _):
        t = idx_ref[r]

        @pl.when(t < n_src)
        def _():
            row_copy(src_ref, t, r).start()

        @pl.when(t >= n_src)
        def _():
            row_copy(zero_ref, 0, r).start()

        @pl.when(r >= GATHER_WINDOW)
        def _():
            wait_one()
        return 0

    lax.fori_loop(0, n, issue, 0)
    lax.fori_loop(0, jnp.minimum(n, GATHER_WINDOW), lambda r, _: (wait_one(), 0)[1], 0)


def _gather_rows(src, idx, count):
    n_src, width = src.shape
    grid_spec = pltpu.PrefetchScalarGridSpec(
        num_scalar_prefetch=2,
        grid=(1,),
        in_specs=[pl.BlockSpec(memory_space=pl.ANY), pl.BlockSpec(memory_space=pl.ANY)],
        out_specs=pl.BlockSpec(memory_space=pl.ANY),
        scratch_shapes=[pltpu.SemaphoreType.DMA(())],
    )
    return pl.pallas_call(
        functools.partial(_gather_body, n_src=n_src),
        out_shape=jax.ShapeDtypeStruct((idx.shape[0], width), src.dtype),
        grid_spec=grid_spec,
        compiler_params=pltpu.CompilerParams(dimension_semantics=("arbitrary",)),
        name="gather_rows",
    )(idx, count, src, jnp.zeros((8, width), src.dtype))


def _expert_up_body(be_ref, first_ref, used_ref, x_ref, w1_ref, w3_ref, h_ref, w1b, w3b):
    b = pl.program_id(1)

    @pl.when(first_ref[b] == 1)
    def _():
        w1b[...] = w1_ref[...].astype(BF16)
        w3b[...] = w3_ref[...].astype(BF16)

    @pl.when(b < used_ref[0])
    def _():
        x = x_ref[...].astype(BF16)
        a = jnp.dot(x, w1b[...], preferred_element_type=F32)
        g = jnp.dot(x, w3b[...], preferred_element_type=F32)
        h_ref[...] = (a * _sigmoid(a) * g).astype(h_ref.dtype)


def _expert_down_body(be_ref, first_ref, used_ref, h_ref, w2_ref, rw_ref, y_ref, w2b):
    b = pl.program_id(1)

    @pl.when(first_ref[b] == 1)
    def _():
        w2b[...] = w2_ref[...].astype(BF16)

    @pl.when(b < used_ref[0])
    def _():
        y_ref[...] = jnp.dot(h_ref[...], w2b[...], preferred_element_type=F32) * rw_ref[...]


def _expert_ffn(xs, row_w, blk_exp, blk_first, n_used, w1, w3, w2):
    n_rows, d = xs.shape
    f = w1.shape[2]
    n_blocks = n_rows // MOE_ROWS
    tf = _tile(f, COL_TILE)
    tn = _tile(d, 1024)
    last = lambda b, used: jnp.minimum(b, used[0] - 1)
    up_spec = pltpu.PrefetchScalarGridSpec(
        num_scalar_prefetch=3,
        grid=(f // tf, n_blocks),
        in_specs=[pl.BlockSpec((MOE_ROWS, d), lambda j, b, be, fi, us: (last(b, us), 0)),
                  pl.BlockSpec((None, d, tf), lambda j, b, be, fi, us: (be[b], 0, j)),
                  pl.BlockSpec((None, d, tf), lambda j, b, be, fi, us: (be[b], 0, j))],
        out_specs=pl.BlockSpec((MOE_ROWS, tf), lambda j, b, be, fi, us: (b, j)),
        scratch_shapes=[pltpu.VMEM((d, tf), BF16), pltpu.VMEM((d, tf), BF16)],
    )
    hidden = pl.pallas_call(
        _expert_up_body,
        out_shape=jax.ShapeDtypeStruct((n_rows, f), BF16),
        grid_spec=up_spec,
        compiler_params=_params("arbitrary", "arbitrary"),
        name="expert_up",
    )(blk_exp, blk_first, n_used, xs, w1, w3)
    down_spec = pltpu.PrefetchScalarGridSpec(
        num_scalar_prefetch=3,
        grid=(d // tn, n_blocks),
        in_specs=[pl.BlockSpec((MOE_ROWS, f), lambda j, b, be, fi, us: (last(b, us), 0)),
                  pl.BlockSpec((None, f, tn), lambda j, b, be, fi, us: (be[b], 0, j)),
                  pl.BlockSpec((MOE_ROWS, 1), lambda j, b, be, fi, us: (b, 0))],
        out_specs=pl.BlockSpec((MOE_ROWS, tn), lambda j, b, be, fi, us: (b, j)),
        scratch_shapes=[pltpu.VMEM((f, tn), BF16)],
    )
    return pl.pallas_call(
        _expert_down_body,
        out_shape=jax.ShapeDtypeStruct((n_rows, d), F32),
        grid_spec=down_spec,
        compiler_params=_params("arbitrary", "arbitrary"),
        name="expert_down",
    )(blk_exp, blk_first, n_used, hidden, w2, row_w.reshape(n_rows, 1))


def _route_layout(ids, wts, n_tok):
    n_assign = n_tok * TOP_K
    flat_e = ids.reshape(n_assign)
    flat_w = wts.reshape(n_assign)
    flat_t = jnp.repeat(jnp.arange(n_tok, dtype=jnp.int32), TOP_K)
    onehot = (flat_e[:, None] == jnp.arange(N_EXPERTS, dtype=jnp.int32)[None, :]).astype(jnp.int32)
    running = jnp.cumsum(onehot, axis=0)
    rank = jnp.take_along_axis(running, flat_e[:, None], axis=1)[:, 0] - 1
    counts = running[-1]
    padded = (counts + MOE_ROWS - 1) // MOE_ROWS * MOE_ROWS
    pad_end = jnp.cumsum(padded)
    dest = (pad_end - padded)[flat_e] + rank
    n_blocks = -(-n_assign // MOE_ROWS) + N_EXPERTS
    n_rows = n_blocks * MOE_ROWS
    row_tok = jnp.full((n_rows,), n_tok, jnp.int32).at[dest].set(flat_t)
    row_w = jnp.zeros((n_rows,), F32).at[dest].set(flat_w)
    blk = jnp.arange(n_blocks, dtype=jnp.int32)
    n_used = (pad_end[-1] // MOE_ROWS).astype(jnp.int32)
    blk_exp = jnp.searchsorted(pad_end, jnp.minimum(blk, n_used - 1) * MOE_ROWS, side='right').astype(jnp.int32)
    blk_exp = jnp.minimum(blk_exp, N_EXPERTS - 1)
    prev = jnp.concatenate([jnp.full((1,), -1, jnp.int32), blk_exp[:-1]])
    blk_first = ((blk_exp != prev) & (blk < n_used)).astype(jnp.int32)
    return row_tok, row_w, dest.astype(jnp.int32), blk_exp, blk_first, n_used.reshape(1)


def _final_body(x_ref, y_ref, g_ref, o_ref, *, d):
    y = y_ref[...]
    x = x_ref[...] + (y[:, :d] + y[:, d:])
    o_ref[...] = _rmsnorm_rows(x, g_ref[...])


def _combine_and_norm(x, y_pairs, g):
    m, d = x.shape
    tm = _tile(m, 256)
    return pl.pallas_call(
        functools.partial(_final_body, d=d),
        out_shape=jax.ShapeDtypeStruct((m, d), F32),
        grid=(m // tm,),
        in_specs=[pl.BlockSpec((tm, d), lambda i: (i, 0)), pl.BlockSpec((tm, TOP_K * d), lambda i: (i, 0)),
                  pl.BlockSpec((1, d), lambda i: (0, 0))],
        out_specs=pl.BlockSpec((tm, d), lambda i: (i, 0)),
        compiler_params=_params("parallel"),
        name="combine_norm",
    )(x, y_pairs, g.reshape(1, d))


def _layer(x, n_prompt, n_seq, n_new, past_len, cache_sb_k, cache_sb_v, state_ret, cache_mem_k, cache_mem_v,
           mem_prompt, norm_mix, w_in, ret_gn, w_out, norm_cross, norm_mem, w_cq, w_ck, w_cv, w_co, norm_ffn,
           wg, bg, we, be, w1, w3, w2):
    n_tok, d = x.shape
    d_ret = d // 2
    d_sb = d - d_ret
    ret_heads = d_ret // RET_HEAD_DIM
    sds = jax.ShapeDtypeStruct

    pos = jnp.concatenate([jnp.arange(n_prompt, dtype=jnp.int32),
                           jnp.tile(past_len + jnp.arange(n_new, dtype=jnp.int32), n_seq)])
    half = RET_HEAD_DIM // 2
    inv = ROPE_BASE ** (-jnp.arange(half, dtype=F32) / half)
    ang = pos.astype(F32)[:, None] * inv[None, :]
    cos, sin = jnp.cos(ang), jnp.sin(ang)
    log_gamma = jnp.log1p(-jnp.exp2(-5.0 - jnp.arange(ret_heads, dtype=F32)))

    h = _rmsnorm(x, norm_mix, BF16)
    rot = [(cos, half), (sin, half)]
    bf = lambda n: [(sds((n_tok, n), BF16), None)]
    (rq,) = _matmul([h], w_in, 0, d_ret, _ep_rotary(1.0), rot, bf(d_ret), "in_proj_ret_q")
    (rk,) = _matmul([h], w_in, d_ret, d_ret, _ep_rotary(RET_HEAD_DIM ** -0.5), rot, bf(d_ret), "in_proj_ret_k")
    (rv,) = _matmul([h], w_in, 2 * d_ret, d_ret, _ep_store(1.0), [], bf(d_ret), "in_proj_ret_v")
    (rg,) = _matmul([h], w_in, 3 * d_ret, d_ret, _ep_store(1.0), [], [(sds((n_tok, d_ret), F32), None)],
                    "in_proj_ret_gate")
    (sq,) = _matmul([h], w_in, 4 * d_ret, d_sb, _ep_store(SB_HEAD_DIM ** -0.5), [], bf(d_sb), "in_proj_sb_q")
    both = [(sds((n_tok, d_sb), F32), None), (sds((n_tok, d_sb), BF16), None)]
    sk, sk_b = _matmul([h], w_in, 4 * d_ret + d_sb, d_sb, _ep_store(1.0), [], both, "in_proj_sb_k")
    sv, sv_b = _matmul([h], w_in, 4 * d_ret + 2 * d_sb, d_sb, _ep_store(1.0), [], both, "in_proj_sb_v")

    rows_p = _tile(n_prompt, RET_BLOCK)
    zeros_state = jnp.zeros((1, ret_heads, RET_HEAD_DIM, RET_HEAD_DIM), F32)
    yr_p, state_p = _retention(log_gamma, rq, rk, rv, rg, ret_gn, zeros_state, 0, 1, n_prompt // rows_p, rows_p)
    yr_s, state_s = _retention(log_gamma, rq, rk, rv, rg, ret_gn, state_ret, n_prompt, n_seq, 1, n_new)
    sb_p = _sb_prompt(sq, sk_b, sv_b, n_prompt)
    sb_s = _sb_sample(sq, sk_b, sv_b, cache_sb_k.reshape(n_seq, past_len, d_sb),
                      cache_sb_v.reshape(n_seq, past_len, d_sb), n_prompt, n_seq, n_new)
    yr = jnp.concatenate([yr_p, yr_s], axis=0)
    sb = jnp.concatenate([sb_p, sb_s], axis=0)
    res = [(sds((n_tok, d), F32), None)]
    (x,) = _matmul([yr, sb], w_out, 0, d, _ep_residual, [(x, None)], res, "out_proj")

    n_mem = mem_prompt.shape[0]
    m = _rmsnorm(mem_prompt, norm_mem, BF16)
    mem_out = [(sds((n_mem, d), F32), None)]
    (mk,) = _matmul([m], w_ck, 0, d, _ep_store(1.0), [], mem_out, "mem_k")
    (mv,) = _matmul([m], w_cv, 0, d, _ep_store(1.0), [], mem_out, "mem_v")
    h = _rmsnorm(x, norm_cross, BF16)
    (cq,) = _matmul([h], w_cq, 0, d, _ep_store(1.0), [], bf(d), "cross_q")
    co_p = _cross_attend(cq, mk.reshape(1, n_mem, d), mv.reshape(1, n_mem, d), 0, n_prompt)
    co_s = _cross_attend(cq, cache_mem_k.reshape(n_seq, n_mem, d), cache_mem_v.reshape(n_seq, n_mem, d),
                         n_prompt, n_new)
    co = jnp.concatenate([co_p, co_s], axis=0)
    (x,) = _matmul([co], w_co, 0, d, _ep_residual, [(x, None)], res, "cross_out")

    hn, ids, wts = _norm_and_route(x, norm_ffn, wg, bg, we, be)
    row_tok, row_w, dest, blk_exp, blk_first, n_used = _route_layout(ids[:, :TOP_K], wts[:, :TOP_K], n_tok)
    xs = _gather_rows(hn, row_tok, n_used * MOE_ROWS)
    y_rows = _expert_ffn(xs, row_w, blk_exp, blk_first, n_used, w1, w3, w2)
    y_pairs = _gather_rows(y_rows, dest, jnp.full((1,), n_tok * TOP_K, jnp.int32))
    return x, y_pairs.reshape(n_tok, TOP_K * d), (sk, sv, state_p, state_s, mk, mv)


def kernel(x_prompt, x_sample, cache_sb_k, cache_sb_v, state_ret, cache_mem_k, cache_mem_v, mem_prompt, norm_mix, w_in, ret_gn, w_out, norm_cross, norm_mem, w_cq, w_ck, w_cv, w_co, norm_ffn, router_group_w, router_group_b, router_expert_w, router_expert_b, expert_w1, expert_w3, expert_w2, norm_final):
    batch, n_prompt, d = x_prompt.shape
    n_seq, n_new, _ = x_sample.shape
    depth, _, past_len, sb_heads, _ = cache_sb_k.shape
    n_mem = mem_prompt.shape[1]
    assert batch == 1 and depth == 1
    ret_heads = (d // 2) // RET_HEAD_DIM
    x = jnp.concatenate([x_prompt.reshape(n_prompt, d), x_sample.reshape(n_seq * n_new, d)], axis=0)
    x, y_pairs, (sk, sv, state_p, state_s, mk, mv) = _layer(
        x, n_prompt, n_seq, n_new, past_len, cache_sb_k[0], cache_sb_v[0], state_ret[0], cache_mem_k[0],
        cache_mem_v[0], mem_prompt[0], norm_mix[0], w_in[0], ret_gn[0], w_out[0], norm_cross[0], norm_mem[0],
        w_cq[0], w_ck[0], w_cv[0], w_co[0], norm_ffn[0], router_group_w[0], router_group_b[0],
        router_expert_w[0], router_expert_b[0], expert_w1[0], expert_w3[0], expert_w2[0])
    y = _combine_and_norm(x, y_pairs, norm_final)
    y_prompt = y[:n_prompt].reshape(1, n_prompt, d)
    y_sample = y[n_prompt:].reshape(n_seq, n_new, d)
    sb_shape_p = (1, 1, n_prompt, sb_heads, SB_HEAD_DIM)
    sb_shape_s = (1, n_seq, n_new, sb_heads, SB_HEAD_DIM)
    mem_shape = (1, 1, n_mem, MEM_HEADS, d // MEM_HEADS)
    return (y_prompt, y_sample,
            sk[:n_prompt].reshape(sb_shape_p), sv[:n_prompt].reshape(sb_shape_p),
            state_p.reshape(1, 1, ret_heads, RET_HEAD_DIM, RET_HEAD_DIM),
            mk.reshape(mem_shape), mv.reshape(mem_shape),
            sk[n_prompt:].reshape(sb_shape_s), sv[n_prompt:].reshape(sb_shape_s),
            state_s.reshape(1, n_seq, ret_heads, RET_HEAD_DIM, RET_HEAD_DIM))
```

```python
import functools

import jax
import jax.numpy as jnp
from jax import lax
from jax.experimental import pallas as pl
from jax.experimental.pallas import tpu as pltpu

BF16 = jnp.bfloat16
F32 = jnp.float32

EPS = 1e-6
CHUNK = 64
RET_HEAD_DIM = 256
SB_HEAD_DIM = 128
MEM_HEADS = 4
N_GROUPS = 4
EXPERTS_PER_GROUP = 8
N_EXPERTS = N_GROUPS * EXPERTS_PER_GROUP
TOP_K = 2
ROPE_BASE = 10000.0

LANES = 128
VMEM_LIMIT_BYTES = 56 * 1024 * 1024
ROW_TILE = 512
COL_TILE = 512
RET_BLOCK = 256
SB_BLOCK = 256
SB_HEADS_PER_STEP = 2
LOG2_E = 1.4426950408889634
MOE_ROWS = 256
GATHER_WINDOW = 32

_NT = (((1,), (1,)), ((), ()))
_TN = (((0,), (0,)), ((), ()))


def _tile(n, pref):
    t = min(pref, n)
    while n % t:
        t //= 2
    return t


def _params(*sem):
    return pltpu.CompilerParams(dimension_semantics=sem, vmem_limit_bytes=VMEM_LIMIT_BYTES)


def _sigmoid(x):
    return 1.0 / (1.0 + jnp.exp(-x))


def _rmsnorm_rows(x, g):
    ms = jnp.mean(x * x, axis=-1, keepdims=True)
    return x * lax.rsqrt(ms + EPS) * g


def _rmsnorm_body(x_ref, g_ref, o_ref):
    o_ref[...] = _rmsnorm_rows(x_ref[...], g_ref[...]).astype(o_ref.dtype)


def _rmsnorm(x, g, out_dtype):
    m, d = x.shape
    tm = _tile(m, 256)
    return pl.pallas_call(
        _rmsnorm_body,
        out_shape=jax.ShapeDtypeStruct((m, d), out_dtype),
        grid=(m // tm,),
        in_specs=[pl.BlockSpec((tm, d), lambda i: (i, 0)), pl.BlockSpec((1, d), lambda i: (0, 0))],
        out_specs=pl.BlockSpec((tm, d), lambda i: (i, 0)),
        compiler_params=_params("parallel"),
        name="rmsnorm",
    )(x, g.reshape(1, d))


def _matmul_body(*refs, n_lhs, n_extra, n_out, epilogue):
    x_refs = refs[:n_lhs]
    w_refs = refs[n_lhs:2 * n_lhs]
    extra = refs[2 * n_lhs:2 * n_lhs + n_extra]
    outs = refs[2 * n_lhs + n_extra:2 * n_lhs + n_extra + n_out]
    wb = refs[2 * n_lhs + n_extra + n_out:]

    @pl.when(pl.program_id(1) == 0)
    def _():
        for p in range(n_lhs):
            wb[p][...] = w_refs[p][...].astype(BF16)

    acc = jnp.dot(x_refs[0][...], wb[0][...], preferred_element_type=F32)
    for p in range(1, n_lhs):
        acc += jnp.dot(x_refs[p][...], wb[p][...], preferred_element_type=F32)
    epilogue(acc, extra, outs)


def _matmul(xs, w, col_off, n_cols, epilogue, extras, outs, name):
    m = xs[0].shape[0]
    tm = _tile(m, ROW_TILE)
    tn = _tile(n_cols, COL_TILE)
    assert col_off % tn == 0
    jo = col_off // tn
    in_specs, args = [], []
    for x in xs:
        in_specs.append(pl.BlockSpec((tm, x.shape[1]), lambda j, i: (i, 0)))
        args.append(x)
    row = 0
    for x in xs:
        kp = x.shape[1]
        assert row % kp == 0
        in_specs.append(pl.BlockSpec((kp, tn), lambda j, i, r=row // kp: (r, j + jo)))
        args.append(w)
        row += kp
    assert row == w.shape[0]

    def spec(cols):
        if cols is None:
            return pl.BlockSpec((tm, tn), lambda j, i: (i, j))
        return pl.BlockSpec((tm, cols), lambda j, i: (i, 0))

    for a, cols in extras:
        in_specs.append(spec(cols))
        args.append(a)
    body = functools.partial(_matmul_body, n_lhs=len(xs), n_extra=len(extras), n_out=len(outs),
                             epilogue=epilogue)
    res = pl.pallas_call(
        body,
        out_shape=[o for o, _ in outs],
        grid=(n_cols // tn, m // tm),
        in_specs=in_specs,
        out_specs=[spec(cols) for _, cols in outs],
        scratch_shapes=[pltpu.VMEM((x.shape[1], tn), BF16) for x in xs],
        compiler_params=_params("arbitrary", "arbitrary"),
        name=name,
    )(*args)
    return res


def _ep_store(scale):
    def ep(acc, extra, outs):
        for o in outs:
            o[...] = (acc * scale).astype(o.dtype) if scale != 1.0 else acc.astype(o.dtype)
    return ep


def _ep_residual(acc, extra, outs):
    outs[0][...] = extra[0][...] + acc


def _ep_rotary(scale):
    half = RET_HEAD_DIM // 2

    def ep(acc, extra, outs):
        cos = extra[0][...]
        sin = extra[1][...]
        for h in range(acc.shape[1] // RET_HEAD_DIM):
            lo = h * RET_HEAD_DIM
            x1 = acc[:, lo:lo + half]
            x2 = acc[:, lo + half:lo + RET_HEAD_DIM]
            outs[0][:, lo:lo + half] = ((x1 * cos - x2 * sin) * scale).astype(outs[0].dtype)
            outs[0][:, lo + half:lo + RET_HEAD_DIM] = ((x1 * sin + x2 * cos) * scale).astype(outs[0].dtype)
    return ep


def _retention_body(lg_ref, q_ref, k_ref, v_ref, g_ref, gn_ref, s0_ref, o_ref, sout_ref, s_scr, *, rows):
    h = pl.program_id(1)

    @pl.when(pl.program_id(2) == 0)
    def _():
        s_scr[...] = s0_ref[...]

    lg = lg_ref[h]
    q = q_ref[...]
    k = k_ref[...]
    v = v_ref[...]
    ii = lax.broadcasted_iota(jnp.int32, (rows, rows), 0)
    jj = lax.broadcasted_iota(jnp.int32, (rows, rows), 1)
    shift = CHUNK.bit_length() - 1
    dist = jnp.abs(ii - jj).astype(F32)
    decay = jnp.where((jj >> shift) <= (ii >> shift), jnp.exp(lg * dist), 0.0)
    scores = lax.dot_general(q, k, _NT, preferred_element_type=F32) * decay
    out = jnp.dot(scores.astype(BF16), v, preferred_element_type=F32)
    s = s_scr[...]
    idx = lax.broadcasted_iota(jnp.int32, (rows, 1), 0).astype(F32)
    out += jnp.exp(lg * (idx + 1.0)) * jnp.dot(q, s.astype(BF16), preferred_element_type=F32)
    kd = (k.astype(F32) * jnp.exp(lg * (rows - 1.0 - idx))).astype(BF16)
    s_new = jnp.exp(lg * rows) * s + lax.dot_general(kd, v, _TN, preferred_element_type=F32)
    s_scr[...] = s_new
    sout_ref[...] = s_new
    mu = jnp.mean(out, axis=-1, keepdims=True)
    cen = out - mu
    var = jnp.mean(cen * cen, axis=-1, keepdims=True)
    yr = cen * lax.rsqrt(var + EPS) * gn_ref[...]
    g = g_ref[...]
    o_ref[...] = (g * _sigmoid(g) * yr).astype(o_ref.dtype)


def _retention(log_gamma, q, k, v, gate, gn, s0, row_off, n_batch, n_steps, rows):
    heads = q.shape[1] // RET_HEAD_DIM
    assert row_off % rows == 0 and rows % CHUNK == 0
    ro = row_off // rows
    tok = lambda b, h, c, lg: (ro + b * n_steps + c, h)
    blk = pl.BlockSpec((rows, RET_HEAD_DIM), tok)
    st = pl.BlockSpec((None, None, RET_HEAD_DIM, RET_HEAD_DIM), lambda b, h, c, lg: (b, h, 0, 0))
    grid_spec = pltpu.PrefetchScalarGridSpec(
        num_scalar_prefetch=1,
        grid=(n_batch, heads, n_steps),
        in_specs=[blk, blk, blk, blk,
                  pl.BlockSpec((None, 1, RET_HEAD_DIM), lambda b, h, c, lg: (h, 0, 0)), st],
        out_specs=[pl.BlockSpec((rows, RET_HEAD_DIM), lambda b, h, c, lg: (b * n_steps + c, h)), st],
        scratch_shapes=[pltpu.VMEM((RET_HEAD_DIM, RET_HEAD_DIM), F32)],
    )
    return pl.pallas_call(
        functools.partial(_retention_body, rows=rows),
        out_shape=[jax.ShapeDtypeStruct((n_batch * n_steps * rows, heads * RET_HEAD_DIM), BF16),
                   jax.ShapeDtypeStruct((n_batch, heads, RET_HEAD_DIM, RET_HEAD_DIM), F32)],
        grid_spec=grid_spec,
        compiler_params=_params("arbitrary", "arbitrary", "arbitrary"),
        name="retention",
    )(log_gamma, q, k, v, gate, gn.reshape(heads, 1, RET_HEAD_DIM), s0)


def _neg_suffix(n):
    j = lax.broadcasted_iota(jnp.int32, (2 * n, n), 0) & (n - 1)
    s = lax.broadcasted_iota(jnp.int32, (2 * n, n), 1)
    return jnp.where(j >= s, -1.0, 0.0).astype(BF16)


def _lanes_to(x, n):
    return x[:, :n] if n <= LANES else jnp.concatenate([x] * (n // LANES), axis=1)


def _suffix_sums(fail, neg_ones):
    tk = fail.shape[1]
    hi, lo = _split_bf16(fail)
    if tk % LANES == 0:
        return jnp.dot(jnp.concatenate([hi, lo], axis=1), neg_ones, preferred_element_type=F32)
    return (jnp.dot(hi, neg_ones[:tk], preferred_element_type=F32)
            + jnp.dot(lo, neg_ones[:tk], preferred_element_type=F32))


def _sb_tiles(qs, kbs, vbs, neg_ones, carry, acc, masked):
    heads = range(len(qs))
    tk = kbs[0].shape[0]
    zs = [lax.dot_general(qs[h], kbs[h], _NT, preferred_element_type=F32) for h in heads]
    fails = [jnp.maximum(z, 0.0) + jnp.log2(1.0 + jnp.exp2(-jnp.abs(z))) for z in zs]
    if masked:
        t = lax.broadcasted_iota(jnp.int32, zs[0].shape, 0)
        s = lax.broadcasted_iota(jnp.int32, zs[0].shape, 1)
        valid = s < t
        fails = [jnp.where(valid, f, 0.0) for f in fails]
    suffixes = [_suffix_sums(f, neg_ones) for f in fails]
    probs = [jnp.exp2(zs[h] + suffixes[h] + _lanes_to(carry[h], tk)) for h in heads]
    if masked:
        probs = [jnp.where(valid, p, 0.0) for p in probs]
    for h in heads:
        acc[h] += jnp.dot(probs[h].astype(BF16), vbs[h], preferred_element_type=F32)
        carry[h] += jnp.broadcast_to(suffixes[h][:, :1], carry.shape[1:])


def _sb_body(q_ref, kd_ref, vd_ref, kp_ref, vp_ref, o_ref, carry, acc, *, tk, n_heads, past_per_step,
             past_static):
    tq = q_ref.shape[0]
    n_past = past_static if past_static is not None else pl.program_id(1) * past_per_step
    cols = lambda h: slice(h * SB_HEAD_DIM, (h + 1) * SB_HEAD_DIM)
    heads = range(n_heads)
    carry[...] = jnp.zeros(carry.shape, F32)
    acc[...] = jnp.zeros(acc.shape, F32)
    queries = lambda: [q_ref[:, cols(h)] for h in heads]
    _sb_tiles(queries(), [kd_ref[:, cols(h)].astype(BF16) for h in heads],
              [vd_ref[:, cols(h)].astype(BF16) for h in heads], _neg_suffix(tq), carry, acc, True)
    neg_ones = _neg_suffix(tk)

    def step(it, _):
        rows = pl.ds(pl.multiple_of((n_past - 1 - it) * tk, tk), tk)
        _sb_tiles(queries(), [kp_ref[rows, cols(h)].astype(BF16) for h in heads],
                  [vp_ref[rows, cols(h)].astype(BF16) for h in heads], neg_ones, carry, acc, False)
        return 0

    lax.fori_loop(0, n_past, step, 0)
    for h in range(n_heads):
        o_ref[:, cols(h)] = acc[h].astype(o_ref.dtype)


def _sb_scratch(n_heads, tq):
    return [pltpu.VMEM((n_heads, tq, LANES), F32), pltpu.VMEM((n_heads, tq, SB_HEAD_DIM), F32)]


def _sb_prompt(q, k, v, n_tok):
    heads = q.shape[1] // SB_HEAD_DIM
    tq = _tile(n_tok, SB_BLOCK)
    nh = SB_HEADS_PER_STEP
    width = nh * SB_HEAD_DIM
    blk = pl.BlockSpec((tq, width), lambda h, i: (i, h))
    full = pl.BlockSpec((n_tok, width), lambda h, i: (0, h))
    return pl.pallas_call(
        functools.partial(_sb_body, tk=tq, n_heads=nh, past_per_step=1, past_static=None),
        out_shape=jax.ShapeDtypeStruct((n_tok, heads * SB_HEAD_DIM), BF16),
        grid=(heads // nh, n_tok // tq),
        in_specs=[blk, blk, blk, full, full],
        out_specs=blk,
        scratch_shapes=_sb_scratch(nh, tq),
        compiler_params=_params("parallel", "arbitrary"),
        name="stick_breaking_prompt",
    )(q, k, v, k, v)


def _sb_sample(q, k, v, cache_k, cache_v, row_off, n_batch, n_new):
    heads = q.shape[1] // SB_HEAD_DIM
    past = cache_k.shape[1]
    tk = _tile(past, SB_BLOCK)
    assert row_off % n_new == 0
    ro = row_off // n_new
    nh = SB_HEADS_PER_STEP
    width = nh * SB_HEAD_DIM
    blk = pl.BlockSpec((n_new, width), lambda b, h: (ro + b, h))
    full = pl.BlockSpec((None, past, width), lambda b, h: (b, 0, h))
    return pl.pallas_call(
        functools.partial(_sb_body, tk=tk, n_heads=nh, past_per_step=0, past_static=past // tk),
        out_shape=jax.ShapeDtypeStruct((n_batch * n_new, heads * SB_HEAD_DIM), BF16),
        grid=(n_batch, heads // nh),
        in_specs=[blk, blk, blk, full, full],
        out_specs=pl.BlockSpec((n_new, width), lambda b, h: (b, h)),
        scratch_shapes=_sb_scratch(nh, n_new),
        compiler_params=_params("parallel", "arbitrary"),
        name="stick_breaking_sample",
    )(q, k, v, cache_k, cache_v)


def _cross_body(q_ref, k_ref, v_ref, o_ref, *, scale):
    s = lax.dot_general(q_ref[...], k_ref[...].astype(BF16), _NT, preferred_element_type=F32) * scale
    e = jnp.exp(s - jnp.max(s, axis=-1, keepdims=True))
    p = e / jnp.sum(e, axis=-1, keepdims=True)
    o_ref[...] = jnp.dot(p.astype(BF16), v_ref[...].astype(BF16),
                         preferred_element_type=F32).astype(o_ref.dtype)


def _cross_attend(q, mem_k, mem_v, row_off, rows_per_seq):
    n_seq, n_mem, d = mem_k.shape
    dh = d // MEM_HEADS
    tm = _tile(rows_per_seq, ROW_TILE)
    steps = rows_per_seq // tm
    assert row_off % tm == 0
    ro = row_off // tm
    mem = pl.BlockSpec((None, n_mem, dh), lambda i, h: (i // steps, 0, h))
    return pl.pallas_call(
        functools.partial(_cross_body, scale=dh ** -0.5),
        out_shape=jax.ShapeDtypeStruct((n_seq * rows_per_seq, d), BF16),
        grid=(n_seq * steps, MEM_HEADS),
        in_specs=[pl.BlockSpec((tm, dh), lambda i, h: (ro + i, h)), mem, mem],
        out_specs=pl.BlockSpec((tm, dh), lambda i, h: (i, h)),
        compiler_params=_params("parallel", "arbitrary"),
        name="cross_attention",
    )(q, mem_k, mem_v)


def _split_bf16(x):
    hi = x.astype(BF16)
    return hi, (x - hi.astype(F32)).astype(BF16)


def _router_body(x_ref, g_ref, wr_ref, br_ref, h_ref, ids_ref, wts_ref):
    h = _rmsnorm_rows(x_ref[...], g_ref[...])
    for c in range(h_ref.shape[1]):
        h_ref[:, c, :] = h[:, c * LANES:(c + 1) * LANES]
    h_hi, h_lo = _split_bf16(h)
    w_hi, w_lo = _split_bf16(wr_ref[...])
    logits = (jnp.dot(h_hi, w_hi, preferred_element_type=F32)
              + jnp.dot(h_hi, w_lo, preferred_element_type=F32)
              + jnp.dot(h_lo, w_hi, preferred_element_type=F32)) + br_ref[...]
    lane = lax.broadcasted_iota(jnp.int32, logits.shape, 1).astype(F32)
    big = float(LANES)
    neg = -jnp.inf

    def first_max(vals):
        top = jnp.max(vals, axis=-1, keepdims=True)
        return top, jnp.min(jnp.where(vals == top, lane, big), axis=-1, keepdims=True)

    is_group = lane < N_GROUPS
    g_max, g_top = first_max(jnp.where(is_group, logits, neg))
    g_w = 1.0 / jnp.sum(jnp.where(is_group, jnp.exp(logits - g_max), 0.0), axis=-1, keepdims=True)
    first = N_GROUPS + EXPERTS_PER_GROUP * g_top
    cand = jnp.where((lane >= first) & (lane < first + EXPERTS_PER_GROUP), logits, neg)
    v1, i1 = first_max(cand)
    v2, i2 = first_max(jnp.where(lane == i1, neg, cand))
    t = jnp.exp(v2 - v1)
    w1 = g_w / (1.0 + t)
    w2 = g_w * t / (1.0 + t)
    ids = jnp.where(lane == 0.0, i1 - N_GROUPS, jnp.where(lane == 1.0, i2 - N_GROUPS, 0.0))
    ids_ref[...] = ids.astype(jnp.int32)
    wts_ref[...] = jnp.where(lane == 0.0, w1, jnp.where(lane == 1.0, w2, 0.0))


def _norm_and_route(x, g, wg, bg, we, be):
    m, d = x.shape
    tm = _tile(m, 256)
    pad = LANES - N_GROUPS - N_EXPERTS
    wr = jnp.concatenate([wg, we, jnp.zeros((d, pad), F32)], axis=1)
    br = jnp.concatenate([bg, be, jnp.zeros((pad,), F32)]).reshape(1, LANES)
    row = lambda c: pl.BlockSpec((tm, c), lambda i: (i, 0))
    return pl.pallas_call(
        _router_body,
        out_shape=[jax.ShapeDtypeStruct((m, d // LANES, LANES), F32),
                   jax.ShapeDtypeStruct((m, LANES), jnp.int32),
                   jax.ShapeDtypeStruct((m, LANES), F32)],
        grid=(m // tm,),
        in_specs=[row(d), pl.BlockSpec((1, d), lambda i: (0, 0)),
                  pl.BlockSpec((d, LANES), lambda i: (0, 0)), pl.BlockSpec((1, LANES), lambda i: (0, 0))],
        out_specs=[pl.BlockSpec((tm, d // LANES, LANES), lambda i: (i, 0, 0)), row(LANES), row(LANES)],
        compiler_params=_params("parallel"),
        name="norm_route",
    )(x, g.reshape(1, d), wr, br)


def _gather_body(idx_ref, cnt_ref, src_ref, zero_ref, out_ref, sem, *, n_src):
    n = cnt_ref[0]

    def row_copy(src, t, r):
        return pltpu.make_async_copy(src.at[pl.ds(t, 1)], out_ref.at[pl.ds(r, 1)], sem)

    def wait_one():
        row_copy(zero_ref, 0, 0).wait()

    def issue(r, _):
        t = idx_ref[r]

        @pl.when(t < n_src)
        def _():
            row_copy(src_ref, t, r).start()

        @pl.when(t >= n_src)
        def _():
            row_copy(zero_ref, 0, r).start()

        @pl.when(r >= GATHER_WINDOW)
        def _():
            wait_one()
        return 0

    lax.fori_loop(0, n, issue, 0)
    lax.fori_loop(0, jnp.minimum(n, GATHER_WINDOW), lambda r, _: (wait_one(), 0)[1], 0)


def _gather_rows(src, idx, count):
    n_src = src.shape[0]
    row_shape = src.shape[1:]
    grid_spec = pltpu.PrefetchScalarGridSpec(
        num_scalar_prefetch=2,
        grid=(1,),
        in_specs=[pl.BlockSpec(memory_space=pl.ANY), pl.BlockSpec(memory_space=pl.ANY)],
        out_specs=pl.BlockSpec(memory_space=pl.ANY),
        scratch_shapes=[pltpu.SemaphoreType.DMA(())],
    )
    return pl.pallas_call(
        functools.partial(_gather_body, n_src=n_src),
        out_shape=jax.ShapeDtypeStruct((idx.shape[0],) + row_shape, src.dtype),
        grid_spec=grid_spec,
        compiler_params=pltpu.CompilerParams(dimension_semantics=("arbitrary",)),
        name="gather_rows",
    )(idx, count, src, jnp.zeros((1,) + row_shape, src.dtype))


def _expert_up_body(be_ref, first_ref, used_ref, x_ref, w1_ref, w3_ref, h_ref, w1b, w3b, xb):
    b = pl.program_id(1)

    @pl.when(first_ref[b] == 1)
    def _():
        w1b[...] = w1_ref[...].astype(BF16)
        w3b[...] = w3_ref[...].astype(BF16)

    @pl.when(b < used_ref[0])
    def _():
        for c in range(x_ref.shape[1]):
            xb[:, c * LANES:(c + 1) * LANES] = x_ref[:, c, :].astype(BF16)
        x = xb[...]
        a = jnp.dot(x, w1b[...], preferred_element_type=F32)
        g = jnp.dot(x, w3b[...], preferred_element_type=F32)
        h_ref[...] = (a * _sigmoid(a) * g).astype(h_ref.dtype)


def _expert_down_body(be_ref, first_ref, used_ref, h_ref, w2_ref, rw_ref, y_ref, w2b):
    b = pl.program_id(1)

    @pl.when(first_ref[b] == 1)
    def _():
        w2b[...] = w2_ref[...].astype(BF16)

    @pl.when(b < used_ref[0])
    def _():
        y = jnp.dot(h_ref[...], w2b[...], preferred_element_type=F32) * rw_ref[...]
        for c in range(y_ref.shape[1]):
            y_ref[:, c, :] = y[:, c * LANES:(c + 1) * LANES]


def _expert_ffn(xs, row_w, blk_exp, blk_first, n_used, w1, w3, w2):
    n_rows = xs.shape[0]
    d = xs.shape[1] * LANES
    f = w1.shape[2]
    n_blocks = n_rows // MOE_ROWS
    tf = _tile(f, COL_TILE)
    tn = _tile(d, 1024)
    last = lambda b, used: jnp.minimum(b, used[0] - 1)
    up_spec = pltpu.PrefetchScalarGridSpec(
        num_scalar_prefetch=3,
        grid=(f // tf, n_blocks),
        in_specs=[pl.BlockSpec((MOE_ROWS, d // LANES, LANES), lambda j, b, be, fi, us: (last(b, us), 0, 0)),
                  pl.BlockSpec((None, d, tf), lambda j, b, be, fi, us: (be[b], 0, j)),
                  pl.BlockSpec((None, d, tf), lambda j, b, be, fi, us: (be[b], 0, j))],
        out_specs=pl.BlockSpec((MOE_ROWS, tf), lambda j, b, be, fi, us: (b, j)),
        scratch_shapes=[pltpu.VMEM((d, tf), BF16), pltpu.VMEM((d, tf), BF16), pltpu.VMEM((MOE_ROWS, d), BF16)],
    )
    hidden = pl.pallas_call(
        _expert_up_body,
        out_shape=jax.ShapeDtypeStruct((n_rows, f), BF16),
        grid_spec=up_spec,
        compiler_params=_params("arbitrary", "arbitrary"),
        name="expert_up",
    )(blk_exp, blk_first, n_used, xs, w1, w3)
    down_spec = pltpu.PrefetchScalarGridSpec(
        num_scalar_prefetch=3,
        grid=(d // tn, n_blocks),
        in_specs=[pl.BlockSpec((MOE_ROWS, f), lambda j, b, be, fi, us: (last(b, us), 0)),
                  pl.BlockSpec((None, f, tn), lambda j, b, be, fi, us: (be[b], 0, j)),
                  pl.BlockSpec((MOE_ROWS, 1), lambda j, b, be, fi, us: (b, 0))],
        out_specs=pl.BlockSpec((MOE_ROWS, tn // LANES, LANES), lambda j, b, be, fi, us: (b, j, 0)),
        scratch_shapes=[pltpu.VMEM((f, tn), BF16)],
    )
    return pl.pallas_call(
        _expert_down_body,
        out_shape=jax.ShapeDtypeStruct((n_rows, d // LANES, LANES), F32),
        grid_spec=down_spec,
        compiler_params=_params("arbitrary", "arbitrary"),
        name="expert_down",
    )(blk_exp, blk_first, n_used, hidden, w2, row_w.reshape(n_rows, 1))


def _route_layout(ids, wts, n_tok):
    n_assign = n_tok * TOP_K
    flat_e = ids.reshape(n_assign)
    flat_w = wts.reshape(n_assign)
    flat_t = jnp.repeat(jnp.arange(n_tok, dtype=jnp.int32), TOP_K)
    onehot = (flat_e[:, None] == jnp.arange(N_EXPERTS, dtype=jnp.int32)[None, :]).astype(jnp.int32)
    running = jnp.cumsum(onehot, axis=0)
    rank = jnp.take_along_axis(running, flat_e[:, None], axis=1)[:, 0] - 1
    counts = running[-1]
    padded = (counts + MOE_ROWS - 1) // MOE_ROWS * MOE_ROWS
    pad_end = jnp.cumsum(padded)
    dest = (pad_end - padded)[flat_e] + rank
    n_blocks = -(-n_assign // MOE_ROWS) + N_EXPERTS
    n_rows = n_blocks * MOE_ROWS
    row_tok = jnp.full((n_rows,), n_tok, jnp.int32).at[dest].set(flat_t)
    row_w = jnp.zeros((n_rows,), F32).at[dest].set(flat_w)
    blk = jnp.arange(n_blocks, dtype=jnp.int32)
    n_used = (pad_end[-1] // MOE_ROWS).astype(jnp.int32)
    blk_exp = jnp.searchsorted(pad_end, jnp.minimum(blk, n_used - 1) * MOE_ROWS, side='right').astype(jnp.int32)
    blk_exp = jnp.minimum(blk_exp, N_EXPERTS - 1)
    prev = jnp.concatenate([jnp.full((1,), -1, jnp.int32), blk_exp[:-1]])
    blk_first = ((blk_exp != prev) & (blk < n_used)).astype(jnp.int32)
    return row_tok, row_w, dest.astype(jnp.int32), blk_exp, blk_first, n_used.reshape(1)


def _final_body(x_ref, y_ref, g_ref, o_ref, acc):
    for c in range(y_ref.shape[2]):
        cols = slice(c * LANES, (c + 1) * LANES)
        acc[:, cols] = x_ref[:, cols] + (y_ref[:, 0, c, :] + y_ref[:, 1, c, :])
    o_ref[...] = _rmsnorm_rows(acc[...], g_ref[...])


def _combine_and_norm(x, y_pairs, g):
    m, d = x.shape
    tm = _tile(m, 256)
    return pl.pallas_call(
        _final_body,
        out_shape=jax.ShapeDtypeStruct((m, d), F32),
        grid=(m // tm,),
        in_specs=[pl.BlockSpec((tm, d), lambda i: (i, 0)),
                  pl.BlockSpec((tm, TOP_K, d // LANES, LANES), lambda i: (i, 0, 0, 0)),
                  pl.BlockSpec((1, d), lambda i: (0, 0))],
        out_specs=pl.BlockSpec((tm, d), lambda i: (i, 0)),
        scratch_shapes=[pltpu.VMEM((tm, d), F32)],
        compiler_params=_params("parallel"),
        name="combine_norm",
    )(x, y_pairs, g.reshape(1, d))


def _layer(x, n_prompt, n_seq, n_new, past_len, cache_sb_k, cache_sb_v, state_ret, cache_mem_k, cache_mem_v,
           mem_prompt, norm_mix, w_in, ret_gn, w_out, norm_cross, norm_mem, w_cq, w_ck, w_cv, w_co, norm_ffn,
           wg, bg, we, be, w1, w3, w2):
    n_tok, d = x.shape
    d_ret = d // 2
    d_sb = d - d_ret
    ret_heads = d_ret // RET_HEAD_DIM
    sds = jax.ShapeDtypeStruct

    pos = jnp.concatenate([jnp.arange(n_prompt, dtype=jnp.int32),
                           jnp.tile(past_len + jnp.arange(n_new, dtype=jnp.int32), n_seq)])
    half = RET_HEAD_DIM // 2
    inv = ROPE_BASE ** (-jnp.arange(half, dtype=F32) / half)
    ang = pos.astype(F32)[:, None] * inv[None, :]
    cos, sin = jnp.cos(ang), jnp.sin(ang)
    log_gamma = jnp.log1p(-jnp.exp2(-5.0 - jnp.arange(ret_heads, dtype=F32)))

    h = _rmsnorm(x, norm_mix, BF16)
    rot = [(cos, half), (sin, half)]
    bf = lambda n: [(sds((n_tok, n), BF16), None)]
    (rq,) = _matmul([h], w_in, 0, d_ret, _ep_rotary(1.0), rot, bf(d_ret), "in_proj_ret_q")
    (rk,) = _matmul([h], w_in, d_ret, d_ret, _ep_rotary(RET_HEAD_DIM ** -0.5), rot, bf(d_ret), "in_proj_ret_k")
    (rv,) = _matmul([h], w_in, 2 * d_ret, d_ret, _ep_store(1.0), [], bf(d_ret), "in_proj_ret_v")
    (rg,) = _matmul([h], w_in, 3 * d_ret, d_ret, _ep_store(1.0), [], [(sds((n_tok, d_ret), F32), None)],
                    "in_proj_ret_gate")
    (sq,) = _matmul([h], w_in, 4 * d_ret, d_sb, _ep_store(SB_HEAD_DIM ** -0.5 * LOG2_E), [], bf(d_sb),
                    "in_proj_sb_q")
    both = [(sds((n_tok, d_sb), F32), None), (sds((n_tok, d_sb), BF16), None)]
    sk, sk_b = _matmul([h], w_in, 4 * d_ret + d_sb, d_sb, _ep_store(1.0), [], both, "in_proj_sb_k")
    sv, sv_b = _matmul([h], w_in, 4 * d_ret + 2 * d_sb, d_sb, _ep_store(1.0), [], both, "in_proj_sb_v")

    rows_p = _tile(n_prompt, RET_BLOCK)
    zeros_state = jnp.zeros((1, ret_heads, RET_HEAD_DIM, RET_HEAD_DIM), F32)
    yr_p, state_p = _retention(log_gamma, rq, rk, rv, rg, ret_gn, zeros_state, 0, 1, n_prompt // rows_p, rows_p)
    yr_s, state_s = _retention(log_gamma, rq, rk, rv, rg, ret_gn, state_ret, n_prompt, n_seq, 1, n_new)
    sb_p = _sb_prompt(sq, sk_b, sv_b, n_prompt)
    sb_s = _sb_sample(sq, sk_b, sv_b, cache_sb_k.reshape(n_seq, past_len, d_sb),
                      cache_sb_v.reshape(n_seq, past_len, d_sb), n_prompt, n_seq, n_new)
    yr = jnp.concatenate([yr_p, yr_s], axis=0)
    sb = jnp.concatenate([sb_p, sb_s], axis=0)
    res = [(sds((n_tok, d), F32), None)]
    (x,) = _matmul([yr, sb], w_out, 0, d, _ep_residual, [(x, None)], res, "out_proj")

    n_mem = mem_prompt.shape[0]
    m = _rmsnorm(mem_prompt, norm_mem, BF16)
    mem_out = [(sds((n_mem, d), F32), None)]
    (mk,) = _matmul([m], w_ck, 0, d, _ep_store(1.0), [], mem_out, "mem_k")
    (mv,) = _matmul([m], w_cv, 0, d, _ep_store(1.0), [], mem_out, "mem_v")
    h = _rmsnorm(x, norm_cross, BF16)
    (cq,) = _matmul([h], w_cq, 0, d, _ep_store(1.0), [], bf(d), "cross_q")
    co_p = _cross_attend(cq, mk.reshape(1, n_mem, d), mv.reshape(1, n_mem, d), 0, n_prompt)
    co_s = _cross_attend(cq, cache_mem_k.reshape(n_seq, n_mem, d), cache_mem_v.reshape(n_seq, n_mem, d),
                         n_prompt, n_new)
    co = jnp.concatenate([co_p, co_s], axis=0)
    (x,) = _matmul([co], w_co, 0, d, _ep_residual, [(x, None)], res, "cross_out")

    hn, ids, wts = _norm_and_route(x, norm_ffn, wg, bg, we, be)
    row_tok, row_w, dest, blk_exp, blk_first, n_used = _route_layout(ids[:, :TOP_K], wts[:, :TOP_K], n_tok)
    xs = _gather_rows(hn, row_tok, n_used * MOE_ROWS)
    y_rows = _expert_ffn(xs, row_w, blk_exp, blk_first, n_used, w1, w3, w2)
    y_pairs = _gather_rows(y_rows, dest, jnp.full((1,), n_tok * TOP_K, jnp.int32))
    return x, y_pairs.reshape(n_tok, TOP_K, d // LANES, LANES), (sk, sv, state_p, state_s, mk, mv)


def kernel(x_prompt, x_sample, cache_sb_k, cache_sb_v, state_ret, cache_mem_k, cache_mem_v, mem_prompt, norm_mix, w_in, ret_gn, w_out, norm_cross, norm_mem, w_cq, w_ck, w_cv, w_co, norm_ffn, router_group_w, router_group_b, router_expert_w, router_expert_b, expert_w1, expert_w3, expert_w2, norm_final):
    batch, n_prompt, d = x_prompt.shape
    n_seq, n_new, _ = x_sample.shape
    depth, _, past_len, sb_heads, _ = cache_sb_k.shape
    n_mem = mem_prompt.shape[1]
    assert batch == 1 and depth == 1
    ret_heads = (d // 2) // RET_HEAD_DIM
    x = jnp.concatenate([x_prompt.reshape(n_prompt, d), x_sample.reshape(n_seq * n_new, d)], axis=0)
    x, y_pairs, (sk, sv, state_p, state_s, mk, mv) = _layer(
        x, n_prompt, n_seq, n_new, past_len, cache_sb_k[0], cache_sb_v[0], state_ret[0], cache_mem_k[0],
        cache_mem_v[0], mem_prompt[0], norm_mix[0], w_in[0], ret_gn[0], w_out[0], norm_cross[0], norm_mem[0],
        w_cq[0], w_ck[0], w_cv[0], w_co[0], norm_ffn[0], router_group_w[0], router_group_b[0],
        router_expert_w[0], router_expert_b[0], expert_w1[0], expert_w3[0], expert_w2[0])
    y = _combine_and_norm(x, y_pairs, norm_final)
    y_prompt = y[:n_prompt].reshape(1, n_prompt, d)
    y_sample = y[n_prompt:].reshape(n_seq, n_new, d)
    sb_shape_p = (1, 1, n_prompt, sb_heads, SB_HEAD_DIM)
    sb_shape_s = (1, n_seq, n_new, sb_heads, SB_HEAD_DIM)
    mem_shape = (1, 1, n_mem, MEM_HEADS, d // MEM_HEADS)
    return (y_prompt, y_sample,
            sk[:n_prompt].reshape(sb_shape_p), sv[:n_prompt].reshape(sb_shape_p),
            state_p.reshape(1, 1, ret_heads, RET_HEAD_DIM, RET_HEAD_DIM),
            mk.reshape(mem_shape), mv.reshape(mem_shape),
            sk[n_prompt:].reshape(sb_shape_s), sv[n_prompt:].reshape(sb_shape_s),
            state_s.reshape(1, n_seq, ret_heads, RET_HEAD_DIM, RET_HEAD_DIM))
```

```python
import functools

import jax
import jax.numpy as jnp
from jax import lax
from jax.experimental import pallas as pl
from jax.experimental.pallas import tpu as pltpu

BF16 = jnp.bfloat16
F32 = jnp.float32

EPS = 1e-6
CHUNK = 64
RET_HEAD_DIM = 256
SB_HEAD_DIM = 128
MEM_HEADS = 4
N_GROUPS = 4
EXPERTS_PER_GROUP = 8
N_EXPERTS = N_GROUPS * EXPERTS_PER_GROUP
TOP_K = 2
ROPE_BASE = 10000.0

LANES = 128
VMEM_LIMIT_BYTES = 56 * 1024 * 1024
ROW_TILE = 512
COL_TILE = 512
RET_BLOCK = 256
SB_BLOCK = 256
SB_HEADS_PER_STEP = 2
LOG2_E = 1.4426950408889634
MOE_ROWS = 256

_NT = (((1,), (1,)), ((), ()))
_TN = (((0,), (0,)), ((), ()))


def _tile(n, pref):
    t = min(pref, n)
    while n % t:
        t //= 2
    return t


def _params(*sem):
    return pltpu.CompilerParams(dimension_semantics=sem, vmem_limit_bytes=VMEM_LIMIT_BYTES)


def _sigmoid(x):
    return 1.0 / (1.0 + jnp.exp(-x))


def _rmsnorm_rows(x, g):
    ms = jnp.mean(x * x, axis=-1, keepdims=True)
    return x * lax.rsqrt(ms + EPS) * g


def _rmsnorm_body(x_ref, g_ref, o_ref):
    o_ref[...] = _rmsnorm_rows(x_ref[...], g_ref[...]).astype(o_ref.dtype)


def _rmsnorm(x, g, out_dtype):
    m, d = x.shape
    tm = _tile(m, 256)
    return pl.pallas_call(
        _rmsnorm_body,
        out_shape=jax.ShapeDtypeStruct((m, d), out_dtype),
        grid=(m // tm,),
        in_specs=[pl.BlockSpec((tm, d), lambda i: (i, 0)), pl.BlockSpec((1, d), lambda i: (0, 0))],
        out_specs=pl.BlockSpec((tm, d), lambda i: (i, 0)),
        compiler_params=_params("parallel"),
        name="rmsnorm",
    )(x, g.reshape(1, d))


def _matmul_body(*refs, n_lhs, n_extra, n_out, epilogue):
    x_refs = refs[:n_lhs]
    w_refs = refs[n_lhs:2 * n_lhs]
    extra = refs[2 * n_lhs:2 * n_lhs + n_extra]
    outs = refs[2 * n_lhs + n_extra:2 * n_lhs + n_extra + n_out]
    wb = refs[2 * n_lhs + n_extra + n_out:]

    @pl.when(pl.program_id(1) == 0)
    def _():
        for p in range(n_lhs):
            wb[p][...] = w_refs[p][...].astype(BF16)

    acc = jnp.dot(x_refs[0][...], wb[0][...], preferred_element_type=F32)
    for p in range(1, n_lhs):
        acc += jnp.dot(x_refs[p][...], wb[p][...], preferred_element_type=F32)
    epilogue(acc, extra, outs)


def _matmul(xs, w, col_off, n_cols, epilogue, extras, outs, name):
    m = xs[0].shape[0]
    tm = _tile(m, ROW_TILE)
    tn = _tile(n_cols, COL_TILE)
    assert col_off % tn == 0
    jo = col_off // tn
    in_specs, args = [], []
    for x in xs:
        in_specs.append(pl.BlockSpec((tm, x.shape[1]), lambda j, i: (i, 0)))
        args.append(x)
    row = 0
    for x in xs:
        kp = x.shape[1]
        assert row % kp == 0
        in_specs.append(pl.BlockSpec((kp, tn), lambda j, i, r=row // kp: (r, j + jo)))
        args.append(w)
        row += kp
    assert row == w.shape[0]

    def spec(cols):
        if cols is None:
            return pl.BlockSpec((tm, tn), lambda j, i: (i, j))
        return pl.BlockSpec((tm, cols), lambda j, i: (i, 0))

    for a, cols in extras:
        in_specs.append(spec(cols))
        args.append(a)
    body = functools.partial(_matmul_body, n_lhs=len(xs), n_extra=len(extras), n_out=len(outs),
                             epilogue=epilogue)
    res = pl.pallas_call(
        body,
        out_shape=[o for o, _ in outs],
        grid=(n_cols // tn, m // tm),
        in_specs=in_specs,
        out_specs=[spec(cols) for _, cols in outs],
        scratch_shapes=[pltpu.VMEM((x.shape[1], tn), BF16) for x in xs],
        compiler_params=_params("arbitrary", "arbitrary"),
        name=name,
    )(*args)
    return res


def _ep_store(scale):
    def ep(acc, extra, outs):
        for o in outs:
            o[...] = (acc * scale).astype(o.dtype) if scale != 1.0 else acc.astype(o.dtype)
    return ep


def _ep_residual(acc, extra, outs):
    outs[0][...] = extra[0][...] + acc


def _ep_rotary(scale):
    half = RET_HEAD_DIM // 2

    def ep(acc, extra, outs):
        cos = extra[0][...]
        sin = extra[1][...]
        for h in range(acc.shape[1] // RET_HEAD_DIM):
            lo = h * RET_HEAD_DIM
            x1 = acc[:, lo:lo + half]
            x2 = acc[:, lo + half:lo + RET_HEAD_DIM]
            outs[0][:, lo:lo + half] = ((x1 * cos - x2 * sin) * scale).astype(outs[0].dtype)
            outs[0][:, lo + half:lo + RET_HEAD_DIM] = ((x1 * sin + x2 * cos) * scale).astype(outs[0].dtype)
    return ep


def _retention_body(lg_ref, q_ref, k_ref, v_ref, g_ref, gn_ref, s0_ref, o_ref, sout_ref, s_scr, *, rows):
    h = pl.program_id(1)

    @pl.when(pl.program_id(2) == 0)
    def _():
        s_scr[...] = s0_ref[...]

    lg = lg_ref[h]
    q = q_ref[...]
    k = k_ref[...]
    v = v_ref[...]
    ii = lax.broadcasted_iota(jnp.int32, (rows, rows), 0)
    jj = lax.broadcasted_iota(jnp.int32, (rows, rows), 1)
    shift = CHUNK.bit_length() - 1
    dist = jnp.abs(ii - jj).astype(F32)
    decay = jnp.where((jj >> shift) <= (ii >> shift), jnp.exp(lg * dist), 0.0)
    scores = lax.dot_general(q, k, _NT, preferred_element_type=F32) * decay
    out = jnp.dot(scores.astype(BF16), v, preferred_element_type=F32)
    s = s_scr[...]
    idx = lax.broadcasted_iota(jnp.int32, (rows, 1), 0).astype(F32)
    out += jnp.exp(lg * (idx + 1.0)) * jnp.dot(q, s.astype(BF16), preferred_element_type=F32)
    kd = (k.astype(F32) * jnp.exp(lg * (rows - 1.0 - idx))).astype(BF16)
    s_new = jnp.exp(lg * rows) * s + lax.dot_general(kd, v, _TN, preferred_element_type=F32)
    s_scr[...] = s_new
    sout_ref[...] = s_new
    mu = jnp.mean(out, axis=-1, keepdims=True)
    cen = out - mu
    var = jnp.mean(cen * cen, axis=-1, keepdims=True)
    yr = cen * lax.rsqrt(var + EPS) * gn_ref[...]
    g = g_ref[...]
    o_ref[...] = (g * _sigmoid(g) * yr).astype(o_ref.dtype)


def _retention(log_gamma, q, k, v, gate, gn, s0, row_off, n_batch, n_steps, rows):
    heads = q.shape[1] // RET_HEAD_DIM
    assert row_off % rows == 0 and rows % CHUNK == 0
    ro = row_off // rows
    tok = lambda b, h, c, lg: (ro + b * n_steps + c, h)
    blk = pl.BlockSpec((rows, RET_HEAD_DIM), tok)
    st = pl.BlockSpec((None, None, RET_HEAD_DIM, RET_HEAD_DIM), lambda b, h, c, lg: (b, h, 0, 0))
    grid_spec = pltpu.PrefetchScalarGridSpec(
        num_scalar_prefetch=1,
        grid=(n_batch, heads, n_steps),
        in_specs=[blk, blk, blk, blk,
                  pl.BlockSpec((None, 1, RET_HEAD_DIM), lambda b, h, c, lg: (h, 0, 0)), st],
        out_specs=[pl.BlockSpec((rows, RET_HEAD_DIM), lambda b, h, c, lg: (b * n_steps + c, h)), st],
        scratch_shapes=[pltpu.VMEM((RET_HEAD_DIM, RET_HEAD_DIM), F32)],
    )
    return pl.pallas_call(
        functools.partial(_retention_body, rows=rows),
        out_shape=[jax.ShapeDtypeStruct((n_batch * n_steps * rows, heads * RET_HEAD_DIM), BF16),
                   jax.ShapeDtypeStruct((n_batch, heads, RET_HEAD_DIM, RET_HEAD_DIM), F32)],
        grid_spec=grid_spec,
        compiler_params=_params("arbitrary", "arbitrary", "arbitrary"),
        name="retention",
    )(log_gamma, q, k, v, gate, gn.reshape(heads, 1, RET_HEAD_DIM), s0)


def _neg_suffix(n):
    j = lax.broadcasted_iota(jnp.int32, (2 * n, n), 0) & (n - 1)
    s = lax.broadcasted_iota(jnp.int32, (2 * n, n), 1)
    return jnp.where(j >= s, -1.0, 0.0).astype(BF16)


def _lanes_to(x, n):
    return x[:, :n] if n <= LANES else jnp.concatenate([x] * (n // LANES), axis=1)


def _suffix_sums(fail, neg_ones):
    tk = fail.shape[1]
    hi, lo = _split_bf16(fail)
    if tk % LANES == 0:
        return jnp.dot(jnp.concatenate([hi, lo], axis=1), neg_ones, preferred_element_type=F32)
    return (jnp.dot(hi, neg_ones[:tk], preferred_element_type=F32)
            + jnp.dot(lo, neg_ones[:tk], preferred_element_type=F32))


def _sb_tiles(qs, kbs, vbs, neg_ones, carry, acc, masked):
    heads = range(len(qs))
    tk = kbs[0].shape[0]
    zs = [lax.dot_general(qs[h], kbs[h], _NT, preferred_element_type=F32) for h in heads]
    fails = [jnp.maximum(z, 0.0) + jnp.log2(1.0 + jnp.exp2(-jnp.abs(z))) for z in zs]
    if masked:
        t = lax.broadcasted_iota(jnp.int32, zs[0].shape, 0)
        s = lax.broadcasted_iota(jnp.int32, zs[0].shape, 1)
        valid = s < t
        fails = [jnp.where(valid, f, 0.0) for f in fails]
    suffixes = [_suffix_sums(f, neg_ones) for f in fails]
    probs = [jnp.exp2(zs[h] + suffixes[h] + _lanes_to(carry[h], tk)) for h in heads]
    if masked:
        probs = [jnp.where(valid, p, 0.0) for p in probs]
    for h in heads:
        acc[h] += jnp.dot(probs[h].astype(BF16), vbs[h], preferred_element_type=F32)
        carry[h] += jnp.broadcast_to(suffixes[h][:, :1], carry.shape[1:])


def _sb_body(q_ref, kd_ref, vd_ref, kp_ref, vp_ref, o_ref, carry, acc, *, tk, n_heads, past_per_step,
             past_static):
    tq = q_ref.shape[0]
    n_past = past_static if past_static is not None else pl.program_id(1) * past_per_step
    cols = lambda h: slice(h * SB_HEAD_DIM, (h + 1) * SB_HEAD_DIM)
    heads = range(n_heads)
    carry[...] = jnp.zeros(carry.shape, F32)
    acc[...] = jnp.zeros(acc.shape, F32)
    queries = lambda: [q_ref[:, cols(h)] for h in heads]
    _sb_tiles(queries(), [kd_ref[:, cols(h)].astype(BF16) for h in heads],
              [vd_ref[:, cols(h)].astype(BF16) for h in heads], _neg_suffix(tq), carry, acc, True)
    neg_ones = _neg_suffix(tk)

    def step(it, _):
        rows = pl.ds(pl.multiple_of((n_past - 1 - it) * tk, tk), tk)
        _sb_tiles(queries(), [kp_ref[rows, cols(h)].astype(BF16) for h in heads],
                  [vp_ref[rows, cols(h)].astype(BF16) for h in heads], neg_ones, carry, acc, False)
        return 0

    lax.fori_loop(0, n_past, step, 0)
    for h in range(n_heads):
        o_ref[:, cols(h)] = acc[h].astype(o_ref.dtype)


def _sb_scratch(n_heads, tq):
    return [pltpu.VMEM((n_heads, tq, LANES), F32), pltpu.VMEM((n_heads, tq, SB_HEAD_DIM), F32)]


def _sb_prompt(q, k, v, n_tok):
    heads = q.shape[1] // SB_HEAD_DIM
    tq = _tile(n_tok, SB_BLOCK)
    nh = SB_HEADS_PER_STEP
    width = nh * SB_HEAD_DIM
    blk = pl.BlockSpec((tq, width), lambda h, i: (i, h))
    full = pl.BlockSpec((n_tok, width), lambda h, i: (0, h))
    return pl.pallas_call(
        functools.partial(_sb_body, tk=tq, n_heads=nh, past_per_step=1, past_static=None),
        out_shape=jax.ShapeDtypeStruct((n_tok, heads * SB_HEAD_DIM), BF16),
        grid=(heads // nh, n_tok // tq),
        in_specs=[blk, blk, blk, full, full],
        out_specs=blk,
        scratch_shapes=_sb_scratch(nh, tq),
        compiler_params=_params("parallel", "arbitrary"),
        name="stick_breaking_prompt",
    )(q, k, v, k, v)


def _sb_sample(q, k, v, cache_k, cache_v, row_off, n_batch, n_new):
    heads = q.shape[1] // SB_HEAD_DIM
    past = cache_k.shape[1]
    tk = _tile(past, SB_BLOCK)
    assert row_off % n_new == 0
    ro = row_off // n_new
    nh = SB_HEADS_PER_STEP
    width = nh * SB_HEAD_DIM
    blk = pl.BlockSpec((n_new, width), lambda b, h: (ro + b, h))
    full = pl.BlockSpec((None, past, width), lambda b, h: (b, 0, h))
    return pl.pallas_call(
        functools.partial(_sb_body, tk=tk, n_heads=nh, past_per_step=0, past_static=past // tk),
        out_shape=jax.ShapeDtypeStruct((n_batch * n_new, heads * SB_HEAD_DIM), BF16),
        grid=(n_batch, heads // nh),
        in_specs=[blk, blk, blk, full, full],
        out_specs=pl.BlockSpec((n_new, width), lambda b, h: (b, h)),
        scratch_shapes=_sb_scratch(nh, n_new),
        compiler_params=_params("parallel", "arbitrary"),
        name="stick_breaking_sample",
    )(q, k, v, cache_k, cache_v)


def _cross_body(q_ref, k_ref, v_ref, o_ref, *, scale):
    s = lax.dot_general(q_ref[...], k_ref[...].astype(BF16), _NT, preferred_element_type=F32) * scale
    e = jnp.exp(s - jnp.max(s, axis=-1, keepdims=True))
    p = e / jnp.sum(e, axis=-1, keepdims=True)
    o_ref[...] = jnp.dot(p.astype(BF16), v_ref[...].astype(BF16),
                         preferred_element_type=F32).astype(o_ref.dtype)


def _cross_attend(q, mem_k, mem_v, row_off, rows_per_seq):
    n_seq, n_mem, d = mem_k.shape
    dh = d // MEM_HEADS
    tm = _tile(rows_per_seq, ROW_TILE)
    steps = rows_per_seq // tm
    assert row_off % tm == 0
    ro = row_off // tm
    mem = pl.BlockSpec((None, n_mem, dh), lambda i, h: (i // steps, 0, h))
    return pl.pallas_call(
        functools.partial(_cross_body, scale=dh ** -0.5),
        out_shape=jax.ShapeDtypeStruct((n_seq * rows_per_seq, d), BF16),
        grid=(n_seq * steps, MEM_HEADS),
        in_specs=[pl.BlockSpec((tm, dh), lambda i, h: (ro + i, h)), mem, mem],
        out_specs=pl.BlockSpec((tm, dh), lambda i, h: (i, h)),
        compiler_params=_params("parallel", "arbitrary"),
        name="cross_attention",
    )(q, mem_k, mem_v)


def _split_bf16(x):
    hi = x.astype(BF16)
    return hi, (x - hi.astype(F32)).astype(BF16)


def _router_body(x_ref, g_ref, wr_ref, br_ref, h_ref, ids_ref, wts_ref):
    h = _rmsnorm_rows(x_ref[...], g_ref[...])
    for c in range(h_ref.shape[1]):
        h_ref[:, c, :] = h[:, c * LANES:(c + 1) * LANES]
    h_hi, h_lo = _split_bf16(h)
    w_hi, w_lo = _split_bf16(wr_ref[...])
    logits = (jnp.dot(h_hi, w_hi, preferred_element_type=F32)
              + jnp.dot(h_hi, w_lo, preferred_element_type=F32)
              + jnp.dot(h_lo, w_hi, preferred_element_type=F32)) + br_ref[...]
    lane = lax.broadcasted_iota(jnp.int32, logits.shape, 1).astype(F32)
    big = float(LANES)
    neg = -jnp.inf

    def first_max(vals):
        top = jnp.max(vals, axis=-1, keepdims=True)
        return top, jnp.min(jnp.where(vals == top, lane, big), axis=-1, keepdims=True)

    is_group = lane < N_GROUPS
    g_max, g_top = first_max(jnp.where(is_group, logits, neg))
    g_w = 1.0 / jnp.sum(jnp.where(is_group, jnp.exp(logits - g_max), 0.0), axis=-1, keepdims=True)
    first = N_GROUPS + EXPERTS_PER_GROUP * g_top
    cand = jnp.where((lane >= first) & (lane < first + EXPERTS_PER_GROUP), logits, neg)
    v1, i1 = first_max(cand)
    v2, i2 = first_max(jnp.where(lane == i1, neg, cand))
    t = jnp.exp(v2 - v1)
    w1 = g_w / (1.0 + t)
    w2 = g_w * t / (1.0 + t)
    ids = jnp.where(lane == 0.0, i1 - N_GROUPS, jnp.where(lane == 1.0, i2 - N_GROUPS, 0.0))
    ids_ref[...] = ids.astype(jnp.int32)
    wts_ref[...] = jnp.where(lane == 0.0, w1, jnp.where(lane == 1.0, w2, 0.0))


def _norm_and_route(x, g, wg, bg, we, be):
    m, d = x.shape
    tm = _tile(m, 256)
    pad = LANES - N_GROUPS - N_EXPERTS
    wr = jnp.concatenate([wg, we, jnp.zeros((d, pad), F32)], axis=1)
    br = jnp.concatenate([bg, be, jnp.zeros((pad,), F32)]).reshape(1, LANES)
    row = lambda c: pl.BlockSpec((tm, c), lambda i: (i, 0))
    return pl.pallas_call(
        _router_body,
        out_shape=[jax.ShapeDtypeStruct((m, d // LANES, LANES), F32),
                   jax.ShapeDtypeStruct((m, LANES), jnp.int32),
                   jax.ShapeDtypeStruct((m, LANES), F32)],
        grid=(m // tm,),
        in_specs=[row(d), pl.BlockSpec((1, d), lambda i: (0, 0)),
                  pl.BlockSpec((d, LANES), lambda i: (0, 0)), pl.BlockSpec((1, LANES), lambda i: (0, 0))],
        out_specs=[pl.BlockSpec((tm, d // LANES, LANES), lambda i: (i, 0, 0)), row(LANES), row(LANES)],
        compiler_params=_params("parallel"),
        name="norm_route",
    )(x, g.reshape(1, d), wr, br)


def _start_row_gather(idx_ref, first, stride, src_ref, buf, sem):
    last = src_ref.shape[0] - 1

    def one(r, _):
        t = jnp.minimum(idx_ref[first + r * stride], last)
        pltpu.make_async_copy(src_ref.at[t], buf.at[r], sem).start()
        return 0

    lax.fori_loop(0, buf.shape[0], one, 0, unroll=8)


def _wait_row_gather(src_ref, buf, sem):
    pltpu.make_async_copy(src_ref.at[pl.ds(0, buf.shape[0])], buf, sem).wait()


def _dispatch_body(idx_ref, used_ref, valid_ref, src_ref, o_ref, buf, sem):
    b = pl.program_id(0)
    rows = buf.shape[1]

    def start(blk):
        _start_row_gather(idx_ref, blk * rows, 1, src_ref, buf.at[blk % 2], sem.at[blk % 2])

    @pl.when(b == 0)
    def _():
        start(0)

    @pl.when(b + 1 < used_ref[0])
    def _():
        start(b + 1)

    @pl.when(b < used_ref[0])
    def _():
        slot = b % 2
        _wait_row_gather(src_ref, buf.at[slot], sem.at[slot])
        valid = valid_ref[...] > 0.0
        for c in range(buf.shape[2]):
            o_ref[:, c * LANES:(c + 1) * LANES] = jnp.where(valid, buf[slot, :, c, :], 0.0).astype(o_ref.dtype)


def _dispatch_rows(src, row_tok, row_valid, n_used):
    n_rows = row_tok.shape[0]
    chunks = src.shape[1]
    grid_spec = pltpu.PrefetchScalarGridSpec(
        num_scalar_prefetch=2,
        grid=(n_rows // MOE_ROWS,),
        in_specs=[pl.BlockSpec((MOE_ROWS, 1), lambda b, idx, used: (b, 0)),
                  pl.BlockSpec(memory_space=pl.ANY)],
        out_specs=pl.BlockSpec((MOE_ROWS, chunks * LANES), lambda b, idx, used: (b, 0)),
        scratch_shapes=[pltpu.VMEM((2, MOE_ROWS, chunks, LANES), src.dtype), pltpu.SemaphoreType.DMA((2,))],
    )
    return pl.pallas_call(
        _dispatch_body,
        out_shape=jax.ShapeDtypeStruct((n_rows, chunks * LANES), BF16),
        grid_spec=grid_spec,
        compiler_params=_params("arbitrary"),
        name="dispatch_rows",
    )(row_tok, n_used, row_valid.reshape(n_rows, 1), src)


def _expert_up_body(be_ref, first_ref, used_ref, x_ref, w1_ref, w3_ref, h_ref, w1b, w3b):
    b = pl.program_id(1)

    @pl.when(first_ref[b] == 1)
    def _():
        w1b[...] = w1_ref[...].astype(BF16)
        w3b[...] = w3_ref[...].astype(BF16)

    @pl.when(b < used_ref[0])
    def _():
        x = x_ref[...]
        a = jnp.dot(x, w1b[...], preferred_element_type=F32)
        g = jnp.dot(x, w3b[...], preferred_element_type=F32)
        h_ref[...] = (a * _sigmoid(a) * g).astype(h_ref.dtype)


def _expert_down_body(be_ref, first_ref, used_ref, h_ref, w2_ref, rw_ref, y_ref, w2b):
    b = pl.program_id(1)

    @pl.when(first_ref[b] == 1)
    def _():
        w2b[...] = w2_ref[...].astype(BF16)

    @pl.when(b < used_ref[0])
    def _():
        y = jnp.dot(h_ref[...], w2b[...], preferred_element_type=F32) * rw_ref[...]
        for c in range(y_ref.shape[1]):
            y_ref[:, c, :] = y[:, c * LANES:(c + 1) * LANES]


def _expert_ffn(xs, row_w, blk_exp, blk_first, n_used, w1, w3, w2):
    n_rows, d = xs.shape
    f = w1.shape[2]
    n_blocks = n_rows // MOE_ROWS
    tf = _tile(f, COL_TILE)
    tn = _tile(d, 1024)
    last = lambda b, used: jnp.minimum(b, used[0] - 1)
    up_spec = pltpu.PrefetchScalarGridSpec(
        num_scalar_prefetch=3,
        grid=(f // tf, n_blocks),
        in_specs=[pl.BlockSpec((MOE_ROWS, d), lambda j, b, be, fi, us: (last(b, us), 0)),
                  pl.BlockSpec((None, d, tf), lambda j, b, be, fi, us: (be[b], 0, j)),
                  pl.BlockSpec((None, d, tf), lambda j, b, be, fi, us: (be[b], 0, j))],
        out_specs=pl.BlockSpec((MOE_ROWS, tf), lambda j, b, be, fi, us: (b, j)),
        scratch_shapes=[pltpu.VMEM((d, tf), BF16), pltpu.VMEM((d, tf), BF16)],
    )
    hidden = pl.pallas_call(
        _expert_up_body,
        out_shape=jax.ShapeDtypeStruct((n_rows, f), BF16),
        grid_spec=up_spec,
        compiler_params=_params("arbitrary", "arbitrary"),
        name="expert_up",
    )(blk_exp, blk_first, n_used, xs, w1, w3)
    down_spec = pltpu.PrefetchScalarGridSpec(
        num_scalar_prefetch=3,
        grid=(d // tn, n_blocks),
        in_specs=[pl.BlockSpec((MOE_ROWS, f), lambda j, b, be, fi, us: (last(b, us), 0)),
                  pl.BlockSpec((None, f, tn), lambda j, b, be, fi, us: (be[b], 0, j)),
                  pl.BlockSpec((MOE_ROWS, 1), lambda j, b, be, fi, us: (b, 0))],
        out_specs=pl.BlockSpec((MOE_ROWS, tn // LANES, LANES), lambda j, b, be, fi, us: (b, j, 0)),
        scratch_shapes=[pltpu.VMEM((f, tn), BF16)],
    )
    return pl.pallas_call(
        _expert_down_body,
        out_shape=jax.ShapeDtypeStruct((n_rows, d // LANES, LANES), F32),
        grid_spec=down_spec,
        compiler_params=_params("arbitrary", "arbitrary"),
        name="expert_down",
    )(blk_exp, blk_first, n_used, hidden, w2, row_w.reshape(n_rows, 1))


def _route_layout(ids, wts, n_tok):
    n_assign = n_tok * TOP_K
    flat_e = ids.reshape(n_assign)
    flat_w = wts.reshape(n_assign)
    flat_t = jnp.repeat(jnp.arange(n_tok, dtype=jnp.int32), TOP_K)
    onehot = (flat_e[:, None] == jnp.arange(N_EXPERTS, dtype=jnp.int32)[None, :]).astype(jnp.int32)
    running = jnp.cumsum(onehot, axis=0)
    rank = jnp.take_along_axis(running, flat_e[:, None], axis=1)[:, 0] - 1
    counts = running[-1]
    padded = (counts + MOE_ROWS - 1) // MOE_ROWS * MOE_ROWS
    pad_end = jnp.cumsum(padded)
    dest = (pad_end - padded)[flat_e] + rank
    n_blocks = -(-n_assign // MOE_ROWS) + N_EXPERTS
    n_rows = n_blocks * MOE_ROWS
    row_tok = jnp.full((n_rows,), n_tok, jnp.int32).at[dest].set(flat_t)
    row_w = jnp.zeros((n_rows,), F32).at[dest].set(flat_w)
    blk = jnp.arange(n_blocks, dtype=jnp.int32)
    n_used = (pad_end[-1] // MOE_ROWS).astype(jnp.int32)
    blk_exp = jnp.searchsorted(pad_end, jnp.minimum(blk, n_used - 1) * MOE_ROWS, side='right').astype(jnp.int32)
    blk_exp = jnp.minimum(blk_exp, N_EXPERTS - 1)
    prev = jnp.concatenate([jnp.full((1,), -1, jnp.int32), blk_exp[:-1]])
    blk_first = ((blk_exp != prev) & (blk < n_used)).astype(jnp.int32)
    row_valid = (row_tok < n_tok).astype(F32)
    return row_tok, row_valid, row_w, dest.astype(jnp.int32), blk_exp, blk_first, n_used.reshape(1)


def _final_body(dest_ref, x_ref, y_ref, g_ref, o_ref, buf, sem, acc):
    i = pl.program_id(0)
    tm = buf.shape[2]

    def start(blk):
        for k in range(TOP_K):
            _start_row_gather(dest_ref, blk * tm * TOP_K + k, TOP_K, y_ref, buf.at[blk % 2, k], sem.at[blk % 2, k])

    @pl.when(i == 0)
    def _():
        start(0)

    @pl.when(i + 1 < pl.num_programs(0))
    def _():
        start(i + 1)

    slot = i % 2
    for k in range(TOP_K):
        _wait_row_gather(y_ref, buf.at[slot, k], sem.at[slot, k])
    for c in range(buf.shape[3]):
        cols = slice(c * LANES, (c + 1) * LANES)
        acc[:, cols] = x_ref[:, cols] + (buf[slot, 0, :, c, :] + buf[slot, 1, :, c, :])
    o_ref[...] = _rmsnorm_rows(acc[...], g_ref[...])


def _combine_and_norm(x, y_rows, dest, g):
    m, d = x.shape
    tm = _tile(m, 256)
    chunks = d // LANES
    grid_spec = pltpu.PrefetchScalarGridSpec(
        num_scalar_prefetch=1,
        grid=(m // tm,),
        in_specs=[pl.BlockSpec((tm, d), lambda i, dest: (i, 0)),
                  pl.BlockSpec(memory_space=pl.ANY),
                  pl.BlockSpec((1, d), lambda i, dest: (0, 0))],
        out_specs=pl.BlockSpec((tm, d), lambda i, dest: (i, 0)),
        scratch_shapes=[pltpu.VMEM((2, TOP_K, tm, chunks, LANES), F32), pltpu.SemaphoreType.DMA((2, TOP_K)),
                        pltpu.VMEM((tm, d), F32)],
    )
    return pl.pallas_call(
        _final_body,
        out_shape=jax.ShapeDtypeStruct((m, d), F32),
        grid_spec=grid_spec,
        compiler_params=_params("arbitrary"),
        name="combine_norm",
    )(dest, x, y_rows, g.reshape(1, d))


def _layer(x, n_prompt, n_seq, n_new, past_len, cache_sb_k, cache_sb_v, state_ret, cache_mem_k, cache_mem_v,
           mem_prompt, norm_mix, w_in, ret_gn, w_out, norm_cross, norm_mem, w_cq, w_ck, w_cv, w_co, norm_ffn,
           wg, bg, we, be, w1, w3, w2):
    n_tok, d = x.shape
    d_ret = d // 2
    d_sb = d - d_ret
    ret_heads = d_ret // RET_HEAD_DIM
    sds = jax.ShapeDtypeStruct

    pos = jnp.concatenate([jnp.arange(n_prompt, dtype=jnp.int32),
                           jnp.tile(past_len + jnp.arange(n_new, dtype=jnp.int32), n_seq)])
    half = RET_HEAD_DIM // 2
    inv = ROPE_BASE ** (-jnp.arange(half, dtype=F32) / half)
    ang = pos.astype(F32)[:, None] * inv[None, :]
    cos, sin = jnp.cos(ang), jnp.sin(ang)
    log_gamma = jnp.log1p(-jnp.exp2(-5.0 - jnp.arange(ret_heads, dtype=F32)))

    h = _rmsnorm(x, norm_mix, BF16)
    rot = [(cos, half), (sin, half)]
    bf = lambda n: [(sds((n_tok, n), BF16), None)]
    (rq,) = _matmul([h], w_in, 0, d_ret, _ep_rotary(1.0), rot, bf(d_ret), "in_proj_ret_q")
    (rk,) = _matmul([h], w_in, d_ret, d_ret, _ep_rotary(RET_HEAD_DIM ** -0.5), rot, bf(d_ret), "in_proj_ret_k")
    (rv,) = _matmul([h], w_in, 2 * d_ret, d_ret, _ep_store(1.0), [], bf(d_ret), "in_proj_ret_v")
    (rg,) = _matmul([h], w_in, 3 * d_ret, d_ret, _ep_store(1.0), [], [(sds((n_tok, d_ret), F32), None)],
                    "in_proj_ret_gate")
    (sq,) = _matmul([h], w_in, 4 * d_ret, d_sb, _ep_store(SB_HEAD_DIM ** -0.5 * LOG2_E), [], bf(d_sb),
                    "in_proj_sb_q")
    both = [(sds((n_tok, d_sb), F32), None), (sds((n_tok, d_sb), BF16), None)]
    sk, sk_b = _matmul([h], w_in, 4 * d_ret + d_sb, d_sb, _ep_store(1.0), [], both, "in_proj_sb_k")
    sv, sv_b = _matmul([h], w_in, 4 * d_ret + 2 * d_sb, d_sb, _ep_store(1.0), [], both, "in_proj_sb_v")

    rows_p = _tile(n_prompt, RET_BLOCK)
    zeros_state = jnp.zeros((1, ret_heads, RET_HEAD_DIM, RET_HEAD_DIM), F32)
    yr_p, state_p = _retention(log_gamma, rq, rk, rv, rg, ret_gn, zeros_state, 0, 1, n_prompt // rows_p, rows_p)
    yr_s, state_s = _retention(log_gamma, rq, rk, rv, rg, ret_gn, state_ret, n_prompt, n_seq, 1, n_new)
    sb_p = _sb_prompt(sq, sk_b, sv_b, n_prompt)
    sb_s = _sb_sample(sq, sk_b, sv_b, cache_sb_k.reshape(n_seq, past_len, d_sb),
                      cache_sb_v.reshape(n_seq, past_len, d_sb), n_prompt, n_seq, n_new)
    yr = jnp.concatenate([yr_p, yr_s], axis=0)
    sb = jnp.concatenate([sb_p, sb_s], axis=0)
    res = [(sds((n_tok, d), F32), None)]
    (x,) = _matmul([yr, sb], w_out, 0, d, _ep_residual, [(x, None)], res, "out_proj")

    n_mem = mem_prompt.shape[0]
    m = _rmsnorm(mem_prompt, norm_mem, BF16)
    mem_out = [(sds((n_mem, d), F32), None)]
    (mk,) = _matmul([m], w_ck, 0, d, _ep_store(1.0), [], mem_out, "mem_k")
    (mv,) = _matmul([m], w_cv, 0, d, _ep_store(1.0), [], mem_out, "mem_v")
    h = _rmsnorm(x, norm_cross, BF16)
    (cq,) = _matmul([h], w_cq, 0, d, _ep_store(1.0), [], bf(d), "cross_q")
    co_p = _cross_attend(cq, mk.reshape(1, n_mem, d), mv.reshape(1, n_mem, d), 0, n_prompt)
    co_s = _cross_attend(cq, cache_mem_k.reshape(n_seq, n_mem, d), cache_mem_v.reshape(n_seq, n_mem, d),
                         n_prompt, n_new)
    co = jnp.concatenate([co_p, co_s], axis=0)
    (x,) = _matmul([co], w_co, 0, d, _ep_residual, [(x, None)], res, "cross_out")

    hn, ids, wts = _norm_and_route(x, norm_ffn, wg, bg, we, be)
    row_tok, row_valid, row_w, dest, blk_exp, blk_first, n_used = _route_layout(
        ids[:, :TOP_K], wts[:, :TOP_K], n_tok)
    xs = _dispatch_rows(hn, row_tok, row_valid, n_used)
    y_rows = _expert_ffn(xs, row_w, blk_exp, blk_first, n_used, w1, w3, w2)
    return x, y_rows, dest, (sk, sv, state_p, state_s, mk, mv)


def kernel(x_prompt, x_sample, cache_sb_k, cache_sb_v, state_ret, cache_mem_k, cache_mem_v, mem_prompt, norm_mix, w_in, ret_gn, w_out, norm_cross, norm_mem, w_cq, w_ck, w_cv, w_co, norm_ffn, router_group_w, router_group_b, router_expert_w, router_expert_b, expert_w1, expert_w3, expert_w2, norm_final):
    batch, n_prompt, d = x_prompt.shape
    n_seq, n_new, _ = x_sample.shape
    depth, _, past_len, sb_heads, _ = cache_sb_k.shape
    n_mem = mem_prompt.shape[1]
    assert batch == 1 and depth == 1
    ret_heads = (d // 2) // RET_HEAD_DIM
    x = jnp.concatenate([x_prompt.reshape(n_prompt, d), x_sample.reshape(n_seq * n_new, d)], axis=0)
    x, y_rows, dest, (sk, sv, state_p, state_s, mk, mv) = _layer(
        x, n_prompt, n_seq, n_new, past_len, cache_sb_k[0], cache_sb_v[0], state_ret[0], cache_mem_k[0],
        cache_mem_v[0], mem_prompt[0], norm_mix[0], w_in[0], ret_gn[0], w_out[0], norm_cross[0], norm_mem[0],
        w_cq[0], w_ck[0], w_cv[0], w_co[0], norm_ffn[0], router_group_w[0], router_group_b[0],
        router_expert_w[0], router_expert_b[0], expert_w1[0], expert_w3[0], expert_w2[0])
    y = _combine_and_norm(x, y_rows, dest, norm_final)
    y_prompt = y[:n_prompt].reshape(1, n_prompt, d)
    y_sample = y[n_prompt:].reshape(n_seq, n_new, d)
    sb_shape_p = (1, 1, n_prompt, sb_heads, SB_HEAD_DIM)
    sb_shape_s = (1, n_seq, n_new, sb_heads, SB_HEAD_DIM)
    mem_shape = (1, 1, n_mem, MEM_HEADS, d // MEM_HEADS)
    return (y_prompt, y_sample,
            sk[:n_prompt].reshape(sb_shape_p), sv[:n_prompt].reshape(sb_shape_p),
            state_p.reshape(1, 1, ret_heads, RET_HEAD_DIM, RET_HEAD_DIM),
            mk.reshape(mem_shape), mv.reshape(mem_shape),
            sk[n_prompt:].reshape(sb_shape_s), sv[n_prompt:].reshape(sb_shape_s),
            state_s.reshape(1, n_seq, ret_heads, RET_HEAD_DIM, RET_HEAD_DIM))
```

```python
import functools

import jax
import jax.numpy as jnp
from jax import lax
from jax.experimental import pallas as pl
from jax.experimental.pallas import tpu as pltpu

BF16 = jnp.bfloat16
F32 = jnp.float32

EPS = 1e-6
CHUNK = 64
RET_HEAD_DIM = 256
SB_HEAD_DIM = 128
MEM_HEADS = 4
N_GROUPS = 4
EXPERTS_PER_GROUP = 8
N_EXPERTS = N_GROUPS * EXPERTS_PER_GROUP
TOP_K = 2
ROPE_BASE = 10000.0

LANES = 128
VMEM_LIMIT_BYTES = 56 * 1024 * 1024
ROW_TILE = 512
COL_TILE = 512
RET_BLOCK = 256
SB_BLOCK = 256
SB_HEADS_PER_STEP = 2
LOG2_E = 1.4426950408889634
SB_DEAD_LOG2 = -64.0
MOE_ROWS = 256

_NT = (((1,), (1,)), ((), ()))
_TN = (((0,), (0,)), ((), ()))


def _tile(n, pref):
    t = min(pref, n)
    while n % t:
        t //= 2
    return t


def _params(*sem):
    return pltpu.CompilerParams(dimension_semantics=sem, vmem_limit_bytes=VMEM_LIMIT_BYTES)


def _sigmoid(x):
    return 1.0 / (1.0 + jnp.exp(-x))


def _rmsnorm_rows(x, g):
    ms = jnp.mean(x * x, axis=-1, keepdims=True)
    return x * lax.rsqrt(ms + EPS) * g


def _rmsnorm_body(x_ref, g_ref, o_ref):
    o_ref[...] = _rmsnorm_rows(x_ref[...], g_ref[...]).astype(o_ref.dtype)


def _rmsnorm(x, g, out_dtype):
    m, d = x.shape
    tm = _tile(m, 256)
    return pl.pallas_call(
        _rmsnorm_body,
        out_shape=jax.ShapeDtypeStruct((m, d), out_dtype),
        grid=(m // tm,),
        in_specs=[pl.BlockSpec((tm, d), lambda i: (i, 0)), pl.BlockSpec((1, d), lambda i: (0, 0))],
        out_specs=pl.BlockSpec((tm, d), lambda i: (i, 0)),
        compiler_params=_params("parallel"),
        name="rmsnorm",
    )(x, g.reshape(1, d))


def _matmul_body(*refs, n_lhs, n_extra, n_out, epilogue):
    x_refs = refs[:n_lhs]
    w_refs = refs[n_lhs:2 * n_lhs]
    extra = refs[2 * n_lhs:2 * n_lhs + n_extra]
    outs = refs[2 * n_lhs + n_extra:2 * n_lhs + n_extra + n_out]
    wb = refs[2 * n_lhs + n_extra + n_out:]

    @pl.when(pl.program_id(1) == 0)
    def _():
        for p in range(n_lhs):
            wb[p][...] = w_refs[p][...].astype(BF16)

    acc = jnp.dot(x_refs[0][...], wb[0][...], preferred_element_type=F32)
    for p in range(1, n_lhs):
        acc += jnp.dot(x_refs[p][...], wb[p][...], preferred_element_type=F32)
    epilogue(acc, extra, outs)


def _matmul(xs, w, col_off, n_cols, epilogue, extras, outs, name):
    m = xs[0].shape[0]
    tm = _tile(m, ROW_TILE)
    tn = _tile(n_cols, COL_TILE)
    assert col_off % tn == 0
    jo = col_off // tn
    in_specs, args = [], []
    for x in xs:
        in_specs.append(pl.BlockSpec((tm, x.shape[1]), lambda j, i: (i, 0)))
        args.append(x)
    row = 0
    for x in xs:
        kp = x.shape[1]
        assert row % kp == 0
        in_specs.append(pl.BlockSpec((kp, tn), lambda j, i, r=row // kp: (r, j + jo)))
        args.append(w)
        row += kp
    assert row == w.shape[0]

    def spec(cols):
        if cols is None:
            return pl.BlockSpec((tm, tn), lambda j, i: (i, j))
        return pl.BlockSpec((tm, cols), lambda j, i: (i, 0))

    for a, cols in extras:
        in_specs.append(spec(cols))
        args.append(a)
    body = functools.partial(_matmul_body, n_lhs=len(xs), n_extra=len(extras), n_out=len(outs),
                             epilogue=epilogue)
    res = pl.pallas_call(
        body,
        out_shape=[o for o, _ in outs],
        grid=(n_cols // tn, m // tm),
        in_specs=in_specs,
        out_specs=[spec(cols) for _, cols in outs],
        scratch_shapes=[pltpu.VMEM((x.shape[1], tn), BF16) for x in xs],
        compiler_params=_params("arbitrary", "arbitrary"),
        name=name,
    )(*args)
    return res


def _ep_store(scale):
    def ep(acc, extra, outs):
        for o in outs:
            o[...] = (acc * scale).astype(o.dtype) if scale != 1.0 else acc.astype(o.dtype)
    return ep


def _ep_residual(acc, extra, outs):
    outs[0][...] = extra[0][...] + acc


def _ep_rotary(scale):
    half = RET_HEAD_DIM // 2

    def ep(acc, extra, outs):
        cos = extra[0][...]
        sin = extra[1][...]
        for h in range(acc.shape[1] // RET_HEAD_DIM):
            lo = h * RET_HEAD_DIM
            x1 = acc[:, lo:lo + half]
            x2 = acc[:, lo + half:lo + RET_HEAD_DIM]
            outs[0][:, lo:lo + half] = ((x1 * cos - x2 * sin) * scale).astype(outs[0].dtype)
            outs[0][:, lo + half:lo + RET_HEAD_DIM] = ((x1 * sin + x2 * cos) * scale).astype(outs[0].dtype)
    return ep


def _retention_body(lg_ref, q_ref, k_ref, v_ref, g_ref, gn_ref, s0_ref, o_ref, sout_ref, s_scr, *, rows):
    h = pl.program_id(1)

    @pl.when(pl.program_id(2) == 0)
    def _():
        s_scr[...] = s0_ref[...]

    lg = lg_ref[h]
    q = q_ref[...]
    k = k_ref[...]
    v = v_ref[...]
    ii = lax.broadcasted_iota(jnp.int32, (rows, rows), 0)
    jj = lax.broadcasted_iota(jnp.int32, (rows, rows), 1)
    shift = CHUNK.bit_length() - 1
    dist = jnp.abs(ii - jj).astype(F32)
    decay = jnp.where((jj >> shift) <= (ii >> shift), jnp.exp(lg * dist), 0.0)
    scores = lax.dot_general(q, k, _NT, preferred_element_type=F32) * decay
    out = jnp.dot(scores.astype(BF16), v, preferred_element_type=F32)
    s = s_scr[...]
    idx = lax.broadcasted_iota(jnp.int32, (rows, 1), 0).astype(F32)
    out += jnp.exp(lg * (idx + 1.0)) * jnp.dot(q, s.astype(BF16), preferred_element_type=F32)
    kd = (k.astype(F32) * jnp.exp(lg * (rows - 1.0 - idx))).astype(BF16)
    s_new = jnp.exp(lg * rows) * s + lax.dot_general(kd, v, _TN, preferred_element_type=F32)
    s_scr[...] = s_new
    sout_ref[...] = s_new
    mu = jnp.mean(out, axis=-1, keepdims=True)
    cen = out - mu
    var = jnp.mean(cen * cen, axis=-1, keepdims=True)
    yr = cen * lax.rsqrt(var + EPS) * gn_ref[...]
    g = g_ref[...]
    o_ref[...] = (g * _sigmoid(g) * yr).astype(o_ref.dtype)


def _retention(log_gamma, q, k, v, gate, gn, s0, row_off, n_batch, n_steps, rows):
    heads = q.shape[1] // RET_HEAD_DIM
    assert row_off % rows == 0 and rows % CHUNK == 0
    ro = row_off // rows
    tok = lambda b, h, c, lg: (ro + b * n_steps + c, h)
    blk = pl.BlockSpec((rows, RET_HEAD_DIM), tok)
    st = pl.BlockSpec((None, None, RET_HEAD_DIM, RET_HEAD_DIM), lambda b, h, c, lg: (b, h, 0, 0))
    grid_spec = pltpu.PrefetchScalarGridSpec(
        num_scalar_prefetch=1,
        grid=(n_batch, heads, n_steps),
        in_specs=[blk, blk, blk, blk,
                  pl.BlockSpec((None, 1, RET_HEAD_DIM), lambda b, h, c, lg: (h, 0, 0)), st],
        out_specs=[pl.BlockSpec((rows, RET_HEAD_DIM), lambda b, h, c, lg: (b * n_steps + c, h)), st],
        scratch_shapes=[pltpu.VMEM((RET_HEAD_DIM, RET_HEAD_DIM), F32)],
    )
    return pl.pallas_call(
        functools.partial(_retention_body, rows=rows),
        out_shape=[jax.ShapeDtypeStruct((n_batch * n_steps * rows, heads * RET_HEAD_DIM), BF16),
                   jax.ShapeDtypeStruct((n_batch, heads, RET_HEAD_DIM, RET_HEAD_DIM), F32)],
        grid_spec=grid_spec,
        compiler_params=_params("arbitrary", "arbitrary", "arbitrary"),
        name="retention",
    )(log_gamma, q, k, v, gate, gn.reshape(heads, 1, RET_HEAD_DIM), s0)


def _neg_suffix(n):
    j = lax.broadcasted_iota(jnp.int32, (2 * n, n), 0) & (n - 1)
    s = lax.broadcasted_iota(jnp.int32, (2 * n, n), 1)
    return jnp.where(j >= s, -1.0, 0.0).astype(BF16)


def _lanes_to(x, n):
    return x[:, :n] if n <= LANES else jnp.concatenate([x] * (n // LANES), axis=1)


def _suffix_sums(fail, neg_ones):
    tk = fail.shape[1]
    hi, lo = _split_bf16(fail)
    if tk % LANES == 0:
        return jnp.dot(jnp.concatenate([hi, lo], axis=1), neg_ones, preferred_element_type=F32)
    return (jnp.dot(hi, neg_ones[:tk], preferred_element_type=F32)
            + jnp.dot(lo, neg_ones[:tk], preferred_element_type=F32))


def _sb_tiles(qs, kbs, vbs, neg_ones, carry, acc, masked):
    heads = range(len(qs))
    tk = kbs[0].shape[0]
    zs = [lax.dot_general(qs[h], kbs[h], _NT, preferred_element_type=F32) for h in heads]
    fails = [jnp.maximum(z, 0.0) + jnp.log2(1.0 + jnp.exp2(-jnp.abs(z))) for z in zs]
    if masked:
        t = lax.broadcasted_iota(jnp.int32, zs[0].shape, 0)
        s = lax.broadcasted_iota(jnp.int32, zs[0].shape, 1)
        valid = s < t
        fails = [jnp.where(valid, f, 0.0) for f in fails]
    suffixes = [_suffix_sums(f, neg_ones) for f in fails]
    probs = [jnp.exp2(zs[h] + suffixes[h] + _lanes_to(carry[h], tk)) for h in heads]
    if masked:
        probs = [jnp.where(valid, p, 0.0) for p in probs]
    for h in heads:
        acc[h] += jnp.dot(probs[h].astype(BF16), vbs[h], preferred_element_type=F32)
        carry[h] += jnp.broadcast_to(suffixes[h][:, :1], carry.shape[1:])


def _sb_body(q_ref, kd_ref, vd_ref, kp_ref, vp_ref, o_ref, carry, acc, *, tk, n_heads, past_per_step,
             past_static):
    tq = q_ref.shape[0]
    n_past = past_static if past_static is not None else pl.program_id(1) * past_per_step
    cols = lambda h: slice(h * SB_HEAD_DIM, (h + 1) * SB_HEAD_DIM)
    heads = range(n_heads)
    carry[...] = jnp.zeros(carry.shape, F32)
    acc[...] = jnp.zeros(acc.shape, F32)
    queries = lambda: [q_ref[:, cols(h)] for h in heads]
    _sb_tiles(queries(), [kd_ref[:, cols(h)].astype(BF16) for h in heads],
              [vd_ref[:, cols(h)].astype(BF16) for h in heads], _neg_suffix(tq), carry, acc, True)
    neg_ones = _neg_suffix(tk)

    alive = lambda: jnp.max(carry[...]) > SB_DEAD_LOG2

    def step(state):
        it = state[0]
        rows = pl.ds(pl.multiple_of((n_past - 1 - it) * tk, tk), tk)
        _sb_tiles(queries(), [kp_ref[rows, cols(h)].astype(BF16) for h in heads],
                  [vp_ref[rows, cols(h)].astype(BF16) for h in heads], neg_ones, carry, acc, False)
        return it + 1, alive()

    lax.while_loop(lambda state: (state[0] < n_past) & state[1], step, (jnp.int32(0), alive()))
    for h in range(n_heads):
        o_ref[:, cols(h)] = acc[h].astype(o_ref.dtype)


def _sb_scratch(n_heads, tq):
    return [pltpu.VMEM((n_heads, tq, LANES), F32), pltpu.VMEM((n_heads, tq, SB_HEAD_DIM), F32)]


def _sb_prompt(q, k, v, n_tok):
    heads = q.shape[1] // SB_HEAD_DIM
    tq = _tile(n_tok, SB_BLOCK)
    nh = SB_HEADS_PER_STEP
    width = nh * SB_HEAD_DIM
    blk = pl.BlockSpec((tq, width), lambda h, i: (i, h))
    full = pl.BlockSpec((n_tok, width), lambda h, i: (0, h))
    return pl.pallas_call(
        functools.partial(_sb_body, tk=tq, n_heads=nh, past_per_step=1, past_static=None),
        out_shape=jax.ShapeDtypeStruct((n_tok, heads * SB_HEAD_DIM), BF16),
        grid=(heads // nh, n_tok // tq),
        in_specs=[blk, blk, blk, full, full],
        out_specs=blk,
        scratch_shapes=_sb_scratch(nh, tq),
        compiler_params=_params("parallel", "arbitrary"),
        name="stick_breaking_prompt",
    )(q, k, v, k, v)


def _sb_sample(q, k, v, cache_k, cache_v, row_off, n_batch, n_new):
    heads = q.shape[1] // SB_HEAD_DIM
    past = cache_k.shape[1]
    tk = _tile(past, SB_BLOCK)
    assert row_off % n_new == 0
    ro = row_off // n_new
    nh = SB_HEADS_PER_STEP
    width = nh * SB_HEAD_DIM
    blk = pl.BlockSpec((n_new, width), lambda b, h: (ro + b, h))
    full = pl.BlockSpec((None, past, width), lambda b, h: (b, 0, h))
    return pl.pallas_call(
        functools.partial(_sb_body, tk=tk, n_heads=nh, past_per_step=0, past_static=past // tk),
        out_shape=jax.ShapeDtypeStruct((n_batch * n_new, heads * SB_HEAD_DIM), BF16),
        grid=(n_batch, heads // nh),
        in_specs=[blk, blk, blk, full, full],
        out_specs=pl.BlockSpec((n_new, width), lambda b, h: (b, h)),
        scratch_shapes=_sb_scratch(nh, n_new),
        compiler_params=_params("parallel", "arbitrary"),
        name="stick_breaking_sample",
    )(q, k, v, cache_k, cache_v)


def _cross_body(q_ref, k_ref, v_ref, o_ref, *, scale):
    s = lax.dot_general(q_ref[...], k_ref[...].astype(BF16), _NT, preferred_element_type=F32) * scale
    e = jnp.exp(s - jnp.max(s, axis=-1, keepdims=True))
    p = e / jnp.sum(e, axis=-1, keepdims=True)
    o_ref[...] = jnp.dot(p.astype(BF16), v_ref[...].astype(BF16),
                         preferred_element_type=F32).astype(o_ref.dtype)


def _cross_attend(q, mem_k, mem_v, row_off, rows_per_seq):
    n_seq, n_mem, d = mem_k.shape
    dh = d // MEM_HEADS
    tm = _tile(rows_per_seq, ROW_TILE)
    steps = rows_per_seq // tm
    assert row_off % tm == 0
    ro = row_off // tm
    mem = pl.BlockSpec((None, n_mem, dh), lambda i, h: (i // steps, 0, h))
    return pl.pallas_call(
        functools.partial(_cross_body, scale=dh ** -0.5),
        out_shape=jax.ShapeDtypeStruct((n_seq * rows_per_seq, d), BF16),
        grid=(n_seq * steps, MEM_HEADS),
        in_specs=[pl.BlockSpec((tm, dh), lambda i, h: (ro + i, h)), mem, mem],
        out_specs=pl.BlockSpec((tm, dh), lambda i, h: (i, h)),
        compiler_params=_params("parallel", "arbitrary"),
        name="cross_attention",
    )(q, mem_k, mem_v)


def _split_bf16(x):
    hi = x.astype(BF16)
    return hi, (x - hi.astype(F32)).astype(BF16)


def _router_body(x_ref, g_ref, wr_ref, br_ref, h_ref, ids_ref, wts_ref):
    h = _rmsnorm_rows(x_ref[...], g_ref[...])
    for c in range(h_ref.shape[1]):
        h_ref[:, c, :] = h[:, c * LANES:(c + 1) * LANES]
    h_hi, h_lo = _split_bf16(h)
    w_hi, w_lo = _split_bf16(wr_ref[...])
    logits = (jnp.dot(h_hi, w_hi, preferred_element_type=F32)
              + jnp.dot(h_hi, w_lo, preferred_element_type=F32)
              + jnp.dot(h_lo, w_hi, preferred_element_type=F32)) + br_ref[...]
    lane = lax.broadcasted_iota(jnp.int32, logits.shape, 1).astype(F32)
    big = float(LANES)
    neg = -jnp.inf

    def first_max(vals):
        top = jnp.max(vals, axis=-1, keepdims=True)
        return top, jnp.min(jnp.where(vals == top, lane, big), axis=-1, keepdims=True)

    is_group = lane < N_GROUPS
    g_max, g_top = first_max(jnp.where(is_group, logits, neg))
    g_w = 1.0 / jnp.sum(jnp.where(is_group, jnp.exp(logits - g_max), 0.0), axis=-1, keepdims=True)
    first = N_GROUPS + EXPERTS_PER_GROUP * g_top
    cand = jnp.where((lane >= first) & (lane < first + EXPERTS_PER_GROUP), logits, neg)
    v1, i1 = first_max(cand)
    v2, i2 = first_max(jnp.where(lane == i1, neg, cand))
    t = jnp.exp(v2 - v1)
    w1 = g_w / (1.0 + t)
    w2 = g_w * t / (1.0 + t)
    ids = jnp.where(lane == 0.0, i1 - N_GROUPS, jnp.where(lane == 1.0, i2 - N_GROUPS, 0.0))
    ids_ref[...] = ids.astype(jnp.int32)
    wts_ref[...] = jnp.where(lane == 0.0, w1, jnp.where(lane == 1.0, w2, 0.0))


def _norm_and_route(x, g, wg, bg, we, be):
    m, d = x.shape
    tm = _tile(m, 256)
    pad = LANES - N_GROUPS - N_EXPERTS
    wr = jnp.concatenate([wg, we, jnp.zeros((d, pad), F32)], axis=1)
    br = jnp.concatenate([bg, be, jnp.zeros((pad,), F32)]).reshape(1, LANES)
    row = lambda c: pl.BlockSpec((tm, c), lambda i: (i, 0))
    return pl.pallas_call(
        _router_body,
        out_shape=[jax.ShapeDtypeStruct((m, d // LANES, LANES), F32),
                   jax.ShapeDtypeStruct((m, LANES), jnp.int32),
                   jax.ShapeDtypeStruct((m, LANES), F32)],
        grid=(m // tm,),
        in_specs=[row(d), pl.BlockSpec((1, d), lambda i: (0, 0)),
                  pl.BlockSpec((d, LANES), lambda i: (0, 0)), pl.BlockSpec((1, LANES), lambda i: (0, 0))],
        out_specs=[pl.BlockSpec((tm, d // LANES, LANES), lambda i: (i, 0, 0)), row(LANES), row(LANES)],
        compiler_params=_params("parallel"),
        name="norm_route",
    )(x, g.reshape(1, d), wr, br)


def _start_row_gather(idx_ref, first, stride, src_ref, buf, sem):
    last = src_ref.shape[0] - 1

    def one(r, _):
        t = jnp.minimum(idx_ref[first + r * stride], last)
        pltpu.make_async_copy(src_ref.at[t], buf.at[r], sem).start()
        return 0

    lax.fori_loop(0, buf.shape[0], one, 0, unroll=8)


def _wait_row_gather(src_ref, buf, sem):
    pltpu.make_async_copy(src_ref.at[pl.ds(0, buf.shape[0])], buf, sem).wait()


def _dispatch_body(idx_ref, used_ref, valid_ref, src_ref, o_ref, buf, sem):
    b = pl.program_id(0)
    rows = buf.shape[1]

    def start(blk):
        _start_row_gather(idx_ref, blk * rows, 1, src_ref, buf.at[blk % 2], sem.at[blk % 2])

    @pl.when(b == 0)
    def _():
        start(0)

    @pl.when(b + 1 < used_ref[0])
    def _():
        start(b + 1)

    @pl.when(b < used_ref[0])
    def _():
        slot = b % 2
        _wait_row_gather(src_ref, buf.at[slot], sem.at[slot])
        valid = valid_ref[...] > 0.0
        for c in range(buf.shape[2]):
            o_ref[:, c * LANES:(c + 1) * LANES] = jnp.where(valid, buf[slot, :, c, :], 0.0).astype(o_ref.dtype)


def _dispatch_rows(src, row_tok, row_valid, n_used):
    n_rows = row_tok.shape[0]
    chunks = src.shape[1]
    grid_spec = pltpu.PrefetchScalarGridSpec(
        num_scalar_prefetch=2,
        grid=(n_rows // MOE_ROWS,),
        in_specs=[pl.BlockSpec((MOE_ROWS, 1), lambda b, idx, used: (b, 0)),
                  pl.BlockSpec(memory_space=pl.ANY)],
        out_specs=pl.BlockSpec((MOE_ROWS, chunks * LANES), lambda b, idx, used: (b, 0)),
        scratch_shapes=[pltpu.VMEM((2, MOE_ROWS, chunks, LANES), src.dtype), pltpu.SemaphoreType.DMA((2,))],
    )
    return pl.pallas_call(
        _dispatch_body,
        out_shape=jax.ShapeDtypeStruct((n_rows, chunks * LANES), BF16),
        grid_spec=grid_spec,
        compiler_params=_params("arbitrary"),
        name="dispatch_rows",
    )(row_tok, n_used, row_valid.reshape(n_rows, 1), src)


def _expert_up_body(be_ref, first_ref, used_ref, x_ref, w1_ref, w3_ref, h_ref, w1b, w3b):
    b = pl.program_id(1)

    @pl.when(first_ref[b] == 1)
    def _():
        w1b[...] = w1_ref[...].astype(BF16)
        w3b[...] = w3_ref[...].astype(BF16)

    @pl.when(b < used_ref[0])
    def _():
        x = x_ref[...]
        a = jnp.dot(x, w1b[...], preferred_element_type=F32)
        g = jnp.dot(x, w3b[...], preferred_element_type=F32)
        h_ref[...] = (a * _sigmoid(a) * g).astype(h_ref.dtype)


def _expert_down_body(be_ref, first_ref, used_ref, h_ref, w2_ref, rw_ref, y_ref, w2b):
    b = pl.program_id(1)

    @pl.when(first_ref[b] == 1)
    def _():
        w2b[...] = w2_ref[...].astype(BF16)

    @pl.when(b < used_ref[0])
    def _():
        y = jnp.dot(h_ref[...], w2b[...], preferred_element_type=F32) * rw_ref[...]
        for c in range(y_ref.shape[1]):
            y_ref[:, c, :] = y[:, c * LANES:(c + 1) * LANES]


def _expert_ffn(xs, row_w, blk_exp, blk_first, n_used, w1, w3, w2):
    n_rows, d = xs.shape
    f = w1.shape[2]
    n_blocks = n_rows // MOE_ROWS
    tf = _tile(f, COL_TILE)
    tn = _tile(d, 1024)
    last = lambda b, used: jnp.minimum(b, used[0] - 1)
    up_spec = pltpu.PrefetchScalarGridSpec(
        num_scalar_prefetch=3,
        grid=(f // tf, n_blocks),
        in_specs=[pl.BlockSpec((MOE_ROWS, d), lambda j, b, be, fi, us: (last(b, us), 0)),
                  pl.BlockSpec((None, d, tf), lambda j, b, be, fi, us: (be[b], 0, j)),
                  pl.BlockSpec((None, d, tf), lambda j, b, be, fi, us: (be[b], 0, j))],
        out_specs=pl.BlockSpec((MOE_ROWS, tf), lambda j, b, be, fi, us: (b, j)),
        scratch_shapes=[pltpu.VMEM((d, tf), BF16), pltpu.VMEM((d, tf), BF16)],
    )
    hidden = pl.pallas_call(
        _expert_up_body,
        out_shape=jax.ShapeDtypeStruct((n_rows, f), BF16),
        grid_spec=up_spec,
        compiler_params=_params("arbitrary", "arbitrary"),
        name="expert_up",
    )(blk_exp, blk_first, n_used, xs, w1, w3)
    down_spec = pltpu.PrefetchScalarGridSpec(
        num_scalar_prefetch=3,
        grid=(d // tn, n_blocks),
        in_specs=[pl.BlockSpec((MOE_ROWS, f), lambda j, b, be, fi, us: (last(b, us), 0)),
                  pl.BlockSpec((None, f, tn), lambda j, b, be, fi, us: (be[b], 0, j)),
                  pl.BlockSpec((MOE_ROWS, 1), lambda j, b, be, fi, us: (b, 0))],
        out_specs=pl.BlockSpec((MOE_ROWS, tn // LANES, LANES), lambda j, b, be, fi, us: (b, j, 0)),
        scratch_shapes=[pltpu.VMEM((f, tn), BF16)],
    )
    return pl.pallas_call(
        _expert_down_body,
        out_shape=jax.ShapeDtypeStruct((n_rows, d // LANES, LANES), F32),
        grid_spec=down_spec,
        compiler_params=_params("arbitrary", "arbitrary"),
        name="expert_down",
    )(blk_exp, blk_first, n_used, hidden, w2, row_w.reshape(n_rows, 1))


def _route_layout(ids, wts, n_tok):
    n_assign = n_tok * TOP_K
    flat_e = ids.reshape(n_assign)
    flat_w = wts.reshape(n_assign)
    flat_t = jnp.repeat(jnp.arange(n_tok, dtype=jnp.int32), TOP_K)
    onehot = (flat_e[:, None] == jnp.arange(N_EXPERTS, dtype=jnp.int32)[None, :]).astype(jnp.int32)
    running = jnp.cumsum(onehot, axis=0)
    rank = jnp.take_along_axis(running, flat_e[:, None], axis=1)[:, 0] - 1
    counts = running[-1]
    padded = (counts + MOE_ROWS - 1) // MOE_ROWS * MOE_ROWS
    pad_end = jnp.cumsum(padded)
    dest = (pad_end - padded)[flat_e] + rank
    n_blocks = -(-n_assign // MOE_ROWS) + N_EXPERTS
    n_rows = n_blocks * MOE_ROWS
    row_tok = jnp.full((n_rows,), n_tok, jnp.int32).at[dest].set(flat_t)
    row_w = jnp.zeros((n_rows,), F32).at[dest].set(flat_w)
    blk = jnp.arange(n_blocks, dtype=jnp.int32)
    n_used = (pad_end[-1] // MOE_ROWS).astype(jnp.int32)
    blk_exp = jnp.searchsorted(pad_end, jnp.minimum(blk, n_used - 1) * MOE_ROWS, side='right').astype(jnp.int32)
    blk_exp = jnp.minimum(blk_exp, N_EXPERTS - 1)
    prev = jnp.concatenate([jnp.full((1,), -1, jnp.int32), blk_exp[:-1]])
    blk_first = ((blk_exp != prev) & (blk < n_used)).astype(jnp.int32)
    row_valid = (row_tok < n_tok).astype(F32)
    return row_tok, row_valid, row_w, dest.astype(jnp.int32), blk_exp, blk_first, n_used.reshape(1)


def _final_body(dest_ref, x_ref, y_ref, g_ref, o_ref, buf, sem, acc):
    i = pl.program_id(0)
    tm = buf.shape[2]

    def start(blk):
        for k in range(TOP_K):
            _start_row_gather(dest_ref, blk * tm * TOP_K + k, TOP_K, y_ref, buf.at[blk % 2, k], sem.at[blk % 2, k])

    @pl.when(i == 0)
    def _():
        start(0)

    @pl.when(i + 1 < pl.num_programs(0))
    def _():
        start(i + 1)

    slot = i % 2
    for k in range(TOP_K):
        _wait_row_gather(y_ref, buf.at[slot, k], sem.at[slot, k])
    for c in range(buf.shape[3]):
        cols = slice(c * LANES, (c + 1) * LANES)
        acc[:, cols] = x_ref[:, cols] + (buf[slot, 0, :, c, :] + buf[slot, 1, :, c, :])
    o_ref[...] = _rmsnorm_rows(acc[...], g_ref[...])


def _combine_and_norm(x, y_rows, dest, g):
    m, d = x.shape
    tm = _tile(m, 256)
    chunks = d // LANES
    grid_spec = pltpu.PrefetchScalarGridSpec(
        num_scalar_prefetch=1,
        grid=(m // tm,),
        in_specs=[pl.BlockSpec((tm, d), lambda i, dest: (i, 0)),
                  pl.BlockSpec(memory_space=pl.ANY),
                  pl.BlockSpec((1, d), lambda i, dest: (0, 0))],
        out_specs=pl.BlockSpec((tm, d), lambda i, dest: (i, 0)),
        scratch_shapes=[pltpu.VMEM((2, TOP_K, tm, chunks, LANES), F32), pltpu.SemaphoreType.DMA((2, TOP_K)),
                        pltpu.VMEM((tm, d), F32)],
    )
    return pl.pallas_call(
        _final_body,
        out_shape=jax.ShapeDtypeStruct((m, d), F32),
        grid_spec=grid_spec,
        compiler_params=_params("arbitrary"),
        name="combine_norm",
    )(dest, x, y_rows, g.reshape(1, d))


def _layer(x, n_prompt, n_seq, n_new, past_len, cache_sb_k, cache_sb_v, state_ret, cache_mem_k, cache_mem_v,
           mem_prompt, norm_mix, w_in, ret_gn, w_out, norm_cross, norm_mem, w_cq, w_ck, w_cv, w_co, norm_ffn,
           wg, bg, we, be, w1, w3, w2):
    n_tok, d = x.shape
    d_ret = d // 2
    d_sb = d - d_ret
    ret_heads = d_ret // RET_HEAD_DIM
    sds = jax.ShapeDtypeStruct

    pos = jnp.concatenate([jnp.arange(n_prompt, dtype=jnp.int32),
                           jnp.tile(past_len + jnp.arange(n_new, dtype=jnp.int32), n_seq)])
    half = RET_HEAD_DIM // 2
    inv = ROPE_BASE ** (-jnp.arange(half, dtype=F32) / half)
    ang = pos.astype(F32)[:, None] * inv[None, :]
    cos, sin = jnp.cos(ang), jnp.sin(ang)
    log_gamma = jnp.log1p(-jnp.exp2(-5.0 - jnp.arange(ret_heads, dtype=F32)))

    h = _rmsnorm(x, norm_mix, BF16)
    rot = [(cos, half), (sin, half)]
    bf = lambda n: [(sds((n_tok, n), BF16), None)]
    (rq,) = _matmul([h], w_in, 0, d_ret, _ep_rotary(1.0), rot, bf(d_ret), "in_proj_ret_q")
    (rk,) = _matmul([h], w_in, d_ret, d_ret, _ep_rotary(RET_HEAD_DIM ** -0.5), rot, bf(d_ret), "in_proj_ret_k")
    (rv,) = _matmul([h], w_in, 2 * d_ret, d_ret, _ep_store(1.0), [], bf(d_ret), "in_proj_ret_v")
    (rg,) = _matmul([h], w_in, 3 * d_ret, d_ret, _ep_store(1.0), [], [(sds((n_tok, d_ret), F32), None)],
                    "in_proj_ret_gate")
    (sq,) = _matmul([h], w_in, 4 * d_ret, d_sb, _ep_store(SB_HEAD_DIM ** -0.5 * LOG2_E), [], bf(d_sb),
                    "in_proj_sb_q")
    both = [(sds((n_tok, d_sb), F32), None), (sds((n_tok, d_sb), BF16), None)]
    sk, sk_b = _matmul([h], w_in, 4 * d_ret + d_sb, d_sb, _ep_store(1.0), [], both, "in_proj_sb_k")
    sv, sv_b = _matmul([h], w_in, 4 * d_ret + 2 * d_sb, d_sb, _ep_store(1.0), [], both, "in_proj_sb_v")

    rows_p = _tile(n_prompt, RET_BLOCK)
    zeros_state = jnp.zeros((1, ret_heads, RET_HEAD_DIM, RET_HEAD_DIM), F32)
    yr_p, state_p = _retention(log_gamma, rq, rk, rv, rg, ret_gn, zeros_state, 0, 1, n_prompt // rows_p, rows_p)
    yr_s, state_s = _retention(log_gamma, rq, rk, rv, rg, ret_gn, state_ret, n_prompt, n_seq, 1, n_new)
    sb_p = _sb_prompt(sq, sk_b, sv_b, n_prompt)
    sb_s = _sb_sample(sq, sk_b, sv_b, cache_sb_k.reshape(n_seq, past_len, d_sb),
                      cache_sb_v.reshape(n_seq, past_len, d_sb), n_prompt, n_seq, n_new)
    yr = jnp.concatenate([yr_p, yr_s], axis=0)
    sb = jnp.concatenate([sb_p, sb_s], axis=0)
    res = [(sds((n_tok, d), F32), None)]
    (x,) = _matmul([yr, sb], w_out, 0, d, _ep_residual, [(x, None)], res, "out_proj")

    n_mem = mem_prompt.shape[0]
    m = _rmsnorm(mem_prompt, norm_mem, BF16)
    mem_out = [(sds((n_mem, d), F32), None)]
    (mk,) = _matmul([m], w_ck, 0, d, _ep_store(1.0), [], mem_out, "mem_k")
    (mv,) = _matmul([m], w_cv, 0, d, _ep_store(1.0), [], mem_out, "mem_v")
    h = _rmsnorm(x, norm_cross, BF16)
    (cq,) = _matmul([h], w_cq, 0, d, _ep_store(1.0), [], bf(d), "cross_q")
    co_p = _cross_attend(cq, mk.reshape(1, n_mem, d), mv.reshape(1, n_mem, d), 0, n_prompt)
    co_s = _cross_attend(cq, cache_mem_k.reshape(n_seq, n_mem, d), cache_mem_v.reshape(n_seq, n_mem, d),
                         n_prompt, n_new)
    co = jnp.concatenate([co_p, co_s], axis=0)
    (x,) = _matmul([co], w_co, 0, d, _ep_residual, [(x, None)], res, "cross_out")

    hn, ids, wts = _norm_and_route(x, norm_ffn, wg, bg, we, be)
    row_tok, row_valid, row_w, dest, blk_exp, blk_first, n_used = _route_layout(
        ids[:, :TOP_K], wts[:, :TOP_K], n_tok)
    xs = _dispatch_rows(hn, row_tok, row_valid, n_used)
    y_rows = _expert_ffn(xs, row_w, blk_exp, blk_first, n_used, w1, w3, w2)
    return x, y_rows, dest, (sk, sv, state_p, state_s, mk, mv)


def kernel(x_prompt, x_sample, cache_sb_k, cache_sb_v, state_ret, cache_mem_k, cache_mem_v, mem_prompt, norm_mix, w_in, ret_gn, w_out, norm_cross, norm_mem, w_cq, w_ck, w_cv, w_co, norm_ffn, router_group_w, router_group_b, router_expert_w, router_expert_b, expert_w1, expert_w3, expert_w2, norm_final):
    batch, n_prompt, d = x_prompt.shape
    n_seq, n_new, _ = x_sample.shape
    depth, _, past_len, sb_heads, _ = cache_sb_k.shape
    n_mem = mem_prompt.shape[1]
    assert batch == 1 and depth == 1
    ret_heads = (d // 2) // RET_HEAD_DIM
    x = jnp.concatenate([x_prompt.reshape(n_prompt, d), x_sample.reshape(n_seq * n_new, d)], axis=0)
    x, y_rows, dest, (sk, sv, state_p, state_s, mk, mv) = _layer(
        x, n_prompt, n_seq, n_new, past_len, cache_sb_k[0], cache_sb_v[0], state_ret[0], cache_mem_k[0],
        cache_mem_v[0], mem_prompt[0], norm_mix[0], w_in[0], ret_gn[0], w_out[0], norm_cross[0], norm_mem[0],
        w_cq[0], w_ck[0], w_cv[0], w_co[0], norm_ffn[0], router_group_w[0], router_group_b[0],
        router_expert_w[0], router_expert_b[0], expert_w1[0], expert_w3[0], expert_w2[0])
    y = _combine_and_norm(x, y_rows, dest, norm_final)
    y_prompt = y[:n_prompt].reshape(1, n_prompt, d)
    y_sample = y[n_prompt:].reshape(n_seq, n_new, d)
    sb_shape_p = (1, 1, n_prompt, sb_heads, SB_HEAD_DIM)
    sb_shape_s = (1, n_seq, n_new, sb_heads, SB_HEAD_DIM)
    mem_shape = (1, 1, n_mem, MEM_HEADS, d // MEM_HEADS)
    return (y_prompt, y_sample,
            sk[:n_prompt].reshape(sb_shape_p), sv[:n_prompt].reshape(sb_shape_p),
            state_p.reshape(1, 1, ret_heads, RET_HEAD_DIM, RET_HEAD_DIM),
            mk.reshape(mem_shape), mv.reshape(mem_shape),
            sk[n_prompt:].reshape(sb_shape_s), sv[n_prompt:].reshape(sb_shape_s),
            state_s.reshape(1, n_seq, ret_heads, RET_HEAD_DIM, RET_HEAD_DIM))
```

```python
import functools

import jax
import jax.numpy as jnp
from jax import lax
from jax.experimental import pallas as pl
from jax.experimental.pallas import tpu as pltpu

BF16 = jnp.bfloat16
F32 = jnp.float32

EPS = 1e-6
CHUNK = 64
RET_HEAD_DIM = 256
SB_HEAD_DIM = 128
MEM_HEADS = 4
N_GROUPS = 4
EXPERTS_PER_GROUP = 8
N_EXPERTS = N_GROUPS * EXPERTS_PER_GROUP
TOP_K = 2
ROPE_BASE = 10000.0

LANES = 128
VMEM_LIMIT_BYTES = 56 * 1024 * 1024
ROW_TILE = 512
COL_TILE = 512
RET_BLOCK = 256
SB_BLOCK = 256
SB_HEADS_PER_STEP = 2
LOG2_E = 1.4426950408889634
SB_DEAD_LOG2 = -64.0
MOE_ROWS = 256

_NT = (((1,), (1,)), ((), ()))
_TN = (((0,), (0,)), ((), ()))


def _tile(n, pref):
    t = min(pref, n)
    while n % t:
        t //= 2
    return t


def _params(*sem):
    return pltpu.CompilerParams(dimension_semantics=sem, vmem_limit_bytes=VMEM_LIMIT_BYTES)


def _sigmoid(x):
    return 1.0 / (1.0 + jnp.exp(-x))


def _rmsnorm_rows(x, g):
    ms = jnp.mean(x * x, axis=-1, keepdims=True)
    return x * lax.rsqrt(ms + EPS) * g


def _rmsnorm_body(x_ref, g_ref, o_ref):
    o_ref[...] = _rmsnorm_rows(x_ref[...], g_ref[...]).astype(o_ref.dtype)


def _rmsnorm(x, g, out_dtype):
    m, d = x.shape
    tm = _tile(m, 256)
    return pl.pallas_call(
        _rmsnorm_body,
        out_shape=jax.ShapeDtypeStruct((m, d), out_dtype),
        grid=(m // tm,),
        in_specs=[pl.BlockSpec((tm, d), lambda i: (i, 0)), pl.BlockSpec((1, d), lambda i: (0, 0))],
        out_specs=pl.BlockSpec((tm, d), lambda i: (i, 0)),
        compiler_params=_params("parallel"),
        name="rmsnorm",
    )(x, g.reshape(1, d))


def _matmul_body(*refs, n_lhs, n_extra, n_out, epilogue):
    x_refs = refs[:n_lhs]
    w_refs = refs[n_lhs:2 * n_lhs]
    extra = refs[2 * n_lhs:2 * n_lhs + n_extra]
    outs = refs[2 * n_lhs + n_extra:2 * n_lhs + n_extra + n_out]
    wb = refs[2 * n_lhs + n_extra + n_out:]

    @pl.when(pl.program_id(1) == 0)
    def _():
        for p in range(n_lhs):
            wb[p][...] = w_refs[p][...].astype(BF16)

    acc = jnp.dot(x_refs[0][...], wb[0][...], preferred_element_type=F32)
    for p in range(1, n_lhs):
        acc += jnp.dot(x_refs[p][...], wb[p][...], preferred_element_type=F32)
    epilogue(acc, extra, outs)


def _matmul(xs, w, col_off, n_cols, epilogue, extras, outs, name):
    m = xs[0].shape[0]
    tm = _tile(m, ROW_TILE)
    tn = _tile(n_cols, COL_TILE)
    assert col_off % tn == 0
    jo = col_off // tn
    in_specs, args = [], []
    for x in xs:
        in_specs.append(pl.BlockSpec((tm, x.shape[1]), lambda j, i: (i, 0)))
        args.append(x)
    row = 0
    for x in xs:
        kp = x.shape[1]
        assert row % kp == 0
        in_specs.append(pl.BlockSpec((kp, tn), lambda j, i, r=row // kp: (r, j + jo)))
        args.append(w)
        row += kp
    assert row == w.shape[0]

    def spec(cols):
        if cols is None:
            return pl.BlockSpec((tm, tn), lambda j, i: (i, j))
        return pl.BlockSpec((tm, cols), lambda j, i: (i, 0))

    for a, cols in extras:
        in_specs.append(spec(cols))
        args.append(a)
    body = functools.partial(_matmul_body, n_lhs=len(xs), n_extra=len(extras), n_out=len(outs),
                             epilogue=epilogue)
    res = pl.pallas_call(
        body,
        out_shape=[o for o, _ in outs],
        grid=(n_cols // tn, m // tm),
        in_specs=in_specs,
        out_specs=[spec(cols) for _, cols in outs],
        scratch_shapes=[pltpu.VMEM((x.shape[1], tn), BF16) for x in xs],
        compiler_params=_params("arbitrary", "arbitrary"),
        name=name,
    )(*args)
    return res


def _ep_store(scale):
    def ep(acc, extra, outs):
        for o in outs:
            o[...] = (acc * scale).astype(o.dtype) if scale != 1.0 else acc.astype(o.dtype)
    return ep


def _ep_residual(acc, extra, outs):
    outs[0][...] = extra[0][...] + acc


def _ep_rotary(scale):
    half = RET_HEAD_DIM // 2

    def ep(acc, extra, outs):
        cos = extra[0][...]
        sin = extra[1][...]
        for h in range(acc.shape[1] // RET_HEAD_DIM):
            lo = h * RET_HEAD_DIM
            x1 = acc[:, lo:lo + half]
            x2 = acc[:, lo + half:lo + RET_HEAD_DIM]
            outs[0][:, lo:lo + half] = ((x1 * cos - x2 * sin) * scale).astype(outs[0].dtype)
            outs[0][:, lo + half:lo + RET_HEAD_DIM] = ((x1 * sin + x2 * cos) * scale).astype(outs[0].dtype)
    return ep


def _retention_body(lg_ref, q_ref, k_ref, v_ref, g_ref, gn_ref, s0_ref, o_ref, sout_ref, s_scr, *, rows):
    h = pl.program_id(1)

    @pl.when(pl.program_id(2) == 0)
    def _():
        s_scr[...] = s0_ref[...]

    lg = lg_ref[h]
    q = q_ref[...]
    k = k_ref[...]
    v = v_ref[...]
    ii = lax.broadcasted_iota(jnp.int32, (rows, rows), 0)
    jj = lax.broadcasted_iota(jnp.int32, (rows, rows), 1)
    shift = CHUNK.bit_length() - 1
    dist = jnp.abs(ii - jj).astype(F32)
    decay = jnp.where((jj >> shift) <= (ii >> shift), jnp.exp(lg * dist), 0.0)
    scores = lax.dot_general(q, k, _NT, preferred_element_type=F32) * decay
    out = jnp.dot(scores.astype(BF16), v, preferred_element_type=F32)
    s = s_scr[...]
    idx = lax.broadcasted_iota(jnp.int32, (rows, 1), 0).astype(F32)
    out += jnp.exp(lg * (idx + 1.0)) * jnp.dot(q, s.astype(BF16), preferred_element_type=F32)
    kd = (k.astype(F32) * jnp.exp(lg * (rows - 1.0 - idx))).astype(BF16)
    s_new = jnp.exp(lg * rows) * s + lax.dot_general(kd, v, _TN, preferred_element_type=F32)
    s_scr[...] = s_new
    sout_ref[...] = s_new
    mu = jnp.mean(out, axis=-1, keepdims=True)
    cen = out - mu
    var = jnp.mean(cen * cen, axis=-1, keepdims=True)
    yr = cen * lax.rsqrt(var + EPS) * gn_ref[...]
    g = g_ref[...]
    o_ref[...] = (g * _sigmoid(g) * yr).astype(o_ref.dtype)


def _retention(log_gamma, q, k, v, gate, gn, s0, row_off, n_batch, n_steps, rows):
    heads = q.shape[1] // RET_HEAD_DIM
    assert row_off % rows == 0 and rows % CHUNK == 0
    ro = row_off // rows
    tok = lambda b, h, c, lg: (ro + b * n_steps + c, h)
    blk = pl.BlockSpec((rows, RET_HEAD_DIM), tok)
    st = pl.BlockSpec((None, None, RET_HEAD_DIM, RET_HEAD_DIM), lambda b, h, c, lg: (b, h, 0, 0))
    grid_spec = pltpu.PrefetchScalarGridSpec(
        num_scalar_prefetch=1,
        grid=(n_batch, heads, n_steps),
        in_specs=[blk, blk, blk, blk,
                  pl.BlockSpec((None, 1, RET_HEAD_DIM), lambda b, h, c, lg: (h, 0, 0)), st],
        out_specs=[pl.BlockSpec((rows, RET_HEAD_DIM), lambda b, h, c, lg: (b * n_steps + c, h)), st],
        scratch_shapes=[pltpu.VMEM((RET_HEAD_DIM, RET_HEAD_DIM), F32)],
    )
    return pl.pallas_call(
        functools.partial(_retention_body, rows=rows),
        out_shape=[jax.ShapeDtypeStruct((n_batch * n_steps * rows, heads * RET_HEAD_DIM), BF16),
                   jax.ShapeDtypeStruct((n_batch, heads, RET_HEAD_DIM, RET_HEAD_DIM), F32)],
        grid_spec=grid_spec,
        compiler_params=_params("arbitrary", "arbitrary", "arbitrary"),
        name="retention",
    )(log_gamma, q, k, v, gate, gn.reshape(heads, 1, RET_HEAD_DIM), s0)


def _neg_suffix(n):
    j = lax.broadcasted_iota(jnp.int32, (2 * n, n), 0) & (n - 1)
    s = lax.broadcasted_iota(jnp.int32, (2 * n, n), 1)
    return jnp.where(j >= s, -1.0, 0.0).astype(BF16)


def _lanes_to(x, n):
    return x[:, :n] if n <= LANES else jnp.concatenate([x] * (n // LANES), axis=1)


def _suffix_sums(fail, neg_ones):
    tk = fail.shape[1]
    hi, lo = _split_bf16(fail)
    if tk % LANES == 0:
        return jnp.dot(jnp.concatenate([hi, lo], axis=1), neg_ones, preferred_element_type=F32)
    return (jnp.dot(hi, neg_ones[:tk], preferred_element_type=F32)
            + jnp.dot(lo, neg_ones[:tk], preferred_element_type=F32))


def _sb_tiles(qs, kbs, vbs, neg_ones, carry, acc, masked):
    heads = range(len(qs))
    tk = kbs[0].shape[0]
    zs = [lax.dot_general(qs[h], kbs[h], _NT, preferred_element_type=F32) for h in heads]
    fails = [jnp.maximum(z, 0.0) + jnp.log2(1.0 + jnp.exp2(-jnp.abs(z))) for z in zs]
    if masked:
        t = lax.broadcasted_iota(jnp.int32, zs[0].shape, 0)
        s = lax.broadcasted_iota(jnp.int32, zs[0].shape, 1)
        valid = s < t
        fails = [jnp.where(valid, f, 0.0) for f in fails]
    suffixes = [_suffix_sums(f, neg_ones) for f in fails]
    probs = [jnp.exp2(zs[h] + suffixes[h] + _lanes_to(carry[h], tk)) for h in heads]
    if masked:
        probs = [jnp.where(valid, p, 0.0) for p in probs]
    for h in heads:
        acc[h] += jnp.dot(probs[h].astype(BF16), vbs[h], preferred_element_type=F32)
        carry[h] += jnp.broadcast_to(suffixes[h][:, :1], carry.shape[1:])


def _sb_body(q_ref, kd_ref, vd_ref, kp_ref, vp_ref, o_ref, carry, acc, *, tk, n_heads, past_per_step,
             past_static):
    tq = q_ref.shape[0]
    n_past = past_static if past_static is not None else pl.program_id(1) * past_per_step
    cols = lambda h: slice(h * SB_HEAD_DIM, (h + 1) * SB_HEAD_DIM)
    heads = range(n_heads)
    carry[...] = jnp.zeros(carry.shape, F32)
    acc[...] = jnp.zeros(acc.shape, F32)
    queries = lambda: [q_ref[:, cols(h)] for h in heads]
    _sb_tiles(queries(), [kd_ref[:, cols(h)].astype(BF16) for h in heads],
              [vd_ref[:, cols(h)].astype(BF16) for h in heads], _neg_suffix(tq), carry, acc, True)
    neg_ones = _neg_suffix(tk)

    alive = lambda: jnp.max(carry[...]) > SB_DEAD_LOG2

    def step(state):
        it = state[0]
        rows = pl.ds(pl.multiple_of((n_past - 1 - it) * tk, tk), tk)
        _sb_tiles(queries(), [kp_ref[rows, cols(h)].astype(BF16) for h in heads],
                  [vp_ref[rows, cols(h)].astype(BF16) for h in heads], neg_ones, carry, acc, False)
        return it + 1, alive()

    lax.while_loop(lambda state: (state[0] < n_past) & state[1], step, (jnp.int32(0), alive()))
    for h in range(n_heads):
        o_ref[:, cols(h)] = acc[h].astype(o_ref.dtype)


def _sb_scratch(n_heads, tq):
    return [pltpu.VMEM((n_heads, tq, LANES), F32), pltpu.VMEM((n_heads, tq, SB_HEAD_DIM), F32)]


def _sb_prompt(q, k, v, n_tok):
    heads = q.shape[1] // SB_HEAD_DIM
    tq = _tile(n_tok, SB_BLOCK)
    nh = SB_HEADS_PER_STEP
    width = nh * SB_HEAD_DIM
    blk = pl.BlockSpec((tq, width), lambda h, i: (i, h))
    full = pl.BlockSpec((n_tok, width), lambda h, i: (0, h))
    return pl.pallas_call(
        functools.partial(_sb_body, tk=tq, n_heads=nh, past_per_step=1, past_static=None),
        out_shape=jax.ShapeDtypeStruct((n_tok, heads * SB_HEAD_DIM), BF16),
        grid=(heads // nh, n_tok // tq),
        in_specs=[blk, blk, blk, full, full],
        out_specs=blk,
        scratch_shapes=_sb_scratch(nh, tq),
        compiler_params=_params("parallel", "arbitrary"),
        name="stick_breaking_prompt",
    )(q, k, v, k, v)


def _sb_sample(q, k, v, cache_k, cache_v, row_off, n_batch, n_new):
    heads = q.shape[1] // SB_HEAD_DIM
    past = cache_k.shape[1]
    tk = _tile(past, SB_BLOCK)
    assert row_off % n_new == 0
    ro = row_off // n_new
    nh = SB_HEADS_PER_STEP
    width = nh * SB_HEAD_DIM
    blk = pl.BlockSpec((n_new, width), lambda b, h: (ro + b, h))
    full = pl.BlockSpec((None, past, width), lambda b, h: (b, 0, h))
    return pl.pallas_call(
        functools.partial(_sb_body, tk=tk, n_heads=nh, past_per_step=0, past_static=past // tk),
        out_shape=jax.ShapeDtypeStruct((n_batch * n_new, heads * SB_HEAD_DIM), BF16),
        grid=(n_batch, heads // nh),
        in_specs=[blk, blk, blk, full, full],
        out_specs=pl.BlockSpec((n_new, width), lambda b, h: (b, h)),
        scratch_shapes=_sb_scratch(nh, n_new),
        compiler_params=_params("parallel", "arbitrary"),
        name="stick_breaking_sample",
    )(q, k, v, cache_k, cache_v)


def _cross_body(q_ref, k_ref, v_ref, o_ref, *, scale):
    s = lax.dot_general(q_ref[...], k_ref[...].astype(BF16), _NT, preferred_element_type=F32) * scale
    e = jnp.exp(s - jnp.max(s, axis=-1, keepdims=True))
    p = e / jnp.sum(e, axis=-1, keepdims=True)
    o_ref[...] = jnp.dot(p.astype(BF16), v_ref[...].astype(BF16),
                         preferred_element_type=F32).astype(o_ref.dtype)


def _cross_attend(q, mem_k, mem_v, row_off, rows_per_seq):
    n_seq, n_mem, d = mem_k.shape
    dh = d // MEM_HEADS
    tm = _tile(rows_per_seq, ROW_TILE)
    steps = rows_per_seq // tm
    assert row_off % tm == 0
    ro = row_off // tm
    mem = pl.BlockSpec((None, n_mem, dh), lambda i, h: (i // steps, 0, h))
    return pl.pallas_call(
        functools.partial(_cross_body, scale=dh ** -0.5),
        out_shape=jax.ShapeDtypeStruct((n_seq * rows_per_seq, d), BF16),
        grid=(n_seq * steps, MEM_HEADS),
        in_specs=[pl.BlockSpec((tm, dh), lambda i, h: (ro + i, h)), mem, mem],
        out_specs=pl.BlockSpec((tm, dh), lambda i, h: (i, h)),
        compiler_params=_params("parallel", "arbitrary"),
        name="cross_attention",
    )(q, mem_k, mem_v)


def _split_bf16(x):
    hi = x.astype(BF16)
    return hi, (x - hi.astype(F32)).astype(BF16)


def _router_body(x_ref, g_ref, wr_ref, br_ref, h_ref, ids_ref, wts_ref):
    h = _rmsnorm_rows(x_ref[...], g_ref[...])
    h_ref[...] = h
    h_hi, h_lo = _split_bf16(h)
    w_hi, w_lo = _split_bf16(wr_ref[...])
    logits = (jnp.dot(h_hi, w_hi, preferred_element_type=F32)
              + jnp.dot(h_hi, w_lo, preferred_element_type=F32)
              + jnp.dot(h_lo, w_hi, preferred_element_type=F32)) + br_ref[...]
    lane = lax.broadcasted_iota(jnp.int32, logits.shape, 1).astype(F32)
    big = float(LANES)
    neg = -jnp.inf

    def first_max(vals):
        top = jnp.max(vals, axis=-1, keepdims=True)
        return top, jnp.min(jnp.where(vals == top, lane, big), axis=-1, keepdims=True)

    is_group = lane < N_GROUPS
    g_max, g_top = first_max(jnp.where(is_group, logits, neg))
    g_w = 1.0 / jnp.sum(jnp.where(is_group, jnp.exp(logits - g_max), 0.0), axis=-1, keepdims=True)
    first = N_GROUPS + EXPERTS_PER_GROUP * g_top
    cand = jnp.where((lane >= first) & (lane < first + EXPERTS_PER_GROUP), logits, neg)
    v1, i1 = first_max(cand)
    v2, i2 = first_max(jnp.where(lane == i1, neg, cand))
    t = jnp.exp(v2 - v1)
    w1 = g_w / (1.0 + t)
    w2 = g_w * t / (1.0 + t)
    ids = jnp.where(lane == 0.0, i1 - N_GROUPS, jnp.where(lane == 1.0, i2 - N_GROUPS, 0.0))
    ids_ref[...] = ids.astype(jnp.int32)
    wts_ref[...] = jnp.where(lane == 0.0, w1, jnp.where(lane == 1.0, w2, 0.0))


def _norm_and_route(x, g, wg, bg, we, be):
    m, d = x.shape
    tm = _tile(m, 256)
    pad = LANES - N_GROUPS - N_EXPERTS
    wr = jnp.concatenate([wg, we, jnp.zeros((d, pad), F32)], axis=1)
    br = jnp.concatenate([bg, be, jnp.zeros((pad,), F32)]).reshape(1, LANES)
    row = lambda c: pl.BlockSpec((tm, c), lambda i: (i, 0))
    return pl.pallas_call(
        _router_body,
        out_shape=[jax.ShapeDtypeStruct((m, d), F32),
                   jax.ShapeDtypeStruct((m, LANES), jnp.int32),
                   jax.ShapeDtypeStruct((m, LANES), F32)],
        grid=(m // tm,),
        in_specs=[row(d), pl.BlockSpec((1, d), lambda i: (0, 0)),
                  pl.BlockSpec((d, LANES), lambda i: (0, 0)), pl.BlockSpec((1, LANES), lambda i: (0, 0))],
        out_specs=[row(d), row(LANES), row(LANES)],
        compiler_params=_params("parallel"),
        name="norm_route",
    )(x, g.reshape(1, d), wr, br)


def _start_row_gather(idx_ref, first, stride, src_ref, buf, sem):
    last = src_ref.shape[0] - 1

    def one(r, _):
        t = jnp.minimum(idx_ref[first + r * stride], last)
        pltpu.make_async_copy(src_ref.at[pl.ds(t, 1)], buf.at[pl.ds(r, 1)], sem).start()
        return 0

    lax.fori_loop(0, buf.shape[0], one, 0, unroll=8)


def _wait_row_gather(src_ref, buf, sem):
    pltpu.make_async_copy(src_ref.at[pl.ds(0, buf.shape[0])], buf, sem).wait()


def _dispatch_body(idx_ref, used_ref, valid_ref, src_ref, o_ref, buf, sem):
    b = pl.program_id(0)
    rows = buf.shape[1]

    def start(blk):
        _start_row_gather(idx_ref, blk * rows, 1, src_ref, buf.at[blk % 2], sem.at[blk % 2])

    @pl.when(b == 0)
    def _():
        start(0)

    @pl.when(b + 1 < used_ref[0])
    def _():
        start(b + 1)

    @pl.when(b < used_ref[0])
    def _():
        slot = b % 2
        _wait_row_gather(src_ref, buf.at[slot], sem.at[slot])
        o_ref[...] = jnp.where(valid_ref[...] > 0.0, buf[slot], 0.0).astype(o_ref.dtype)


def _dispatch_rows(src, row_tok, row_valid, n_used):
    n_rows = row_tok.shape[0]
    d = src.shape[1]
    grid_spec = pltpu.PrefetchScalarGridSpec(
        num_scalar_prefetch=2,
        grid=(n_rows // MOE_ROWS,),
        in_specs=[pl.BlockSpec((MOE_ROWS, 1), lambda b, idx, used: (b, 0)),
                  pl.BlockSpec(memory_space=pl.ANY)],
        out_specs=pl.BlockSpec((MOE_ROWS, d), lambda b, idx, used: (b, 0)),
        scratch_shapes=[pltpu.VMEM((2, MOE_ROWS, d), src.dtype), pltpu.SemaphoreType.DMA((2,))],
    )
    return pl.pallas_call(
        _dispatch_body,
        out_shape=jax.ShapeDtypeStruct((n_rows, d), BF16),
        grid_spec=grid_spec,
        compiler_params=_params("arbitrary"),
        name="dispatch_rows",
    )(row_tok, n_used, row_valid.reshape(n_rows, 1), src)


def _expert_up_body(be_ref, first_ref, used_ref, x_ref, w1_ref, w3_ref, h_ref, w1b, w3b):
    b = pl.program_id(1)

    @pl.when(first_ref[b] == 1)
    def _():
        w1b[...] = w1_ref[...].astype(BF16)
        w3b[...] = w3_ref[...].astype(BF16)

    @pl.when(b < used_ref[0])
    def _():
        x = x_ref[...]
        a = jnp.dot(x, w1b[...], preferred_element_type=F32)
        g = jnp.dot(x, w3b[...], preferred_element_type=F32)
        h_ref[...] = (a * _sigmoid(a) * g).astype(h_ref.dtype)


def _expert_down_body(be_ref, first_ref, used_ref, h_ref, w2_ref, rw_ref, y_ref, w2b):
    b = pl.program_id(1)

    @pl.when(first_ref[b] == 1)
    def _():
        w2b[...] = w2_ref[...].astype(BF16)

    @pl.when(b < used_ref[0])
    def _():
        y_ref[...] = jnp.dot(h_ref[...], w2b[...], preferred_element_type=F32) * rw_ref[...]


def _expert_ffn(xs, row_w, blk_exp, blk_first, n_used, w1, w3, w2):
    n_rows, d = xs.shape
    f = w1.shape[2]
    n_blocks = n_rows // MOE_ROWS
    tf = _tile(f, COL_TILE)
    tn = _tile(d, 1024)
    last = lambda b, used: jnp.minimum(b, used[0] - 1)
    up_spec = pltpu.PrefetchScalarGridSpec(
        num_scalar_prefetch=3,
        grid=(f // tf, n_blocks),
        in_specs=[pl.BlockSpec((MOE_ROWS, d), lambda j, b, be, fi, us: (last(b, us), 0)),
                  pl.BlockSpec((None, d, tf), lambda j, b, be, fi, us: (be[b], 0, j)),
                  pl.BlockSpec((None, d, tf), lambda j, b, be, fi, us: (be[b], 0, j))],
        out_specs=pl.BlockSpec((MOE_ROWS, tf), lambda j, b, be, fi, us: (b, j)),
        scratch_shapes=[pltpu.VMEM((d, tf), BF16), pltpu.VMEM((d, tf), BF16)],
    )
    hidden = pl.pallas_call(
        _expert_up_body,
        out_shape=jax.ShapeDtypeStruct((n_rows, f), BF16),
        grid_spec=up_spec,
        compiler_params=_params("arbitrary", "arbitrary"),
        name="expert_up",
    )(blk_exp, blk_first, n_used, xs, w1, w3)
    down_spec = pltpu.PrefetchScalarGridSpec(
        num_scalar_prefetch=3,
        grid=(d // tn, n_blocks),
        in_specs=[pl.BlockSpec((MOE_ROWS, f), lambda j, b, be, fi, us: (last(b, us), 0)),
                  pl.BlockSpec((None, f, tn), lambda j, b, be, fi, us: (be[b], 0, j)),
                  pl.BlockSpec((MOE_ROWS, 1), lambda j, b, be, fi, us: (b, 0))],
        out_specs=pl.BlockSpec((MOE_ROWS, tn), lambda j, b, be, fi, us: (b, j)),
        scratch_shapes=[pltpu.VMEM((f, tn), BF16)],
    )
    return pl.pallas_call(
        _expert_down_body,
        out_shape=jax.ShapeDtypeStruct((n_rows, d), F32),
        grid_spec=down_spec,
        compiler_params=_params("arbitrary", "arbitrary"),
        name="expert_down",
    )(blk_exp, blk_first, n_used, hidden, w2, row_w.reshape(n_rows, 1))


def _route_layout(ids, wts, n_tok):
    n_assign = n_tok * TOP_K
    flat_e = ids.reshape(n_assign)
    flat_w = wts.reshape(n_assign)
    flat_t = jnp.repeat(jnp.arange(n_tok, dtype=jnp.int32), TOP_K)
    onehot = (flat_e[:, None] == jnp.arange(N_EXPERTS, dtype=jnp.int32)[None, :]).astype(jnp.int32)
    running = jnp.cumsum(onehot, axis=0)
    rank = jnp.take_along_axis(running, flat_e[:, None], axis=1)[:, 0] - 1
    counts = running[-1]
    padded = (counts + MOE_ROWS - 1) // MOE_ROWS * MOE_ROWS
    pad_end = jnp.cumsum(padded)
    dest = (pad_end - padded)[flat_e] + rank
    n_blocks = -(-n_assign // MOE_ROWS) + N_EXPERTS
    n_rows = n_blocks * MOE_ROWS
    row_tok = jnp.full((n_rows,), n_tok, jnp.int32).at[dest].set(flat_t)
    row_w = jnp.zeros((n_rows,), F32).at[dest].set(flat_w)
    blk = jnp.arange(n_blocks, dtype=jnp.int32)
    n_used = (pad_end[-1] // MOE_ROWS).astype(jnp.int32)
    blk_exp = jnp.searchsorted(pad_end, jnp.minimum(blk, n_used - 1) * MOE_ROWS, side='right').astype(jnp.int32)
    blk_exp = jnp.minimum(blk_exp, N_EXPERTS - 1)
    prev = jnp.concatenate([jnp.full((1,), -1, jnp.int32), blk_exp[:-1]])
    blk_first = ((blk_exp != prev) & (blk < n_used)).astype(jnp.int32)
    row_valid = (row_tok < n_tok).astype(F32)
    return row_tok, row_valid, row_w, dest.astype(jnp.int32), blk_exp, blk_first, n_used.reshape(1)


def _final_body(dest_ref, x_ref, y_ref, g_ref, op_ref, os_ref, buf, sem, *, prompt_tiles):
    i = pl.program_id(0)
    tm = buf.shape[2]

    def start(blk):
        for k in range(TOP_K):
            _start_row_gather(dest_ref, blk * tm * TOP_K + k, TOP_K, y_ref, buf.at[blk % 2, k], sem.at[blk % 2, k])

    @pl.when(i == 0)
    def _():
        start(0)

    @pl.when(i + 1 < pl.num_programs(0))
    def _():
        start(i + 1)

    slot = i % 2
    for k in range(TOP_K):
        _wait_row_gather(y_ref, buf.at[slot, k], sem.at[slot, k])
    y = _rmsnorm_rows(x_ref[...] + (buf[slot, 0] + buf[slot, 1]), g_ref[...])

    @pl.when(i < prompt_tiles)
    def _():
        op_ref[...] = y

    @pl.when(i >= prompt_tiles)
    def _():
        os_ref[...] = y


def _combine_and_norm(x, y_rows, dest, g, n_prompt):
    m, d = x.shape
    tm = _tile(m - n_prompt, _tile(n_prompt, 256))
    assert n_prompt % tm == 0
    pt = n_prompt // tm
    grid_spec = pltpu.PrefetchScalarGridSpec(
        num_scalar_prefetch=1,
        grid=(m // tm,),
        in_specs=[pl.BlockSpec((tm, d), lambda i, dest: (i, 0)),
                  pl.BlockSpec(memory_space=pl.ANY),
                  pl.BlockSpec((1, d), lambda i, dest: (0, 0))],
        out_specs=[pl.BlockSpec((tm, d), lambda i, dest: (jnp.minimum(i, pt - 1), 0)),
                   pl.BlockSpec((tm, d), lambda i, dest: (jnp.maximum(i - pt, 0), 0))],
        scratch_shapes=[pltpu.VMEM((2, TOP_K, tm, d), F32), pltpu.SemaphoreType.DMA((2, TOP_K))],
    )
    return pl.pallas_call(
        functools.partial(_final_body, prompt_tiles=pt),
        out_shape=[jax.ShapeDtypeStruct((n_prompt, d), F32), jax.ShapeDtypeStruct((m - n_prompt, d), F32)],
        grid_spec=grid_spec,
        compiler_params=_params("arbitrary"),
        name="combine_norm",
    )(dest, x, y_rows, g.reshape(1, d))


def _layer(x, n_prompt, n_seq, n_new, past_len, cache_sb_k, cache_sb_v, state_ret, cache_mem_k, cache_mem_v,
           mem_prompt, norm_mix, w_in, ret_gn, w_out, norm_cross, norm_mem, w_cq, w_ck, w_cv, w_co, norm_ffn,
           wg, bg, we, be, w1, w3, w2):
    n_tok, d = x.shape
    d_ret = d // 2
    d_sb = d - d_ret
    ret_heads = d_ret // RET_HEAD_DIM
    sds = jax.ShapeDtypeStruct

    pos = jnp.concatenate([jnp.arange(n_prompt, dtype=jnp.int32),
                           jnp.tile(past_len + jnp.arange(n_new, dtype=jnp.int32), n_seq)])
    half = RET_HEAD_DIM // 2
    inv = ROPE_BASE ** (-jnp.arange(half, dtype=F32) / half)
    ang = pos.astype(F32)[:, None] * inv[None, :]
    cos, sin = jnp.cos(ang), jnp.sin(ang)
    log_gamma = jnp.log1p(-jnp.exp2(-5.0 - jnp.arange(ret_heads, dtype=F32)))

    h = _rmsnorm(x, norm_mix, BF16)
    rot = [(cos, half), (sin, half)]
    bf = lambda n: [(sds((n_tok, n), BF16), None)]
    (rq,) = _matmul([h], w_in, 0, d_ret, _ep_rotary(1.0), rot, bf(d_ret), "in_proj_ret_q")
    (rk,) = _matmul([h], w_in, d_ret, d_ret, _ep_rotary(RET_HEAD_DIM ** -0.5), rot, bf(d_ret), "in_proj_ret_k")
    (rv,) = _matmul([h], w_in, 2 * d_ret, d_ret, _ep_store(1.0), [], bf(d_ret), "in_proj_ret_v")
    (rg,) = _matmul([h], w_in, 3 * d_ret, d_ret, _ep_store(1.0), [], [(sds((n_tok, d_ret), F32), None)],
                    "in_proj_ret_gate")
    (sq,) = _matmul([h], w_in, 4 * d_ret, d_sb, _ep_store(SB_HEAD_DIM ** -0.5 * LOG2_E), [], bf(d_sb),
                    "in_proj_sb_q")
    both = [(sds((n_tok, d_sb), F32), None), (sds((n_tok, d_sb), BF16), None)]
    sk, sk_b = _matmul([h], w_in, 4 * d_ret + d_sb, d_sb, _ep_store(1.0), [], both, "in_proj_sb_k")
    sv, sv_b = _matmul([h], w_in, 4 * d_ret + 2 * d_sb, d_sb, _ep_store(1.0), [], both, "in_proj_sb_v")

    rows_p = _tile(n_prompt, RET_BLOCK)
    zeros_state = jnp.zeros((1, ret_heads, RET_HEAD_DIM, RET_HEAD_DIM), F32)
    yr_p, state_p = _retention(log_gamma, rq, rk, rv, rg, ret_gn, zeros_state, 0, 1, n_prompt // rows_p, rows_p)
    yr_s, state_s = _retention(log_gamma, rq, rk, rv, rg, ret_gn, state_ret, n_prompt, n_seq, 1, n_new)
    sb_p = _sb_prompt(sq, sk_b, sv_b, n_prompt)
    sb_s = _sb_sample(sq, sk_b, sv_b, cache_sb_k.reshape(n_seq, past_len, d_sb),
                      cache_sb_v.reshape(n_seq, past_len, d_sb), n_prompt, n_seq, n_new)
    yr = jnp.concatenate([yr_p, yr_s], axis=0)
    sb = jnp.concatenate([sb_p, sb_s], axis=0)
    res = [(sds((n_tok, d), F32), None)]
    (x,) = _matmul([yr, sb], w_out, 0, d, _ep_residual, [(x, None)], res, "out_proj")

    n_mem = mem_prompt.shape[0]
    m = _rmsnorm(mem_prompt, norm_mem, BF16)
    mem_out = [(sds((n_mem, d), F32), None)]
    (mk,) = _matmul([m], w_ck, 0, d, _ep_store(1.0), [], mem_out, "mem_k")
    (mv,) = _matmul([m], w_cv, 0, d, _ep_store(1.0), [], mem_out, "mem_v")
    h = _rmsnorm(x, norm_cross, BF16)
    (cq,) = _matmul([h], w_cq, 0, d, _ep_store(1.0), [], bf(d), "cross_q")
    co_p = _cross_attend(cq, mk.reshape(1, n_mem, d), mv.reshape(1, n_mem, d), 0, n_prompt)
    co_s = _cross_attend(cq, cache_mem_k.reshape(n_seq, n_mem, d), cache_mem_v.reshape(n_seq, n_mem, d),
                         n_prompt, n_new)
    co = jnp.concatenate([co_p, co_s], axis=0)
    (x,) = _matmul([co], w_co, 0, d, _ep_residual, [(x, None)], res, "cross_out")

    hn, ids, wts = _norm_and_route(x, norm_ffn, wg, bg, we, be)
    row_tok, row_valid, row_w, dest, blk_exp, blk_first, n_used = _route_layout(
        ids[:, :TOP_K], wts[:, :TOP_K], n_tok)
    xs = _dispatch_rows(hn, row_tok, row_valid, n_used)
    y_rows = _expert_ffn(xs, row_w, blk_exp, blk_first, n_used, w1, w3, w2)
    return x, y_rows, dest, (sk, sv, state_p, state_s, mk, mv)


def kernel(x_prompt, x_sample, cache_sb_k, cache_sb_v, state_ret, cache_mem_k, cache_mem_v, mem_prompt, norm_mix, w_in, ret_gn, w_out, norm_cross, norm_mem, w_cq, w_ck, w_cv, w_co, norm_ffn, router_group_w, router_group_b, router_expert_w, router_expert_b, expert_w1, expert_w3, expert_w2, norm_final):
    batch, n_prompt, d = x_prompt.shape
    n_seq, n_new, _ = x_sample.shape
    depth, _, past_len, sb_heads, _ = cache_sb_k.shape
    n_mem = mem_prompt.shape[1]
    assert batch == 1 and depth == 1
    ret_heads = (d // 2) // RET_HEAD_DIM
    x = jnp.concatenate([x_prompt.reshape(n_prompt, d), x_sample.reshape(n_seq * n_new, d)], axis=0)
    x, y_rows, dest, (sk, sv, state_p, state_s, mk, mv) = _layer(
        x, n_prompt, n_seq, n_new, past_len, cache_sb_k[0], cache_sb_v[0], state_ret[0], cache_mem_k[0],
        cache_mem_v[0], mem_prompt[0], norm_mix[0], w_in[0], ret_gn[0], w_out[0], norm_cross[0], norm_mem[0],
        w_cq[0], w_ck[0], w_cv[0], w_co[0], norm_ffn[0], router_group_w[0], router_group_b[0],
        router_expert_w[0], router_expert_b[0], expert_w1[0], expert_w3[0], expert_w2[0])
    y_prompt, y_sample = _combine_and_norm(x, y_rows, dest, norm_final, n_prompt)
    y_prompt = y_prompt.reshape(1, n_prompt, d)
    y_sample = y_sample.reshape(n_seq, n_new, d)
    sb_shape_p = (1, 1, n_prompt, sb_heads, SB_HEAD_DIM)
    sb_shape_s = (1, n_seq, n_new, sb_heads, SB_HEAD_DIM)
    mem_shape = (1, 1, n_mem, MEM_HEADS, d // MEM_HEADS)
    return (y_prompt, y_sample,
            sk[:n_prompt].reshape(sb_shape_p), sv[:n_prompt].reshape(sb_shape_p),
            state_p.reshape(1, 1, ret_heads, RET_HEAD_DIM, RET_HEAD_DIM),
            mk.reshape(mem_shape), mv.reshape(mem_shape),
            sk[n_prompt:].reshape(sb_shape_s), sv[n_prompt:].reshape(sb_shape_s),
            state_s.reshape(1, n_seq, ret_heads, RET_HEAD_DIM, RET_HEAD_DIM))
```

```python
import functools

import jax
import jax.numpy as jnp
from jax import lax
from jax.experimental import pallas as pl
from jax.experimental.pallas import tpu as pltpu

BF16 = jnp.bfloat16
F32 = jnp.float32

EPS = 1e-6
CHUNK = 64
RET_HEAD_DIM = 256
SB_HEAD_DIM = 128
MEM_HEADS = 4
N_GROUPS = 4
EXPERTS_PER_GROUP = 8
N_EXPERTS = N_GROUPS * EXPERTS_PER_GROUP
TOP_K = 2
ROPE_BASE = 10000.0

LANES = 128
VMEM_LIMIT_BYTES = 56 * 1024 * 1024
ROW_TILE = 512
COL_TILE = 512
RET_BLOCK = 256
SB_BLOCK = 256
SB_HEADS_PER_STEP = 2
LOG2_E = 1.4426950408889634
SB_DEAD_LOG2 = -64.0
MOE_ROWS = 256

PROMPT = "prompt"
SAMPLE = "sample"

_NT = (((1,), (1,)), ((), ()))
_TN = (((0,), (0,)), ((), ()))


def _tile(n, pref):
    t = min(pref, n)
    while n % t:
        t //= 2
    return t


def _params(*sem):
    return pltpu.CompilerParams(dimension_semantics=sem, vmem_limit_bytes=VMEM_LIMIT_BYTES)


def _sigmoid(x):
    return 1.0 / (1.0 + jnp.exp(-x))


def _rmsnorm_rows(x, g):
    ms = jnp.mean(x * x, axis=-1, keepdims=True)
    return x * lax.rsqrt(ms + EPS) * g


def _rmsnorm_body(x_ref, g_ref, o_ref):
    o_ref[...] = _rmsnorm_rows(x_ref[...], g_ref[...]).astype(o_ref.dtype)


def _rmsnorm(x, g, out_dtype):
    m, d = x.shape
    tm = _tile(m, 256)
    return pl.pallas_call(
        _rmsnorm_body,
        out_shape=jax.ShapeDtypeStruct((m, d), out_dtype),
        grid=(m // tm,),
        in_specs=[pl.BlockSpec((tm, d), lambda i: (i, 0)), pl.BlockSpec((1, d), lambda i: (0, 0))],
        out_specs=pl.BlockSpec((tm, d), lambda i: (i, 0)),
        compiler_params=_params("parallel"),
        name="rmsnorm",
    )(x, g.reshape(1, d))


def _rmsnorm_stacked_body(xp_ref, xs_ref, g_ref, o_ref, *, prompt_tiles):
    is_prompt = pl.program_id(0) < prompt_tiles

    @pl.when(is_prompt)
    def _():
        o_ref[...] = _rmsnorm_rows(xp_ref[...], g_ref[...]).astype(o_ref.dtype)

    @pl.when(jnp.logical_not(is_prompt))
    def _():
        o_ref[...] = _rmsnorm_rows(xs_ref[...], g_ref[...]).astype(o_ref.dtype)


def _rmsnorm_stacked(xp, xs, g, out_dtype):
    (n_p, d), n_s = xp.shape, xs.shape[0]
    tm = _tile(n_s, _tile(n_p, 256))
    pt = n_p // tm
    return pl.pallas_call(
        functools.partial(_rmsnorm_stacked_body, prompt_tiles=pt),
        out_shape=jax.ShapeDtypeStruct((n_p + n_s, d), out_dtype),
        grid=((n_p + n_s) // tm,),
        in_specs=[pl.BlockSpec((tm, d), lambda i: (jnp.minimum(i, pt - 1), 0)),
                  pl.BlockSpec((tm, d), lambda i: (jnp.maximum(i - pt, 0), 0)),
                  pl.BlockSpec((1, d), lambda i: (0, 0))],
        out_specs=pl.BlockSpec((tm, d), lambda i: (i, 0)),
        compiler_params=_params("arbitrary"),
        name="rmsnorm_stacked",
    )(xp, xs, g.reshape(1, d))


def _matmul_body(*refs, n_lhs, n_extra, n_out, epilogue):
    x_refs = refs[:n_lhs]
    w_refs = refs[n_lhs:2 * n_lhs]
    extra = refs[2 * n_lhs:2 * n_lhs + n_extra]
    outs = refs[2 * n_lhs + n_extra:2 * n_lhs + n_extra + n_out]
    wb = refs[2 * n_lhs + n_extra + n_out:]

    @pl.when(pl.program_id(1) == 0)
    def _():
        for p in range(n_lhs):
            wb[p][...] = w_refs[p][...].astype(BF16)

    acc = jnp.dot(x_refs[0][...], wb[0][...], preferred_element_type=F32)
    for p in range(1, n_lhs):
        acc += jnp.dot(x_refs[p][...], wb[p][...], preferred_element_type=F32)
    epilogue(acc, extra, outs)


def _matmul(xs, w, col_off, n_cols, epilogue, extras, outs, name, n_prompt=None):
    m = xs[0].shape[0]
    tm = _tile(m, ROW_TILE) if n_prompt is None else _tile(m - n_prompt, _tile(n_prompt, ROW_TILE))
    tn = _tile(n_cols, COL_TILE)
    assert col_off % tn == 0
    jo = col_off // tn
    pt = None if n_prompt is None else n_prompt // tm
    in_specs, args = [], []
    for x in xs:
        in_specs.append(pl.BlockSpec((tm, x.shape[1]), lambda j, i: (i, 0)))
        args.append(x)
    row = 0
    for x in xs:
        kp = x.shape[1]
        assert row % kp == 0
        in_specs.append(pl.BlockSpec((None, kp, tn), lambda j, i, r=row // kp: (0, r, j + jo)))
        args.append(w)
        row += kp
    assert row == w.shape[1]

    def spec(kind):
        if kind is None:
            return pl.BlockSpec((tm, tn), lambda j, i: (i, j))
        if kind == PROMPT:
            return pl.BlockSpec((tm, tn), lambda j, i: (jnp.minimum(i, pt - 1), j))
        if kind == SAMPLE:
            return pl.BlockSpec((tm, tn), lambda j, i: (jnp.maximum(i - pt, 0), j))
        return pl.BlockSpec((tm, kind), lambda j, i: (i, 0))

    for a, kind in extras:
        in_specs.append(spec(kind))
        args.append(a)
    body = functools.partial(_matmul_body, n_lhs=len(xs), n_extra=len(extras), n_out=len(outs),
                             epilogue=epilogue if n_prompt is None else functools.partial(epilogue, pt))
    res = pl.pallas_call(
        body,
        out_shape=[o for o, _ in outs],
        grid=(n_cols // tn, m // tm),
        in_specs=in_specs,
        out_specs=[spec(kind) for _, kind in outs],
        scratch_shapes=[pltpu.VMEM((x.shape[1], tn), BF16) for x in xs],
        compiler_params=_params("arbitrary", "arbitrary"),
        name=name,
    )(*args)
    return res


def _ep_store(scale):
    def ep(acc, extra, outs):
        for o in outs:
            o[...] = (acc * scale).astype(o.dtype) if scale != 1.0 else acc.astype(o.dtype)
    return ep


def _ep_store_split(prompt_tiles, acc, extra, outs):
    is_prompt = pl.program_id(1) < prompt_tiles

    @pl.when(is_prompt)
    def _():
        outs[0][...] = acc

    @pl.when(jnp.logical_not(is_prompt))
    def _():
        outs[1][...] = acc

    outs[2][...] = acc.astype(outs[2].dtype)


def _ep_residual(acc, extra, outs):
    outs[0][...] = extra[0][...] + acc


def _ep_residual_split(prompt_tiles, acc, extra, outs):
    res = jnp.where(pl.program_id(1) < prompt_tiles, extra[0][...], extra[1][...])
    outs[0][...] = res + acc


def _ep_rotary(scale):
    half = RET_HEAD_DIM // 2

    def ep(acc, extra, outs):
        cos = extra[0][...]
        sin = extra[1][...]
        for h in range(acc.shape[1] // RET_HEAD_DIM):
            lo = h * RET_HEAD_DIM
            x1 = acc[:, lo:lo + half]
            x2 = acc[:, lo + half:lo + RET_HEAD_DIM]
            outs[0][:, lo:lo + half] = ((x1 * cos - x2 * sin) * scale).astype(outs[0].dtype)
            outs[0][:, lo + half:lo + RET_HEAD_DIM] = ((x1 * sin + x2 * cos) * scale).astype(outs[0].dtype)
    return ep


def _retention_body(lg_ref, q_ref, k_ref, v_ref, g_ref, gn_ref, s0_ref, *rest, rows):
    o_ref, sout_ref, s_scr = rest[-3:]
    h = pl.program_id(1)

    @pl.when(pl.program_id(2) == 0)
    def _():
        s_scr[...] = s0_ref[...]

    lg = lg_ref[h]
    q = q_ref[...]
    k = k_ref[...]
    v = v_ref[...]
    ii = lax.broadcasted_iota(jnp.int32, (rows, rows), 0)
    jj = lax.broadcasted_iota(jnp.int32, (rows, rows), 1)
    shift = CHUNK.bit_length() - 1
    dist = jnp.abs(ii - jj).astype(F32)
    decay = jnp.where((jj >> shift) <= (ii >> shift), jnp.exp(lg * dist), 0.0)
    scores = lax.dot_general(q, k, _NT, preferred_element_type=F32) * decay
    out = jnp.dot(scores.astype(BF16), v, preferred_element_type=F32)
    s = s_scr[...]
    idx = lax.broadcasted_iota(jnp.int32, (rows, 1), 0).astype(F32)
    out += jnp.exp(lg * (idx + 1.0)) * jnp.dot(q, s.astype(BF16), preferred_element_type=F32)
    kd = (k.astype(F32) * jnp.exp(lg * (rows - 1.0 - idx))).astype(BF16)
    s_new = jnp.exp(lg * rows) * s + lax.dot_general(kd, v, _TN, preferred_element_type=F32)
    s_scr[...] = s_new
    sout_ref[...] = s_new
    mu = jnp.mean(out, axis=-1, keepdims=True)
    cen = out - mu
    var = jnp.mean(cen * cen, axis=-1, keepdims=True)
    yr = cen * lax.rsqrt(var + EPS) * gn_ref[...]
    g = g_ref[...]
    o_ref[...] = (g * _sigmoid(g) * yr).astype(o_ref.dtype)


def _into(out_buf):
    if out_buf is None:
        return [], []
    return [out_buf], [pl.BlockSpec(memory_space=pl.ANY)]


def _retention(log_gamma, q, k, v, gate, gn, s0, row_off, n_batch, n_steps, rows, out_buf=None):
    heads = q.shape[1] // RET_HEAD_DIM
    assert row_off % rows == 0 and rows % CHUNK == 0
    ro = row_off // rows
    tok = lambda b, h, c, lg: (ro + b * n_steps + c, h)
    blk = pl.BlockSpec((rows, RET_HEAD_DIM), tok)
    st = pl.BlockSpec((None, None, RET_HEAD_DIM, RET_HEAD_DIM), lambda b, h, c, lg: (b, h, 0, 0))
    args = [log_gamma, q, k, v, gate, gn.reshape(heads, 1, RET_HEAD_DIM), s0]
    extra_args, extra_specs = _into(out_buf)
    grid_spec = pltpu.PrefetchScalarGridSpec(
        num_scalar_prefetch=1,
        grid=(n_batch, heads, n_steps),
        in_specs=[blk, blk, blk, blk,
                  pl.BlockSpec((None, 1, RET_HEAD_DIM), lambda b, h, c, lg: (h, 0, 0)), st] + extra_specs,
        out_specs=[blk, st],
        scratch_shapes=[pltpu.VMEM((RET_HEAD_DIM, RET_HEAD_DIM), F32)],
    )
    return pl.pallas_call(
        functools.partial(_retention_body, rows=rows),
        out_shape=[jax.ShapeDtypeStruct((q.shape[0], heads * RET_HEAD_DIM), BF16),
                   jax.ShapeDtypeStruct((n_batch, heads, RET_HEAD_DIM, RET_HEAD_DIM), F32)],
        grid_spec=grid_spec,
        input_output_aliases={len(args): 0} if extra_args else {},
        compiler_params=_params("arbitrary", "arbitrary", "arbitrary"),
        name="retention",
    )(*args, *extra_args)


def _neg_suffix(n):
    j = lax.broadcasted_iota(jnp.int32, (2 * n, n), 0) & (n - 1)
    s = lax.broadcasted_iota(jnp.int32, (2 * n, n), 1)
    return jnp.where(j >= s, -1.0, 0.0).astype(BF16)


def _lanes_to(x, n):
    return x[:, :n] if n <= LANES else jnp.concatenate([x] * (n // LANES), axis=1)


def _suffix_sums(fail, neg_ones):
    tk = fail.shape[1]
    hi, lo = _split_bf16(fail)
    if tk % LANES == 0:
        return jnp.dot(jnp.concatenate([hi, lo], axis=1), neg_ones, preferred_element_type=F32)
    return (jnp.dot(hi, neg_ones[:tk], preferred_element_type=F32)
            + jnp.dot(lo, neg_ones[:tk], preferred_element_type=F32))


def _sb_tiles(qs, kbs, vbs, neg_ones, carry, acc, masked):
    heads = range(len(qs))
    tk = kbs[0].shape[0]
    zs = [lax.dot_general(qs[h], kbs[h], _NT, preferred_element_type=F32) for h in heads]
    fails = [jnp.maximum(z, 0.0) + jnp.log2(1.0 + jnp.exp2(-jnp.abs(z))) for z in zs]
    if masked:
        t = lax.broadcasted_iota(jnp.int32, zs[0].shape, 0)
        s = lax.broadcasted_iota(jnp.int32, zs[0].shape, 1)
        valid = s < t
        fails = [jnp.where(valid, f, 0.0) for f in fails]
    suffixes = [_suffix_sums(f, neg_ones) for f in fails]
    probs = [jnp.exp2(zs[h] + suffixes[h] + _lanes_to(carry[h], tk)) for h in heads]
    if masked:
        probs = [jnp.where(valid, p, 0.0) for p in probs]
    for h in heads:
        acc[h] += jnp.dot(probs[h].astype(BF16), vbs[h], preferred_element_type=F32)
        carry[h] += jnp.broadcast_to(suffixes[h][:, :1], carry.shape[1:])


def _sb_body(q_ref, kd_ref, vd_ref, kp_ref, vp_ref, *rest, tk, n_heads, past_per_step, past_static):
    o_ref, carry, acc = rest[-3:]
    tq = q_ref.shape[0]
    n_past = past_static if past_static is not None else pl.program_id(1) * past_per_step
    cols = lambda h: slice(h * SB_HEAD_DIM, (h + 1) * SB_HEAD_DIM)
    heads = range(n_heads)
    carry[...] = jnp.zeros(carry.shape, F32)
    acc[...] = jnp.zeros(acc.shape, F32)
    queries = lambda: [q_ref[:, cols(h)] for h in heads]
    _sb_tiles(queries(), [kd_ref[:, cols(h)].astype(BF16) for h in heads],
              [vd_ref[:, cols(h)].astype(BF16) for h in heads], _neg_suffix(tq), carry, acc, True)
    neg_ones = _neg_suffix(tk)

    alive = lambda: jnp.max(carry[...]) > SB_DEAD_LOG2

    def step(state):
        it = state[0]
        rows = pl.ds(pl.multiple_of((n_past - 1 - it) * tk, tk), tk)
        _sb_tiles(queries(), [kp_ref[rows, cols(h)].astype(BF16) for h in heads],
                  [vp_ref[rows, cols(h)].astype(BF16) for h in heads], neg_ones, carry, acc, False)
        return it + 1, alive()

    lax.while_loop(lambda state: (state[0] < n_past) & state[1], step, (jnp.int32(0), alive()))
    for h in range(n_heads):
        o_ref[:, cols(h)] = acc[h].astype(o_ref.dtype)


def _sb_scratch(n_heads, tq):
    return [pltpu.VMEM((n_heads, tq, LANES), F32), pltpu.VMEM((n_heads, tq, SB_HEAD_DIM), F32)]


def _sb_prompt(q, k, v, n_tok):
    heads = q.shape[1] // SB_HEAD_DIM
    tq = _tile(n_tok, SB_BLOCK)
    nh = SB_HEADS_PER_STEP
    width = nh * SB_HEAD_DIM
    blk = pl.BlockSpec((tq, width), lambda h, i: (i, h))
    full = pl.BlockSpec((n_tok, width), lambda h, i: (0, h))
    return pl.pallas_call(
        functools.partial(_sb_body, tk=tq, n_heads=nh, past_per_step=1, past_static=None),
        out_shape=jax.ShapeDtypeStruct((q.shape[0], heads * SB_HEAD_DIM), BF16),
        grid=(heads // nh, n_tok // tq),
        in_specs=[blk, blk, blk, full, full],
        out_specs=blk,
        scratch_shapes=_sb_scratch(nh, tq),
        compiler_params=_params("parallel", "arbitrary"),
        name="stick_breaking_prompt",
    )(q, k, v, k, v)


def _sb_sample(q, k, v, cache_k, cache_v, row_off, n_new, out_buf):
    heads = q.shape[1] // SB_HEAD_DIM
    n_batch, past, _ = cache_k.shape
    tk = _tile(past, SB_BLOCK)
    assert row_off % n_new == 0
    ro = row_off // n_new
    nh = SB_HEADS_PER_STEP
    width = nh * SB_HEAD_DIM
    blk = pl.BlockSpec((n_new, width), lambda b, h: (ro + b, h))
    full = pl.BlockSpec((None, past, width), lambda b, h: (b, 0, h))
    extra_args, extra_specs = _into(out_buf)
    return pl.pallas_call(
        functools.partial(_sb_body, tk=tk, n_heads=nh, past_per_step=0, past_static=past // tk),
        out_shape=jax.ShapeDtypeStruct(out_buf.shape, out_buf.dtype),
        grid=(n_batch, heads // nh),
        in_specs=[blk, blk, blk, full, full] + extra_specs,
        out_specs=blk,
        scratch_shapes=_sb_scratch(nh, n_new),
        input_output_aliases={5: 0},
        compiler_params=_params("parallel", "arbitrary"),
        name="stick_breaking_sample",
    )(q, k, v, cache_k, cache_v, *extra_args)


def _cross_body(q_ref, k_ref, v_ref, *rest, scale):
    o_ref = rest[-1]
    s = lax.dot_general(q_ref[...], k_ref[...].astype(BF16), _NT, preferred_element_type=F32) * scale
    e = jnp.exp(s - jnp.max(s, axis=-1, keepdims=True))
    p = e / jnp.sum(e, axis=-1, keepdims=True)
    o_ref[...] = jnp.dot(p.astype(BF16), v_ref[...].astype(BF16),
                         preferred_element_type=F32).astype(o_ref.dtype)


def _cross_attend(q, mem_k, mem_v, row_off, rows_per_seq, out_buf=None):
    n_seq, n_mem, d = mem_k.shape
    dh = d // MEM_HEADS
    tm = _tile(rows_per_seq, ROW_TILE)
    steps = rows_per_seq // tm
    assert row_off % tm == 0
    ro = row_off // tm
    mem = pl.BlockSpec((None, n_mem, dh), lambda i, h: (i // steps, 0, h))
    blk = pl.BlockSpec((tm, dh), lambda i, h: (ro + i, h))
    extra_args, extra_specs = _into(out_buf)
    return pl.pallas_call(
        functools.partial(_cross_body, scale=dh ** -0.5),
        out_shape=jax.ShapeDtypeStruct((q.shape[0], d), BF16),
        grid=(n_seq * steps, MEM_HEADS),
        in_specs=[blk, mem, mem] + extra_specs,
        out_specs=blk,
        input_output_aliases={3: 0} if extra_args else {},
        compiler_params=_params("parallel", "arbitrary"),
        name="cross_attention",
    )(q, mem_k, mem_v, *extra_args)


def _split_bf16(x):
    hi = x.astype(BF16)
    return hi, (x - hi.astype(F32)).astype(BF16)


def _router_body(x_ref, g_ref, wr_ref, br_ref, h_ref, ids_ref, wts_ref):
    h = _rmsnorm_rows(x_ref[...], g_ref[...])
    h_ref[...] = h
    h_hi, h_lo = _split_bf16(h)
    w_hi, w_lo = _split_bf16(wr_ref[...])
    logits = (jnp.dot(h_hi, w_hi, preferred_element_type=F32)
              + jnp.dot(h_hi, w_lo, preferred_element_type=F32)
              + jnp.dot(h_lo, w_hi, preferred_element_type=F32)) + br_ref[...]
    lane = lax.broadcasted_iota(jnp.int32, logits.shape, 1).astype(F32)
    big = float(LANES)
    neg = -jnp.inf

    def first_max(vals):
        top = jnp.max(vals, axis=-1, keepdims=True)
        return top, jnp.min(jnp.where(vals == top, lane, big), axis=-1, keepdims=True)

    is_group = lane < N_GROUPS
    g_max, g_top = first_max(jnp.where(is_group, logits, neg))
    g_w = 1.0 / jnp.sum(jnp.where(is_group, jnp.exp(logits - g_max), 0.0), axis=-1, keepdims=True)
    first = N_GROUPS + EXPERTS_PER_GROUP * g_top
    cand = jnp.where((lane >= first) & (lane < first + EXPERTS_PER_GROUP), logits, neg)
    v1, i1 = first_max(cand)
    v2, i2 = first_max(jnp.where(lane == i1, neg, cand))
    t = jnp.exp(v2 - v1)
    w1 = g_w / (1.0 + t)
    w2 = g_w * t / (1.0 + t)
    ids = jnp.where(lane == 0.0, i1 - N_GROUPS, jnp.where(lane == 1.0, i2 - N_GROUPS, 0.0))
    ids_ref[...] = ids.astype(jnp.int32)
    wts_ref[...] = jnp.where(lane == 0.0, w1, jnp.where(lane == 1.0, w2, 0.0))


def _norm_and_route(x, g, wg, bg, we, be):
    m, d = x.shape
    tm = _tile(m, 256)
    pad = LANES - N_GROUPS - N_EXPERTS
    wr = jnp.concatenate([wg, we, jnp.zeros((d, pad), F32)], axis=1)
    br = jnp.concatenate([bg, be, jnp.zeros((pad,), F32)]).reshape(1, LANES)
    row = lambda c: pl.BlockSpec((tm, c), lambda i: (i, 0))
    return pl.pallas_call(
        _router_body,
        out_shape=[jax.ShapeDtypeStruct((m, d), F32),
                   jax.ShapeDtypeStruct((m, LANES), jnp.int32),
                   jax.ShapeDtypeStruct((m, LANES), F32)],
        grid=(m // tm,),
        in_specs=[row(d), pl.BlockSpec((1, d), lambda i: (0, 0)),
                  pl.BlockSpec((d, LANES), lambda i: (0, 0)), pl.BlockSpec((1, LANES), lambda i: (0, 0))],
        out_specs=[row(d), row(LANES), row(LANES)],
        compiler_params=_params("parallel"),
        name="norm_route",
    )(x, g.reshape(1, d), wr, br)


def _start_row_gather(idx_ref, first, stride, src_ref, buf, sem):
    last = src_ref.shape[0] - 1

    def one(r, _):
        t = jnp.minimum(idx_ref[first + r * stride], last)
        pltpu.make_async_copy(src_ref.at[pl.ds(t, 1)], buf.at[pl.ds(r, 1)], sem).start()
        return 0

    lax.fori_loop(0, buf.shape[0], one, 0, unroll=8)


def _wait_row_gather(src_ref, buf, sem):
    pltpu.make_async_copy(src_ref.at[pl.ds(0, buf.shape[0])], buf, sem).wait()


def _dispatch_body(idx_ref, used_ref, valid_ref, src_ref, o_ref, buf, sem):
    b = pl.program_id(0)
    rows = buf.shape[1]

    def start(blk):
        _start_row_gather(idx_ref, blk * rows, 1, src_ref, buf.at[blk % 2], sem.at[blk % 2])

    @pl.when(b == 0)
    def _():
        start(0)

    @pl.when(b + 1 < used_ref[0])
    def _():
        start(b + 1)

    @pl.when(b < used_ref[0])
    def _():
        slot = b % 2
        _wait_row_gather(src_ref, buf.at[slot], sem.at[slot])
        o_ref[...] = jnp.where(valid_ref[...] > 0.0, buf[slot], 0.0).astype(o_ref.dtype)


def _dispatch_rows(src, row_tok, row_valid, n_used):
    n_rows = row_tok.shape[0]
    d = src.shape[1]
    grid_spec = pltpu.PrefetchScalarGridSpec(
        num_scalar_prefetch=2,
        grid=(n_rows // MOE_ROWS,),
        in_specs=[pl.BlockSpec((MOE_ROWS, 1), lambda b, idx, used: (b, 0)),
                  pl.BlockSpec(memory_space=pl.ANY)],
        out_specs=pl.BlockSpec((MOE_ROWS, d), lambda b, idx, used: (b, 0)),
        scratch_shapes=[pltpu.VMEM((2, MOE_ROWS, d), src.dtype), pltpu.SemaphoreType.DMA((2,))],
    )
    return pl.pallas_call(
        _dispatch_body,
        out_shape=jax.ShapeDtypeStruct((n_rows, d), BF16),
        grid_spec=grid_spec,
        compiler_params=_params("arbitrary"),
        name="dispatch_rows",
    )(row_tok, n_used, row_valid.reshape(n_rows, 1), src)


def _expert_up_body(be_ref, first_ref, used_ref, x_ref, w1_ref, w3_ref, h_ref, w1b, w3b):
    b = pl.program_id(1)

    @pl.when(first_ref[b] == 1)
    def _():
        w1b[...] = w1_ref[...].astype(BF16)
        w3b[...] = w3_ref[...].astype(BF16)

    @pl.when(b < used_ref[0])
    def _():
        x = x_ref[...]
        a = jnp.dot(x, w1b[...], preferred_element_type=F32)
        g = jnp.dot(x, w3b[...], preferred_element_type=F32)
        h_ref[...] = (a * _sigmoid(a) * g).astype(h_ref.dtype)


def _expert_down_body(be_ref, first_ref, used_ref, h_ref, w2_ref, rw_ref, y_ref, w2b):
    b = pl.program_id(1)

    @pl.when(first_ref[b] == 1)
    def _():
        w2b[...] = w2_ref[...].astype(BF16)

    @pl.when(b < used_ref[0])
    def _():
        y_ref[...] = jnp.dot(h_ref[...], w2b[...], preferred_element_type=F32) * rw_ref[...]


def _expert_ffn(xs, row_w, blk_exp, blk_first, n_used, w1, w3, w2):
    n_rows, d = xs.shape
    f = w1.shape[2]
    n_blocks = n_rows // MOE_ROWS
    tf = _tile(f, COL_TILE)
    tn = _tile(d, 1024)
    last = lambda b, used: jnp.minimum(b, used[0] - 1)
    up_spec = pltpu.PrefetchScalarGridSpec(
        num_scalar_prefetch=3,
        grid=(f // tf, n_blocks),
        in_specs=[pl.BlockSpec((MOE_ROWS, d), lambda j, b, be, fi, us: (last(b, us), 0)),
                  pl.BlockSpec((None, d, tf), lambda j, b, be, fi, us: (be[b], 0, j)),
                  pl.BlockSpec((None, d, tf), lambda j, b, be, fi, us: (be[b], 0, j))],
        out_specs=pl.BlockSpec((MOE_ROWS, tf), lambda j, b, be, fi, us: (b, j)),
        scratch_shapes=[pltpu.VMEM((d, tf), BF16), pltpu.VMEM((d, tf), BF16)],
    )
    hidden = pl.pallas_call(
        _expert_up_body,
        out_shape=jax.ShapeDtypeStruct((n_rows, f), BF16),
        grid_spec=up_spec,
        compiler_params=_params("arbitrary", "arbitrary"),
        name="expert_up",
    )(blk_exp, blk_first, n_used, xs, w1, w3)
    down_spec = pltpu.PrefetchScalarGridSpec(
        num_scalar_prefetch=3,
        grid=(d // tn, n_blocks),
        in_specs=[pl.BlockSpec((MOE_ROWS, f), lambda j, b, be, fi, us: (last(b, us), 0)),
                  pl.BlockSpec((None, f, tn), lambda j, b, be, fi, us: (be[b], 0, j)),
                  pl.BlockSpec((MOE_ROWS, 1), lambda j, b, be, fi, us: (b, 0))],
        out_specs=pl.BlockSpec((MOE_ROWS, tn), lambda j, b, be, fi, us: (b, j)),
        scratch_shapes=[pltpu.VMEM((f, tn), BF16)],
    )
    return pl.pallas_call(
        _expert_down_body,
        out_shape=jax.ShapeDtypeStruct((n_rows, d), F32),
        grid_spec=down_spec,
        compiler_params=_params("arbitrary", "arbitrary"),
        name="expert_down",
    )(blk_exp, blk_first, n_used, hidden, w2, row_w.reshape(n_rows, 1))


def _route_layout(ids, wts, n_tok):
    n_assign = n_tok * TOP_K
    flat_e = ids.reshape(n_assign)
    flat_w = wts.reshape(n_assign)
    flat_t = jnp.repeat(jnp.arange(n_tok, dtype=jnp.int32), TOP_K)
    onehot = (flat_e[:, None] == jnp.arange(N_EXPERTS, dtype=jnp.int32)[None, :]).astype(jnp.int32)
    running = jnp.cumsum(onehot, axis=0)
    rank = jnp.take_along_axis(running, flat_e[:, None], axis=1)[:, 0] - 1
    counts = running[-1]
    padded = (counts + MOE_ROWS - 1) // MOE_ROWS * MOE_ROWS
    pad_end = jnp.cumsum(padded)
    dest = (pad_end - padded)[flat_e] + rank
    n_blocks = -(-n_assign // MOE_ROWS) + N_EXPERTS
    n_rows = n_blocks * MOE_ROWS
    row_tok = jnp.full((n_rows,), n_tok, jnp.int32).at[dest].set(flat_t)
    row_w = jnp.zeros((n_rows,), F32).at[dest].set(flat_w)
    blk = jnp.arange(n_blocks, dtype=jnp.int32)
    n_used = (pad_end[-1] // MOE_ROWS).astype(jnp.int32)
    blk_exp = jnp.searchsorted(pad_end, jnp.minimum(blk, n_used - 1) * MOE_ROWS, side='right').astype(jnp.int32)
    blk_exp = jnp.minimum(blk_exp, N_EXPERTS - 1)
    prev = jnp.concatenate([jnp.full((1,), -1, jnp.int32), blk_exp[:-1]])
    blk_first = ((blk_exp != prev) & (blk < n_used)).astype(jnp.int32)
    row_valid = (row_tok < n_tok).astype(F32)
    return row_tok, row_valid, row_w, dest.astype(jnp.int32), blk_exp, blk_first, n_used.reshape(1)


def _final_body(dest_ref, x_ref, y_ref, g_ref, op_ref, os_ref, buf, sem, *, prompt_tiles):
    i = pl.program_id(0)
    tm = buf.shape[2]

    def start(blk):
        for k in range(TOP_K):
            _start_row_gather(dest_ref, blk * tm * TOP_K + k, TOP_K, y_ref, buf.at[blk % 2, k], sem.at[blk % 2, k])

    @pl.when(i == 0)
    def _():
        start(0)

    @pl.when(i + 1 < pl.num_programs(0))
    def _():
        start(i + 1)

    slot = i % 2
    for k in range(TOP_K):
        _wait_row_gather(y_ref, buf.at[slot, k], sem.at[slot, k])
    y = _rmsnorm_rows(x_ref[...] + (buf[slot, 0] + buf[slot, 1]), g_ref[...])

    @pl.when(i < prompt_tiles)
    def _():
        op_ref[...] = y

    @pl.when(i >= prompt_tiles)
    def _():
        os_ref[...] = y


def _combine_and_norm(x, y_rows, dest, g, n_prompt):
    m, d = x.shape
    tm = _tile(m - n_prompt, _tile(n_prompt, 256))
    assert n_prompt % tm == 0
    pt = n_prompt // tm
    grid_spec = pltpu.PrefetchScalarGridSpec(
        num_scalar_prefetch=1,
        grid=(m // tm,),
        in_specs=[pl.BlockSpec((tm, d), lambda i, dest: (i, 0)),
                  pl.BlockSpec(memory_space=pl.ANY),
                  pl.BlockSpec((1, d), lambda i, dest: (0, 0))],
        out_specs=[pl.BlockSpec((tm, d), lambda i, dest: (jnp.minimum(i, pt - 1), 0)),
                   pl.BlockSpec((tm, d), lambda i, dest: (jnp.maximum(i - pt, 0), 0))],
        scratch_shapes=[pltpu.VMEM((2, TOP_K, tm, d), F32), pltpu.SemaphoreType.DMA((2, TOP_K))],
    )
    return pl.pallas_call(
        functools.partial(_final_body, prompt_tiles=pt),
        out_shape=[jax.ShapeDtypeStruct((n_prompt, d), F32), jax.ShapeDtypeStruct((m - n_prompt, d), F32)],
        grid_spec=grid_spec,
        compiler_params=_params("arbitrary"),
        name="combine_norm",
    )(dest, x, y_rows, g.reshape(1, d))


def _layer(xp, xs, n_seq, n_new, past_len, cache_sb_k, cache_sb_v, state_ret, cache_mem_k, cache_mem_v,
           mem_prompt, norm_mix, w_in, ret_gn, w_out, norm_cross, norm_mem, w_cq, w_ck, w_cv, w_co, norm_ffn,
           wg, bg, we, be, w1, w3, w2):
    n_prompt, d = xp.shape
    n_tok = n_prompt + xs.shape[0]
    d_ret = d // 2
    d_sb = d - d_ret
    ret_heads = d_ret // RET_HEAD_DIM
    sds = jax.ShapeDtypeStruct

    pos = jnp.concatenate([jnp.arange(n_prompt, dtype=jnp.int32),
                           jnp.tile(past_len + jnp.arange(n_new, dtype=jnp.int32), n_seq)])
    half = RET_HEAD_DIM // 2
    inv = ROPE_BASE ** (-jnp.arange(half, dtype=F32) / half)
    ang = pos.astype(F32)[:, None] * inv[None, :]
    cos, sin = jnp.cos(ang), jnp.sin(ang)
    log_gamma = jnp.log1p(-jnp.exp2(-5.0 - jnp.arange(ret_heads, dtype=F32)))

    h = _rmsnorm_stacked(xp, xs, norm_mix, BF16)
    rot = [(cos, half), (sin, half)]
    bf = lambda n: [(sds((n_tok, n), BF16), None)]
    (rq,) = _matmul([h], w_in, 0, d_ret, _ep_rotary(1.0), rot, bf(d_ret), "in_proj_ret_q")
    (rk,) = _matmul([h], w_in, d_ret, d_ret, _ep_rotary(RET_HEAD_DIM ** -0.5), rot, bf(d_ret), "in_proj_ret_k")
    (rv,) = _matmul([h], w_in, 2 * d_ret, d_ret, _ep_store(1.0), [], bf(d_ret), "in_proj_ret_v")
    (rg,) = _matmul([h], w_in, 3 * d_ret, d_ret, _ep_store(1.0), [], [(sds((n_tok, d_ret), F32), None)],
                    "in_proj_ret_gate")
    (sq,) = _matmul([h], w_in, 4 * d_ret, d_sb, _ep_store(SB_HEAD_DIM ** -0.5 * LOG2_E), [], bf(d_sb),
                    "in_proj_sb_q")
    split = [(sds((n_prompt, d_sb), F32), PROMPT), (sds((n_tok - n_prompt, d_sb), F32), SAMPLE),
             (sds((n_tok, d_sb), BF16), None)]
    sk_p, sk_s, sk_b = _matmul([h], w_in, 4 * d_ret + d_sb, d_sb, _ep_store_split, [], split, "in_proj_sb_k",
                               n_prompt=n_prompt)
    sv_p, sv_s, sv_b = _matmul([h], w_in, 4 * d_ret + 2 * d_sb, d_sb, _ep_store_split, [], split,
                               "in_proj_sb_v", n_prompt=n_prompt)

    rows_p = _tile(n_prompt, RET_BLOCK)
    zeros_state = jnp.zeros((1, ret_heads, RET_HEAD_DIM, RET_HEAD_DIM), F32)
    yr, state_p = _retention(log_gamma, rq, rk, rv, rg, ret_gn, zeros_state, 0, 1, n_prompt // rows_p, rows_p)
    yr, state_s = _retention(log_gamma, rq, rk, rv, rg, ret_gn, state_ret, n_prompt, n_seq, 1, n_new,
                             out_buf=yr)
    sb = _sb_prompt(sq, sk_b, sv_b, n_prompt)
    sb = _sb_sample(sq, sk_b, sv_b, cache_sb_k.reshape(n_seq, past_len, d_sb),
                    cache_sb_v.reshape(n_seq, past_len, d_sb), n_prompt, n_new, sb)
    res = [(sds((n_tok, d), F32), None)]
    (x,) = _matmul([yr, sb], w_out, 0, d, _ep_residual_split, [(xp, PROMPT), (xs, SAMPLE)], res, "out_proj",
                   n_prompt=n_prompt)

    n_mem = mem_prompt.shape[0]
    m = _rmsnorm(mem_prompt, norm_mem, BF16)
    mem_out = [(sds((n_mem, d), F32), None)]
    (mk,) = _matmul([m], w_ck, 0, d, _ep_store(1.0), [], mem_out, "mem_k")
    (mv,) = _matmul([m], w_cv, 0, d, _ep_store(1.0), [], mem_out, "mem_v")
    h = _rmsnorm(x, norm_cross, BF16)
    (cq,) = _matmul([h], w_cq, 0, d, _ep_store(1.0), [], bf(d), "cross_q")
    co = _cross_attend(cq, mk.reshape(1, n_mem, d), mv.reshape(1, n_mem, d), 0, n_prompt)
    co = _cross_attend(cq, cache_mem_k.reshape(n_seq, n_mem, d), cache_mem_v.reshape(n_seq, n_mem, d),
                       n_prompt, n_new, out_buf=co)
    (x,) = _matmul([co], w_co, 0, d, _ep_residual, [(x, None)], res, "cross_out")

    hn, ids, wts = _norm_and_route(x, norm_ffn, wg, bg, we, be)
    row_tok, row_valid, row_w, dest, blk_exp, blk_first, n_used = _route_layout(
        ids[:, :TOP_K], wts[:, :TOP_K], n_tok)
    x_rows = _dispatch_rows(hn, row_tok, row_valid, n_used)
    y_rows = _expert_ffn(x_rows, row_w, blk_exp, blk_first, n_used, w1, w3, w2)
    return x, y_rows, dest, (sk_p, sv_p, sk_s, sv_s, state_p, state_s, mk, mv)


def kernel(x_prompt, x_sample, cache_sb_k, cache_sb_v, state_ret, cache_mem_k, cache_mem_v, mem_prompt, norm_mix, w_in, ret_gn, w_out, norm_cross, norm_mem, w_cq, w_ck, w_cv, w_co, norm_ffn, router_group_w, router_group_b, router_expert_w, router_expert_b, expert_w1, expert_w3, expert_w2, norm_final):
    batch, n_prompt, d = x_prompt.shape
    n_seq, n_new, _ = x_sample.shape
    depth, _, past_len, sb_heads, _ = cache_sb_k.shape
    n_mem = mem_prompt.shape[1]
    assert batch == 1 and depth == 1
    ret_heads = (d // 2) // RET_HEAD_DIM
    x, y_rows, dest, (sk_p, sv_p, sk_s, sv_s, state_p, state_s, mk, mv) = _layer(
        x_prompt.reshape(n_prompt, d), x_sample.reshape(n_seq * n_new, d), n_seq, n_new, past_len,
        cache_sb_k, cache_sb_v, state_ret[0], cache_mem_k, cache_mem_v, mem_prompt[0], norm_mix[0], w_in,
        ret_gn[0], w_out, norm_cross[0], norm_mem[0], w_cq, w_ck, w_cv, w_co, norm_ffn[0], router_group_w[0],
        router_group_b[0], router_expert_w[0], router_expert_b[0], expert_w1[0], expert_w3[0], expert_w2[0])
    y_prompt, y_sample = _combine_and_norm(x, y_rows, dest, norm_final, n_prompt)
    y_prompt = y_prompt.reshape(1, n_prompt, d)
    y_sample = y_sample.reshape(n_seq, n_new, d)
    sb_shape_p = (1, 1, n_prompt, sb_heads, SB_HEAD_DIM)
    sb_shape_s = (1, n_seq, n_new, sb_heads, SB_HEAD_DIM)
    mem_shape = (1, 1, n_mem, MEM_HEADS, d // MEM_HEADS)
    return (y_prompt, y_sample,
            sk_p.reshape(sb_shape_p), sv_p.reshape(sb_shape_p),
            state_p.reshape(1, 1, ret_heads, RET_HEAD_DIM, RET_HEAD_DIM),
            mk.reshape(mem_shape), mv.reshape(mem_shape),
            sk_s.reshape(sb_shape_s), sv_s.reshape(sb_shape_s),
            state_s.reshape(1, n_seq, ret_heads, RET_HEAD_DIM, RET_HEAD_DIM))
```

```python
import functools

import jax
import jax.numpy as jnp
from jax import lax
from jax.experimental import pallas as pl
from jax.experimental.pallas import tpu as pltpu

BF16 = jnp.bfloat16
F32 = jnp.float32

EPS = 1e-6
CHUNK = 64
RET_HEAD_DIM = 256
SB_HEAD_DIM = 128
MEM_HEADS = 4
N_GROUPS = 4
EXPERTS_PER_GROUP = 8
N_EXPERTS = N_GROUPS * EXPERTS_PER_GROUP
TOP_K = 2
ROPE_BASE = 10000.0

LANES = 128
VMEM_LIMIT_BYTES = 56 * 1024 * 1024
ROW_TILE = 512
COL_TILE = 512
RET_BLOCK = 256
SB_BLOCK = 256
SB_HEADS_PER_STEP = 2
LOG2_E = 1.4426950408889634
SB_DEAD_LOG2 = -64.0
MOE_ROWS = 512
EXPERT_OUT_TILE = 2048

PROMPT = "prompt"
SAMPLE = "sample"

_NT = (((1,), (1,)), ((), ()))
_TN = (((0,), (0,)), ((), ()))


def _tile(n, pref):
    t = min(pref, n)
    while n % t:
        t //= 2
    return t


def _params(*sem):
    return pltpu.CompilerParams(dimension_semantics=sem, vmem_limit_bytes=VMEM_LIMIT_BYTES)


def _sigmoid(x):
    return 1.0 / (1.0 + jnp.exp(-x))


def _rmsnorm_rows(x, g):
    ms = jnp.mean(x * x, axis=-1, keepdims=True)
    return x * lax.rsqrt(ms + EPS) * g


def _rmsnorm_body(x_ref, g_ref, o_ref):
    o_ref[...] = _rmsnorm_rows(x_ref[...], g_ref[...]).astype(o_ref.dtype)


def _rmsnorm(x, g, out_dtype):
    m, d = x.shape
    tm = _tile(m, 256)
    return pl.pallas_call(
        _rmsnorm_body,
        out_shape=jax.ShapeDtypeStruct((m, d), out_dtype),
        grid=(m // tm,),
        in_specs=[pl.BlockSpec((tm, d), lambda i: (i, 0)), pl.BlockSpec((1, d), lambda i: (0, 0))],
        out_specs=pl.BlockSpec((tm, d), lambda i: (i, 0)),
        compiler_params=_params("parallel"),
        name="rmsnorm",
    )(x, g.reshape(1, d))


def _rmsnorm_stacked_body(xp_ref, xs_ref, g_ref, o_ref, *, prompt_tiles):
    is_prompt = pl.program_id(0) < prompt_tiles

    @pl.when(is_prompt)
    def _():
        o_ref[...] = _rmsnorm_rows(xp_ref[...], g_ref[...]).astype(o_ref.dtype)

    @pl.when(jnp.logical_not(is_prompt))
    def _():
        o_ref[...] = _rmsnorm_rows(xs_ref[...], g_ref[...]).astype(o_ref.dtype)


def _rmsnorm_stacked(xp, xs, g, out_dtype):
    (n_p, d), n_s = xp.shape, xs.shape[0]
    tm = _tile(n_s, _tile(n_p, 256))
    pt = n_p // tm
    return pl.pallas_call(
        functools.partial(_rmsnorm_stacked_body, prompt_tiles=pt),
        out_shape=jax.ShapeDtypeStruct((n_p + n_s, d), out_dtype),
        grid=((n_p + n_s) // tm,),
        in_specs=[pl.BlockSpec((tm, d), lambda i: (jnp.minimum(i, pt - 1), 0)),
                  pl.BlockSpec((tm, d), lambda i: (jnp.maximum(i - pt, 0), 0)),
                  pl.BlockSpec((1, d), lambda i: (0, 0))],
        out_specs=pl.BlockSpec((tm, d), lambda i: (i, 0)),
        compiler_params=_params("arbitrary"),
        name="rmsnorm_stacked",
    )(xp, xs, g.reshape(1, d))


def _matmul_body(*refs, n_lhs, n_extra, n_out, epilogue):
    x_refs = refs[:n_lhs]
    w_refs = refs[n_lhs:2 * n_lhs]
    extra = refs[2 * n_lhs:2 * n_lhs + n_extra]
    outs = refs[2 * n_lhs + n_extra:2 * n_lhs + n_extra + n_out]
    wb = refs[2 * n_lhs + n_extra + n_out:]

    @pl.when(pl.program_id(1) == 0)
    def _():
        for p in range(n_lhs):
            wb[p][...] = w_refs[p][...].astype(BF16)

    acc = jnp.dot(x_refs[0][...], wb[0][...], preferred_element_type=F32)
    for p in range(1, n_lhs):
        acc += jnp.dot(x_refs[p][...], wb[p][...], preferred_element_type=F32)
    epilogue(acc, extra, outs)


def _matmul(xs, w, col_off, n_cols, epilogue, extras, outs, name, n_prompt=None):
    m = xs[0].shape[0]
    tm = _tile(m, ROW_TILE) if n_prompt is None else _tile(m - n_prompt, _tile(n_prompt, ROW_TILE))
    tn = _tile(n_cols, COL_TILE)
    assert col_off % tn == 0
    jo = col_off // tn
    pt = None if n_prompt is None else n_prompt // tm
    in_specs, args = [], []
    for x in xs:
        in_specs.append(pl.BlockSpec((tm, x.shape[1]), lambda j, i: (i, 0)))
        args.append(x)
    row = 0
    for x in xs:
        kp = x.shape[1]
        assert row % kp == 0
        in_specs.append(pl.BlockSpec((None, kp, tn), lambda j, i, r=row // kp: (0, r, j + jo)))
        args.append(w)
        row += kp
    assert row == w.shape[1]

    def spec(kind):
        if kind is None:
            return pl.BlockSpec((tm, tn), lambda j, i: (i, j))
        if kind == PROMPT:
            return pl.BlockSpec((tm, tn), lambda j, i: (jnp.minimum(i, pt - 1), j))
        if kind == SAMPLE:
            return pl.BlockSpec((tm, tn), lambda j, i: (jnp.maximum(i - pt, 0), j))
        return pl.BlockSpec((tm, kind), lambda j, i: (i, 0))

    for a, kind in extras:
        in_specs.append(spec(kind))
        args.append(a)
    body = functools.partial(_matmul_body, n_lhs=len(xs), n_extra=len(extras), n_out=len(outs),
                             epilogue=epilogue if n_prompt is None else functools.partial(epilogue, pt))
    res = pl.pallas_call(
        body,
        out_shape=[o for o, _ in outs],
        grid=(n_cols // tn, m // tm),
        in_specs=in_specs,
        out_specs=[spec(kind) for _, kind in outs],
        scratch_shapes=[pltpu.VMEM((x.shape[1], tn), BF16) for x in xs],
        compiler_params=_params("arbitrary", "arbitrary"),
        name=name,
    )(*args)
    return res


def _ep_store(scale):
    def ep(acc, extra, outs):
        for o in outs:
            o[...] = (acc * scale).astype(o.dtype) if scale != 1.0 else acc.astype(o.dtype)
    return ep


def _ep_store_split(prompt_tiles, acc, extra, outs):
    is_prompt = pl.program_id(1) < prompt_tiles

    @pl.when(is_prompt)
    def _():
        outs[0][...] = acc

    @pl.when(jnp.logical_not(is_prompt))
    def _():
        outs[1][...] = acc

    outs[2][...] = acc.astype(outs[2].dtype)


def _ep_residual(acc, extra, outs):
    outs[0][...] = extra[0][...] + acc


def _ep_residual_split(prompt_tiles, acc, extra, outs):
    res = jnp.where(pl.program_id(1) < prompt_tiles, extra[0][...], extra[1][...])
    outs[0][...] = res + acc


def _ep_rotary(scale):
    half = RET_HEAD_DIM // 2

    def ep(acc, extra, outs):
        cos = extra[0][...]
        sin = extra[1][...]
        for h in range(acc.shape[1] // RET_HEAD_DIM):
            lo = h * RET_HEAD_DIM
            x1 = acc[:, lo:lo + half]
            x2 = acc[:, lo + half:lo + RET_HEAD_DIM]
            outs[0][:, lo:lo + half] = ((x1 * cos - x2 * sin) * scale).astype(outs[0].dtype)
            outs[0][:, lo + half:lo + RET_HEAD_DIM] = ((x1 * sin + x2 * cos) * scale).astype(outs[0].dtype)
    return ep


def _retention_body(lg_ref, q_ref, k_ref, v_ref, g_ref, gn_ref, s0_ref, *rest, rows):
    o_ref, sout_ref, s_scr = rest[-3:]
    h = pl.program_id(1)

    @pl.when(pl.program_id(2) == 0)
    def _():
        s_scr[...] = s0_ref[...]

    lg = lg_ref[h]
    q = q_ref[...]
    k = k_ref[...]
    v = v_ref[...]
    ii = lax.broadcasted_iota(jnp.int32, (rows, rows), 0)
    jj = lax.broadcasted_iota(jnp.int32, (rows, rows), 1)
    shift = CHUNK.bit_length() - 1
    dist = jnp.abs(ii - jj).astype(F32)
    decay = jnp.where((jj >> shift) <= (ii >> shift), jnp.exp(lg * dist), 0.0)
    scores = lax.dot_general(q, k, _NT, preferred_element_type=F32) * decay
    out = jnp.dot(scores.astype(BF16), v, preferred_element_type=F32)
    s = s_scr[...]
    idx = lax.broadcasted_iota(jnp.int32, (rows, 1), 0).astype(F32)
    out += jnp.exp(lg * (idx + 1.0)) * jnp.dot(q, s.astype(BF16), preferred_element_type=F32)
    kd = (k.astype(F32) * jnp.exp(lg * (rows - 1.0 - idx))).astype(BF16)
    s_new = jnp.exp(lg * rows) * s + lax.dot_general(kd, v, _TN, preferred_element_type=F32)
    s_scr[...] = s_new
    sout_ref[...] = s_new
    mu = jnp.mean(out, axis=-1, keepdims=True)
    cen = out - mu
    var = jnp.mean(cen * cen, axis=-1, keepdims=True)
    yr = cen * lax.rsqrt(var + EPS) * gn_ref[...]
    g = g_ref[...]
    o_ref[...] = (g * _sigmoid(g) * yr).astype(o_ref.dtype)


def _into(out_buf):
    if out_buf is None:
        return [], []
    return [out_buf], [pl.BlockSpec(memory_space=pl.ANY)]


def _retention(log_gamma, q, k, v, gate, gn, s0, row_off, n_batch, n_steps, rows, out_buf=None):
    heads = q.shape[1] // RET_HEAD_DIM
    assert row_off % rows == 0 and rows % CHUNK == 0
    ro = row_off // rows
    tok = lambda b, h, c, lg: (ro + b * n_steps + c, h)
    blk = pl.BlockSpec((rows, RET_HEAD_DIM), tok)
    st = pl.BlockSpec((None, None, RET_HEAD_DIM, RET_HEAD_DIM), lambda b, h, c, lg: (b, h, 0, 0))
    args = [log_gamma, q, k, v, gate, gn.reshape(heads, 1, RET_HEAD_DIM), s0]
    extra_args, extra_specs = _into(out_buf)
    grid_spec = pltpu.PrefetchScalarGridSpec(
        num_scalar_prefetch=1,
        grid=(n_batch, heads, n_steps),
        in_specs=[blk, blk, blk, blk,
                  pl.BlockSpec((None, 1, RET_HEAD_DIM), lambda b, h, c, lg: (h, 0, 0)), st] + extra_specs,
        out_specs=[blk, st],
        scratch_shapes=[pltpu.VMEM((RET_HEAD_DIM, RET_HEAD_DIM), F32)],
    )
    return pl.pallas_call(
        functools.partial(_retention_body, rows=rows),
        out_shape=[jax.ShapeDtypeStruct((q.shape[0], heads * RET_HEAD_DIM), BF16),
                   jax.ShapeDtypeStruct((n_batch, heads, RET_HEAD_DIM, RET_HEAD_DIM), F32)],
        grid_spec=grid_spec,
        input_output_aliases={len(args): 0} if extra_args else {},
        compiler_params=_params("arbitrary", "arbitrary", "arbitrary"),
        name="retention",
    )(*args, *extra_args)


def _neg_suffix(n):
    j = lax.broadcasted_iota(jnp.int32, (2 * n, n), 0) & (n - 1)
    s = lax.broadcasted_iota(jnp.int32, (2 * n, n), 1)
    return jnp.where(j >= s, -1.0, 0.0).astype(BF16)


def _lanes_to(x, n):
    return x[:, :n] if n <= LANES else jnp.concatenate([x] * (n // LANES), axis=1)


def _suffix_sums(fail, neg_ones):
    tk = fail.shape[1]
    hi, lo = _split_bf16(fail)
    if tk % LANES == 0:
        return jnp.dot(jnp.concatenate([hi, lo], axis=1), neg_ones, preferred_element_type=F32)
    return (jnp.dot(hi, neg_ones[:tk], preferred_element_type=F32)
            + jnp.dot(lo, neg_ones[:tk], preferred_element_type=F32))


def _sb_tiles(qs, kbs, vbs, neg_ones, carry, acc, masked):
    heads = range(len(qs))
    tk = kbs[0].shape[0]
    zs = [lax.dot_general(qs[h], kbs[h], _NT, preferred_element_type=F32) for h in heads]
    fails = [jnp.maximum(z, 0.0) + jnp.log2(1.0 + jnp.exp2(-jnp.abs(z))) for z in zs]
    if masked:
        t = lax.broadcasted_iota(jnp.int32, zs[0].shape, 0)
        s = lax.broadcasted_iota(jnp.int32, zs[0].shape, 1)
        valid = s < t
        fails = [jnp.where(valid, f, 0.0) for f in fails]
    suffixes = [_suffix_sums(f, neg_ones) for f in fails]
    probs = [jnp.exp2(zs[h] + suffixes[h] + _lanes_to(carry[h], tk)) for h in heads]
    if masked:
        probs = [jnp.where(valid, p, 0.0) for p in probs]
    for h in heads:
        acc[h] += jnp.dot(probs[h].astype(BF16), vbs[h], preferred_element_type=F32)
        carry[h] += jnp.broadcast_to(suffixes[h][:, :1], carry.shape[1:])


def _sb_body(q_ref, kd_ref, vd_ref, kp_ref, vp_ref, *rest, tk, n_heads, past_per_step, past_static):
    o_ref, carry, acc = rest[-3:]
    tq = q_ref.shape[0]
    n_past = past_static if past_static is not None else pl.program_id(1) * past_per_step
    cols = lambda h: slice(h * SB_HEAD_DIM, (h + 1) * SB_HEAD_DIM)
    heads = range(n_heads)
    carry[...] = jnp.zeros(carry.shape, F32)
    acc[...] = jnp.zeros(acc.shape, F32)
    queries = lambda: [q_ref[:, cols(h)] for h in heads]
    _sb_tiles(queries(), [kd_ref[:, cols(h)].astype(BF16) for h in heads],
              [vd_ref[:, cols(h)].astype(BF16) for h in heads], _neg_suffix(tq), carry, acc, True)
    neg_ones = _neg_suffix(tk)

    alive = lambda: jnp.max(carry[...]) > SB_DEAD_LOG2

    def step(state):
        it = state[0]
        rows = pl.ds(pl.multiple_of((n_past - 1 - it) * tk, tk), tk)
        _sb_tiles(queries(), [kp_ref[rows, cols(h)].astype(BF16) for h in heads],
                  [vp_ref[rows, cols(h)].astype(BF16) for h in heads], neg_ones, carry, acc, False)
        return it + 1, alive()

    lax.while_loop(lambda state: (state[0] < n_past) & state[1], step, (jnp.int32(0), alive()))
    for h in range(n_heads):
        o_ref[:, cols(h)] = acc[h].astype(o_ref.dtype)


def _sb_scratch(n_heads, tq):
    return [pltpu.VMEM((n_heads, tq, LANES), F32), pltpu.VMEM((n_heads, tq, SB_HEAD_DIM), F32)]


def _sb_prompt(q, k, v, n_tok):
    heads = q.shape[1] // SB_HEAD_DIM
    tq = _tile(n_tok, SB_BLOCK)
    nh = SB_HEADS_PER_STEP
    width = nh * SB_HEAD_DIM
    blk = pl.BlockSpec((tq, width), lambda h, i: (i, h))
    full = pl.BlockSpec((n_tok, width), lambda h, i: (0, h))
    return pl.pallas_call(
        functools.partial(_sb_body, tk=tq, n_heads=nh, past_per_step=1, past_static=None),
        out_shape=jax.ShapeDtypeStruct((q.shape[0], heads * SB_HEAD_DIM), BF16),
        grid=(heads // nh, n_tok // tq),
        in_specs=[blk, blk, blk, full, full],
        out_specs=blk,
        scratch_shapes=_sb_scratch(nh, tq),
        compiler_params=_params("parallel", "arbitrary"),
        name="stick_breaking_prompt",
    )(q, k, v, k, v)


def _sb_sample(q, k, v, cache_k, cache_v, row_off, n_new, out_buf):
    heads = q.shape[1] // SB_HEAD_DIM
    n_batch, past, _ = cache_k.shape
    tk = _tile(past, SB_BLOCK)
    assert row_off % n_new == 0
    ro = row_off // n_new
    nh = SB_HEADS_PER_STEP
    width = nh * SB_HEAD_DIM
    blk = pl.BlockSpec((n_new, width), lambda b, h: (ro + b, h))
    full = pl.BlockSpec((None, past, width), lambda b, h: (b, 0, h))
    extra_args, extra_specs = _into(out_buf)
    return pl.pallas_call(
        functools.partial(_sb_body, tk=tk, n_heads=nh, past_per_step=0, past_static=past // tk),
        out_shape=jax.ShapeDtypeStruct(out_buf.shape, out_buf.dtype),
        grid=(n_batch, heads // nh),
        in_specs=[blk, blk, blk, full, full] + extra_specs,
        out_specs=blk,
        scratch_shapes=_sb_scratch(nh, n_new),
        input_output_aliases={5: 0},
        compiler_params=_params("parallel", "arbitrary"),
        name="stick_breaking_sample",
    )(q, k, v, cache_k, cache_v, *extra_args)


def _cross_body(q_ref, k_ref, v_ref, *rest, scale):
    o_ref = rest[-1]
    s = lax.dot_general(q_ref[...], k_ref[...].astype(BF16), _NT, preferred_element_type=F32) * scale
    e = jnp.exp(s - jnp.max(s, axis=-1, keepdims=True))
    p = e / jnp.sum(e, axis=-1, keepdims=True)
    o_ref[...] = jnp.dot(p.astype(BF16), v_ref[...].astype(BF16),
                         preferred_element_type=F32).astype(o_ref.dtype)


def _cross_attend(q, mem_k, mem_v, row_off, rows_per_seq, out_buf=None):
    n_seq, n_mem, d = mem_k.shape
    dh = d // MEM_HEADS
    tm = _tile(rows_per_seq, ROW_TILE)
    steps = rows_per_seq // tm
    assert row_off % tm == 0
    ro = row_off // tm
    mem = pl.BlockSpec((None, n_mem, dh), lambda i, h: (i // steps, 0, h))
    blk = pl.BlockSpec((tm, dh), lambda i, h: (ro + i, h))
    extra_args, extra_specs = _into(out_buf)
    return pl.pallas_call(
        functools.partial(_cross_body, scale=dh ** -0.5),
        out_shape=jax.ShapeDtypeStruct((q.shape[0], d), BF16),
        grid=(n_seq * steps, MEM_HEADS),
        in_specs=[blk, mem, mem] + extra_specs,
        out_specs=blk,
        input_output_aliases={3: 0} if extra_args else {},
        compiler_params=_params("parallel", "arbitrary"),
        name="cross_attention",
    )(q, mem_k, mem_v, *extra_args)


def _split_bf16(x):
    hi = x.astype(BF16)
    return hi, (x - hi.astype(F32)).astype(BF16)


def _router_body(x_ref, g_ref, wr_ref, br_ref, h_ref, ids_ref, wts_ref):
    h = _rmsnorm_rows(x_ref[...], g_ref[...])
    h_ref[...] = h
    h_hi, h_lo = _split_bf16(h)
    w_hi, w_lo = _split_bf16(wr_ref[...])
    logits = (jnp.dot(h_hi, w_hi, preferred_element_type=F32)
              + jnp.dot(h_hi, w_lo, preferred_element_type=F32)
              + jnp.dot(h_lo, w_hi, preferred_element_type=F32)) + br_ref[...]
    lane = lax.broadcasted_iota(jnp.int32, logits.shape, 1).astype(F32)
    big = float(LANES)
    neg = -jnp.inf

    def first_max(vals):
        top = jnp.max(vals, axis=-1, keepdims=True)
        return top, jnp.min(jnp.where(vals == top, lane, big), axis=-1, keepdims=True)

    is_group = lane < N_GROUPS
    g_max, g_top = first_max(jnp.where(is_group, logits, neg))
    g_w = 1.0 / jnp.sum(jnp.where(is_group, jnp.exp(logits - g_max), 0.0), axis=-1, keepdims=True)
    first = N_GROUPS + EXPERTS_PER_GROUP * g_top
    cand = jnp.where((lane >= first) & (lane < first + EXPERTS_PER_GROUP), logits, neg)
    v1, i1 = first_max(cand)
    v2, i2 = first_max(jnp.where(lane == i1, neg, cand))
    t = jnp.exp(v2 - v1)
    w1 = g_w / (1.0 + t)
    w2 = g_w * t / (1.0 + t)
    ids = jnp.where(lane == 0.0, i1 - N_GROUPS, jnp.where(lane == 1.0, i2 - N_GROUPS, 0.0))
    ids_ref[...] = ids.astype(jnp.int32)
    wts_ref[...] = jnp.where(lane == 0.0, w1, jnp.where(lane == 1.0, w2, 0.0))


def _norm_and_route(x, g, wg, bg, we, be):
    m, d = x.shape
    tm = _tile(m, 256)
    pad = LANES - N_GROUPS - N_EXPERTS
    wr = jnp.concatenate([wg, we, jnp.zeros((d, pad), F32)], axis=1)
    br = jnp.concatenate([bg, be, jnp.zeros((pad,), F32)]).reshape(1, LANES)
    row = lambda c: pl.BlockSpec((tm, c), lambda i: (i, 0))
    return pl.pallas_call(
        _router_body,
        out_shape=[jax.ShapeDtypeStruct((m, d), F32),
                   jax.ShapeDtypeStruct((m, LANES), jnp.int32),
                   jax.ShapeDtypeStruct((m, LANES), F32)],
        grid=(m // tm,),
        in_specs=[row(d), pl.BlockSpec((1, d), lambda i: (0, 0)),
                  pl.BlockSpec((d, LANES), lambda i: (0, 0)), pl.BlockSpec((1, LANES), lambda i: (0, 0))],
        out_specs=[row(d), row(LANES), row(LANES)],
        compiler_params=_params("parallel"),
        name="norm_route",
    )(x, g.reshape(1, d), wr, br)


def _start_row_gather(idx_ref, first, stride, src_ref, buf, sem):
    last = src_ref.shape[0] - 1

    def one(r, _):
        t = jnp.minimum(idx_ref[first + r * stride], last)
        pltpu.make_async_copy(src_ref.at[pl.ds(t, 1)], buf.at[pl.ds(r, 1)], sem).start()
        return 0

    lax.fori_loop(0, buf.shape[0], one, 0, unroll=8)


def _wait_row_gather(src_ref, buf, sem):
    pltpu.make_async_copy(src_ref.at[pl.ds(0, buf.shape[0])], buf, sem).wait()


def _dispatch_body(idx_ref, used_ref, valid_ref, src_ref, o_ref, buf, sem):
    b = pl.program_id(0)
    rows = buf.shape[1]

    def start(blk):
        _start_row_gather(idx_ref, blk * rows, 1, src_ref, buf.at[blk % 2], sem.at[blk % 2])

    @pl.when(b == 0)
    def _():
        start(0)

    @pl.when(b + 1 < used_ref[0])
    def _():
        start(b + 1)

    @pl.when(b < used_ref[0])
    def _():
        slot = b % 2
        _wait_row_gather(src_ref, buf.at[slot], sem.at[slot])
        o_ref[...] = jnp.where(valid_ref[...] > 0.0, buf[slot], 0.0).astype(o_ref.dtype)


def _dispatch_rows(src, row_tok, row_valid, n_used):
    n_rows = row_tok.shape[0]
    d = src.shape[1]
    grid_spec = pltpu.PrefetchScalarGridSpec(
        num_scalar_prefetch=2,
        grid=(n_rows // MOE_ROWS,),
        in_specs=[pl.BlockSpec((MOE_ROWS, 1), lambda b, idx, used: (b, 0)),
                  pl.BlockSpec(memory_space=pl.ANY)],
        out_specs=pl.BlockSpec((MOE_ROWS, d), lambda b, idx, used: (b, 0)),
        scratch_shapes=[pltpu.VMEM((2, MOE_ROWS, d), src.dtype), pltpu.SemaphoreType.DMA((2,))],
    )
    return pl.pallas_call(
        _dispatch_body,
        out_shape=jax.ShapeDtypeStruct((n_rows, d), BF16),
        grid_spec=grid_spec,
        compiler_params=_params("arbitrary"),
        name="dispatch_rows",
    )(row_tok, n_used, row_valid.reshape(n_rows, 1), src)


def _expert_up_body(be_ref, first_ref, used_ref, x_ref, w1_ref, w3_ref, h_ref, w1b, w3b):
    b = pl.program_id(1)

    @pl.when(first_ref[b] == 1)
    def _():
        w1b[...] = w1_ref[...].astype(BF16)
        w3b[...] = w3_ref[...].astype(BF16)

    @pl.when(b < used_ref[0])
    def _():
        x = x_ref[...]
        a = jnp.dot(x, w1b[...], preferred_element_type=F32)
        g = jnp.dot(x, w3b[...], preferred_element_type=F32)
        h_ref[...] = (a * _sigmoid(a) * g).astype(h_ref.dtype)


def _expert_down_body(be_ref, first_ref, used_ref, h_ref, w2_ref, y_ref, w2b):
    b = pl.program_id(1)

    @pl.when(first_ref[b] == 1)
    def _():
        w2b[...] = w2_ref[...].astype(BF16)

    @pl.when(b < used_ref[0])
    def _():
        y_ref[...] = jnp.dot(h_ref[...], w2b[...], preferred_element_type=F32)


def _expert_ffn(xs, blk_exp, blk_first, n_used, w1, w3, w2):
    n_rows, d = xs.shape
    f = w1.shape[2]
    n_blocks = n_rows // MOE_ROWS
    tf = _tile(f, COL_TILE)
    tn = _tile(d, EXPERT_OUT_TILE)
    last = lambda b, used: jnp.minimum(b, used[0] - 1)
    up_spec = pltpu.PrefetchScalarGridSpec(
        num_scalar_prefetch=3,
        grid=(f // tf, n_blocks),
        in_specs=[pl.BlockSpec((MOE_ROWS, d), lambda j, b, be, fi, us: (last(b, us), 0)),
                  pl.BlockSpec((None, d, tf), lambda j, b, be, fi, us: (be[b], 0, j)),
                  pl.BlockSpec((None, d, tf), lambda j, b, be, fi, us: (be[b], 0, j))],
        out_specs=pl.BlockSpec((MOE_ROWS, tf), lambda j, b, be, fi, us: (b, j)),
        scratch_shapes=[pltpu.VMEM((d, tf), BF16), pltpu.VMEM((d, tf), BF16)],
    )
    hidden = pl.pallas_call(
        _expert_up_body,
        out_shape=jax.ShapeDtypeStruct((n_rows, f), BF16),
        grid_spec=up_spec,
        compiler_params=_params("arbitrary", "arbitrary"),
        name="expert_up",
    )(blk_exp, blk_first, n_used, xs, w1, w3)
    down_spec = pltpu.PrefetchScalarGridSpec(
        num_scalar_prefetch=3,
        grid=(d // tn, n_blocks),
        in_specs=[pl.BlockSpec((MOE_ROWS, f), lambda j, b, be, fi, us: (last(b, us), 0)),
                  pl.BlockSpec((None, f, tn), lambda j, b, be, fi, us: (be[b], 0, j))],
        out_specs=pl.BlockSpec((MOE_ROWS, tn), lambda j, b, be, fi, us: (b, j)),
        scratch_shapes=[pltpu.VMEM((f, tn), BF16)],
    )
    return pl.pallas_call(
        _expert_down_body,
        out_shape=jax.ShapeDtypeStruct((n_rows, d), F32),
        grid_spec=down_spec,
        compiler_params=_params("arbitrary", "arbitrary"),
        name="expert_down",
    )(blk_exp, blk_first, n_used, hidden, w2)


def _route_layout(ids, n_tok):
    n_assign = n_tok * TOP_K
    flat_e = ids.reshape(n_assign)
    flat_t = jnp.repeat(jnp.arange(n_tok, dtype=jnp.int32), TOP_K)
    onehot = (flat_e[:, None] == jnp.arange(N_EXPERTS, dtype=jnp.int32)[None, :]).astype(jnp.int32)
    running = jnp.cumsum(onehot, axis=0)
    rank = jnp.take_along_axis(running, flat_e[:, None], axis=1)[:, 0] - 1
    counts = running[-1]
    padded = (counts + MOE_ROWS - 1) // MOE_ROWS * MOE_ROWS
    pad_end = jnp.cumsum(padded)
    dest = (pad_end - padded)[flat_e] + rank
    n_blocks = -(-n_assign // MOE_ROWS) + N_EXPERTS
    n_rows = n_blocks * MOE_ROWS
    row_tok = jnp.full((n_rows,), n_tok, jnp.int32).at[dest].set(flat_t)
    blk = jnp.arange(n_blocks, dtype=jnp.int32)
    n_used = (pad_end[-1] // MOE_ROWS).astype(jnp.int32)
    blk_exp = jnp.searchsorted(pad_end, jnp.minimum(blk, n_used - 1) * MOE_ROWS, side='right').astype(jnp.int32)
    blk_exp = jnp.minimum(blk_exp, N_EXPERTS - 1)
    prev = jnp.concatenate([jnp.full((1,), -1, jnp.int32), blk_exp[:-1]])
    blk_first = ((blk_exp != prev) & (blk < n_used)).astype(jnp.int32)
    row_valid = (row_tok < n_tok).astype(F32)
    return row_tok, row_valid, dest.astype(jnp.int32), blk_exp, blk_first, n_used.reshape(1)


def _final_body(dest_ref, x_ref, w_ref, y_ref, g_ref, op_ref, os_ref, buf, sem, *, prompt_tiles):
    i = pl.program_id(0)
    tm = buf.shape[2]

    def start(blk):
        for k in range(TOP_K):
            _start_row_gather(dest_ref, blk * tm * TOP_K + k, TOP_K, y_ref, buf.at[blk % 2, k], sem.at[blk % 2, k])

    @pl.when(i == 0)
    def _():
        start(0)

    @pl.when(i + 1 < pl.num_programs(0))
    def _():
        start(i + 1)

    slot = i % 2
    for k in range(TOP_K):
        _wait_row_gather(y_ref, buf.at[slot, k], sem.at[slot, k])
    w = w_ref[...]
    moe = buf[slot, 0] * w[:, 0:1] + buf[slot, 1] * w[:, 1:2]
    y = _rmsnorm_rows(x_ref[...] + moe, g_ref[...])

    @pl.when(i < prompt_tiles)
    def _():
        op_ref[...] = y

    @pl.when(i >= prompt_tiles)
    def _():
        os_ref[...] = y


def _combine_and_norm(x, wts, y_rows, dest, g, n_prompt):
    m, d = x.shape
    tm = _tile(m - n_prompt, _tile(n_prompt, 256))
    assert n_prompt % tm == 0
    pt = n_prompt // tm
    grid_spec = pltpu.PrefetchScalarGridSpec(
        num_scalar_prefetch=1,
        grid=(m // tm,),
        in_specs=[pl.BlockSpec((tm, d), lambda i, dest: (i, 0)),
                  pl.BlockSpec((tm, LANES), lambda i, dest: (i, 0)),
                  pl.BlockSpec(memory_space=pl.ANY),
                  pl.BlockSpec((1, d), lambda i, dest: (0, 0))],
        out_specs=[pl.BlockSpec((tm, d), lambda i, dest: (jnp.minimum(i, pt - 1), 0)),
                   pl.BlockSpec((tm, d), lambda i, dest: (jnp.maximum(i - pt, 0), 0))],
        scratch_shapes=[pltpu.VMEM((2, TOP_K, tm, d), F32), pltpu.SemaphoreType.DMA((2, TOP_K))],
    )
    return pl.pallas_call(
        functools.partial(_final_body, prompt_tiles=pt),
        out_shape=[jax.ShapeDtypeStruct((n_prompt, d), F32), jax.ShapeDtypeStruct((m - n_prompt, d), F32)],
        grid_spec=grid_spec,
        compiler_params=_params("arbitrary"),
        name="combine_norm",
    )(dest, x, wts, y_rows, g.reshape(1, d))


def _layer(xp, xs, n_seq, n_new, past_len, cache_sb_k, cache_sb_v, state_ret, cache_mem_k, cache_mem_v,
           mem_prompt, norm_mix, w_in, ret_gn, w_out, norm_cross, norm_mem, w_cq, w_ck, w_cv, w_co, norm_ffn,
           wg, bg, we, be, w1, w3, w2):
    n_prompt, d = xp.shape
    n_tok = n_prompt + xs.shape[0]
    d_ret = d // 2
    d_sb = d - d_ret
    ret_heads = d_ret // RET_HEAD_DIM
    sds = jax.ShapeDtypeStruct

    pos = jnp.concatenate([jnp.arange(n_prompt, dtype=jnp.int32),
                           jnp.tile(past_len + jnp.arange(n_new, dtype=jnp.int32), n_seq)])
    half = RET_HEAD_DIM // 2
    inv = ROPE_BASE ** (-jnp.arange(half, dtype=F32) / half)
    ang = pos.astype(F32)[:, None] * inv[None, :]
    cos, sin = jnp.cos(ang), jnp.sin(ang)
    log_gamma = jnp.log1p(-jnp.exp2(-5.0 - jnp.arange(ret_heads, dtype=F32)))

    h = _rmsnorm_stacked(xp, xs, norm_mix, BF16)
    rot = [(cos, half), (sin, half)]
    bf = lambda n: [(sds((n_tok, n), BF16), None)]
    (rq,) = _matmul([h], w_in, 0, d_ret, _ep_rotary(1.0), rot, bf(d_ret), "in_proj_ret_q")
    (rk,) = _matmul([h], w_in, d_ret, d_ret, _ep_rotary(RET_HEAD_DIM ** -0.5), rot, bf(d_ret), "in_proj_ret_k")
    (rv,) = _matmul([h], w_in, 2 * d_ret, d_ret, _ep_store(1.0), [], bf(d_ret), "in_proj_ret_v")
    (rg,) = _matmul([h], w_in, 3 * d_ret, d_ret, _ep_store(1.0), [], [(sds((n_tok, d_ret), F32), None)],
                    "in_proj_ret_gate")
    (sq,) = _matmul([h], w_in, 4 * d_ret, d_sb, _ep_store(SB_HEAD_DIM ** -0.5 * LOG2_E), [], bf(d_sb),
                    "in_proj_sb_q")
    split = [(sds((n_prompt, d_sb), F32), PROMPT), (sds((n_tok - n_prompt, d_sb), F32), SAMPLE),
             (sds((n_tok, d_sb), BF16), None)]
    sk_p, sk_s, sk_b = _matmul([h], w_in, 4 * d_ret + d_sb, d_sb, _ep_store_split, [], split, "in_proj_sb_k",
                               n_prompt=n_prompt)
    sv_p, sv_s, sv_b = _matmul([h], w_in, 4 * d_ret + 2 * d_sb, d_sb, _ep_store_split, [], split,
                               "in_proj_sb_v", n_prompt=n_prompt)

    rows_p = _tile(n_prompt, RET_BLOCK)
    zeros_state = jnp.zeros((1, ret_heads, RET_HEAD_DIM, RET_HEAD_DIM), F32)
    yr, state_p = _retention(log_gamma, rq, rk, rv, rg, ret_gn, zeros_state, 0, 1, n_prompt // rows_p, rows_p)
    yr, state_s = _retention(log_gamma, rq, rk, rv, rg, ret_gn, state_ret, n_prompt, n_seq, 1, n_new,
                             out_buf=yr)
    sb = _sb_prompt(sq, sk_b, sv_b, n_prompt)
    sb = _sb_sample(sq, sk_b, sv_b, cache_sb_k.reshape(n_seq, past_len, d_sb),
                    cache_sb_v.reshape(n_seq, past_len, d_sb), n_prompt, n_new, sb)
    res = [(sds((n_tok, d), F32), None)]
    (x,) = _matmul([yr, sb], w_out, 0, d, _ep_residual_split, [(xp, PROMPT), (xs, SAMPLE)], res, "out_proj",
                   n_prompt=n_prompt)

    n_mem = mem_prompt.shape[0]
    m = _rmsnorm(mem_prompt, norm_mem, BF16)
    mem_out = [(sds((n_mem, d), F32), None)]
    (mk,) = _matmul([m], w_ck, 0, d, _ep_store(1.0), [], mem_out, "mem_k")
    (mv,) = _matmul([m], w_cv, 0, d, _ep_store(1.0), [], mem_out, "mem_v")
    h = _rmsnorm(x, norm_cross, BF16)
    (cq,) = _matmul([h], w_cq, 0, d, _ep_store(1.0), [], bf(d), "cross_q")
    co = _cross_attend(cq, mk.reshape(1, n_mem, d), mv.reshape(1, n_mem, d), 0, n_prompt)
    co = _cross_attend(cq, cache_mem_k.reshape(n_seq, n_mem, d), cache_mem_v.reshape(n_seq, n_mem, d),
                       n_prompt, n_new, out_buf=co)
    (x,) = _matmul([co], w_co, 0, d, _ep_residual, [(x, None)], res, "cross_out")

    hn, ids, wts = _norm_and_route(x, norm_ffn, wg, bg, we, be)
    row_tok, row_valid, dest, blk_exp, blk_first, n_used = _route_layout(ids[:, :TOP_K], n_tok)
    x_rows = _dispatch_rows(hn, row_tok, row_valid, n_used)
    y_rows = _expert_ffn(x_rows, blk_exp, blk_first, n_used, w1, w3, w2)
    return x, wts, y_rows, dest, (sk_p, sv_p, sk_s, sv_s, state_p, state_s, mk, mv)


def kernel(x_prompt, x_sample, cache_sb_k, cache_sb_v, state_ret, cache_mem_k, cache_mem_v, mem_prompt, norm_mix, w_in, ret_gn, w_out, norm_cross, norm_mem, w_cq, w_ck, w_cv, w_co, norm_ffn, router_group_w, router_group_b, router_expert_w, router_expert_b, expert_w1, expert_w3, expert_w2, norm_final):
    batch, n_prompt, d = x_prompt.shape
    n_seq, n_new, _ = x_sample.shape
    depth, _, past_len, sb_heads, _ = cache_sb_k.shape
    n_mem = mem_prompt.shape[1]
    assert batch == 1 and depth == 1
    ret_heads = (d // 2) // RET_HEAD_DIM
    x, wts, y_rows, dest, (sk_p, sv_p, sk_s, sv_s, state_p, state_s, mk, mv) = _layer(
        x_prompt.reshape(n_prompt, d), x_sample.reshape(n_seq * n_new, d), n_seq, n_new, past_len,
        cache_sb_k, cache_sb_v, state_ret[0], cache_mem_k, cache_mem_v, mem_prompt[0], norm_mix[0], w_in,
        ret_gn[0], w_out, norm_cross[0], norm_mem[0], w_cq, w_ck, w_cv, w_co, norm_ffn[0], router_group_w[0],
        router_group_b[0], router_expert_w[0], router_expert_b[0], expert_w1[0], expert_w3[0], expert_w2[0])
    y_prompt, y_sample = _combine_and_norm(x, wts, y_rows, dest, norm_final, n_prompt)
    y_prompt = y_prompt.reshape(1, n_prompt, d)
    y_sample = y_sample.reshape(n_seq, n_new, d)
    sb_shape_p = (1, 1, n_prompt, sb_heads, SB_HEAD_DIM)
    sb_shape_s = (1, n_seq, n_new, sb_heads, SB_HEAD_DIM)
    mem_shape = (1, 1, n_mem, MEM_HEADS, d // MEM_HEADS)
    return (y_prompt, y_sample,
            sk_p.reshape(sb_shape_p), sv_p.reshape(sb_shape_p),
            state_p.reshape(1, 1, ret_heads, RET_HEAD_DIM, RET_HEAD_DIM),
            mk.reshape(mem_shape), mv.reshape(mem_shape),
            sk_s.reshape(sb_shape_s), sv_s.reshape(sb_shape_s),
            state_s.reshape(1, n_seq, ret_heads, RET_HEAD_DIM, RET_HEAD_DIM))
```

```python
import functools

import jax
import jax.numpy as jnp
from jax import lax
from jax.experimental import pallas as pl
from jax.experimental.pallas import tpu as pltpu

BF16 = jnp.bfloat16
F32 = jnp.float32

EPS = 1e-6
CHUNK = 64
RET_HEAD_DIM = 256
SB_HEAD_DIM = 128
MEM_HEADS = 4
N_GROUPS = 4
EXPERTS_PER_GROUP = 8
N_EXPERTS = N_GROUPS * EXPERTS_PER_GROUP
TOP_K = 2
ROPE_BASE = 10000.0

LANES = 128
VMEM_LIMIT_BYTES = 56 * 1024 * 1024
ROW_TILE = 512
COL_TILE = 512
RET_BLOCK = 256
SB_BLOCK = 256
SB_HEADS_PER_STEP = 2
LOG2_E = 1.4426950408889634
SB_DEAD_LOG2 = -64.0
MOE_ROWS = 512
EXPERT_OUT_TILE = 2048

PROMPT = "prompt"
SAMPLE = "sample"

_NT = (((1,), (1,)), ((), ()))
_TN = (((0,), (0,)), ((), ()))


def _tile(n, pref):
    t = min(pref, n)
    while n % t:
        t //= 2
    return t


def _params(*sem):
    return pltpu.CompilerParams(dimension_semantics=sem, vmem_limit_bytes=VMEM_LIMIT_BYTES)


def _sigmoid(x):
    return 1.0 / (1.0 + jnp.exp(-x))


def _rmsnorm_rows(x, g):
    ms = jnp.mean(x * x, axis=-1, keepdims=True)
    return x * lax.rsqrt(ms + EPS) * g


def _rmsnorm_body(x_ref, g_ref, o_ref):
    o_ref[...] = _rmsnorm_rows(x_ref[...], g_ref[...]).astype(o_ref.dtype)


def _rmsnorm(x, g, out_dtype):
    m, d = x.shape
    tm = _tile(m, 256)
    return pl.pallas_call(
        _rmsnorm_body,
        out_shape=jax.ShapeDtypeStruct((m, d), out_dtype),
        grid=(m // tm,),
        in_specs=[pl.BlockSpec((tm, d), lambda i: (i, 0)), pl.BlockSpec((1, d), lambda i: (0, 0))],
        out_specs=pl.BlockSpec((tm, d), lambda i: (i, 0)),
        compiler_params=_params("parallel"),
        name="rmsnorm",
    )(x, g.reshape(1, d))


def _rmsnorm_stacked_body(xp_ref, xs_ref, g_ref, o_ref, *, prompt_tiles):
    is_prompt = pl.program_id(0) < prompt_tiles

    @pl.when(is_prompt)
    def _():
        o_ref[...] = _rmsnorm_rows(xp_ref[...], g_ref[...]).astype(o_ref.dtype)

    @pl.when(jnp.logical_not(is_prompt))
    def _():
        o_ref[...] = _rmsnorm_rows(xs_ref[...], g_ref[...]).astype(o_ref.dtype)


def _rmsnorm_stacked(xp, xs, g, out_dtype):
    (n_p, d), n_s = xp.shape, xs.shape[0]
    tm = _tile(n_s, _tile(n_p, 256))
    pt = n_p // tm
    return pl.pallas_call(
        functools.partial(_rmsnorm_stacked_body, prompt_tiles=pt),
        out_shape=jax.ShapeDtypeStruct((n_p + n_s, d), out_dtype),
        grid=((n_p + n_s) // tm,),
        in_specs=[pl.BlockSpec((tm, d), lambda i: (jnp.minimum(i, pt - 1), 0)),
                  pl.BlockSpec((tm, d), lambda i: (jnp.maximum(i - pt, 0), 0)),
                  pl.BlockSpec((1, d), lambda i: (0, 0))],
        out_specs=pl.BlockSpec((tm, d), lambda i: (i, 0)),
        compiler_params=_params("arbitrary"),
        name="rmsnorm_stacked",
    )(xp, xs, g.reshape(1, d))


def _matmul_body(*refs, n_lhs, n_extra, n_out, epilogue):
    x_refs = refs[:n_lhs]
    w_refs = refs[n_lhs:2 * n_lhs]
    extra = refs[2 * n_lhs:2 * n_lhs + n_extra]
    outs = refs[2 * n_lhs + n_extra:2 * n_lhs + n_extra + n_out]
    wb = refs[2 * n_lhs + n_extra + n_out:]

    @pl.when(pl.program_id(1) == 0)
    def _():
        for p in range(n_lhs):
            wb[p][...] = w_refs[p][...].astype(BF16)

    acc = jnp.dot(x_refs[0][...], wb[0][...], preferred_element_type=F32)
    for p in range(1, n_lhs):
        acc += jnp.dot(x_refs[p][...], wb[p][...], preferred_element_type=F32)
    epilogue(acc, extra, outs)


def _matmul(xs, w, col_off, n_cols, epilogue, extras, outs, name, n_prompt=None):
    m = xs[0].shape[0]
    tm = _tile(m, ROW_TILE) if n_prompt is None else _tile(m - n_prompt, _tile(n_prompt, ROW_TILE))
    tn = _tile(n_cols, COL_TILE)
    assert col_off % tn == 0
    jo = col_off // tn
    pt = None if n_prompt is None else n_prompt // tm
    in_specs, args = [], []
    for x in xs:
        in_specs.append(pl.BlockSpec((tm, x.shape[1]), lambda j, i: (i, 0)))
        args.append(x)
    row = 0
    for x in xs:
        kp = x.shape[1]
        assert row % kp == 0
        in_specs.append(pl.BlockSpec((None, kp, tn), lambda j, i, r=row // kp: (0, r, j + jo)))
        args.append(w)
        row += kp
    assert row == w.shape[1]

    def spec(kind):
        if kind is None:
            return pl.BlockSpec((tm, tn), lambda j, i: (i, j))
        if kind == PROMPT:
            return pl.BlockSpec((tm, tn), lambda j, i: (jnp.minimum(i, pt - 1), j))
        if kind == SAMPLE:
            return pl.BlockSpec((tm, tn), lambda j, i: (jnp.maximum(i - pt, 0), j))
        return pl.BlockSpec((tm, kind), lambda j, i: (i, 0))

    for a, kind in extras:
        in_specs.append(spec(kind))
        args.append(a)
    body = functools.partial(_matmul_body, n_lhs=len(xs), n_extra=len(extras), n_out=len(outs),
                             epilogue=epilogue if n_prompt is None else functools.partial(epilogue, pt))
    res = pl.pallas_call(
        body,
        out_shape=[o for o, _ in outs],
        grid=(n_cols // tn, m // tm),
        in_specs=in_specs,
        out_specs=[spec(kind) for _, kind in outs],
        scratch_shapes=[pltpu.VMEM((x.shape[1], tn), BF16) for x in xs],
        compiler_params=_params("arbitrary", "arbitrary"),
        name=name,
    )(*args)
    return res


def _ep_store(scale):
    def ep(acc, extra, outs):
        for o in outs:
            o[...] = (acc * scale).astype(o.dtype) if scale != 1.0 else acc.astype(o.dtype)
    return ep


def _ep_store_split(prompt_tiles, acc, extra, outs):
    is_prompt = pl.program_id(1) < prompt_tiles

    @pl.when(is_prompt)
    def _():
        outs[0][...] = acc

    @pl.when(jnp.logical_not(is_prompt))
    def _():
        outs[1][...] = acc

    outs[2][...] = acc.astype(outs[2].dtype)


def _ep_residual(acc, extra, outs):
    outs[0][...] = extra[0][...] + acc


def _ep_residual_split(prompt_tiles, acc, extra, outs):
    res = jnp.where(pl.program_id(1) < prompt_tiles, extra[0][...], extra[1][...])
    outs[0][...] = res + acc


def _ep_rotary(scale):
    half = RET_HEAD_DIM // 2

    def ep(acc, extra, outs):
        cos = extra[0][...]
        sin = extra[1][...]
        for h in range(acc.shape[1] // RET_HEAD_DIM):
            lo = h * RET_HEAD_DIM
            x1 = acc[:, lo:lo + half]
            x2 = acc[:, lo + half:lo + RET_HEAD_DIM]
            outs[0][:, lo:lo + half] = ((x1 * cos - x2 * sin) * scale).astype(outs[0].dtype)
            outs[0][:, lo + half:lo + RET_HEAD_DIM] = ((x1 * sin + x2 * cos) * scale).astype(outs[0].dtype)
    return ep


def _retention_body(lg_ref, q_ref, k_ref, v_ref, g_ref, gn_ref, s0_ref, *rest, rows):
    o_ref, sout_ref, s_scr = rest[-3:]
    h = pl.program_id(1)

    @pl.when(pl.program_id(2) == 0)
    def _():
        s_scr[...] = s0_ref[...]

    lg = lg_ref[h]
    q = q_ref[...]
    k = k_ref[...]
    v = v_ref[...]
    ii = lax.broadcasted_iota(jnp.int32, (rows, rows), 0)
    jj = lax.broadcasted_iota(jnp.int32, (rows, rows), 1)
    shift = CHUNK.bit_length() - 1
    dist = jnp.abs(ii - jj).astype(F32)
    decay = jnp.where((jj >> shift) <= (ii >> shift), jnp.exp(lg * dist), 0.0)
    scores = lax.dot_general(q, k, _NT, preferred_element_type=F32) * decay
    out = jnp.dot(scores.astype(BF16), v, preferred_element_type=F32)
    s = s_scr[...]
    idx = lax.broadcasted_iota(jnp.int32, (rows, 1), 0).astype(F32)
    out += jnp.exp(lg * (idx + 1.0)) * jnp.dot(q, s.astype(BF16), preferred_element_type=F32)
    kd = (k.astype(F32) * jnp.exp(lg * (rows - 1.0 - idx))).astype(BF16)
    s_new = jnp.exp(lg * rows) * s + lax.dot_general(kd, v, _TN, preferred_element_type=F32)
    s_scr[...] = s_new
    sout_ref[...] = s_new
    mu = jnp.mean(out, axis=-1, keepdims=True)
    cen = out - mu
    var = jnp.mean(cen * cen, axis=-1, keepdims=True)
    yr = cen * lax.rsqrt(var + EPS) * gn_ref[...]
    g = g_ref[...]
    o_ref[...] = (g * _sigmoid(g) * yr).astype(o_ref.dtype)


def _into(out_buf):
    if out_buf is None:
        return [], []
    return [out_buf], [pl.BlockSpec(memory_space=pl.ANY)]


def _retention(log_gamma, q, k, v, gate, gn, s0, row_off, n_batch, n_steps, rows, out_buf=None):
    heads = q.shape[1] // RET_HEAD_DIM
    assert row_off % rows == 0 and rows % CHUNK == 0
    ro = row_off // rows
    tok = lambda b, h, c, lg: (ro + b * n_steps + c, h)
    blk = pl.BlockSpec((rows, RET_HEAD_DIM), tok)
    st = pl.BlockSpec((None, None, RET_HEAD_DIM, RET_HEAD_DIM), lambda b, h, c, lg: (b, h, 0, 0))
    args = [log_gamma, q, k, v, gate, gn.reshape(heads, 1, RET_HEAD_DIM), s0]
    extra_args, extra_specs = _into(out_buf)
    grid_spec = pltpu.PrefetchScalarGridSpec(
        num_scalar_prefetch=1,
        grid=(n_batch, heads, n_steps),
        in_specs=[blk, blk, blk, blk,
                  pl.BlockSpec((None, 1, RET_HEAD_DIM), lambda b, h, c, lg: (h, 0, 0)), st] + extra_specs,
        out_specs=[blk, st],
        scratch_shapes=[pltpu.VMEM((RET_HEAD_DIM, RET_HEAD_DIM), F32)],
    )
    return pl.pallas_call(
        functools.partial(_retention_body, rows=rows),
        out_shape=[jax.ShapeDtypeStruct((q.shape[0], heads * RET_HEAD_DIM), BF16),
                   jax.ShapeDtypeStruct((n_batch, heads, RET_HEAD_DIM, RET_HEAD_DIM), F32)],
        grid_spec=grid_spec,
        input_output_aliases={len(args): 0} if extra_args else {},
        compiler_params=_params("arbitrary", "arbitrary", "arbitrary"),
        name="retention",
    )(*args, *extra_args)


def _neg_suffix(n):
    j = lax.broadcasted_iota(jnp.int32, (2 * n, n), 0) & (n - 1)
    s = lax.broadcasted_iota(jnp.int32, (2 * n, n), 1)
    return jnp.where(j >= s, -1.0, 0.0).astype(BF16)


def _lanes_to(x, n):
    return x[:, :n] if n <= LANES else jnp.concatenate([x] * (n // LANES), axis=1)


def _suffix_sums(fail, neg_ones):
    tk = fail.shape[1]
    hi, lo = _split_bf16(fail)
    if tk % LANES == 0:
        return jnp.dot(jnp.concatenate([hi, lo], axis=1), neg_ones, preferred_element_type=F32)
    return (jnp.dot(hi, neg_ones[:tk], preferred_element_type=F32)
            + jnp.dot(lo, neg_ones[:tk], preferred_element_type=F32))


def _sb_tiles(qs, kbs, vbs, neg_ones, carry, acc, masked):
    heads = range(len(qs))
    tk = kbs[0].shape[0]
    zs = [lax.dot_general(qs[h], kbs[h], _NT, preferred_element_type=F32) for h in heads]
    fails = [jnp.maximum(z, 0.0) + jnp.log2(1.0 + jnp.exp2(-jnp.abs(z))) for z in zs]
    if masked:
        t = lax.broadcasted_iota(jnp.int32, zs[0].shape, 0)
        s = lax.broadcasted_iota(jnp.int32, zs[0].shape, 1)
        valid = s < t
        fails = [jnp.where(valid, f, 0.0) for f in fails]
    suffixes = [_suffix_sums(f, neg_ones) for f in fails]
    probs = [jnp.exp2(zs[h] + suffixes[h] + _lanes_to(carry[h], tk)) for h in heads]
    if masked:
        probs = [jnp.where(valid, p, 0.0) for p in probs]
    for h in heads:
        acc[h] += jnp.dot(probs[h].astype(BF16), vbs[h], preferred_element_type=F32)
        carry[h] += jnp.broadcast_to(suffixes[h][:, :1], carry.shape[1:])


def _sb_body(q_ref, kd_ref, vd_ref, kp_ref, vp_ref, *rest, tk, n_heads, past_per_step, past_static):
    o_ref, carry, acc = rest[-3:]
    tq = q_ref.shape[0]
    n_past = past_static if past_static is not None else pl.program_id(1) * past_per_step
    cols = lambda h: slice(h * SB_HEAD_DIM, (h + 1) * SB_HEAD_DIM)
    heads = range(n_heads)
    carry[...] = jnp.zeros(carry.shape, F32)
    acc[...] = jnp.zeros(acc.shape, F32)
    queries = lambda: [q_ref[:, cols(h)] for h in heads]
    _sb_tiles(queries(), [kd_ref[:, cols(h)].astype(BF16) for h in heads],
              [vd_ref[:, cols(h)].astype(BF16) for h in heads], _neg_suffix(tq), carry, acc, True)
    neg_ones = _neg_suffix(tk)

    alive = lambda: jnp.max(carry[...]) > SB_DEAD_LOG2

    def step(state):
        it = state[0]
        rows = pl.ds(pl.multiple_of((n_past - 1 - it) * tk, tk), tk)
        _sb_tiles(queries(), [kp_ref[rows, cols(h)].astype(BF16) for h in heads],
                  [vp_ref[rows, cols(h)].astype(BF16) for h in heads], neg_ones, carry, acc, False)
        return it + 1, alive()

    lax.while_loop(lambda state: (state[0] < n_past) & state[1], step, (jnp.int32(0), alive()))
    for h in range(n_heads):
        o_ref[:, cols(h)] = acc[h].astype(o_ref.dtype)


def _sb_scratch(n_heads, tq):
    return [pltpu.VMEM((n_heads, tq, LANES), F32), pltpu.VMEM((n_heads, tq, SB_HEAD_DIM), F32)]


def _sb_prompt(q, k, v, n_tok):
    heads = q.shape[1] // SB_HEAD_DIM
    tq = _tile(n_tok, SB_BLOCK)
    nh = SB_HEADS_PER_STEP
    width = nh * SB_HEAD_DIM
    blk = pl.BlockSpec((tq, width), lambda h, i: (i, h))
    full = pl.BlockSpec((n_tok, width), lambda h, i: (0, h))
    return pl.pallas_call(
        functools.partial(_sb_body, tk=tq, n_heads=nh, past_per_step=1, past_static=None),
        out_shape=jax.ShapeDtypeStruct((q.shape[0], heads * SB_HEAD_DIM), BF16),
        grid=(heads // nh, n_tok // tq),
        in_specs=[blk, blk, blk, full, full],
        out_specs=blk,
        scratch_shapes=_sb_scratch(nh, tq),
        compiler_params=_params("parallel", "arbitrary"),
        name="stick_breaking_prompt",
    )(q, k, v, k, v)


def _sb_sample(q, k, v, cache_k, cache_v, row_off, n_new, out_buf):
    heads = q.shape[1] // SB_HEAD_DIM
    n_batch, past, _ = cache_k.shape
    tk = _tile(past, SB_BLOCK)
    assert row_off % n_new == 0
    ro = row_off // n_new
    nh = SB_HEADS_PER_STEP
    width = nh * SB_HEAD_DIM
    blk = pl.BlockSpec((n_new, width), lambda b, h: (ro + b, h))
    full = pl.BlockSpec((None, past, width), lambda b, h: (b, 0, h))
    extra_args, extra_specs = _into(out_buf)
    return pl.pallas_call(
        functools.partial(_sb_body, tk=tk, n_heads=nh, past_per_step=0, past_static=past // tk),
        out_shape=jax.ShapeDtypeStruct(out_buf.shape, out_buf.dtype),
        grid=(n_batch, heads // nh),
        in_specs=[blk, blk, blk, full, full] + extra_specs,
        out_specs=blk,
        scratch_shapes=_sb_scratch(nh, n_new),
        input_output_aliases={5: 0},
        compiler_params=_params("parallel", "arbitrary"),
        name="stick_breaking_sample",
    )(q, k, v, cache_k, cache_v, *extra_args)


def _cross_body(q_ref, k_ref, v_ref, *rest, scale):
    o_ref = rest[-1]
    s = lax.dot_general(q_ref[...], k_ref[...].astype(BF16), _NT, preferred_element_type=F32) * scale
    e = jnp.exp(s - jnp.max(s, axis=-1, keepdims=True))
    p = e / jnp.sum(e, axis=-1, keepdims=True)
    o_ref[...] = jnp.dot(p.astype(BF16), v_ref[...].astype(BF16),
                         preferred_element_type=F32).astype(o_ref.dtype)


def _cross_attend(q, mem_k, mem_v, row_off, rows_per_seq, out_buf=None):
    n_seq, n_mem, d = mem_k.shape
    dh = d // MEM_HEADS
    tm = _tile(rows_per_seq, ROW_TILE)
    steps = rows_per_seq // tm
    assert row_off % tm == 0
    ro = row_off // tm
    mem = pl.BlockSpec((None, n_mem, dh), lambda i, h: (i // steps, 0, h))
    blk = pl.BlockSpec((tm, dh), lambda i, h: (ro + i, h))
    extra_args, extra_specs = _into(out_buf)
    return pl.pallas_call(
        functools.partial(_cross_body, scale=dh ** -0.5),
        out_shape=jax.ShapeDtypeStruct((q.shape[0], d), BF16),
        grid=(n_seq * steps, MEM_HEADS),
        in_specs=[blk, mem, mem] + extra_specs,
        out_specs=blk,
        input_output_aliases={3: 0} if extra_args else {},
        compiler_params=_params("parallel", "arbitrary"),
        name="cross_attention",
    )(q, mem_k, mem_v, *extra_args)


def _split_bf16(x):
    hi = x.astype(BF16)
    return hi, (x - hi.astype(F32)).astype(BF16)


def _router_body(x_ref, g_ref, wr_ref, br_ref, h_ref, ids_ref, wts_ref):
    h = _rmsnorm_rows(x_ref[...], g_ref[...])
    h_ref[...] = _pack_bf16_pairs(h)
    h_hi, h_lo = _split_bf16(h)
    w_hi, w_lo = _split_bf16(wr_ref[...])
    logits = (jnp.dot(h_hi, w_hi, preferred_element_type=F32)
              + jnp.dot(h_hi, w_lo, preferred_element_type=F32)
              + jnp.dot(h_lo, w_hi, preferred_element_type=F32)) + br_ref[...]
    lane = lax.broadcasted_iota(jnp.int32, logits.shape, 1).astype(F32)
    big = float(LANES)
    neg = -jnp.inf

    def first_max(vals):
        top = jnp.max(vals, axis=-1, keepdims=True)
        return top, jnp.min(jnp.where(vals == top, lane, big), axis=-1, keepdims=True)

    is_group = lane < N_GROUPS
    g_max, g_top = first_max(jnp.where(is_group, logits, neg))
    g_w = 1.0 / jnp.sum(jnp.where(is_group, jnp.exp(logits - g_max), 0.0), axis=-1, keepdims=True)
    first = N_GROUPS + EXPERTS_PER_GROUP * g_top
    cand = jnp.where((lane >= first) & (lane < first + EXPERTS_PER_GROUP), logits, neg)
    v1, i1 = first_max(cand)
    v2, i2 = first_max(jnp.where(lane == i1, neg, cand))
    t = jnp.exp(v2 - v1)
    w1 = g_w / (1.0 + t)
    w2 = g_w * t / (1.0 + t)
    ids = jnp.where(lane == 0.0, i1 - N_GROUPS, jnp.where(lane == 1.0, i2 - N_GROUPS, 0.0))
    ids_ref[...] = ids.astype(jnp.int32)
    wts_ref[...] = jnp.where(lane == 0.0, w1, jnp.where(lane == 1.0, w2, 0.0))


def _norm_and_route(x, g, wg, bg, we, be):
    m, d = x.shape
    tm = _tile(m, 256)
    pad = LANES - N_GROUPS - N_EXPERTS
    wr = jnp.concatenate([wg, we, jnp.zeros((d, pad), F32)], axis=1)
    br = jnp.concatenate([bg, be, jnp.zeros((pad,), F32)]).reshape(1, LANES)
    row = lambda c: pl.BlockSpec((tm, c), lambda i: (i, 0))
    return pl.pallas_call(
        _router_body,
        out_shape=[jax.ShapeDtypeStruct((m, d // 2), jnp.int32),
                   jax.ShapeDtypeStruct((m, LANES), jnp.int32),
                   jax.ShapeDtypeStruct((m, LANES), F32)],
        grid=(m // tm,),
        in_specs=[row(d), pl.BlockSpec((1, d), lambda i: (0, 0)),
                  pl.BlockSpec((d, LANES), lambda i: (0, 0)), pl.BlockSpec((1, LANES), lambda i: (0, 0))],
        out_specs=[row(d // 2), row(LANES), row(LANES)],
        compiler_params=_params("parallel"),
        name="norm_route",
    )(x, g.reshape(1, d), wr, br)


GATHER_UNROLL = 8


def _start_row_gather(idx_ref, first, stride, src_ref, buf, sem, n_rows):
    def one(r, _):
        t = idx_ref[first + r * stride]
        pltpu.make_async_copy(src_ref.at[pl.ds(t, 1)], buf.at[pl.ds(r, 1)], sem).start()
        return 0

    def group(g, _):
        for u in range(GATHER_UNROLL):
            one(g * GATHER_UNROLL + u, 0)
        return 0

    groups = n_rows // GATHER_UNROLL
    lax.fori_loop(0, groups, group, 0)
    lax.fori_loop(groups * GATHER_UNROLL, n_rows, one, 0)


def _wait_row_gather(src_ref, buf, sem, n_rows):
    pltpu.make_async_copy(src_ref.at[pl.ds(0, n_rows)], buf.at[pl.ds(0, n_rows)], sem).wait()


def _pack_bf16_pairs(x):
    half = x.shape[1] // 2
    hi = lax.bitcast_convert_type(x[:, :half].astype(BF16).astype(F32), jnp.int32)
    lo = lax.bitcast_convert_type(x[:, half:].astype(BF16).astype(F32), jnp.int32)
    return hi | lax.shift_right_logical(lo, jnp.int32(16))


def _unpack_bf16_pairs(p):
    hi = lax.bitcast_convert_type(p & jnp.int32(-65536), F32).astype(BF16)
    lo = lax.bitcast_convert_type(lax.shift_left(p, jnp.int32(16)), F32).astype(BF16)
    return hi, lo


def _dispatch_body(idx_ref, used_ref, cnt_ref, src_ref, o_ref, buf, sem):
    b = pl.program_id(0)
    rows = buf.shape[1]

    copied = lambda blk: pl.multiple_of(pl.cdiv(cnt_ref[blk], GATHER_UNROLL) * GATHER_UNROLL, GATHER_UNROLL)

    def start(blk):
        _start_row_gather(idx_ref, blk * rows, 1, src_ref, buf.at[blk % 2], sem.at[blk % 2], copied(blk))

    @pl.when(b == 0)
    def _():
        buf[...] = jnp.zeros(buf.shape, buf.dtype)
        start(0)

    @pl.when(b + 1 < used_ref[0])
    def _():
        start(b + 1)

    @pl.when(b < used_ref[0])
    def _():
        slot = b % 2
        cnt = cnt_ref[b]
        _wait_row_gather(src_ref, buf.at[slot], sem.at[slot], copied(b))
        half = buf.shape[2]
        row = lax.broadcasted_iota(jnp.int32, (rows, 1), 0)
        hi, lo = _unpack_bf16_pairs(jnp.where(row < cnt, buf[slot], 0))
        o_ref[:, :half] = hi
        o_ref[:, half:] = lo


def _dispatch_rows(src, row_tok, blk_cnt, n_used):
    n_rows = row_tok.shape[0]
    half = src.shape[1]
    grid_spec = pltpu.PrefetchScalarGridSpec(
        num_scalar_prefetch=3,
        grid=(n_rows // MOE_ROWS,),
        in_specs=[pl.BlockSpec(memory_space=pl.ANY)],
        out_specs=pl.BlockSpec((MOE_ROWS, 2 * half), lambda b, idx, used, cnt: (b, 0)),
        scratch_shapes=[pltpu.VMEM((2, MOE_ROWS, half), src.dtype), pltpu.SemaphoreType.DMA((2,))],
    )
    return pl.pallas_call(
        _dispatch_body,
        out_shape=jax.ShapeDtypeStruct((n_rows, 2 * half), BF16),
        grid_spec=grid_spec,
        compiler_params=_params("arbitrary"),
        name="dispatch_rows",
    )(row_tok, n_used, blk_cnt, src)


def _expert_up_body(be_ref, first_ref, used_ref, x_ref, w1_ref, w3_ref, h_ref, w1b, w3b):
    b = pl.program_id(1)

    @pl.when(first_ref[b] == 1)
    def _():
        w1b[...] = w1_ref[...].astype(BF16)
        w3b[...] = w3_ref[...].astype(BF16)

    @pl.when(b < used_ref[0])
    def _():
        x = x_ref[...]
        a = jnp.dot(x, w1b[...], preferred_element_type=F32)
        g = jnp.dot(x, w3b[...], preferred_element_type=F32)
        h_ref[...] = (a * _sigmoid(a) * g).astype(h_ref.dtype)


def _expert_down_body(be_ref, first_ref, used_ref, h_ref, w2_ref, y_ref, w2b):
    b = pl.program_id(1)

    @pl.when(first_ref[b] == 1)
    def _():
        w2b[...] = w2_ref[...].astype(BF16)

    @pl.when(b < used_ref[0])
    def _():
        y_ref[...] = jnp.dot(h_ref[...], w2b[...], preferred_element_type=F32)


def _expert_ffn(xs, blk_exp, blk_first, n_used, w1, w3, w2):
    n_rows, d = xs.shape
    f = w1.shape[2]
    n_blocks = n_rows // MOE_ROWS
    tf = _tile(f, COL_TILE)
    tn = _tile(d, EXPERT_OUT_TILE)
    last = lambda b, used: jnp.minimum(b, used[0] - 1)
    up_spec = pltpu.PrefetchScalarGridSpec(
        num_scalar_prefetch=3,
        grid=(f // tf, n_blocks),
        in_specs=[pl.BlockSpec((MOE_ROWS, d), lambda j, b, be, fi, us: (last(b, us), 0)),
                  pl.BlockSpec((None, d, tf), lambda j, b, be, fi, us: (be[b], 0, j)),
                  pl.BlockSpec((None, d, tf), lambda j, b, be, fi, us: (be[b], 0, j))],
        out_specs=pl.BlockSpec((MOE_ROWS, tf), lambda j, b, be, fi, us: (b, j)),
        scratch_shapes=[pltpu.VMEM((d, tf), BF16), pltpu.VMEM((d, tf), BF16)],
    )
    hidden = pl.pallas_call(
        _expert_up_body,
        out_shape=jax.ShapeDtypeStruct((n_rows, f), BF16),
        grid_spec=up_spec,
        compiler_params=_params("arbitrary", "arbitrary"),
        name="expert_up",
    )(blk_exp, blk_first, n_used, xs, w1, w3)
    down_spec = pltpu.PrefetchScalarGridSpec(
        num_scalar_prefetch=3,
        grid=(d // tn, n_blocks),
        in_specs=[pl.BlockSpec((MOE_ROWS, f), lambda j, b, be, fi, us: (last(b, us), 0)),
                  pl.BlockSpec((None, f, tn), lambda j, b, be, fi, us: (be[b], 0, j))],
        out_specs=pl.BlockSpec((MOE_ROWS, tn), lambda j, b, be, fi, us: (b, j)),
        scratch_shapes=[pltpu.VMEM((f, tn), BF16)],
    )
    return pl.pallas_call(
        _expert_down_body,
        out_shape=jax.ShapeDtypeStruct((n_rows, d), F32),
        grid_spec=down_spec,
        compiler_params=_params("arbitrary", "arbitrary"),
        name="expert_down",
    )(blk_exp, blk_first, n_used, hidden, w2)


def _route_layout(ids, n_tok):
    n_assign = n_tok * TOP_K
    flat_e = ids.reshape(n_assign)
    flat_t = jnp.repeat(jnp.arange(n_tok, dtype=jnp.int32), TOP_K)
    onehot = (flat_e[:, None] == jnp.arange(N_EXPERTS, dtype=jnp.int32)[None, :]).astype(jnp.int32)
    running = jnp.cumsum(onehot, axis=0)
    rank = jnp.take_along_axis(running, flat_e[:, None], axis=1)[:, 0] - 1
    counts = running[-1]
    padded = (counts + MOE_ROWS - 1) // MOE_ROWS * MOE_ROWS
    pad_end = jnp.cumsum(padded)
    dest = (pad_end - padded)[flat_e] + rank
    n_blocks = -(-n_assign // MOE_ROWS) + N_EXPERTS
    n_rows = n_blocks * MOE_ROWS
    row_tok = jnp.zeros((n_rows,), jnp.int32).at[dest].set(flat_t)
    blk = jnp.arange(n_blocks, dtype=jnp.int32)
    n_used = (pad_end[-1] // MOE_ROWS).astype(jnp.int32)
    blk_exp = jnp.searchsorted(pad_end, jnp.minimum(blk, n_used - 1) * MOE_ROWS, side='right').astype(jnp.int32)
    blk_exp = jnp.minimum(blk_exp, N_EXPERTS - 1)
    prev = jnp.concatenate([jnp.full((1,), -1, jnp.int32), blk_exp[:-1]])
    blk_first = ((blk_exp != prev) & (blk < n_used)).astype(jnp.int32)
    into_range = blk * MOE_ROWS - (pad_end - padded)[blk_exp]
    blk_cnt = jnp.where(blk < n_used, jnp.clip(counts[blk_exp] - into_range, 0, MOE_ROWS), 0).astype(jnp.int32)
    return row_tok, blk_cnt, dest.astype(jnp.int32), blk_exp, blk_first, n_used.reshape(1)


def _final_body(dest_ref, x_ref, w_ref, y_ref, g_ref, op_ref, os_ref, buf, sem, *, prompt_tiles):
    i = pl.program_id(0)
    tm = buf.shape[2]

    def start(blk):
        for k in range(TOP_K):
            _start_row_gather(dest_ref, blk * tm * TOP_K + k, TOP_K, y_ref, buf.at[blk % 2, k], sem.at[blk % 2, k],
                              tm)

    @pl.when(i == 0)
    def _():
        start(0)

    @pl.when(i + 1 < pl.num_programs(0))
    def _():
        start(i + 1)

    slot = i % 2
    for k in range(TOP_K):
        _wait_row_gather(y_ref, buf.at[slot, k], sem.at[slot, k], tm)
    w = w_ref[...]
    moe = buf[slot, 0] * w[:, 0:1] + buf[slot, 1] * w[:, 1:2]
    y = _rmsnorm_rows(x_ref[...] + moe, g_ref[...])

    @pl.when(i < prompt_tiles)
    def _():
        op_ref[...] = y

    @pl.when(i >= prompt_tiles)
    def _():
        os_ref[...] = y


def _combine_and_norm(x, wts, y_rows, dest, g, n_prompt):
    m, d = x.shape
    tm = _tile(m - n_prompt, _tile(n_prompt, 256))
    assert n_prompt % tm == 0
    pt = n_prompt // tm
    grid_spec = pltpu.PrefetchScalarGridSpec(
        num_scalar_prefetch=1,
        grid=(m // tm,),
        in_specs=[pl.BlockSpec((tm, d), lambda i, dest: (i, 0)),
                  pl.BlockSpec((tm, LANES), lambda i, dest: (i, 0)),
                  pl.BlockSpec(memory_space=pl.ANY),
                  pl.BlockSpec((1, d), lambda i, dest: (0, 0))],
        out_specs=[pl.BlockSpec((tm, d), lambda i, dest: (jnp.minimum(i, pt - 1), 0)),
                   pl.BlockSpec((tm, d), lambda i, dest: (jnp.maximum(i - pt, 0), 0))],
        scratch_shapes=[pltpu.VMEM((2, TOP_K, tm, d), F32), pltpu.SemaphoreType.DMA((2, TOP_K))],
    )
    return pl.pallas_call(
        functools.partial(_final_body, prompt_tiles=pt),
        out_shape=[jax.ShapeDtypeStruct((n_prompt, d), F32), jax.ShapeDtypeStruct((m - n_prompt, d), F32)],
        grid_spec=grid_spec,
        compiler_params=_params("arbitrary"),
        name="combine_norm",
    )(dest, x, wts, y_rows, g.reshape(1, d))


def _layer(xp, xs, n_seq, n_new, past_len, cache_sb_k, cache_sb_v, state_ret, cache_mem_k, cache_mem_v,
           mem_prompt, norm_mix, w_in, ret_gn, w_out, norm_cross, norm_mem, w_cq, w_ck, w_cv, w_co, norm_ffn,
           wg, bg, we, be, w1, w3, w2):
    n_prompt, d = xp.shape
    n_tok = n_prompt + xs.shape[0]
    d_ret = d // 2
    d_sb = d - d_ret
    ret_heads = d_ret // RET_HEAD_DIM
    sds = jax.ShapeDtypeStruct

    pos = jnp.concatenate([jnp.arange(n_prompt, dtype=jnp.int32),
                           jnp.tile(past_len + jnp.arange(n_new, dtype=jnp.int32), n_seq)])
    half = RET_HEAD_DIM // 2
    inv = ROPE_BASE ** (-jnp.arange(half, dtype=F32) / half)
    ang = pos.astype(F32)[:, None] * inv[None, :]
    cos, sin = jnp.cos(ang), jnp.sin(ang)
    log_gamma = jnp.log1p(-jnp.exp2(-5.0 - jnp.arange(ret_heads, dtype=F32)))

    h = _rmsnorm_stacked(xp, xs, norm_mix, BF16)
    rot = [(cos, half), (sin, half)]
    bf = lambda n: [(sds((n_tok, n), BF16), None)]
    (rq,) = _matmul([h], w_in, 0, d_ret, _ep_rotary(1.0), rot, bf(d_ret), "in_proj_ret_q")
    (rk,) = _matmul([h], w_in, d_ret, d_ret, _ep_rotary(RET_HEAD_DIM ** -0.5), rot, bf(d_ret), "in_proj_ret_k")
    (rv,) = _matmul([h], w_in, 2 * d_ret, d_ret, _ep_store(1.0), [], bf(d_ret), "in_proj_ret_v")
    (rg,) = _matmul([h], w_in, 3 * d_ret, d_ret, _ep_store(1.0), [], [(sds((n_tok, d_ret), F32), None)],
                    "in_proj_ret_gate")
    (sq,) = _matmul([h], w_in, 4 * d_ret, d_sb, _ep_store(SB_HEAD_DIM ** -0.5 * LOG2_E), [], bf(d_sb),
                    "in_proj_sb_q")
    split = [(sds((n_prompt, d_sb), F32), PROMPT), (sds((n_tok - n_prompt, d_sb), F32), SAMPLE),
             (sds((n_tok, d_sb), BF16), None)]
    sk_p, sk_s, sk_b = _matmul([h], w_in, 4 * d_ret + d_sb, d_sb, _ep_store_split, [], split, "in_proj_sb_k",
                               n_prompt=n_prompt)
    sv_p, sv_s, sv_b = _matmul([h], w_in, 4 * d_ret + 2 * d_sb, d_sb, _ep_store_split, [], split,
                               "in_proj_sb_v", n_prompt=n_prompt)

    rows_p = _tile(n_prompt, RET_BLOCK)
    zeros_state = jnp.zeros((1, ret_heads, RET_HEAD_DIM, RET_HEAD_DIM), F32)
    yr, state_p = _retention(log_gamma, rq, rk, rv, rg, ret_gn, zeros_state, 0, 1, n_prompt // rows_p, rows_p)
    yr, state_s = _retention(log_gamma, rq, rk, rv, rg, ret_gn, state_ret, n_prompt, n_seq, 1, n_new,
                             out_buf=yr)
    sb = _sb_prompt(sq, sk_b, sv_b, n_prompt)
    sb = _sb_sample(sq, sk_b, sv_b, cache_sb_k.reshape(n_seq, past_len, d_sb),
                    cache_sb_v.reshape(n_seq, past_len, d_sb), n_prompt, n_new, sb)
    res = [(sds((n_tok, d), F32), None)]
    (x,) = _matmul([yr, sb], w_out, 0, d, _ep_residual_split, [(xp, PROMPT), (xs, SAMPLE)], res, "out_proj",
                   n_prompt=n_prompt)

    n_mem = mem_prompt.shape[0]
    m = _rmsnorm(mem_prompt, norm_mem, BF16)
    mem_out = [(sds((n_mem, d), F32), None)]
    (mk,) = _matmul([m], w_ck, 0, d, _ep_store(1.0), [], mem_out, "mem_k")
    (mv,) = _matmul([m], w_cv, 0, d, _ep_store(1.0), [], mem_out, "mem_v")
    h = _rmsnorm(x, norm_cross, BF16)
    (cq,) = _matmul([h], w_cq, 0, d, _ep_store(1.0), [], bf(d), "cross_q")
    co = _cross_attend(cq, mk.reshape(1, n_mem, d), mv.reshape(1, n_mem, d), 0, n_prompt)
    co = _cross_attend(cq, cache_mem_k.reshape(n_seq, n_mem, d), cache_mem_v.reshape(n_seq, n_mem, d),
                       n_prompt, n_new, out_buf=co)
    (x,) = _matmul([co], w_co, 0, d, _ep_residual, [(x, None)], res, "cross_out")

    hn, ids, wts = _norm_and_route(x, norm_ffn, wg, bg, we, be)
    row_tok, blk_cnt, dest, blk_exp, blk_first, n_used = _route_layout(ids[:, :TOP_K], n_tok)
    x_rows = _dispatch_rows(hn, row_tok, blk_cnt, n_used)
    y_rows = _expert_ffn(x_rows, blk_exp, blk_first, n_used, w1, w3, w2)
    return x, wts, y_rows, dest, (sk_p, sv_p, sk_s, sv_s, state_p, state_s, mk, mv)


def kernel(x_prompt, x_sample, cache_sb_k, cache_sb_v, state_ret, cache_mem_k, cache_mem_v, mem_prompt, norm_mix, w_in, ret_gn, w_out, norm_cross, norm_mem, w_cq, w_ck, w_cv, w_co, norm_ffn, router_group_w, router_group_b, router_expert_w, router_expert_b, expert_w1, expert_w3, expert_w2, norm_final):
    batch, n_prompt, d = x_prompt.shape
    n_seq, n_new, _ = x_sample.shape
    depth, _, past_len, sb_heads, _ = cache_sb_k.shape
    n_mem = mem_prompt.shape[1]
    assert batch == 1 and depth == 1
    ret_heads = (d // 2) // RET_HEAD_DIM
    x, wts, y_rows, dest, (sk_p, sv_p, sk_s, sv_s, state_p, state_s, mk, mv) = _layer(
        x_prompt.reshape(n_prompt, d), x_sample.reshape(n_seq * n_new, d), n_seq, n_new, past_len,
        cache_sb_k, cache_sb_v, state_ret[0], cache_mem_k, cache_mem_v, mem_prompt[0], norm_mix[0], w_in,
        ret_gn[0], w_out, norm_cross[0], norm_mem[0], w_cq, w_ck, w_cv, w_co, norm_ffn[0], router_group_w[0],
        router_group_b[0], router_expert_w[0], router_expert_b[0], expert_w1[0], expert_w3[0], expert_w2[0])
    y_prompt, y_sample = _combine_and_norm(x, wts, y_rows, dest, norm_final, n_prompt)
    y_prompt = y_prompt.reshape(1, n_prompt, d)
    y_sample = y_sample.reshape(n_seq, n_new, d)
    sb_shape_p = (1, 1, n_prompt, sb_heads, SB_HEAD_DIM)
    sb_shape_s = (1, n_seq, n_new, sb_heads, SB_HEAD_DIM)
    mem_shape = (1, 1, n_mem, MEM_HEADS, d // MEM_HEADS)
    return (y_prompt, y_sample,
            sk_p.reshape(sb_shape_p), sv_p.reshape(sb_shape_p),
            state_p.reshape(1, 1, ret_heads, RET_HEAD_DIM, RET_HEAD_DIM),
            mk.reshape(mem_shape), mv.reshape(mem_shape),
            sk_s.reshape(sb_shape_s), sv_s.reshape(sb_shape_s),
            state_s.reshape(1, n_seq, ret_heads, RET_HEAD_DIM, RET_HEAD_DIM))
```

```python
import functools

import jax
import jax.numpy as jnp
from jax import lax
from jax.experimental import pallas as pl
from jax.experimental.pallas import tpu as pltpu

BF16 = jnp.bfloat16
F32 = jnp.float32

EPS = 1e-6
CHUNK = 64
RET_HEAD_DIM = 256
SB_HEAD_DIM = 128
MEM_HEADS = 4
N_GROUPS = 4
EXPERTS_PER_GROUP = 8
N_EXPERTS = N_GROUPS * EXPERTS_PER_GROUP
TOP_K = 2
ROPE_BASE = 10000.0

LANES = 128
VMEM_LIMIT_BYTES = 56 * 1024 * 1024
ROW_TILE = 512
COL_TILE = 512
RET_BLOCK = 256
RET_HEADS_PER_STEP = 2
SB_BLOCK = 256
SB_HEADS_PER_STEP = 4
LOG2_E = 1.4426950408889634
SB_DEAD_LOG2 = -64.0
MOE_ROWS = 512
EXPERT_OUT_TILE = 2048

PROMPT = "prompt"
SAMPLE = "sample"

_NT = (((1,), (1,)), ((), ()))
_TN = (((0,), (0,)), ((), ()))


def _tile(n, pref):
    t = min(pref, n)
    while n % t:
        t //= 2
    return t


def _params(*sem):
    return pltpu.CompilerParams(dimension_semantics=sem, vmem_limit_bytes=VMEM_LIMIT_BYTES)


def _sigmoid(x):
    return 1.0 / (1.0 + jnp.exp(-x))


def _rmsnorm_rows(x, g):
    ms = jnp.mean(x * x, axis=-1, keepdims=True)
    return x * lax.rsqrt(ms + EPS) * g


def _rmsnorm_body(x_ref, g_ref, o_ref):
    o_ref[...] = _rmsnorm_rows(x_ref[...], g_ref[...]).astype(o_ref.dtype)


def _rmsnorm(x, g, out_dtype):
    m, d = x.shape
    tm = _tile(m, 256)
    return pl.pallas_call(
        _rmsnorm_body,
        out_shape=jax.ShapeDtypeStruct((m, d), out_dtype),
        grid=(m // tm,),
        in_specs=[pl.BlockSpec((tm, d), lambda i: (i, 0)), pl.BlockSpec((1, d), lambda i: (0, 0))],
        out_specs=pl.BlockSpec((tm, d), lambda i: (i, 0)),
        compiler_params=_params("parallel"),
        name="rmsnorm",
    )(x, g.reshape(1, d))


def _rmsnorm_stacked_body(xp_ref, xs_ref, g_ref, o_ref, *, prompt_tiles):
    is_prompt = pl.program_id(0) < prompt_tiles

    @pl.when(is_prompt)
    def _():
        o_ref[...] = _rmsnorm_rows(xp_ref[...], g_ref[...]).astype(o_ref.dtype)

    @pl.when(jnp.logical_not(is_prompt))
    def _():
        o_ref[...] = _rmsnorm_rows(xs_ref[...], g_ref[...]).astype(o_ref.dtype)


def _rmsnorm_stacked(xp, xs, g, out_dtype):
    (n_p, d), n_s = xp.shape, xs.shape[0]
    tm = _tile(n_s, _tile(n_p, 256))
    pt = n_p // tm
    return pl.pallas_call(
        functools.partial(_rmsnorm_stacked_body, prompt_tiles=pt),
        out_shape=jax.ShapeDtypeStruct((n_p + n_s, d), out_dtype),
        grid=((n_p + n_s) // tm,),
        in_specs=[pl.BlockSpec((tm, d), lambda i: (jnp.minimum(i, pt - 1), 0)),
                  pl.BlockSpec((tm, d), lambda i: (jnp.maximum(i - pt, 0), 0)),
                  pl.BlockSpec((1, d), lambda i: (0, 0))],
        out_specs=pl.BlockSpec((tm, d), lambda i: (i, 0)),
        compiler_params=_params("arbitrary"),
        name="rmsnorm_stacked",
    )(xp, xs, g.reshape(1, d))


def _matmul_body(*refs, n_lhs, n_extra, n_out, epilogue):
    x_refs = refs[:n_lhs]
    w_refs = refs[n_lhs:2 * n_lhs]
    extra = refs[2 * n_lhs:2 * n_lhs + n_extra]
    outs = refs[2 * n_lhs + n_extra:2 * n_lhs + n_extra + n_out]
    wb = refs[2 * n_lhs + n_extra + n_out:]

    @pl.when(pl.program_id(1) == 0)
    def _():
        for p in range(n_lhs):
            wb[p][...] = w_refs[p][...].astype(BF16)

    acc = jnp.dot(x_refs[0][...], wb[0][...], preferred_element_type=F32)
    for p in range(1, n_lhs):
        acc += jnp.dot(x_refs[p][...], wb[p][...], preferred_element_type=F32)
    epilogue(acc, extra, outs)


def _matmul(xs, w, col_off, n_cols, epilogue, extras, outs, name, n_prompt=None):
    m = xs[0].shape[0]
    tm = _tile(m, ROW_TILE) if n_prompt is None else _tile(m - n_prompt, _tile(n_prompt, ROW_TILE))
    tn = _tile(n_cols, COL_TILE)
    assert col_off % tn == 0
    jo = col_off // tn
    pt = None if n_prompt is None else n_prompt // tm
    in_specs, args = [], []
    for x in xs:
        in_specs.append(pl.BlockSpec((tm, x.shape[1]), lambda j, i: (i, 0)))
        args.append(x)
    row = 0
    for x in xs:
        kp = x.shape[1]
        assert row % kp == 0
        in_specs.append(pl.BlockSpec((None, kp, tn), lambda j, i, r=row // kp: (0, r, j + jo)))
        args.append(w)
        row += kp
    assert row == w.shape[1]

    def spec(kind):
        if kind is None:
            return pl.BlockSpec((tm, tn), lambda j, i: (i, j))
        if kind == PROMPT:
            return pl.BlockSpec((tm, tn), lambda j, i: (jnp.minimum(i, pt - 1), j))
        if kind == SAMPLE:
            return pl.BlockSpec((tm, tn), lambda j, i: (jnp.maximum(i - pt, 0), j))
        return pl.BlockSpec((tm, kind), lambda j, i: (i, 0))

    for a, kind in extras:
        in_specs.append(spec(kind))
        args.append(a)
    body = functools.partial(_matmul_body, n_lhs=len(xs), n_extra=len(extras), n_out=len(outs),
                             epilogue=epilogue if n_prompt is None else functools.partial(epilogue, pt))
    res = pl.pallas_call(
        body,
        out_shape=[o for o, _ in outs],
        grid=(n_cols // tn, m // tm),
        in_specs=in_specs,
        out_specs=[spec(kind) for _, kind in outs],
        scratch_shapes=[pltpu.VMEM((x.shape[1], tn), BF16) for x in xs],
        compiler_params=_params("arbitrary", "arbitrary"),
        name=name,
    )(*args)
    return res


def _ep_store(scale):
    def ep(acc, extra, outs):
        for o in outs:
            o[...] = (acc * scale).astype(o.dtype) if scale != 1.0 else acc.astype(o.dtype)
    return ep


def _ep_store_split(prompt_tiles, acc, extra, outs):
    is_prompt = pl.program_id(1) < prompt_tiles

    @pl.when(is_prompt)
    def _():
        outs[0][...] = acc

    @pl.when(jnp.logical_not(is_prompt))
    def _():
        outs[1][...] = acc

    outs[2][...] = acc.astype(outs[2].dtype)


def _ep_residual(acc, extra, outs):
    outs[0][...] = extra[0][...] + acc


def _ep_residual_split(prompt_tiles, acc, extra, outs):
    res = jnp.where(pl.program_id(1) < prompt_tiles, extra[0][...], extra[1][...])
    outs[0][...] = res + acc


def _ep_rotary(scale):
    half = RET_HEAD_DIM // 2

    def ep(acc, extra, outs):
        cos = extra[0][...]
        sin = extra[1][...]
        for h in range(acc.shape[1] // RET_HEAD_DIM):
            lo = h * RET_HEAD_DIM
            x1 = acc[:, lo:lo + half]
            x2 = acc[:, lo + half:lo + RET_HEAD_DIM]
            outs[0][:, lo:lo + half] = ((x1 * cos - x2 * sin) * scale).astype(outs[0].dtype)
            outs[0][:, lo + half:lo + RET_HEAD_DIM] = ((x1 * sin + x2 * cos) * scale).astype(outs[0].dtype)
    return ep


def _retention_body(lg_ref, q_ref, k_ref, v_ref, g_ref, gn_ref, s0_ref, *rest, rows, n_heads):
    o_ref, sout_ref, s_scr, decay_scr = rest[-4:]
    heads = range(n_heads)
    cols = lambda h: slice(h * RET_HEAD_DIM, (h + 1) * RET_HEAD_DIM)
    lgs = [lg_ref[pl.program_id(1) * n_heads + h] for h in heads]

    @pl.when(pl.program_id(2) == 0)
    def _():
        s_scr[...] = s0_ref[...]
        ii = lax.broadcasted_iota(jnp.int32, (rows, rows), 0)
        jj = lax.broadcasted_iota(jnp.int32, (rows, rows), 1)
        shift = CHUNK.bit_length() - 1
        dist = jnp.abs(ii - jj).astype(F32)
        for h in heads:
            decay_scr[h] = jnp.where((jj >> shift) <= (ii >> shift), jnp.exp(lgs[h] * dist), 0.0)

    idx = lax.broadcasted_iota(jnp.int32, (rows, 1), 0).astype(F32)
    qs = [q_ref[:, cols(h)] for h in heads]
    ks = [k_ref[:, cols(h)] for h in heads]
    vs = [v_ref[:, cols(h)] for h in heads]
    scores = [lax.dot_general(qs[h], ks[h], _NT, preferred_element_type=F32) * decay_scr[h] for h in heads]
    outs = [jnp.dot(scores[h].astype(BF16), vs[h], preferred_element_type=F32) for h in heads]
    states = [s_scr[h] for h in heads]
    outs = [outs[h] + jnp.exp(lgs[h] * (idx + 1.0))
            * jnp.dot(qs[h], states[h].astype(BF16), preferred_element_type=F32) for h in heads]
    kds = [(ks[h].astype(F32) * jnp.exp(lgs[h] * (rows - 1.0 - idx))).astype(BF16) for h in heads]
    for h in heads:
        s_new = (jnp.exp(lgs[h] * rows) * states[h]
                 + lax.dot_general(kds[h], vs[h], _TN, preferred_element_type=F32))
        s_scr[h] = s_new
        sout_ref[h] = s_new
    for h in heads:
        mu = jnp.mean(outs[h], axis=-1, keepdims=True)
        cen = outs[h] - mu
        var = jnp.mean(cen * cen, axis=-1, keepdims=True)
        yr = cen * lax.rsqrt(var + EPS) * gn_ref[h]
        g = g_ref[:, cols(h)]
        o_ref[:, cols(h)] = (g * _sigmoid(g) * yr).astype(o_ref.dtype)


def _into(out_buf):
    if out_buf is None:
        return [], []
    return [out_buf], [pl.BlockSpec(memory_space=pl.ANY)]


def _retention(log_gamma, q, k, v, gate, gn, s0, row_off, n_batch, n_steps, rows, out_buf=None):
    heads = q.shape[1] // RET_HEAD_DIM
    nh = min(RET_HEADS_PER_STEP, heads)
    assert row_off % rows == 0 and rows % CHUNK == 0 and heads % nh == 0
    ro = row_off // rows
    tok = lambda b, h, c, lg: (ro + b * n_steps + c, h)
    blk = pl.BlockSpec((rows, nh * RET_HEAD_DIM), tok)
    st = pl.BlockSpec((None, nh, RET_HEAD_DIM, RET_HEAD_DIM), lambda b, h, c, lg: (b, h, 0, 0))
    args = [log_gamma, q, k, v, gate, gn.reshape(heads, 1, RET_HEAD_DIM), s0]
    extra_args, extra_specs = _into(out_buf)
    grid_spec = pltpu.PrefetchScalarGridSpec(
        num_scalar_prefetch=1,
        grid=(n_batch, heads // nh, n_steps),
        in_specs=[blk, blk, blk, blk,
                  pl.BlockSpec((nh, 1, RET_HEAD_DIM), lambda b, h, c, lg: (h, 0, 0)), st] + extra_specs,
        out_specs=[blk, st],
        scratch_shapes=[pltpu.VMEM((nh, RET_HEAD_DIM, RET_HEAD_DIM), F32), pltpu.VMEM((nh, rows, rows), F32)],
    )
    return pl.pallas_call(
        functools.partial(_retention_body, rows=rows, n_heads=nh),
        out_shape=[jax.ShapeDtypeStruct((q.shape[0], heads * RET_HEAD_DIM), BF16),
                   jax.ShapeDtypeStruct((n_batch, heads, RET_HEAD_DIM, RET_HEAD_DIM), F32)],
        grid_spec=grid_spec,
        input_output_aliases={len(args): 0} if extra_args else {},
        compiler_params=_params("arbitrary", "arbitrary", "arbitrary"),
        name="retention",
    )(*args, *extra_args)


def _neg_suffix(n):
    j = lax.broadcasted_iota(jnp.int32, (2 * n, n), 0) & (n - 1)
    s = lax.broadcasted_iota(jnp.int32, (2 * n, n), 1)
    return jnp.where(j >= s, -1.0, 0.0).astype(BF16)


def _lanes_to(x, n):
    return x[:, :n] if n <= LANES else jnp.concatenate([x] * (n // LANES), axis=1)


def _suffix_sums(fail, neg_ones):
    tk = fail.shape[1]
    hi, lo = _split_bf16(fail)
    if tk % LANES == 0:
        return jnp.dot(jnp.concatenate([hi, lo], axis=1), neg_ones, preferred_element_type=F32)
    return (jnp.dot(hi, neg_ones[:tk], preferred_element_type=F32)
            + jnp.dot(lo, neg_ones[:tk], preferred_element_type=F32))


def _sb_tiles(qs, kbs, vbs, neg_ones, carry, acc, masked):
    heads = range(len(qs))
    tk = kbs[0].shape[0]
    zs = [lax.dot_general(qs[h], kbs[h], _NT, preferred_element_type=F32) for h in heads]
    fails = [jnp.maximum(z, 0.0) + jnp.log2(1.0 + jnp.exp2(-jnp.abs(z))) for z in zs]
    if masked:
        t = lax.broadcasted_iota(jnp.int32, zs[0].shape, 0)
        s = lax.broadcasted_iota(jnp.int32, zs[0].shape, 1)
        valid = s < t
        fails = [jnp.where(valid, f, 0.0) for f in fails]
    suffixes = [_suffix_sums(f, neg_ones) for f in fails]
    probs = [jnp.exp2(zs[h] + suffixes[h] + _lanes_to(carry[h], tk)) for h in heads]
    if masked:
        probs = [jnp.where(valid, p, 0.0) for p in probs]
    for h in heads:
        acc[h] += jnp.dot(probs[h].astype(BF16), vbs[h], preferred_element_type=F32)
        carry[h] += jnp.broadcast_to(suffixes[h][:, :1], carry.shape[1:])


def _sb_body(q_ref, kd_ref, vd_ref, kp_ref, vp_ref, *rest, tk, n_heads, past_per_step, past_static):
    o_ref, carry, acc = rest[-3:]
    tq = q_ref.shape[0]
    n_past = past_static if past_static is not None else pl.program_id(1) * past_per_step
    cols = lambda h: slice(h * SB_HEAD_DIM, (h + 1) * SB_HEAD_DIM)
    heads = range(n_heads)
    carry[...] = jnp.zeros(carry.shape, F32)
    acc[...] = jnp.zeros(acc.shape, F32)
    queries = lambda: [q_ref[:, cols(h)] for h in heads]
    _sb_tiles(queries(), [kd_ref[:, cols(h)].astype(BF16) for h in heads],
              [vd_ref[:, cols(h)].astype(BF16) for h in heads], _neg_suffix(tq), carry, acc, True)
    neg_ones = _neg_suffix(tk)

    alive = lambda: jnp.max(carry[...]) > SB_DEAD_LOG2

    def step(state):
        it = state[0]
        rows = pl.ds(pl.multiple_of((n_past - 1 - it) * tk, tk), tk)
        _sb_tiles(queries(), [kp_ref[rows, cols(h)].astype(BF16) for h in heads],
                  [vp_ref[rows, cols(h)].astype(BF16) for h in heads], neg_ones, carry, acc, False)
        return it + 1, alive()

    lax.while_loop(lambda state: (state[0] < n_past) & state[1], step, (jnp.int32(0), alive()))
    for h in range(n_heads):
        o_ref[:, cols(h)] = acc[h].astype(o_ref.dtype)


def _sb_scratch(n_heads, tq):
    return [pltpu.VMEM((n_heads, tq, LANES), F32), pltpu.VMEM((n_heads, tq, SB_HEAD_DIM), F32)]


def _sb_prompt(q, k, v, n_tok):
    heads = q.shape[1] // SB_HEAD_DIM
    tq = _tile(n_tok, SB_BLOCK)
    nh = SB_HEADS_PER_STEP
    width = nh * SB_HEAD_DIM
    blk = pl.BlockSpec((tq, width), lambda h, i: (i, h))
    full = pl.BlockSpec((n_tok, width), lambda h, i: (0, h))
    return pl.pallas_call(
        functools.partial(_sb_body, tk=tq, n_heads=nh, past_per_step=1, past_static=None),
        out_shape=jax.ShapeDtypeStruct((q.shape[0], heads * SB_HEAD_DIM), BF16),
        grid=(heads // nh, n_tok // tq),
        in_specs=[blk, blk, blk, full, full],
        out_specs=blk,
        scratch_shapes=_sb_scratch(nh, tq),
        compiler_params=_params("parallel", "arbitrary"),
        name="stick_breaking_prompt",
    )(q, k, v, k, v)


def _sb_sample(q, k, v, cache_k, cache_v, row_off, n_new, out_buf):
    heads = q.shape[1] // SB_HEAD_DIM
    n_batch, past, _ = cache_k.shape
    tk = _tile(past, SB_BLOCK)
    assert row_off % n_new == 0
    ro = row_off // n_new
    nh = SB_HEADS_PER_STEP
    width = nh * SB_HEAD_DIM
    blk = pl.BlockSpec((n_new, width), lambda b, h: (ro + b, h))
    full = pl.BlockSpec((None, past, width), lambda b, h: (b, 0, h))
    extra_args, extra_specs = _into(out_buf)
    return pl.pallas_call(
        functools.partial(_sb_body, tk=tk, n_heads=nh, past_per_step=0, past_static=past // tk),
        out_shape=jax.ShapeDtypeStruct(out_buf.shape, out_buf.dtype),
        grid=(n_batch, heads // nh),
        in_specs=[blk, blk, blk, full, full] + extra_specs,
        out_specs=blk,
        scratch_shapes=_sb_scratch(nh, n_new),
        input_output_aliases={5: 0},
        compiler_params=_params("parallel", "arbitrary"),
        name="stick_breaking_sample",
    )(q, k, v, cache_k, cache_v, *extra_args)


def _cross_body(q_ref, k_ref, v_ref, *rest, scale):
    o_ref = rest[-1]
    s = lax.dot_general(q_ref[...], k_ref[...].astype(BF16), _NT, preferred_element_type=F32) * scale
    e = jnp.exp(s - jnp.max(s, axis=-1, keepdims=True))
    p = e / jnp.sum(e, axis=-1, keepdims=True)
    o_ref[...] = jnp.dot(p.astype(BF16), v_ref[...].astype(BF16),
                         preferred_element_type=F32).astype(o_ref.dtype)


def _cross_attend(q, mem_k, mem_v, row_off, rows_per_seq, out_buf=None):
    n_seq, n_mem, d = mem_k.shape
    dh = d // MEM_HEADS
    tm = _tile(rows_per_seq, ROW_TILE)
    steps = rows_per_seq // tm
    assert row_off % tm == 0
    ro = row_off // tm
    mem = pl.BlockSpec((None, n_mem, dh), lambda i, h: (i // steps, 0, h))
    blk = pl.BlockSpec((tm, dh), lambda i, h: (ro + i, h))
    extra_args, extra_specs = _into(out_buf)
    return pl.pallas_call(
        functools.partial(_cross_body, scale=dh ** -0.5),
        out_shape=jax.ShapeDtypeStruct((q.shape[0], d), BF16),
        grid=(n_seq * steps, MEM_HEADS),
        in_specs=[blk, mem, mem] + extra_specs,
        out_specs=blk,
        input_output_aliases={3: 0} if extra_args else {},
        compiler_params=_params("parallel", "arbitrary"),
        name="cross_attention",
    )(q, mem_k, mem_v, *extra_args)


def _split_bf16(x):
    hi = x.astype(BF16)
    return hi, (x - hi.astype(F32)).astype(BF16)


def _router_body(x_ref, g_ref, wr_ref, br_ref, h_ref, ids_ref, wts_ref):
    h = _rmsnorm_rows(x_ref[...], g_ref[...])
    h_ref[...] = _pack_bf16_pairs(h)
    h_hi, h_lo = _split_bf16(h)
    w_hi, w_lo = _split_bf16(wr_ref[...])
    logits = (jnp.dot(h_hi, w_hi, preferred_element_type=F32)
              + jnp.dot(h_hi, w_lo, preferred_element_type=F32)
              + jnp.dot(h_lo, w_hi, preferred_element_type=F32)) + br_ref[...]
    lane = lax.broadcasted_iota(jnp.int32, logits.shape, 1).astype(F32)
    big = float(LANES)
    neg = -jnp.inf

    def first_max(vals):
        top = jnp.max(vals, axis=-1, keepdims=True)
        return top, jnp.min(jnp.where(vals == top, lane, big), axis=-1, keepdims=True)

    is_group = lane < N_GROUPS
    g_max, g_top = first_max(jnp.where(is_group, logits, neg))
    g_w = 1.0 / jnp.sum(jnp.where(is_group, jnp.exp(logits - g_max), 0.0), axis=-1, keepdims=True)
    first = N_GROUPS + EXPERTS_PER_GROUP * g_top
    cand = jnp.where((lane >= first) & (lane < first + EXPERTS_PER_GROUP), logits, neg)
    v1, i1 = first_max(cand)
    v2, i2 = first_max(jnp.where(lane == i1, neg, cand))
    t = jnp.exp(v2 - v1)
    w1 = g_w / (1.0 + t)
    w2 = g_w * t / (1.0 + t)
    ids = jnp.where(lane == 0.0, i1 - N_GROUPS, jnp.where(lane == 1.0, i2 - N_GROUPS, 0.0))
    ids_ref[...] = ids.astype(jnp.int32)
    wts_ref[...] = jnp.where(lane == 0.0, w1, jnp.where(lane == 1.0, w2, 0.0))


def _norm_and_route(x, g, wg, bg, we, be):
    m, d = x.shape
    tm = _tile(m, 256)
    pad = LANES - N_GROUPS - N_EXPERTS
    wr = jnp.concatenate([wg, we, jnp.zeros((d, pad), F32)], axis=1)
    br = jnp.concatenate([bg, be, jnp.zeros((pad,), F32)]).reshape(1, LANES)
    row = lambda c: pl.BlockSpec((tm, c), lambda i: (i, 0))
    return pl.pallas_call(
        _router_body,
        out_shape=[jax.ShapeDtypeStruct((m, d // 2), jnp.int32),
                   jax.ShapeDtypeStruct((m, LANES), jnp.int32),
                   jax.ShapeDtypeStruct((m, LANES), F32)],
        grid=(m // tm,),
        in_specs=[row(d), pl.BlockSpec((1, d), lambda i: (0, 0)),
                  pl.BlockSpec((d, LANES), lambda i: (0, 0)), pl.BlockSpec((1, LANES), lambda i: (0, 0))],
        out_specs=[row(d // 2), row(LANES), row(LANES)],
        compiler_params=_params("parallel"),
        name="norm_route",
    )(x, g.reshape(1, d), wr, br)


GATHER_UNROLL = 8


def _start_row_gather(idx_ref, first, stride, src_ref, buf, sem, n_rows):
    def one(r, _):
        t = idx_ref[first + r * stride]
        pltpu.make_async_copy(src_ref.at[pl.ds(t, 1)], buf.at[pl.ds(r, 1)], sem).start()
        return 0

    def group(g, _):
        for u in range(GATHER_UNROLL):
            one(g * GATHER_UNROLL + u, 0)
        return 0

    groups = n_rows // GATHER_UNROLL
    lax.fori_loop(0, groups, group, 0)
    lax.fori_loop(groups * GATHER_UNROLL, n_rows, one, 0)


def _wait_row_gather(src_ref, buf, sem, n_rows):
    pltpu.make_async_copy(src_ref.at[pl.ds(0, n_rows)], buf.at[pl.ds(0, n_rows)], sem).wait()


def _pack_bf16_pairs(x):
    half = x.shape[1] // 2
    hi = lax.bitcast_convert_type(x[:, :half].astype(BF16).astype(F32), jnp.int32)
    lo = lax.bitcast_convert_type(x[:, half:].astype(BF16).astype(F32), jnp.int32)
    return hi | lax.shift_right_logical(lo, jnp.int32(16))


def _unpack_bf16_pairs(p):
    hi = lax.bitcast_convert_type(p & jnp.int32(-65536), F32).astype(BF16)
    lo = lax.bitcast_convert_type(lax.shift_left(p, jnp.int32(16)), F32).astype(BF16)
    return hi, lo


def _dispatch_body(idx_ref, used_ref, cnt_ref, src_ref, o_ref, buf, sem):
    b = pl.program_id(0)
    rows = buf.shape[1]

    copied = lambda blk: pl.multiple_of(pl.cdiv(cnt_ref[blk], GATHER_UNROLL) * GATHER_UNROLL, GATHER_UNROLL)

    def start(blk):
        _start_row_gather(idx_ref, blk * rows, 1, src_ref, buf.at[blk % 2], sem.at[blk % 2], copied(blk))

    @pl.when(b == 0)
    def _():
        buf[...] = jnp.zeros(buf.shape, buf.dtype)
        start(0)

    @pl.when(b + 1 < used_ref[0])
    def _():
        start(b + 1)

    @pl.when(b < used_ref[0])
    def _():
        slot = b % 2
        cnt = cnt_ref[b]
        _wait_row_gather(src_ref, buf.at[slot], sem.at[slot], copied(b))
        half = buf.shape[2]
        row = lax.broadcasted_iota(jnp.int32, (rows, 1), 0)
        hi, lo = _unpack_bf16_pairs(jnp.where(row < cnt, buf[slot], 0))
        o_ref[:, :half] = hi
        o_ref[:, half:] = lo


def _dispatch_rows(src, row_tok, blk_cnt, n_used):
    n_rows = row_tok.shape[0]
    half = src.shape[1]
    grid_spec = pltpu.PrefetchScalarGridSpec(
        num_scalar_prefetch=3,
        grid=(n_rows // MOE_ROWS,),
        in_specs=[pl.BlockSpec(memory_space=pl.ANY)],
        out_specs=pl.BlockSpec((MOE_ROWS, 2 * half), lambda b, idx, used, cnt: (b, 0)),
        scratch_shapes=[pltpu.VMEM((2, MOE_ROWS, half), src.dtype), pltpu.SemaphoreType.DMA((2,))],
    )
    return pl.pallas_call(
        _dispatch_body,
        out_shape=jax.ShapeDtypeStruct((n_rows, 2 * half), BF16),
        grid_spec=grid_spec,
        compiler_params=_params("arbitrary"),
        name="dispatch_rows",
    )(row_tok, n_used, blk_cnt, src)


def _expert_up_body(be_ref, first_ref, used_ref, x_ref, w1_ref, w3_ref, h_ref, w1b, w3b):
    b = pl.program_id(1)

    @pl.when(first_ref[b] == 1)
    def _():
        w1b[...] = w1_ref[...].astype(BF16)
        w3b[...] = w3_ref[...].astype(BF16)

    @pl.when(b < used_ref[0])
    def _():
        x = x_ref[...]
        a = jnp.dot(x, w1b[...], preferred_element_type=F32)
        g = jnp.dot(x, w3b[...], preferred_element_type=F32)
        h_ref[...] = (a * _sigmoid(a) * g).astype(h_ref.dtype)


def _expert_down_body(be_ref, first_ref, used_ref, h_ref, w2_ref, y_ref, w2b):
    b = pl.program_id(1)

    @pl.when(first_ref[b] == 1)
    def _():
        w2b[...] = w2_ref[...].astype(BF16)

    @pl.when(b < used_ref[0])
    def _():
        y_ref[...] = jnp.dot(h_ref[...], w2b[...], preferred_element_type=F32)


def _expert_ffn(xs, blk_exp, blk_first, n_used, w1, w3, w2):
    n_rows, d = xs.shape
    f = w1.shape[2]
    n_blocks = n_rows // MOE_ROWS
    tf = _tile(f, COL_TILE)
    tn = _tile(d, EXPERT_OUT_TILE)
    last = lambda b, used: jnp.minimum(b, used[0] - 1)
    up_spec = pltpu.PrefetchScalarGridSpec(
        num_scalar_prefetch=3,
        grid=(f // tf, n_blocks),
        in_specs=[pl.BlockSpec((MOE_ROWS, d), lambda j, b, be, fi, us: (last(b, us), 0)),
                  pl.BlockSpec((None, d, tf), lambda j, b, be, fi, us: (be[b], 0, j)),
                  pl.BlockSpec((None, d, tf), lambda j, b, be, fi, us: (be[b], 0, j))],
        out_specs=pl.BlockSpec((MOE_ROWS, tf), lambda j, b, be, fi, us: (b, j)),
        scratch_shapes=[pltpu.VMEM((d, tf), BF16), pltpu.VMEM((d, tf), BF16)],
    )
    hidden = pl.pallas_call(
        _expert_up_body,
        out_shape=jax.ShapeDtypeStruct((n_rows, f), BF16),
        grid_spec=up_spec,
        compiler_params=_params("arbitrary", "arbitrary"),
        name="expert_up",
    )(blk_exp, blk_first, n_used, xs, w1, w3)
    down_spec = pltpu.PrefetchScalarGridSpec(
        num_scalar_prefetch=3,
        grid=(d // tn, n_blocks),
        in_specs=[pl.BlockSpec((MOE_ROWS, f), lambda j, b, be, fi, us: (last(b, us), 0)),
                  pl.BlockSpec((None, f, tn), lambda j, b, be, fi, us: (be[b], 0, j))],
        out_specs=pl.BlockSpec((MOE_ROWS, tn), lambda j, b, be, fi, us: (b, j)),
        scratch_shapes=[pltpu.VMEM((f, tn), BF16)],
    )
    return pl.pallas_call(
        _expert_down_body,
        out_shape=jax.ShapeDtypeStruct((n_rows, d), F32),
        grid_spec=down_spec,
        compiler_params=_params("arbitrary", "arbitrary"),
        name="expert_down",
    )(blk_exp, blk_first, n_used, hidden, w2)


def _route_layout(ids, n_tok):
    n_assign = n_tok * TOP_K
    flat_e = ids.reshape(n_assign)
    flat_t = jnp.repeat(jnp.arange(n_tok, dtype=jnp.int32), TOP_K)
    onehot = (flat_e[:, None] == jnp.arange(N_EXPERTS, dtype=jnp.int32)[None, :]).astype(jnp.int32)
    running = jnp.cumsum(onehot, axis=0)
    rank = jnp.take_along_axis(running, flat_e[:, None], axis=1)[:, 0] - 1
    counts = running[-1]
    padded = (counts + MOE_ROWS - 1) // MOE_ROWS * MOE_ROWS
    pad_end = jnp.cumsum(padded)
    dest = (pad_end - padded)[flat_e] + rank
    n_blocks = -(-n_assign // MOE_ROWS) + N_EXPERTS
    n_rows = n_blocks * MOE_ROWS
    row_tok = jnp.zeros((n_rows,), jnp.int32).at[dest].set(flat_t)
    blk = jnp.arange(n_blocks, dtype=jnp.int32)
    n_used = (pad_end[-1] // MOE_ROWS).astype(jnp.int32)
    blk_exp = jnp.searchsorted(pad_end, jnp.minimum(blk, n_used - 1) * MOE_ROWS, side='right').astype(jnp.int32)
    blk_exp = jnp.minimum(blk_exp, N_EXPERTS - 1)
    prev = jnp.concatenate([jnp.full((1,), -1, jnp.int32), blk_exp[:-1]])
    blk_first = ((blk_exp != prev) & (blk < n_used)).astype(jnp.int32)
    into_range = blk * MOE_ROWS - (pad_end - padded)[blk_exp]
    blk_cnt = jnp.where(blk < n_used, jnp.clip(counts[blk_exp] - into_range, 0, MOE_ROWS), 0).astype(jnp.int32)
    return row_tok, blk_cnt, dest.astype(jnp.int32), blk_exp, blk_first, n_used.reshape(1)


def _final_body(dest_ref, x_ref, w_ref, y_ref, g_ref, op_ref, os_ref, buf, sem, *, prompt_tiles):
    i = pl.program_id(0)
    tm = buf.shape[2]

    def start(blk):
        for k in range(TOP_K):
            _start_row_gather(dest_ref, blk * tm * TOP_K + k, TOP_K, y_ref, buf.at[blk % 2, k], sem.at[blk % 2, k],
                              tm)

    @pl.when(i == 0)
    def _():
        start(0)

    @pl.when(i + 1 < pl.num_programs(0))
    def _():
        start(i + 1)

    slot = i % 2
    for k in range(TOP_K):
        _wait_row_gather(y_ref, buf.at[slot, k], sem.at[slot, k], tm)
    w = w_ref[...]
    moe = buf[slot, 0] * w[:, 0:1] + buf[slot, 1] * w[:, 1:2]
    y = _rmsnorm_rows(x_ref[...] + moe, g_ref[...])

    @pl.when(i < prompt_tiles)
    def _():
        op_ref[...] = y

    @pl.when(i >= prompt_tiles)
    def _():
        os_ref[...] = y


def _combine_and_norm(x, wts, y_rows, dest, g, n_prompt):
    m, d = x.shape
    tm = _tile(m - n_prompt, _tile(n_prompt, 256))
    assert n_prompt % tm == 0
    pt = n_prompt // tm
    grid_spec = pltpu.PrefetchScalarGridSpec(
        num_scalar_prefetch=1,
        grid=(m // tm,),
        in_specs=[pl.BlockSpec((tm, d), lambda i, dest: (i, 0)),
                  pl.BlockSpec((tm, LANES), lambda i, dest: (i, 0)),
                  pl.BlockSpec(memory_space=pl.ANY),
                  pl.BlockSpec((1, d), lambda i, dest: (0, 0))],
        out_specs=[pl.BlockSpec((tm, d), lambda i, dest: (jnp.minimum(i, pt - 1), 0)),
                   pl.BlockSpec((tm, d), lambda i, dest: (jnp.maximum(i - pt, 0), 0))],
        scratch_shapes=[pltpu.VMEM((2, TOP_K, tm, d), F32), pltpu.SemaphoreType.DMA((2, TOP_K))],
    )
    return pl.pallas_call(
        functools.partial(_final_body, prompt_tiles=pt),
        out_shape=[jax.ShapeDtypeStruct((n_prompt, d), F32), jax.ShapeDtypeStruct((m - n_prompt, d), F32)],
        grid_spec=grid_spec,
        compiler_params=_params("arbitrary"),
        name="combine_norm",
    )(dest, x, wts, y_rows, g.reshape(1, d))


def _layer(xp, xs, n_seq, n_new, past_len, cache_sb_k, cache_sb_v, state_ret, cache_mem_k, cache_mem_v,
           mem_prompt, norm_mix, w_in, ret_gn, w_out, norm_cross, norm_mem, w_cq, w_ck, w_cv, w_co, norm_ffn,
           wg, bg, we, be, w1, w3, w2):
    n_prompt, d = xp.shape
    n_tok = n_prompt + xs.shape[0]
    d_ret = d // 2
    d_sb = d - d_ret
    ret_heads = d_ret // RET_HEAD_DIM
    sds = jax.ShapeDtypeStruct

    pos = jnp.concatenate([jnp.arange(n_prompt, dtype=jnp.int32),
                           jnp.tile(past_len + jnp.arange(n_new, dtype=jnp.int32), n_seq)])
    half = RET_HEAD_DIM // 2
    inv = ROPE_BASE ** (-jnp.arange(half, dtype=F32) / half)
    ang = pos.astype(F32)[:, None] * inv[None, :]
    cos, sin = jnp.cos(ang), jnp.sin(ang)
    log_gamma = jnp.log1p(-jnp.exp2(-5.0 - jnp.arange(ret_heads, dtype=F32)))

    h = _rmsnorm_stacked(xp, xs, norm_mix, BF16)
    rot = [(cos, half), (sin, half)]
    bf = lambda n: [(sds((n_tok, n), BF16), None)]
    (rq,) = _matmul([h], w_in, 0, d_ret, _ep_rotary(1.0), rot, bf(d_ret), "in_proj_ret_q")
    (rk,) = _matmul([h], w_in, d_ret, d_ret, _ep_rotary(RET_HEAD_DIM ** -0.5), rot, bf(d_ret), "in_proj_ret_k")
    (rv,) = _matmul([h], w_in, 2 * d_ret, d_ret, _ep_store(1.0), [], bf(d_ret), "in_proj_ret_v")
    (rg,) = _matmul([h], w_in, 3 * d_ret, d_ret, _ep_store(1.0), [], [(sds((n_tok, d_ret), F32), None)],
                    "in_proj_ret_gate")
    (sq,) = _matmul([h], w_in, 4 * d_ret, d_sb, _ep_store(SB_HEAD_DIM ** -0.5 * LOG2_E), [], bf(d_sb),
                    "in_proj_sb_q")
    split = [(sds((n_prompt, d_sb), F32), PROMPT), (sds((n_tok - n_prompt, d_sb), F32), SAMPLE),
             (sds((n_tok, d_sb), BF16), None)]
    sk_p, sk_s, sk_b = _matmul([h], w_in, 4 * d_ret + d_sb, d_sb, _ep_store_split, [], split, "in_proj_sb_k",
                               n_prompt=n_prompt)
    sv_p, sv_s, sv_b = _matmul([h], w_in, 4 * d_ret + 2 * d_sb, d_sb, _ep_store_split, [], split,
                               "in_proj_sb_v", n_prompt=n_prompt)

    rows_p = _tile(n_prompt, RET_BLOCK)
    zeros_state = jnp.zeros((1, ret_heads, RET_HEAD_DIM, RET_HEAD_DIM), F32)
    yr, state_p = _retention(log_gamma, rq, rk, rv, rg, ret_gn, zeros_state, 0, 1, n_prompt // rows_p, rows_p)
    yr, state_s = _retention(log_gamma, rq, rk, rv, rg, ret_gn, state_ret, n_prompt, n_seq, 1, n_new,
                             out_buf=yr)
    sb = _sb_prompt(sq, sk_b, sv_b, n_prompt)
    sb = _sb_sample(sq, sk_b, sv_b, cache_sb_k.reshape(n_seq, past_len, d_sb),
                    cache_sb_v.reshape(n_seq, past_len, d_sb), n_prompt, n_new, sb)
    res = [(sds((n_tok, d), F32), None)]
    (x,) = _matmul([yr, sb], w_out, 0, d, _ep_residual_split, [(xp, PROMPT), (xs, SAMPLE)], res, "out_proj",
                   n_prompt=n_prompt)

    n_mem = mem_prompt.shape[0]
    m = _rmsnorm(mem_prompt, norm_mem, BF16)
    mem_out = [(sds((n_mem, d), F32), None)]
    (mk,) = _matmul([m], w_ck, 0, d, _ep_store(1.0), [], mem_out, "mem_k")
    (mv,) = _matmul([m], w_cv, 0, d, _ep_store(1.0), [], mem_out, "mem_v")
    h = _rmsnorm(x, norm_cross, BF16)
    (cq,) = _matmul([h], w_cq, 0, d, _ep_store(1.0), [], bf(d), "cross_q")
    co = _cross_attend(cq, mk.reshape(1, n_mem, d), mv.reshape(1, n_mem, d), 0, n_prompt)
    co = _cross_attend(cq, cache_mem_k.reshape(n_seq, n_mem, d), cache_mem_v.reshape(n_seq, n_mem, d),
                       n_prompt, n_new, out_buf=co)
    (x,) = _matmul([co], w_co, 0, d, _ep_residual, [(x, None)], res, "cross_out")

    hn, ids, wts = _norm_and_route(x, norm_ffn, wg, bg, we, be)
    row_tok, blk_cnt, dest, blk_exp, blk_first, n_used = _route_layout(ids[:, :TOP_K], n_tok)
    x_rows = _dispatch_rows(hn, row_tok, blk_cnt, n_used)
    y_rows = _expert_ffn(x_rows, blk_exp, blk_first, n_used, w1, w3, w2)
    return x, wts, y_rows, dest, (sk_p, sv_p, sk_s, sv_s, state_p, state_s, mk, mv)


def kernel(x_prompt, x_sample, cache_sb_k, cache_sb_v, state_ret, cache_mem_k, cache_mem_v, mem_prompt, norm_mix, w_in, ret_gn, w_out, norm_cross, norm_mem, w_cq, w_ck, w_cv, w_co, norm_ffn, router_group_w, router_group_b, router_expert_w, router_expert_b, expert_w1, expert_w3, expert_w2, norm_final):
    batch, n_prompt, d = x_prompt.shape
    n_seq, n_new, _ = x_sample.shape
    depth, _, past_len, sb_heads, _ = cache_sb_k.shape
    n_mem = mem_prompt.shape[1]
    assert batch == 1 and depth == 1
    ret_heads = (d // 2) // RET_HEAD_DIM
    x, wts, y_rows, dest, (sk_p, sv_p, sk_s, sv_s, state_p, state_s, mk, mv) = _layer(
        x_prompt.reshape(n_prompt, d), x_sample.reshape(n_seq * n_new, d), n_seq, n_new, past_len,
        cache_sb_k, cache_sb_v, state_ret[0], cache_mem_k, cache_mem_v, mem_prompt[0], norm_mix[0], w_in,
        ret_gn[0], w_out, norm_cross[0], norm_mem[0], w_cq, w_ck, w_cv, w_co, norm_ffn[0], router_group_w[0],
        router_group_b[0], router_expert_w[0], router_expert_b[0], expert_w1[0], expert_w3[0], expert_w2[0])
    y_prompt, y_sample = _combine_and_norm(x, wts, y_rows, dest, norm_final, n_prompt)
    y_prompt = y_prompt.reshape(1, n_prompt, d)
    y_sample = y_sample.reshape(n_seq, n_new, d)
    sb_shape_p = (1, 1, n_prompt, sb_heads, SB_HEAD_DIM)
    sb_shape_s = (1, n_seq, n_new, sb_heads, SB_HEAD_DIM)
    mem_shape = (1, 1, n_mem, MEM_HEADS, d // MEM_HEADS)
    return (y_prompt, y_sample,
            sk_p.reshape(sb_shape_p), sv_p.reshape(sb_shape_p),
            state_p.reshape(1, 1, ret_heads, RET_HEAD_DIM, RET_HEAD_DIM),
            mk.reshape(mem_shape), mv.reshape(mem_shape),
            sk_s.reshape(sb_shape_s), sv_s.reshape(sb_shape_s),
            state_s.reshape(1, n_seq, ret_heads, RET_HEAD_DIM, RET_HEAD_DIM))
```

```python
import functools

import jax
import jax.numpy as jnp
from jax import lax
from jax.experimental import pallas as pl
from jax.experimental.pallas import tpu as pltpu

BF16 = jnp.bfloat16
F32 = jnp.float32

EPS = 1e-6
CHUNK = 64
RET_HEAD_DIM = 256
SB_HEAD_DIM = 128
MEM_HEADS = 4
N_GROUPS = 4
EXPERTS_PER_GROUP = 8
N_EXPERTS = N_GROUPS * EXPERTS_PER_GROUP
TOP_K = 2
ROPE_BASE = 10000.0

LANES = 128
VMEM_LIMIT_BYTES = 56 * 1024 * 1024
MATMUL_VMEM_BUDGET = 52 * 1024 * 1024
ROW_TILE = 512
COL_TILE = 1024
EXPERT_HIDDEN_TILE = 512
RET_BLOCK = 256
RET_HEADS_PER_STEP = 4
SB_BLOCK = 256
SB_HEADS_PER_STEP = 4
LOG2_E = 1.4426950408889634
SB_DEAD_LOG2 = -64.0
MOE_ROWS = 512
EXPERT_OUT_TILE = 2048

PROMPT = "prompt"
SAMPLE = "sample"

_NT = (((1,), (1,)), ((), ()))
_TN = (((0,), (0,)), ((), ()))


def _tile(n, pref):
    t = min(pref, n)
    while n % t:
        t //= 2
    return t


def _params(*sem):
    return pltpu.CompilerParams(dimension_semantics=sem, vmem_limit_bytes=VMEM_LIMIT_BYTES)


def _sigmoid(x):
    return 1.0 / (1.0 + jnp.exp(-x))


def _rmsnorm_rows(x, g):
    ms = jnp.mean(x * x, axis=-1, keepdims=True)
    return x * lax.rsqrt(ms + EPS) * g


def _rmsnorm_body(x_ref, g_ref, o_ref):
    o_ref[...] = _rmsnorm_rows(x_ref[...], g_ref[...]).astype(o_ref.dtype)


def _rmsnorm(x, g, out_dtype):
    m, d = x.shape
    tm = _tile(m, 256)
    return pl.pallas_call(
        _rmsnorm_body,
        out_shape=jax.ShapeDtypeStruct((m, d), out_dtype),
        grid=(m // tm,),
        in_specs=[pl.BlockSpec((tm, d), lambda i: (i, 0)), pl.BlockSpec((1, d), lambda i: (0, 0))],
        out_specs=pl.BlockSpec((tm, d), lambda i: (i, 0)),
        compiler_params=_params("parallel"),
        name="rmsnorm",
    )(x, g.reshape(1, d))


def _rmsnorm_stacked_body(xp_ref, xs_ref, g_ref, o_ref, *, prompt_tiles):
    is_prompt = pl.program_id(0) < prompt_tiles

    @pl.when(is_prompt)
    def _():
        o_ref[...] = _rmsnorm_rows(xp_ref[...], g_ref[...]).astype(o_ref.dtype)

    @pl.when(jnp.logical_not(is_prompt))
    def _():
        o_ref[...] = _rmsnorm_rows(xs_ref[...], g_ref[...]).astype(o_ref.dtype)


def _rmsnorm_stacked(xp, xs, g, out_dtype):
    (n_p, d), n_s = xp.shape, xs.shape[0]
    tm = _tile(n_s, _tile(n_p, 256))
    pt = n_p // tm
    return pl.pallas_call(
        functools.partial(_rmsnorm_stacked_body, prompt_tiles=pt),
        out_shape=jax.ShapeDtypeStruct((n_p + n_s, d), out_dtype),
        grid=((n_p + n_s) // tm,),
        in_specs=[pl.BlockSpec((tm, d), lambda i: (jnp.minimum(i, pt - 1), 0)),
                  pl.BlockSpec((tm, d), lambda i: (jnp.maximum(i - pt, 0), 0)),
                  pl.BlockSpec((1, d), lambda i: (0, 0))],
        out_specs=pl.BlockSpec((tm, d), lambda i: (i, 0)),
        compiler_params=_params("arbitrary"),
        name="rmsnorm_stacked",
    )(xp, xs, g.reshape(1, d))


def _matmul_body(*refs, n_lhs, n_extra, n_out, epilogue):
    x_refs = refs[:n_lhs]
    w_refs = refs[n_lhs:2 * n_lhs]
    extra = refs[2 * n_lhs:2 * n_lhs + n_extra]
    outs = refs[2 * n_lhs + n_extra:2 * n_lhs + n_extra + n_out]
    wb = refs[2 * n_lhs + n_extra + n_out:]

    @pl.when(pl.program_id(1) == 0)
    def _():
        for p in range(n_lhs):
            wb[p][...] = w_refs[p][...].astype(BF16)

    acc = jnp.dot(x_refs[0][...], wb[0][...], preferred_element_type=F32)
    for p in range(1, n_lhs):
        acc += jnp.dot(x_refs[p][...], wb[p][...], preferred_element_type=F32)
    epilogue(acc, extra, outs)


def _matmul(xs, w, col_off, n_cols, epilogue, extras, outs, name, n_prompt=None):
    m = xs[0].shape[0]
    tm = _tile(m, ROW_TILE) if n_prompt is None else _tile(m - n_prompt, _tile(n_prompt, ROW_TILE))
    k_total = sum(x.shape[1] for x in xs)

    def vmem_bytes(tn):
        tiles = sum((tn if kind is None or isinstance(kind, str) else kind) * jnp.dtype(a.dtype).itemsize
                    for a, kind in list(extras) + list(outs))
        return k_total * tn * (2 * 4 + 2) + 2 * tm * k_total * 2 + 2 * tm * tiles

    tn = _tile(n_cols, COL_TILE)
    while vmem_bytes(tn) > MATMUL_VMEM_BUDGET and tn > LANES:
        tn //= 2
    assert col_off % tn == 0
    jo = col_off // tn
    pt = None if n_prompt is None else n_prompt // tm
    in_specs, args = [], []
    for x in xs:
        in_specs.append(pl.BlockSpec((tm, x.shape[1]), lambda j, i: (i, 0)))
        args.append(x)
    row = 0
    for x in xs:
        kp = x.shape[1]
        assert row % kp == 0
        in_specs.append(pl.BlockSpec((None, kp, tn), lambda j, i, r=row // kp: (0, r, j + jo)))
        args.append(w)
        row += kp
    assert row == w.shape[1]

    def spec(kind):
        if kind is None:
            return pl.BlockSpec((tm, tn), lambda j, i: (i, j))
        if kind == PROMPT:
            return pl.BlockSpec((tm, tn), lambda j, i: (jnp.minimum(i, pt - 1), j))
        if kind == SAMPLE:
            return pl.BlockSpec((tm, tn), lambda j, i: (jnp.maximum(i - pt, 0), j))
        return pl.BlockSpec((tm, kind), lambda j, i: (i, 0))

    for a, kind in extras:
        in_specs.append(spec(kind))
        args.append(a)
    body = functools.partial(_matmul_body, n_lhs=len(xs), n_extra=len(extras), n_out=len(outs),
                             epilogue=epilogue if n_prompt is None else functools.partial(epilogue, pt))
    res = pl.pallas_call(
        body,
        out_shape=[o for o, _ in outs],
        grid=(n_cols // tn, m // tm),
        in_specs=in_specs,
        out_specs=[spec(kind) for _, kind in outs],
        scratch_shapes=[pltpu.VMEM((x.shape[1], tn), BF16) for x in xs],
        compiler_params=_params("arbitrary", "arbitrary"),
        name=name,
    )(*args)
    return res


def _ep_store(scale):
    def ep(acc, extra, outs):
        for o in outs:
            o[...] = (acc * scale).astype(o.dtype) if scale != 1.0 else acc.astype(o.dtype)
    return ep


def _ep_store_split(prompt_tiles, acc, extra, outs):
    is_prompt = pl.program_id(1) < prompt_tiles

    @pl.when(is_prompt)
    def _():
        outs[0][...] = acc

    @pl.when(jnp.logical_not(is_prompt))
    def _():
        outs[1][...] = acc

    outs[2][...] = acc.astype(outs[2].dtype)


def _ep_residual(acc, extra, outs):
    outs[0][...] = extra[0][...] + acc


def _ep_residual_split(prompt_tiles, acc, extra, outs):
    res = jnp.where(pl.program_id(1) < prompt_tiles, extra[0][...], extra[1][...])
    outs[0][...] = res + acc


def _ep_rotary(scale):
    half = RET_HEAD_DIM // 2

    def ep(acc, extra, outs):
        cos = extra[0][...]
        sin = extra[1][...]
        for h in range(acc.shape[1] // RET_HEAD_DIM):
            lo = h * RET_HEAD_DIM
            x1 = acc[:, lo:lo + half]
            x2 = acc[:, lo + half:lo + RET_HEAD_DIM]
            outs[0][:, lo:lo + half] = ((x1 * cos - x2 * sin) * scale).astype(outs[0].dtype)
            outs[0][:, lo + half:lo + RET_HEAD_DIM] = ((x1 * sin + x2 * cos) * scale).astype(outs[0].dtype)
    return ep


def _retention_body(lg_ref, q_ref, k_ref, v_ref, g_ref, gn_ref, s0_ref, *rest, rows, n_heads):
    o_ref, sout_ref, s_scr, decay_scr = rest[-4:]
    heads = range(n_heads)
    cols = lambda h: slice(h * RET_HEAD_DIM, (h + 1) * RET_HEAD_DIM)
    lgs = [lg_ref[pl.program_id(1) * n_heads + h] for h in heads]

    @pl.when(pl.program_id(2) == 0)
    def _():
        s_scr[...] = s0_ref[...]
        ii = lax.broadcasted_iota(jnp.int32, (rows, rows), 0)
        jj = lax.broadcasted_iota(jnp.int32, (rows, rows), 1)
        shift = CHUNK.bit_length() - 1
        dist = jnp.abs(ii - jj).astype(F32)
        for h in heads:
            decay_scr[h] = jnp.where((jj >> shift) <= (ii >> shift), jnp.exp(lgs[h] * dist), 0.0)

    idx = lax.broadcasted_iota(jnp.int32, (rows, 1), 0).astype(F32)
    qs = [q_ref[:, cols(h)] for h in heads]
    ks = [k_ref[:, cols(h)] for h in heads]
    vs = [v_ref[:, cols(h)] for h in heads]
    scores = [lax.dot_general(qs[h], ks[h], _NT, preferred_element_type=F32) * decay_scr[h] for h in heads]
    outs = [jnp.dot(scores[h].astype(BF16), vs[h], preferred_element_type=F32) for h in heads]
    states = [s_scr[h] for h in heads]
    outs = [outs[h] + jnp.exp(lgs[h] * (idx + 1.0))
            * jnp.dot(qs[h], states[h].astype(BF16), preferred_element_type=F32) for h in heads]
    kds = [(ks[h].astype(F32) * jnp.exp(lgs[h] * (rows - 1.0 - idx))).astype(BF16) for h in heads]
    for h in heads:
        s_new = (jnp.exp(lgs[h] * rows) * states[h]
                 + lax.dot_general(kds[h], vs[h], _TN, preferred_element_type=F32))
        s_scr[h] = s_new
        sout_ref[h] = s_new
    for h in heads:
        mu = jnp.mean(outs[h], axis=-1, keepdims=True)
        cen = outs[h] - mu
        var = jnp.mean(cen * cen, axis=-1, keepdims=True)
        yr = cen * lax.rsqrt(var + EPS) * gn_ref[h]
        g = g_ref[:, cols(h)]
        o_ref[:, cols(h)] = (g * _sigmoid(g) * yr).astype(o_ref.dtype)


def _into(out_buf):
    if out_buf is None:
        return [], []
    return [out_buf], [pl.BlockSpec(memory_space=pl.ANY)]


def _retention(log_gamma, q, k, v, gate, gn, s0, row_off, n_batch, n_steps, rows, out_buf=None):
    heads = q.shape[1] // RET_HEAD_DIM
    nh = min(RET_HEADS_PER_STEP, heads)
    assert row_off % rows == 0 and rows % CHUNK == 0 and heads % nh == 0
    ro = row_off // rows
    tok = lambda b, h, c, lg: (ro + b * n_steps + c, h)
    blk = pl.BlockSpec((rows, nh * RET_HEAD_DIM), tok)
    st = pl.BlockSpec((None, nh, RET_HEAD_DIM, RET_HEAD_DIM), lambda b, h, c, lg: (b, h, 0, 0))
    args = [log_gamma, q, k, v, gate, gn.reshape(heads, 1, RET_HEAD_DIM), s0]
    extra_args, extra_specs = _into(out_buf)
    grid_spec = pltpu.PrefetchScalarGridSpec(
        num_scalar_prefetch=1,
        grid=(n_batch, heads // nh, n_steps),
        in_specs=[blk, blk, blk, blk,
                  pl.BlockSpec((nh, 1, RET_HEAD_DIM), lambda b, h, c, lg: (h, 0, 0)), st] + extra_specs,
        out_specs=[blk, st],
        scratch_shapes=[pltpu.VMEM((nh, RET_HEAD_DIM, RET_HEAD_DIM), F32), pltpu.VMEM((nh, rows, rows), F32)],
    )
    return pl.pallas_call(
        functools.partial(_retention_body, rows=rows, n_heads=nh),
        out_shape=[jax.ShapeDtypeStruct((q.shape[0], heads * RET_HEAD_DIM), BF16),
                   jax.ShapeDtypeStruct((n_batch, heads, RET_HEAD_DIM, RET_HEAD_DIM), F32)],
        grid_spec=grid_spec,
        input_output_aliases={len(args): 0} if extra_args else {},
        compiler_params=_params("arbitrary", "arbitrary", "arbitrary"),
        name="retention",
    )(*args, *extra_args)


def _neg_suffix(n):
    j = lax.broadcasted_iota(jnp.int32, (2 * n, n), 0) & (n - 1)
    s = lax.broadcasted_iota(jnp.int32, (2 * n, n), 1)
    return jnp.where(j >= s, -1.0, 0.0).astype(BF16)


def _lanes_to(x, n):
    return x[:, :n] if n <= LANES else jnp.concatenate([x] * (n // LANES), axis=1)


def _suffix_sums(fail, neg_ones):
    tk = fail.shape[1]
    hi, lo = _split_bf16(fail)
    if tk % LANES == 0:
        return jnp.dot(jnp.concatenate([hi, lo], axis=1), neg_ones, preferred_element_type=F32)
    return (jnp.dot(hi, neg_ones[:tk], preferred_element_type=F32)
            + jnp.dot(lo, neg_ones[:tk], preferred_element_type=F32))


def _sb_tiles(qs, kbs, vbs, neg_ones, carry, acc, masked):
    heads = range(len(qs))
    tk = kbs[0].shape[0]
    zs = [lax.dot_general(qs[h], kbs[h], _NT, preferred_element_type=F32) for h in heads]
    fails = [jnp.maximum(z, 0.0) + jnp.log2(1.0 + jnp.exp2(-jnp.abs(z))) for z in zs]
    if masked:
        t = lax.broadcasted_iota(jnp.int32, zs[0].shape, 0)
        s = lax.broadcasted_iota(jnp.int32, zs[0].shape, 1)
        valid = s < t
        fails = [jnp.where(valid, f, 0.0) for f in fails]
    suffixes = [_suffix_sums(f, neg_ones) for f in fails]
    probs = [jnp.exp2(zs[h] + suffixes[h] + _lanes_to(carry[h], tk)) for h in heads]
    if masked:
        probs = [jnp.where(valid, p, 0.0) for p in probs]
    for h in heads:
        acc[h] += jnp.dot(probs[h].astype(BF16), vbs[h], preferred_element_type=F32)
        carry[h] += jnp.broadcast_to(suffixes[h][:, :1], carry.shape[1:])


def _sb_body(q_ref, kd_ref, vd_ref, kp_ref, vp_ref, *rest, tk, n_heads, past_per_step, past_static):
    o_ref, carry, acc = rest[-3:]
    tq = q_ref.shape[0]
    n_past = past_static if past_static is not None else pl.program_id(1) * past_per_step
    cols = lambda h: slice(h * SB_HEAD_DIM, (h + 1) * SB_HEAD_DIM)
    heads = range(n_heads)
    carry[...] = jnp.zeros(carry.shape, F32)
    acc[...] = jnp.zeros(acc.shape, F32)
    queries = lambda: [q_ref[:, cols(h)] for h in heads]
    _sb_tiles(queries(), [kd_ref[:, cols(h)].astype(BF16) for h in heads],
              [vd_ref[:, cols(h)].astype(BF16) for h in heads], _neg_suffix(tq), carry, acc, True)
    neg_ones = _neg_suffix(tk)

    alive = lambda: jnp.max(carry[...]) > SB_DEAD_LOG2

    def step(state):
        it = state[0]
        rows = pl.ds(pl.multiple_of((n_past - 1 - it) * tk, tk), tk)
        _sb_tiles(queries(), [kp_ref[rows, cols(h)].astype(BF16) for h in heads],
                  [vp_ref[rows, cols(h)].astype(BF16) for h in heads], neg_ones, carry, acc, False)
        return it + 1, alive()

    lax.while_loop(lambda state: (state[0] < n_past) & state[1], step, (jnp.int32(0), alive()))
    for h in range(n_heads):
        o_ref[:, cols(h)] = acc[h].astype(o_ref.dtype)


def _sb_scratch(n_heads, tq):
    return [pltpu.VMEM((n_heads, tq, LANES), F32), pltpu.VMEM((n_heads, tq, SB_HEAD_DIM), F32)]


def _sb_prompt(q, k, v, n_tok):
    heads = q.shape[1] // SB_HEAD_DIM
    tq = _tile(n_tok, SB_BLOCK)
    nh = SB_HEADS_PER_STEP
    width = nh * SB_HEAD_DIM
    blk = pl.BlockSpec((tq, width), lambda h, i: (i, h))
    full = pl.BlockSpec((n_tok, width), lambda h, i: (0, h))
    return pl.pallas_call(
        functools.partial(_sb_body, tk=tq, n_heads=nh, past_per_step=1, past_static=None),
        out_shape=jax.ShapeDtypeStruct((q.shape[0], heads * SB_HEAD_DIM), BF16),
        grid=(heads // nh, n_tok // tq),
        in_specs=[blk, blk, blk, full, full],
        out_specs=blk,
        scratch_shapes=_sb_scratch(nh, tq),
        compiler_params=_params("parallel", "arbitrary"),
        name="stick_breaking_prompt",
    )(q, k, v, k, v)


def _sb_sample(q, k, v, cache_k, cache_v, row_off, n_new, out_buf):
    heads = q.shape[1] // SB_HEAD_DIM
    n_batch, past, _ = cache_k.shape
    tk = _tile(past, SB_BLOCK)
    assert row_off % n_new == 0
    ro = row_off // n_new
    nh = SB_HEADS_PER_STEP
    width = nh * SB_HEAD_DIM
    blk = pl.BlockSpec((n_new, width), lambda b, h: (ro + b, h))
    full = pl.BlockSpec((None, past, width), lambda b, h: (b, 0, h))
    extra_args, extra_specs = _into(out_buf)
    return pl.pallas_call(
        functools.partial(_sb_body, tk=tk, n_heads=nh, past_per_step=0, past_static=past // tk),
        out_shape=jax.ShapeDtypeStruct(out_buf.shape, out_buf.dtype),
        grid=(n_batch, heads // nh),
        in_specs=[blk, blk, blk, full, full] + extra_specs,
        out_specs=blk,
        scratch_shapes=_sb_scratch(nh, n_new),
        input_output_aliases={5: 0},
        compiler_params=_params("parallel", "arbitrary"),
        name="stick_breaking_sample",
    )(q, k, v, cache_k, cache_v, *extra_args)


def _cross_body(q_ref, k_ref, v_ref, *rest, scale):
    o_ref = rest[-1]
    s = lax.dot_general(q_ref[...], k_ref[...].astype(BF16), _NT, preferred_element_type=F32) * scale
    e = jnp.exp(s - jnp.max(s, axis=-1, keepdims=True))
    p = e / jnp.sum(e, axis=-1, keepdims=True)
    o_ref[...] = jnp.dot(p.astype(BF16), v_ref[...].astype(BF16),
                         preferred_element_type=F32).astype(o_ref.dtype)


def _cross_attend(q, mem_k, mem_v, row_off, rows_per_seq, out_buf=None):
    n_seq, n_mem, d = mem_k.shape
    dh = d // MEM_HEADS
    tm = _tile(rows_per_seq, ROW_TILE)
    steps = rows_per_seq // tm
    assert row_off % tm == 0
    ro = row_off // tm
    mem = pl.BlockSpec((None, n_mem, dh), lambda i, h: (i // steps, 0, h))
    blk = pl.BlockSpec((tm, dh), lambda i, h: (ro + i, h))
    extra_args, extra_specs = _into(out_buf)
    return pl.pallas_call(
        functools.partial(_cross_body, scale=dh ** -0.5),
        out_shape=jax.ShapeDtypeStruct((q.shape[0], d), BF16),
        grid=(n_seq * steps, MEM_HEADS),
        in_specs=[blk, mem, mem] + extra_specs,
        out_specs=blk,
        input_output_aliases={3: 0} if extra_args else {},
        compiler_params=_params("parallel", "arbitrary"),
        name="cross_attention",
    )(q, mem_k, mem_v, *extra_args)


def _split_bf16(x):
    hi = x.astype(BF16)
    return hi, (x - hi.astype(F32)).astype(BF16)


def _router_body(x_ref, g_ref, wr_ref, br_ref, h_ref, ids_ref, wts_ref):
    h = _rmsnorm_rows(x_ref[...], g_ref[...])
    h_ref[...] = _pack_bf16_pairs(h)
    h_hi, h_lo = _split_bf16(h)
    w_hi, w_lo = _split_bf16(wr_ref[...])
    logits = (jnp.dot(h_hi, w_hi, preferred_element_type=F32)
              + jnp.dot(h_hi, w_lo, preferred_element_type=F32)
              + jnp.dot(h_lo, w_hi, preferred_element_type=F32)) + br_ref[...]
    lane = lax.broadcasted_iota(jnp.int32, logits.shape, 1).astype(F32)
    big = float(LANES)
    neg = -jnp.inf

    def first_max(vals):
        top = jnp.max(vals, axis=-1, keepdims=True)
        return top, jnp.min(jnp.where(vals == top, lane, big), axis=-1, keepdims=True)

    is_group = lane < N_GROUPS
    g_max, g_top = first_max(jnp.where(is_group, logits, neg))
    g_w = 1.0 / jnp.sum(jnp.where(is_group, jnp.exp(logits - g_max), 0.0), axis=-1, keepdims=True)
    first = N_GROUPS + EXPERTS_PER_GROUP * g_top
    cand = jnp.where((lane >= first) & (lane < first + EXPERTS_PER_GROUP), logits, neg)
    v1, i1 = first_max(cand)
    v2, i2 = first_max(jnp.where(lane == i1, neg, cand))
    t = jnp.exp(v2 - v1)
    w1 = g_w / (1.0 + t)
    w2 = g_w * t / (1.0 + t)
    ids = jnp.where(lane == 0.0, i1 - N_GROUPS, jnp.where(lane == 1.0, i2 - N_GROUPS, 0.0))
    ids_ref[...] = ids.astype(jnp.int32)
    wts_ref[...] = jnp.where(lane == 0.0, w1, jnp.where(lane == 1.0, w2, 0.0))


def _norm_and_route(x, g, wg, bg, we, be):
    m, d = x.shape
    tm = _tile(m, 256)
    pad = LANES - N_GROUPS - N_EXPERTS
    wr = jnp.concatenate([wg, we, jnp.zeros((d, pad), F32)], axis=1)
    br = jnp.concatenate([bg, be, jnp.zeros((pad,), F32)]).reshape(1, LANES)
    row = lambda c: pl.BlockSpec((tm, c), lambda i: (i, 0))
    return pl.pallas_call(
        _router_body,
        out_shape=[jax.ShapeDtypeStruct((m, d // 2), jnp.int32),
                   jax.ShapeDtypeStruct((m, LANES), jnp.int32),
                   jax.ShapeDtypeStruct((m, LANES), F32)],
        grid=(m // tm,),
        in_specs=[row(d), pl.BlockSpec((1, d), lambda i: (0, 0)),
                  pl.BlockSpec((d, LANES), lambda i: (0, 0)), pl.BlockSpec((1, LANES), lambda i: (0, 0))],
        out_specs=[row(d // 2), row(LANES), row(LANES)],
        compiler_params=_params("parallel"),
        name="norm_route",
    )(x, g.reshape(1, d), wr, br)


GATHER_UNROLL = 8


def _start_row_gather(idx_ref, first, stride, src_ref, buf, sem, n_rows):
    def one(r, _):
        t = idx_ref[first + r * stride]
        pltpu.make_async_copy(src_ref.at[pl.ds(t, 1)], buf.at[pl.ds(r, 1)], sem).start()
        return 0

    def group(g, _):
        for u in range(GATHER_UNROLL):
            one(g * GATHER_UNROLL + u, 0)
        return 0

    groups = n_rows // GATHER_UNROLL
    lax.fori_loop(0, groups, group, 0)
    lax.fori_loop(groups * GATHER_UNROLL, n_rows, one, 0)


def _wait_row_gather(src_ref, buf, sem, n_rows):
    pltpu.make_async_copy(src_ref.at[pl.ds(0, n_rows)], buf.at[pl.ds(0, n_rows)], sem).wait()


def _pack_bf16_pairs(x):
    half = x.shape[1] // 2
    hi = lax.bitcast_convert_type(x[:, :half].astype(BF16).astype(F32), jnp.int32)
    lo = lax.bitcast_convert_type(x[:, half:].astype(BF16).astype(F32), jnp.int32)
    return hi | lax.shift_right_logical(lo, jnp.int32(16))


def _unpack_bf16_pairs(p):
    hi = lax.bitcast_convert_type(p & jnp.int32(-65536), F32).astype(BF16)
    lo = lax.bitcast_convert_type(lax.shift_left(p, jnp.int32(16)), F32).astype(BF16)
    return hi, lo


def _dispatch_body(idx_ref, used_ref, cnt_ref, src_ref, o_ref, buf, sem):
    b = pl.program_id(0)
    rows = buf.shape[1]

    copied = lambda blk: pl.multiple_of(pl.cdiv(cnt_ref[blk], GATHER_UNROLL) * GATHER_UNROLL, GATHER_UNROLL)

    def start(blk):
        _start_row_gather(idx_ref, blk * rows, 1, src_ref, buf.at[blk % 2], sem.at[blk % 2], copied(blk))

    @pl.when(b == 0)
    def _():
        buf[...] = jnp.zeros(buf.shape, buf.dtype)
        start(0)

    @pl.when(b + 1 < used_ref[0])
    def _():
        start(b + 1)

    @pl.when(b < used_ref[0])
    def _():
        slot = b % 2
        cnt = cnt_ref[b]
        _wait_row_gather(src_ref, buf.at[slot], sem.at[slot], copied(b))
        half = buf.shape[2]
        row = lax.broadcasted_iota(jnp.int32, (rows, 1), 0)
        hi, lo = _unpack_bf16_pairs(jnp.where(row < cnt, buf[slot], 0))
        o_ref[:, :half] = hi
        o_ref[:, half:] = lo


def _dispatch_rows(src, row_tok, blk_cnt, n_used):
    n_rows = row_tok.shape[0]
    half = src.shape[1]
    grid_spec = pltpu.PrefetchScalarGridSpec(
        num_scalar_prefetch=3,
        grid=(n_rows // MOE_ROWS,),
        in_specs=[pl.BlockSpec(memory_space=pl.ANY)],
        out_specs=pl.BlockSpec((MOE_ROWS, 2 * half), lambda b, idx, used, cnt: (b, 0)),
        scratch_shapes=[pltpu.VMEM((2, MOE_ROWS, half), src.dtype), pltpu.SemaphoreType.DMA((2,))],
    )
    return pl.pallas_call(
        _dispatch_body,
        out_shape=jax.ShapeDtypeStruct((n_rows, 2 * half), BF16),
        grid_spec=grid_spec,
        compiler_params=_params("arbitrary"),
        name="dispatch_rows",
    )(row_tok, n_used, blk_cnt, src)


def _expert_up_body(be_ref, first_ref, used_ref, x_ref, w1_ref, w3_ref, h_ref, w1b, w3b):
    b = pl.program_id(1)

    @pl.when(first_ref[b] == 1)
    def _():
        w1b[...] = w1_ref[...].astype(BF16)
        w3b[...] = w3_ref[...].astype(BF16)

    @pl.when(b < used_ref[0])
    def _():
        x = x_ref[...]
        a = jnp.dot(x, w1b[...], preferred_element_type=F32)
        g = jnp.dot(x, w3b[...], preferred_element_type=F32)
        h_ref[...] = (a * _sigmoid(a) * g).astype(h_ref.dtype)


def _expert_down_body(be_ref, first_ref, used_ref, h_ref, w2_ref, y_ref, w2b):
    b = pl.program_id(1)

    @pl.when(first_ref[b] == 1)
    def _():
        w2b[...] = w2_ref[...].astype(BF16)

    @pl.when(b < used_ref[0])
    def _():
        y_ref[...] = jnp.dot(h_ref[...], w2b[...], preferred_element_type=F32)


def _expert_ffn(xs, blk_exp, blk_first, n_used, w1, w3, w2):
    n_rows, d = xs.shape
    f = w1.shape[2]
    n_blocks = n_rows // MOE_ROWS
    tf = _tile(f, EXPERT_HIDDEN_TILE)
    tn = _tile(d, EXPERT_OUT_TILE)
    last = lambda b, used: jnp.minimum(b, used[0] - 1)
    up_spec = pltpu.PrefetchScalarGridSpec(
        num_scalar_prefetch=3,
        grid=(f // tf, n_blocks),
        in_specs=[pl.BlockSpec((MOE_ROWS, d), lambda j, b, be, fi, us: (last(b, us), 0)),
                  pl.BlockSpec((None, d, tf), lambda j, b, be, fi, us: (be[b], 0, j)),
                  pl.BlockSpec((None, d, tf), lambda j, b, be, fi, us: (be[b], 0, j))],
        out_specs=pl.BlockSpec((MOE_ROWS, tf), lambda j, b, be, fi, us: (b, j)),
        scratch_shapes=[pltpu.VMEM((d, tf), BF16), pltpu.VMEM((d, tf), BF16)],
    )
    hidden = pl.pallas_call(
        _expert_up_body,
        out_shape=jax.ShapeDtypeStruct((n_rows, f), BF16),
        grid_spec=up_spec,
        compiler_params=_params("arbitrary", "arbitrary"),
        name="expert_up",
    )(blk_exp, blk_first, n_used, xs, w1, w3)
    down_spec = pltpu.PrefetchScalarGridSpec(
        num_scalar_prefetch=3,
        grid=(d // tn, n_blocks),
        in_specs=[pl.BlockSpec((MOE_ROWS, f), lambda j, b, be, fi, us: (last(b, us), 0)),
                  pl.BlockSpec((None, f, tn), lambda j, b, be, fi, us: (be[b], 0, j))],
        out_specs=pl.BlockSpec((MOE_ROWS, tn), lambda j, b, be, fi, us: (b, j)),
        scratch_shapes=[pltpu.VMEM((f, tn), BF16)],
    )
    return pl.pallas_call(
        _expert_down_body,
        out_shape=jax.ShapeDtypeStruct((n_rows, d), F32),
        grid_spec=down_spec,
        compiler_params=_params("arbitrary", "arbitrary"),
        name="expert_down",
    )(blk_exp, blk_first, n_used, hidden, w2)


def _route_layout(ids, n_tok):
    n_assign = n_tok * TOP_K
    flat_e = ids.reshape(n_assign)
    flat_t = jnp.repeat(jnp.arange(n_tok, dtype=jnp.int32), TOP_K)
    onehot = (flat_e[:, None] == jnp.arange(N_EXPERTS, dtype=jnp.int32)[None, :]).astype(jnp.int32)
    running = jnp.cumsum(onehot, axis=0)
    rank = jnp.take_along_axis(running, flat_e[:, None], axis=1)[:, 0] - 1
    counts = running[-1]
    padded = (counts + MOE_ROWS - 1) // MOE_ROWS * MOE_ROWS
    pad_end = jnp.cumsum(padded)
    dest = (pad_end - padded)[flat_e] + rank
    n_blocks = -(-n_assign // MOE_ROWS) + N_EXPERTS
    n_rows = n_blocks * MOE_ROWS
    row_tok = jnp.zeros((n_rows,), jnp.int32).at[dest].set(flat_t)
    blk = jnp.arange(n_blocks, dtype=jnp.int32)
    n_used = (pad_end[-1] // MOE_ROWS).astype(jnp.int32)
    blk_exp = jnp.searchsorted(pad_end, jnp.minimum(blk, n_used - 1) * MOE_ROWS, side='right').astype(jnp.int32)
    blk_exp = jnp.minimum(blk_exp, N_EXPERTS - 1)
    prev = jnp.concatenate([jnp.full((1,), -1, jnp.int32), blk_exp[:-1]])
    blk_first = ((blk_exp != prev) & (blk < n_used)).astype(jnp.int32)
    into_range = blk * MOE_ROWS - (pad_end - padded)[blk_exp]
    blk_cnt = jnp.where(blk < n_used, jnp.clip(counts[blk_exp] - into_range, 0, MOE_ROWS), 0).astype(jnp.int32)
    return row_tok, blk_cnt, dest.astype(jnp.int32), blk_exp, blk_first, n_used.reshape(1)


def _final_body(dest_ref, x_ref, w_ref, y_ref, g_ref, op_ref, os_ref, buf, sem, *, prompt_tiles):
    i = pl.program_id(0)
    tm = buf.shape[2]

    def start(blk):
        for k in range(TOP_K):
            _start_row_gather(dest_ref, blk * tm * TOP_K + k, TOP_K, y_ref, buf.at[blk % 2, k], sem.at[blk % 2, k],
                              tm)

    @pl.when(i == 0)
    def _():
        start(0)

    @pl.when(i + 1 < pl.num_programs(0))
    def _():
        start(i + 1)

    slot = i % 2
    for k in range(TOP_K):
        _wait_row_gather(y_ref, buf.at[slot, k], sem.at[slot, k], tm)
    w = w_ref[...]
    moe = buf[slot, 0] * w[:, 0:1] + buf[slot, 1] * w[:, 1:2]
    y = _rmsnorm_rows(x_ref[...] + moe, g_ref[...])

    @pl.when(i < prompt_tiles)
    def _():
        op_ref[...] = y

    @pl.when(i >= prompt_tiles)
    def _():
        os_ref[...] = y


def _combine_and_norm(x, wts, y_rows, dest, g, n_prompt):
    m, d = x.shape
    tm = _tile(m - n_prompt, _tile(n_prompt, 256))
    assert n_prompt % tm == 0
    pt = n_prompt // tm
    grid_spec = pltpu.PrefetchScalarGridSpec(
        num_scalar_prefetch=1,
        grid=(m // tm,),
        in_specs=[pl.BlockSpec((tm, d), lambda i, dest: (i, 0)),
                  pl.BlockSpec((tm, LANES), lambda i, dest: (i, 0)),
                  pl.BlockSpec(memory_space=pl.ANY),
                  pl.BlockSpec((1, d), lambda i, dest: (0, 0))],
        out_specs=[pl.BlockSpec((tm, d), lambda i, dest: (jnp.minimum(i, pt - 1), 0)),
                   pl.BlockSpec((tm, d), lambda i, dest: (jnp.maximum(i - pt, 0), 0))],
        scratch_shapes=[pltpu.VMEM((2, TOP_K, tm, d), F32), pltpu.SemaphoreType.DMA((2, TOP_K))],
    )
    return pl.pallas_call(
        functools.partial(_final_body, prompt_tiles=pt),
        out_shape=[jax.ShapeDtypeStruct((n_prompt, d), F32), jax.ShapeDtypeStruct((m - n_prompt, d), F32)],
        grid_spec=grid_spec,
        compiler_params=_params("arbitrary"),
        name="combine_norm",
    )(dest, x, wts, y_rows, g.reshape(1, d))


def _layer(xp, xs, n_seq, n_new, past_len, cache_sb_k, cache_sb_v, state_ret, cache_mem_k, cache_mem_v,
           mem_prompt, norm_mix, w_in, ret_gn, w_out, norm_cross, norm_mem, w_cq, w_ck, w_cv, w_co, norm_ffn,
           wg, bg, we, be, w1, w3, w2):
    n_prompt, d = xp.shape
    n_tok = n_prompt + xs.shape[0]
    d_ret = d // 2
    d_sb = d - d_ret
    ret_heads = d_ret // RET_HEAD_DIM
    sds = jax.ShapeDtypeStruct

    pos = jnp.concatenate([jnp.arange(n_prompt, dtype=jnp.int32),
                           jnp.tile(past_len + jnp.arange(n_new, dtype=jnp.int32), n_seq)])
    half = RET_HEAD_DIM // 2
    inv = ROPE_BASE ** (-jnp.arange(half, dtype=F32) / half)
    ang = pos.astype(F32)[:, None] * inv[None, :]
    cos, sin = jnp.cos(ang), jnp.sin(ang)
    log_gamma = jnp.log1p(-jnp.exp2(-5.0 - jnp.arange(ret_heads, dtype=F32)))

    h = _rmsnorm_stacked(xp, xs, norm_mix, BF16)
    rot = [(cos, half), (sin, half)]
    bf = lambda n: [(sds((n_tok, n), BF16), None)]
    (rq,) = _matmul([h], w_in, 0, d_ret, _ep_rotary(1.0), rot, bf(d_ret), "in_proj_ret_q")
    (rk,) = _matmul([h], w_in, d_ret, d_ret, _ep_rotary(RET_HEAD_DIM ** -0.5), rot, bf(d_ret), "in_proj_ret_k")
    (rv,) = _matmul([h], w_in, 2 * d_ret, d_ret, _ep_store(1.0), [], bf(d_ret), "in_proj_ret_v")
    (rg,) = _matmul([h], w_in, 3 * d_ret, d_ret, _ep_store(1.0), [], [(sds((n_tok, d_ret), F32), None)],
                    "in_proj_ret_gate")
    (sq,) = _matmul([h], w_in, 4 * d_ret, d_sb, _ep_store(SB_HEAD_DIM ** -0.5 * LOG2_E), [], bf(d_sb),
                    "in_proj_sb_q")
    split = [(sds((n_prompt, d_sb), F32), PROMPT), (sds((n_tok - n_prompt, d_sb), F32), SAMPLE),
             (sds((n_tok, d_sb), BF16), None)]
    sk_p, sk_s, sk_b = _matmul([h], w_in, 4 * d_ret + d_sb, d_sb, _ep_store_split, [], split, "in_proj_sb_k",
                               n_prompt=n_prompt)
    sv_p, sv_s, sv_b = _matmul([h], w_in, 4 * d_ret + 2 * d_sb, d_sb, _ep_store_split, [], split,
                               "in_proj_sb_v", n_prompt=n_prompt)

    rows_p = _tile(n_prompt, RET_BLOCK)
    zeros_state = jnp.zeros((1, ret_heads, RET_HEAD_DIM, RET_HEAD_DIM), F32)
    yr, state_p = _retention(log_gamma, rq, rk, rv, rg, ret_gn, zeros_state, 0, 1, n_prompt // rows_p, rows_p)
    yr, state_s = _retention(log_gamma, rq, rk, rv, rg, ret_gn, state_ret, n_prompt, n_seq, 1, n_new,
                             out_buf=yr)
    sb = _sb_prompt(sq, sk_b, sv_b, n_prompt)
    sb = _sb_sample(sq, sk_b, sv_b, cache_sb_k.reshape(n_seq, past_len, d_sb),
                    cache_sb_v.reshape(n_seq, past_len, d_sb), n_prompt, n_new, sb)
    res = [(sds((n_tok, d), F32), None)]
    (x,) = _matmul([yr, sb], w_out, 0, d, _ep_residual_split, [(xp, PROMPT), (xs, SAMPLE)], res, "out_proj",
                   n_prompt=n_prompt)

    n_mem = mem_prompt.shape[0]
    m = _rmsnorm(mem_prompt, norm_mem, BF16)
    mem_out = [(sds((n_mem, d), F32), None)]
    (mk,) = _matmul([m], w_ck, 0, d, _ep_store(1.0), [], mem_out, "mem_k")
    (mv,) = _matmul([m], w_cv, 0, d, _ep_store(1.0), [], mem_out, "mem_v")
    h = _rmsnorm(x, norm_cross, BF16)
    (cq,) = _matmul([h], w_cq, 0, d, _ep_store(1.0), [], bf(d), "cross_q")
    co = _cross_attend(cq, mk.reshape(1, n_mem, d), mv.reshape(1, n_mem, d), 0, n_prompt)
    co = _cross_attend(cq, cache_mem_k.reshape(n_seq, n_mem, d), cache_mem_v.reshape(n_seq, n_mem, d),
                       n_prompt, n_new, out_buf=co)
    (x,) = _matmul([co], w_co, 0, d, _ep_residual, [(x, None)], res, "cross_out")

    hn, ids, wts = _norm_and_route(x, norm_ffn, wg, bg, we, be)
    row_tok, blk_cnt, dest, blk_exp, blk_first, n_used = _route_layout(ids[:, :TOP_K], n_tok)
    x_rows = _dispatch_rows(hn, row_tok, blk_cnt, n_used)
    y_rows = _expert_ffn(x_rows, blk_exp, blk_first, n_used, w1, w3, w2)
    return x, wts, y_rows, dest, (sk_p, sv_p, sk_s, sv_s, state_p, state_s, mk, mv)


def kernel(x_prompt, x_sample, cache_sb_k, cache_sb_v, state_ret, cache_mem_k, cache_mem_v, mem_prompt, norm_mix, w_in, ret_gn, w_out, norm_cross, norm_mem, w_cq, w_ck, w_cv, w_co, norm_ffn, router_group_w, router_group_b, router_expert_w, router_expert_b, expert_w1, expert_w3, expert_w2, norm_final):
    batch, n_prompt, d = x_prompt.shape
    n_seq, n_new, _ = x_sample.shape
    depth, _, past_len, sb_heads, _ = cache_sb_k.shape
    n_mem = mem_prompt.shape[1]
    assert batch == 1 and depth == 1
    ret_heads = (d // 2) // RET_HEAD_DIM
    x, wts, y_rows, dest, (sk_p, sv_p, sk_s, sv_s, state_p, state_s, mk, mv) = _layer(
        x_prompt.reshape(n_prompt, d), x_sample.reshape(n_seq * n_new, d), n_seq, n_new, past_len,
        cache_sb_k, cache_sb_v, state_ret[0], cache_mem_k, cache_mem_v, mem_prompt[0], norm_mix[0], w_in,
        ret_gn[0], w_out, norm_cross[0], norm_mem[0], w_cq, w_ck, w_cv, w_co, norm_ffn[0], router_group_w[0],
        router_group_b[0], router_expert_w[0], router_expert_b[0], expert_w1[0], expert_w3[0], expert_w2[0])
    y_prompt, y_sample = _combine_and_norm(x, wts, y_rows, dest, norm_final, n_prompt)
    y_prompt = y_prompt.reshape(1, n_prompt, d)
    y_sample = y_sample.reshape(n_seq, n_new, d)
    sb_shape_p = (1, 1, n_prompt, sb_heads, SB_HEAD_DIM)
    sb_shape_s = (1, n_seq, n_new, sb_heads, SB_HEAD_DIM)
    mem_shape = (1, 1, n_mem, MEM_HEADS, d // MEM_HEADS)
    return (y_prompt, y_sample,
            sk_p.reshape(sb_shape_p), sv_p.reshape(sb_shape_p),
            state_p.reshape(1, 1, ret_heads, RET_HEAD_DIM, RET_HEAD_DIM),
            mk.reshape(mem_shape), mv.reshape(mem_shape),
            sk_s.reshape(sb_shape_s), sv_s.reshape(sb_shape_s),
            state_s.reshape(1, n_seq, ret_heads, RET_HEAD_DIM, RET_HEAD_DIM))
```

```python
import functools

import jax
import jax.numpy as jnp
import numpy as np
from jax import lax
from jax.experimental import pallas as pl
from jax.experimental.pallas import tpu as pltpu

BF16 = jnp.bfloat16
F32 = jnp.float32

EPS = 1e-6
CHUNK = 64
RET_HEAD_DIM = 256
SB_HEAD_DIM = 128
MEM_HEADS = 4
N_GROUPS = 4
EXPERTS_PER_GROUP = 8
N_EXPERTS = N_GROUPS * EXPERTS_PER_GROUP
TOP_K = 2
ROPE_BASE = 10000.0

LANES = 128
VMEM_LIMIT_BYTES = 56 * 1024 * 1024
MATMUL_VMEM_BUDGET = 52 * 1024 * 1024
ROW_TILE = 512
COL_TILE = 1024
EXPERT_HIDDEN_TILE = 512
RET_BLOCK = 256
RET_HEADS_PER_STEP = 4
SB_BLOCK = 256
SB_HEADS_PER_STEP = 4
LOG2_E = 1.4426950408889634
SB_DEAD_LOG2 = -64.0
MOE_ROWS = 512
EXPERT_OUT_TILE = 2048

PROMPT = "prompt"
SAMPLE = "sample"

_NT = (((1,), (1,)), ((), ()))
_TN = (((0,), (0,)), ((), ()))


def _tile(n, pref):
    t = min(pref, n)
    while n % t:
        t //= 2
    return t


def _params(*sem):
    return pltpu.CompilerParams(dimension_semantics=sem, vmem_limit_bytes=VMEM_LIMIT_BYTES)


def _sigmoid(x):
    return 1.0 / (1.0 + jnp.exp(-x))


def _rmsnorm_rows(x, g):
    ms = jnp.mean(x * x, axis=-1, keepdims=True)
    return x * lax.rsqrt(ms + EPS) * g


def _rmsnorm_body(x_ref, g_ref, o_ref):
    o_ref[...] = _rmsnorm_rows(x_ref[...], g_ref[...]).astype(o_ref.dtype)


def _rmsnorm(x, g, out_dtype):
    m, d = x.shape
    tm = _tile(m, 256)
    return pl.pallas_call(
        _rmsnorm_body,
        out_shape=jax.ShapeDtypeStruct((m, d), out_dtype),
        grid=(m // tm,),
        in_specs=[pl.BlockSpec((tm, d), lambda i: (i, 0)), pl.BlockSpec((1, d), lambda i: (0, 0))],
        out_specs=pl.BlockSpec((tm, d), lambda i: (i, 0)),
        compiler_params=_params("parallel"),
        name="rmsnorm",
    )(x, g.reshape(1, d))


def _rmsnorm_stacked_body(xp_ref, xs_ref, g_ref, o_ref, *, prompt_tiles):
    is_prompt = pl.program_id(0) < prompt_tiles

    @pl.when(is_prompt)
    def _():
        o_ref[...] = _rmsnorm_rows(xp_ref[...], g_ref[...]).astype(o_ref.dtype)

    @pl.when(jnp.logical_not(is_prompt))
    def _():
        o_ref[...] = _rmsnorm_rows(xs_ref[...], g_ref[...]).astype(o_ref.dtype)


def _rmsnorm_stacked(xp, xs, g, out_dtype):
    (n_p, d), n_s = xp.shape, xs.shape[0]
    tm = _tile(n_s, _tile(n_p, 256))
    pt = n_p // tm
    return pl.pallas_call(
        functools.partial(_rmsnorm_stacked_body, prompt_tiles=pt),
        out_shape=jax.ShapeDtypeStruct((n_p + n_s, d), out_dtype),
        grid=((n_p + n_s) // tm,),
        in_specs=[pl.BlockSpec((tm, d), lambda i: (jnp.minimum(i, pt - 1), 0)),
                  pl.BlockSpec((tm, d), lambda i: (jnp.maximum(i - pt, 0), 0)),
                  pl.BlockSpec((1, d), lambda i: (0, 0))],
        out_specs=pl.BlockSpec((tm, d), lambda i: (i, 0)),
        compiler_params=_params("arbitrary"),
        name="rmsnorm_stacked",
    )(xp, xs, g.reshape(1, d))


def _matmul_body(*refs, n_lhs, n_extra, n_out, epilogue):
    x_refs = refs[:n_lhs]
    w_refs = refs[n_lhs:2 * n_lhs]
    extra = refs[2 * n_lhs:2 * n_lhs + n_extra]
    outs = refs[2 * n_lhs + n_extra:2 * n_lhs + n_extra + n_out]
    wb = refs[2 * n_lhs + n_extra + n_out:]

    @pl.when(pl.program_id(1) == 0)
    def _():
        for p in range(n_lhs):
            wb[p][...] = w_refs[p][...].astype(BF16)

    acc = jnp.dot(x_refs[0][...], wb[0][...], preferred_element_type=F32)
    for p in range(1, n_lhs):
        acc += jnp.dot(x_refs[p][...], wb[p][...], preferred_element_type=F32)
    epilogue(acc, extra, outs)


def _matmul(xs, w, col_off, n_cols, epilogue, extras, outs, name, n_prompt=None):
    m = xs[0].shape[0]
    tm = _tile(m, ROW_TILE) if n_prompt is None else _tile(m - n_prompt, _tile(n_prompt, ROW_TILE))
    k_total = sum(x.shape[1] for x in xs)

    def vmem_bytes(tn):
        tiles = sum((tn if kind is None or isinstance(kind, str) else kind) * jnp.dtype(a.dtype).itemsize
                    for a, kind in list(extras) + list(outs))
        return k_total * tn * (2 * 4 + 2) + 2 * tm * k_total * 2 + 2 * tm * tiles

    tn = _tile(n_cols, COL_TILE)
    while vmem_bytes(tn) > MATMUL_VMEM_BUDGET and tn > LANES:
        tn //= 2
    assert col_off % tn == 0
    jo = col_off // tn
    pt = None if n_prompt is None else n_prompt // tm
    in_specs, args = [], []
    for x in xs:
        in_specs.append(pl.BlockSpec((tm, x.shape[1]), lambda j, i: (i, 0)))
        args.append(x)
    row = 0
    for x in xs:
        kp = x.shape[1]
        assert row % kp == 0
        in_specs.append(pl.BlockSpec((None, kp, tn), lambda j, i, r=row // kp: (0, r, j + jo)))
        args.append(w)
        row += kp
    assert row == w.shape[1]

    def spec(kind):
        if kind is None:
            return pl.BlockSpec((tm, tn), lambda j, i: (i, j))
        if kind == PROMPT:
            return pl.BlockSpec((tm, tn), lambda j, i: (jnp.minimum(i, pt - 1), j))
        if kind == SAMPLE:
            return pl.BlockSpec((tm, tn), lambda j, i: (jnp.maximum(i - pt, 0), j))
        return pl.BlockSpec((tm, kind), lambda j, i: (i, 0))

    for a, kind in extras:
        in_specs.append(spec(kind))
        args.append(a)
    body = functools.partial(_matmul_body, n_lhs=len(xs), n_extra=len(extras), n_out=len(outs),
                             epilogue=epilogue if n_prompt is None else functools.partial(epilogue, pt))
    res = pl.pallas_call(
        body,
        out_shape=[o for o, _ in outs],
        grid=(n_cols // tn, m // tm),
        in_specs=in_specs,
        out_specs=[spec(kind) for _, kind in outs],
        scratch_shapes=[pltpu.VMEM((x.shape[1], tn), BF16) for x in xs],
        compiler_params=_params("arbitrary", "arbitrary"),
        name=name,
    )(*args)
    return res


def _ep_store(scale):
    def ep(acc, extra, outs):
        for o in outs:
            o[...] = (acc * scale).astype(o.dtype) if scale != 1.0 else acc.astype(o.dtype)
    return ep


def _ep_store_split(prompt_tiles, acc, extra, outs):
    is_prompt = pl.program_id(1) < prompt_tiles

    @pl.when(is_prompt)
    def _():
        outs[0][...] = acc

    @pl.when(jnp.logical_not(is_prompt))
    def _():
        outs[1][...] = acc

    outs[2][...] = acc.astype(outs[2].dtype)


def _ep_residual(acc, extra, outs):
    outs[0][...] = extra[0][...] + acc


def _ep_residual_split(prompt_tiles, acc, extra, outs):
    res = jnp.where(pl.program_id(1) < prompt_tiles, extra[0][...], extra[1][...])
    outs[0][...] = res + acc


def _ep_rotary(scale):
    half = RET_HEAD_DIM // 2

    def ep(acc, extra, outs):
        cos = extra[0][...]
        sin = extra[1][...]
        for h in range(acc.shape[1] // RET_HEAD_DIM):
            lo = h * RET_HEAD_DIM
            x1 = acc[:, lo:lo + half]
            x2 = acc[:, lo + half:lo + RET_HEAD_DIM]
            outs[0][:, lo:lo + half] = ((x1 * cos - x2 * sin) * scale).astype(outs[0].dtype)
            outs[0][:, lo + half:lo + RET_HEAD_DIM] = ((x1 * sin + x2 * cos) * scale).astype(outs[0].dtype)
    return ep


def _retention_body(lg_ref, q_ref, k_ref, v_ref, g_ref, gn_ref, s0_ref, *rest, rows, n_heads):
    o_ref, sout_ref, s_scr, decay_scr = rest[-4:]
    heads = range(n_heads)
    cols = lambda h: slice(h * RET_HEAD_DIM, (h + 1) * RET_HEAD_DIM)
    lgs = [lg_ref[pl.program_id(1) * n_heads + h] for h in heads]

    @pl.when(pl.program_id(2) == 0)
    def _():
        s_scr[...] = s0_ref[...]
        ii = lax.broadcasted_iota(jnp.int32, (rows, rows), 0)
        jj = lax.broadcasted_iota(jnp.int32, (rows, rows), 1)
        shift = CHUNK.bit_length() - 1
        dist = jnp.abs(ii - jj).astype(F32)
        for h in heads:
            decay_scr[h] = jnp.where((jj >> shift) <= (ii >> shift), jnp.exp(lgs[h] * dist), 0.0)

    idx = lax.broadcasted_iota(jnp.int32, (rows, 1), 0).astype(F32)
    qs = [q_ref[:, cols(h)] for h in heads]
    ks = [k_ref[:, cols(h)] for h in heads]
    vs = [v_ref[:, cols(h)] for h in heads]
    scores = [lax.dot_general(qs[h], ks[h], _NT, preferred_element_type=F32) * decay_scr[h] for h in heads]
    outs = [jnp.dot(scores[h].astype(BF16), vs[h], preferred_element_type=F32) for h in heads]
    states = [s_scr[h] for h in heads]
    outs = [outs[h] + jnp.exp(lgs[h] * (idx + 1.0))
            * jnp.dot(qs[h], states[h].astype(BF16), preferred_element_type=F32) for h in heads]
    kds = [(ks[h].astype(F32) * jnp.exp(lgs[h] * (rows - 1.0 - idx))).astype(BF16) for h in heads]
    for h in heads:
        s_new = (jnp.exp(lgs[h] * rows) * states[h]
                 + lax.dot_general(kds[h], vs[h], _TN, preferred_element_type=F32))
        s_scr[h] = s_new
        sout_ref[h] = s_new
    for h in heads:
        mu = jnp.mean(outs[h], axis=-1, keepdims=True)
        cen = outs[h] - mu
        var = jnp.mean(cen * cen, axis=-1, keepdims=True)
        yr = cen * lax.rsqrt(var + EPS) * gn_ref[h]
        g = g_ref[:, cols(h)]
        o_ref[:, cols(h)] = (g * _sigmoid(g) * yr).astype(o_ref.dtype)


def _into(out_buf):
    if out_buf is None:
        return [], []
    return [out_buf], [pl.BlockSpec(memory_space=pl.ANY)]


def _retention(log_gamma, q, k, v, gate, gn, s0, row_off, n_batch, n_steps, rows, out_buf=None):
    heads = q.shape[1] // RET_HEAD_DIM
    nh = min(RET_HEADS_PER_STEP, heads)
    assert row_off % rows == 0 and rows % CHUNK == 0 and heads % nh == 0
    ro = row_off // rows
    tok = lambda b, h, c, lg: (ro + b * n_steps + c, h)
    blk = pl.BlockSpec((rows, nh * RET_HEAD_DIM), tok)
    st = pl.BlockSpec((None, nh, RET_HEAD_DIM, RET_HEAD_DIM), lambda b, h, c, lg: (b, h, 0, 0))
    args = [log_gamma, q, k, v, gate, gn.reshape(heads, 1, RET_HEAD_DIM), s0]
    extra_args, extra_specs = _into(out_buf)
    grid_spec = pltpu.PrefetchScalarGridSpec(
        num_scalar_prefetch=1,
        grid=(n_batch, heads // nh, n_steps),
        in_specs=[blk, blk, blk, blk,
                  pl.BlockSpec((nh, 1, RET_HEAD_DIM), lambda b, h, c, lg: (h, 0, 0)), st] + extra_specs,
        out_specs=[blk, st],
        scratch_shapes=[pltpu.VMEM((nh, RET_HEAD_DIM, RET_HEAD_DIM), F32), pltpu.VMEM((nh, rows, rows), F32)],
    )
    return pl.pallas_call(
        functools.partial(_retention_body, rows=rows, n_heads=nh),
        out_shape=[jax.ShapeDtypeStruct((q.shape[0], heads * RET_HEAD_DIM), BF16),
                   jax.ShapeDtypeStruct((n_batch, heads, RET_HEAD_DIM, RET_HEAD_DIM), F32)],
        grid_spec=grid_spec,
        input_output_aliases={len(args): 0} if extra_args else {},
        compiler_params=_params("arbitrary", "arbitrary", "arbitrary"),
        name="retention",
    )(*args, *extra_args)


def _neg_suffix(n):
    j = lax.broadcasted_iota(jnp.int32, (2 * n, n), 0) & (n - 1)
    s = lax.broadcasted_iota(jnp.int32, (2 * n, n), 1)
    return jnp.where(j >= s, -1.0, 0.0).astype(BF16)


def _lanes_to(x, n):
    return x[:, :n] if n <= LANES else jnp.concatenate([x] * (n // LANES), axis=1)


def _suffix_sums(fail, neg_ones):
    tk = fail.shape[1]
    hi, lo = _split_bf16(fail)
    if tk % LANES == 0:
        return jnp.dot(jnp.concatenate([hi, lo], axis=1), neg_ones, preferred_element_type=F32)
    return (jnp.dot(hi, neg_ones[:tk], preferred_element_type=F32)
            + jnp.dot(lo, neg_ones[:tk], preferred_element_type=F32))


def _sb_tiles(qs, kbs, vbs, neg_ones, carry, acc, masked):
    heads = range(len(qs))
    tk = kbs[0].shape[0]
    zs = [lax.dot_general(qs[h], kbs[h], _NT, preferred_element_type=F32) for h in heads]
    fails = [jnp.maximum(z, 0.0) + jnp.log2(1.0 + jnp.exp2(-jnp.abs(z))) for z in zs]
    if masked:
        t = lax.broadcasted_iota(jnp.int32, zs[0].shape, 0)
        s = lax.broadcasted_iota(jnp.int32, zs[0].shape, 1)
        valid = s < t
        fails = [jnp.where(valid, f, 0.0) for f in fails]
    suffixes = [_suffix_sums(f, neg_ones) for f in fails]
    probs = [jnp.exp2(zs[h] + suffixes[h] + _lanes_to(carry[h], tk)) for h in heads]
    if masked:
        probs = [jnp.where(valid, p, 0.0) for p in probs]
    for h in heads:
        acc[h] += jnp.dot(probs[h].astype(BF16), vbs[h], preferred_element_type=F32)
        carry[h] += jnp.broadcast_to(suffixes[h][:, :1], carry.shape[1:])


def _sb_body(q_ref, kd_ref, vd_ref, kp_ref, vp_ref, *rest, tk, n_heads, past_per_step, past_static):
    o_ref, carry, acc = rest[-3:]
    tq = q_ref.shape[0]
    n_past = past_static if past_static is not None else pl.program_id(1) * past_per_step
    cols = lambda h: slice(h * SB_HEAD_DIM, (h + 1) * SB_HEAD_DIM)
    heads = range(n_heads)
    carry[...] = jnp.zeros(carry.shape, F32)
    acc[...] = jnp.zeros(acc.shape, F32)
    queries = lambda: [q_ref[:, cols(h)] for h in heads]
    _sb_tiles(queries(), [kd_ref[:, cols(h)].astype(BF16) for h in heads],
              [vd_ref[:, cols(h)].astype(BF16) for h in heads], _neg_suffix(tq), carry, acc, True)
    neg_ones = _neg_suffix(tk)

    alive = lambda: jnp.max(carry[...]) > SB_DEAD_LOG2

    def step(state):
        it = state[0]
        rows = pl.ds(pl.multiple_of((n_past - 1 - it) * tk, tk), tk)
        _sb_tiles(queries(), [kp_ref[rows, cols(h)].astype(BF16) for h in heads],
                  [vp_ref[rows, cols(h)].astype(BF16) for h in heads], neg_ones, carry, acc, False)
        return it + 1, alive()

    lax.while_loop(lambda state: (state[0] < n_past) & state[1], step, (jnp.int32(0), alive()))
    for h in range(n_heads):
        o_ref[:, cols(h)] = acc[h].astype(o_ref.dtype)


def _sb_scratch(n_heads, tq):
    return [pltpu.VMEM((n_heads, tq, LANES), F32), pltpu.VMEM((n_heads, tq, SB_HEAD_DIM), F32)]


def _sb_prompt(q, k, v, n_tok):
    heads = q.shape[1] // SB_HEAD_DIM
    tq = _tile(n_tok, SB_BLOCK)
    nh = SB_HEADS_PER_STEP
    width = nh * SB_HEAD_DIM
    blk = pl.BlockSpec((tq, width), lambda h, i: (i, h))
    full = pl.BlockSpec((n_tok, width), lambda h, i: (0, h))
    return pl.pallas_call(
        functools.partial(_sb_body, tk=tq, n_heads=nh, past_per_step=1, past_static=None),
        out_shape=jax.ShapeDtypeStruct((q.shape[0], heads * SB_HEAD_DIM), BF16),
        grid=(heads // nh, n_tok // tq),
        in_specs=[blk, blk, blk, full, full],
        out_specs=blk,
        scratch_shapes=_sb_scratch(nh, tq),
        compiler_params=_params("parallel", "arbitrary"),
        name="stick_breaking_prompt",
    )(q, k, v, k, v)


def _sb_sample(q, k, v, cache_k, cache_v, row_off, n_new, out_buf):
    heads = q.shape[1] // SB_HEAD_DIM
    n_batch, past, _ = cache_k.shape
    tk = _tile(past, SB_BLOCK)
    assert row_off % n_new == 0
    ro = row_off // n_new
    nh = SB_HEADS_PER_STEP
    width = nh * SB_HEAD_DIM
    blk = pl.BlockSpec((n_new, width), lambda b, h: (ro + b, h))
    full = pl.BlockSpec((None, past, width), lambda b, h: (b, 0, h))
    extra_args, extra_specs = _into(out_buf)
    return pl.pallas_call(
        functools.partial(_sb_body, tk=tk, n_heads=nh, past_per_step=0, past_static=past // tk),
        out_shape=jax.ShapeDtypeStruct(out_buf.shape, out_buf.dtype),
        grid=(n_batch, heads // nh),
        in_specs=[blk, blk, blk, full, full] + extra_specs,
        out_specs=blk,
        scratch_shapes=_sb_scratch(nh, n_new),
        input_output_aliases={5: 0},
        compiler_params=_params("parallel", "arbitrary"),
        name="stick_breaking_sample",
    )(q, k, v, cache_k, cache_v, *extra_args)


def _cross_body(q_ref, k_ref, v_ref, *rest, scale):
    o_ref = rest[-1]
    s = lax.dot_general(q_ref[...], k_ref[...].astype(BF16), _NT, preferred_element_type=F32) * scale
    e = jnp.exp(s - jnp.max(s, axis=-1, keepdims=True))
    p = e / jnp.sum(e, axis=-1, keepdims=True)
    o_ref[...] = jnp.dot(p.astype(BF16), v_ref[...].astype(BF16),
                         preferred_element_type=F32).astype(o_ref.dtype)


def _cross_attend(q, mem_k, mem_v, row_off, rows_per_seq, out_buf=None):
    n_seq, n_mem, d = mem_k.shape
    dh = d // MEM_HEADS
    tm = _tile(rows_per_seq, ROW_TILE)
    steps = rows_per_seq // tm
    assert row_off % tm == 0
    ro = row_off // tm
    mem = pl.BlockSpec((None, n_mem, dh), lambda i, h: (i // steps, 0, h))
    blk = pl.BlockSpec((tm, dh), lambda i, h: (ro + i, h))
    extra_args, extra_specs = _into(out_buf)
    return pl.pallas_call(
        functools.partial(_cross_body, scale=dh ** -0.5),
        out_shape=jax.ShapeDtypeStruct((q.shape[0], d), BF16),
        grid=(n_seq * steps, MEM_HEADS),
        in_specs=[blk, mem, mem] + extra_specs,
        out_specs=blk,
        input_output_aliases={3: 0} if extra_args else {},
        compiler_params=_params("parallel", "arbitrary"),
        name="cross_attention",
    )(q, mem_k, mem_v, *extra_args)


def _split_bf16(x):
    hi = x.astype(BF16)
    return hi, (x - hi.astype(F32)).astype(BF16)


def _router_body(x_ref, g_ref, wr_ref, br_ref, h_ref, ids_ref, wts_ref):
    h = _rmsnorm_rows(x_ref[...], g_ref[...])
    h_ref[...] = _pack_bf16_pairs(h)
    h_hi, h_lo = _split_bf16(h)
    w_hi, w_lo = _split_bf16(wr_ref[...])
    logits = (jnp.dot(h_hi, w_hi, preferred_element_type=F32)
              + jnp.dot(h_hi, w_lo, preferred_element_type=F32)
              + jnp.dot(h_lo, w_hi, preferred_element_type=F32)) + br_ref[...]
    lane = lax.broadcasted_iota(jnp.int32, logits.shape, 1).astype(F32)
    big = float(LANES)
    neg = -jnp.inf

    def first_max(vals):
        top = jnp.max(vals, axis=-1, keepdims=True)
        return top, jnp.min(jnp.where(vals == top, lane, big), axis=-1, keepdims=True)

    is_group = lane < N_GROUPS
    g_max, g_top = first_max(jnp.where(is_group, logits, neg))
    g_w = 1.0 / jnp.sum(jnp.where(is_group, jnp.exp(logits - g_max), 0.0), axis=-1, keepdims=True)
    first = N_GROUPS + EXPERTS_PER_GROUP * g_top
    cand = jnp.where((lane >= first) & (lane < first + EXPERTS_PER_GROUP), logits, neg)
    v1, i1 = first_max(cand)
    v2, i2 = first_max(jnp.where(lane == i1, neg, cand))
    t = jnp.exp(v2 - v1)
    w1 = g_w / (1.0 + t)
    w2 = g_w * t / (1.0 + t)
    ids = jnp.where(lane == 0.0, i1 - N_GROUPS, jnp.where(lane == 1.0, i2 - N_GROUPS, 0.0))
    ids_ref[...] = ids.astype(jnp.int32)
    wts_ref[...] = jnp.where(lane == 0.0, w1, jnp.where(lane == 1.0, w2, 0.0))


def _norm_and_route(x, g, wg, bg, we, be):
    m, d = x.shape
    tm = _tile(m, 256)
    pad = LANES - N_GROUPS - N_EXPERTS
    wr = jnp.concatenate([wg, we, jnp.zeros((d, pad), F32)], axis=1)
    br = jnp.concatenate([bg, be, jnp.zeros((pad,), F32)]).reshape(1, LANES)
    row = lambda c: pl.BlockSpec((tm, c), lambda i: (i, 0))
    return pl.pallas_call(
        _router_body,
        out_shape=[jax.ShapeDtypeStruct((m, d // 2), jnp.int32),
                   jax.ShapeDtypeStruct((m, LANES), jnp.int32),
                   jax.ShapeDtypeStruct((m, LANES), F32)],
        grid=(m // tm,),
        in_specs=[row(d), pl.BlockSpec((1, d), lambda i: (0, 0)),
                  pl.BlockSpec((d, LANES), lambda i: (0, 0)), pl.BlockSpec((1, LANES), lambda i: (0, 0))],
        out_specs=[row(d // 2), row(LANES), row(LANES)],
        compiler_params=_params("parallel"),
        name="norm_route",
    )(x, g.reshape(1, d), wr, br)


GATHER_UNROLL = 8


def _start_row_gather(idx_ref, first, stride, src_ref, buf, sem, n_rows):
    def one(r, _):
        t = idx_ref[first + r * stride]
        pltpu.make_async_copy(src_ref.at[pl.ds(t, 1)], buf.at[pl.ds(r, 1)], sem).start()
        return 0

    def group(g, _):
        for u in range(GATHER_UNROLL):
            one(g * GATHER_UNROLL + u, 0)
        return 0

    groups = n_rows // GATHER_UNROLL
    lax.fori_loop(0, groups, group, 0)
    lax.fori_loop(groups * GATHER_UNROLL, n_rows, one, 0)


def _wait_row_gather(src_ref, buf, sem, n_rows):
    pltpu.make_async_copy(src_ref.at[pl.ds(0, n_rows)], buf.at[pl.ds(0, n_rows)], sem).wait()


def _pack_bf16_pairs(x):
    half = x.shape[1] // 2
    hi = lax.bitcast_convert_type(x[:, :half].astype(BF16).astype(F32), jnp.int32)
    lo = lax.bitcast_convert_type(x[:, half:].astype(BF16).astype(F32), jnp.int32)
    return hi | lax.shift_right_logical(lo, jnp.int32(16))


def _unpack_bf16_pairs(p):
    hi = lax.bitcast_convert_type(p & jnp.int32(-65536), F32)
    lo = lax.bitcast_convert_type(lax.shift_left(p, jnp.int32(16)), F32)
    return hi, lo


def _dispatch_body(idx_ref, used_ref, cnt_ref, src_ref, o_ref, buf, sem):
    b = pl.program_id(0)
    rows = buf.shape[1]

    copied = lambda blk: pl.multiple_of(pl.cdiv(cnt_ref[blk], GATHER_UNROLL) * GATHER_UNROLL, GATHER_UNROLL)

    def start(blk):
        _start_row_gather(idx_ref, blk * rows, 1, src_ref, buf.at[blk % 2], sem.at[blk % 2], copied(blk))

    @pl.when(b == 0)
    def _():
        buf[...] = jnp.zeros(buf.shape, buf.dtype)
        start(0)

    @pl.when(b + 1 < used_ref[0])
    def _():
        start(b + 1)

    @pl.when(b < used_ref[0])
    def _():
        slot = b % 2
        cnt = cnt_ref[b]
        _wait_row_gather(src_ref, buf.at[slot], sem.at[slot], copied(b))
        half = buf.shape[2]
        row = lax.broadcasted_iota(jnp.int32, (rows, 1), 0)
        hi, lo = _unpack_bf16_pairs(jnp.where(row < cnt, buf[slot], 0))
        o_ref[:, :half] = hi.astype(o_ref.dtype)
        o_ref[:, half:] = lo.astype(o_ref.dtype)


def _dispatch_rows(src, row_tok, blk_cnt, n_used):
    n_rows = row_tok.shape[0]
    half = src.shape[1]
    grid_spec = pltpu.PrefetchScalarGridSpec(
        num_scalar_prefetch=3,
        grid=(n_rows // MOE_ROWS,),
        in_specs=[pl.BlockSpec(memory_space=pl.ANY)],
        out_specs=pl.BlockSpec((MOE_ROWS, 2 * half), lambda b, idx, used, cnt: (b, 0)),
        scratch_shapes=[pltpu.VMEM((2, MOE_ROWS, half), src.dtype), pltpu.SemaphoreType.DMA((2,))],
    )
    return pl.pallas_call(
        _dispatch_body,
        out_shape=jax.ShapeDtypeStruct((n_rows, 2 * half), BF16),
        grid_spec=grid_spec,
        compiler_params=_params("arbitrary"),
        name="dispatch_rows",
    )(row_tok, n_used, blk_cnt, src)


def _expert_up_body(be_ref, first_ref, used_ref, x_ref, w1_ref, w3_ref, h_ref, w1b, w3b):
    b = pl.program_id(1)

    @pl.when(first_ref[b] == 1)
    def _():
        w1b[...] = w1_ref[...].astype(BF16)
        w3b[...] = w3_ref[...].astype(BF16)

    @pl.when(b < used_ref[0])
    def _():
        x = x_ref[...]
        a = jnp.dot(x, w1b[...], preferred_element_type=F32)
        g = jnp.dot(x, w3b[...], preferred_element_type=F32)
        h_ref[...] = (a * _sigmoid(a) * g).astype(h_ref.dtype)


def _expert_down_body(be_ref, first_ref, used_ref, h_ref, w2_ref, y_ref, w2b):
    b = pl.program_id(1)

    @pl.when(first_ref[b] == 1)
    def _():
        w2b[...] = w2_ref[...].astype(BF16)

    @pl.when(b < used_ref[0])
    def _():
        y_ref[...] = _pack_bf16_pairs(jnp.dot(h_ref[...], w2b[...], preferred_element_type=F32))


def _expert_ffn(xs, blk_exp, blk_first, n_used, w1, w3, w2):
    n_rows, d = xs.shape
    f = w1.shape[2]
    n_blocks = n_rows // MOE_ROWS
    tf = _tile(f, EXPERT_HIDDEN_TILE)
    tn = _tile(d, EXPERT_OUT_TILE)
    last = lambda b, used: jnp.minimum(b, used[0] - 1)
    up_spec = pltpu.PrefetchScalarGridSpec(
        num_scalar_prefetch=3,
        grid=(f // tf, n_blocks),
        in_specs=[pl.BlockSpec((MOE_ROWS, d), lambda j, b, be, fi, us: (last(b, us), 0)),
                  pl.BlockSpec((None, d, tf), lambda j, b, be, fi, us: (be[b], 0, j)),
                  pl.BlockSpec((None, d, tf), lambda j, b, be, fi, us: (be[b], 0, j))],
        out_specs=pl.BlockSpec((MOE_ROWS, tf), lambda j, b, be, fi, us: (b, j)),
        scratch_shapes=[pltpu.VMEM((d, tf), BF16), pltpu.VMEM((d, tf), BF16)],
    )
    hidden = pl.pallas_call(
        _expert_up_body,
        out_shape=jax.ShapeDtypeStruct((n_rows, f), BF16),
        grid_spec=up_spec,
        compiler_params=_params("arbitrary", "arbitrary"),
        name="expert_up",
    )(blk_exp, blk_first, n_used, xs, w1, w3)
    down_spec = pltpu.PrefetchScalarGridSpec(
        num_scalar_prefetch=3,
        grid=(d // tn, n_blocks),
        in_specs=[pl.BlockSpec((MOE_ROWS, f), lambda j, b, be, fi, us: (last(b, us), 0)),
                  pl.BlockSpec((None, f, tn), lambda j, b, be, fi, us: (be[b], 0, j))],
        out_specs=pl.BlockSpec((MOE_ROWS, tn // 2), lambda j, b, be, fi, us: (b, j)),
        scratch_shapes=[pltpu.VMEM((f, tn), BF16)],
    )
    return pl.pallas_call(
        _expert_down_body,
        out_shape=jax.ShapeDtypeStruct((n_rows, d // 2), jnp.int32),
        grid_spec=down_spec,
        compiler_params=_params("arbitrary", "arbitrary"),
        name="expert_down",
    )(blk_exp, blk_first, n_used, hidden, w2)


def _route_layout(ids, n_tok):
    n_assign = n_tok * TOP_K
    flat_e = ids.reshape(n_assign)
    flat_t = jnp.repeat(jnp.arange(n_tok, dtype=jnp.int32), TOP_K)
    onehot = (flat_e[:, None] == jnp.arange(N_EXPERTS, dtype=jnp.int32)[None, :]).astype(jnp.int32)
    running = jnp.cumsum(onehot, axis=0)
    rank = jnp.take_along_axis(running, flat_e[:, None], axis=1)[:, 0] - 1
    counts = running[-1]
    padded = (counts + MOE_ROWS - 1) // MOE_ROWS * MOE_ROWS
    pad_end = jnp.cumsum(padded)
    dest = (pad_end - padded)[flat_e] + rank
    n_blocks = -(-n_assign // MOE_ROWS) + N_EXPERTS
    n_rows = n_blocks * MOE_ROWS
    row_tok = jnp.zeros((n_rows,), jnp.int32).at[dest].set(flat_t)
    blk = jnp.arange(n_blocks, dtype=jnp.int32)
    n_used = (pad_end[-1] // MOE_ROWS).astype(jnp.int32)
    blk_exp = jnp.searchsorted(pad_end, jnp.minimum(blk, n_used - 1) * MOE_ROWS, side='right').astype(jnp.int32)
    blk_exp = jnp.minimum(blk_exp, N_EXPERTS - 1)
    prev = jnp.concatenate([jnp.full((1,), -1, jnp.int32), blk_exp[:-1]])
    blk_first = ((blk_exp != prev) & (blk < n_used)).astype(jnp.int32)
    into_range = blk * MOE_ROWS - (pad_end - padded)[blk_exp]
    blk_cnt = jnp.where(blk < n_used, jnp.clip(counts[blk_exp] - into_range, 0, MOE_ROWS), 0).astype(jnp.int32)
    return row_tok, blk_cnt, dest.astype(jnp.int32), blk_exp, blk_first, n_used.reshape(1)


def _final_body(dest_ref, x_ref, w_ref, y_ref, g_ref, op_ref, os_ref, buf, sem, *, prompt_tiles, pack_tile):
    i = pl.program_id(0)
    tm = buf.shape[2]

    def start(blk):
        for k in range(TOP_K):
            _start_row_gather(dest_ref, blk * tm * TOP_K + k, TOP_K, y_ref, buf.at[blk % 2, k], sem.at[blk % 2, k],
                              tm)

    @pl.when(i == 0)
    def _():
        start(0)

    @pl.when(i + 1 < pl.num_programs(0))
    def _():
        start(i + 1)

    slot = i % 2
    for k in range(TOP_K):
        _wait_row_gather(y_ref, buf.at[slot, k], sem.at[slot, k], tm)
    w = w_ref[...]
    pieces = []
    for j in range(buf.shape[3] // pack_tile):
        words = slice(j * pack_tile, (j + 1) * pack_tile)
        hi0, lo0 = _unpack_bf16_pairs(buf[slot, 0, :, words])
        hi1, lo1 = _unpack_bf16_pairs(buf[slot, 1, :, words])
        pieces += [hi0 * w[:, 0:1] + hi1 * w[:, 1:2], lo0 * w[:, 0:1] + lo1 * w[:, 1:2]]
    y = _rmsnorm_rows(x_ref[...] + jnp.concatenate(pieces, axis=1), g_ref[...])

    @pl.when(i < prompt_tiles)
    def _():
        op_ref[...] = y

    @pl.when(i >= prompt_tiles)
    def _():
        os_ref[...] = y


def _combine_and_norm(x, wts, y_rows, dest, g, n_prompt):
    m, d = x.shape
    tm = _tile(m - n_prompt, _tile(n_prompt, 256))
    assert n_prompt % tm == 0
    pt = n_prompt // tm
    grid_spec = pltpu.PrefetchScalarGridSpec(
        num_scalar_prefetch=1,
        grid=(m // tm,),
        in_specs=[pl.BlockSpec((tm, d), lambda i, dest: (i, 0)),
                  pl.BlockSpec((tm, LANES), lambda i, dest: (i, 0)),
                  pl.BlockSpec(memory_space=pl.ANY),
                  pl.BlockSpec((1, d), lambda i, dest: (0, 0))],
        out_specs=[pl.BlockSpec((tm, d), lambda i, dest: (jnp.minimum(i, pt - 1), 0)),
                   pl.BlockSpec((tm, d), lambda i, dest: (jnp.maximum(i - pt, 0), 0))],
        scratch_shapes=[pltpu.VMEM((2, TOP_K, tm, d // 2), y_rows.dtype), pltpu.SemaphoreType.DMA((2, TOP_K))],
    )
    return pl.pallas_call(
        functools.partial(_final_body, prompt_tiles=pt, pack_tile=_tile(d, EXPERT_OUT_TILE) // 2),
        out_shape=[jax.ShapeDtypeStruct((n_prompt, d), F32), jax.ShapeDtypeStruct((m - n_prompt, d), F32)],
        grid_spec=grid_spec,
        compiler_params=_params("arbitrary"),
        name="combine_norm",
    )(dest, x, wts, y_rows, g.reshape(1, d))


def _layer(xp, xs, n_seq, n_new, past_len, cache_sb_k, cache_sb_v, state_ret, cache_mem_k, cache_mem_v,
           mem_prompt, norm_mix, w_in, ret_gn, w_out, norm_cross, norm_mem, w_cq, w_ck, w_cv, w_co, norm_ffn,
           wg, bg, we, be, w1, w3, w2):
    n_prompt, d = xp.shape
    n_tok = n_prompt + xs.shape[0]
    d_ret = d // 2
    d_sb = d - d_ret
    ret_heads = d_ret // RET_HEAD_DIM
    sds = jax.ShapeDtypeStruct

    pos = np.concatenate([np.arange(n_prompt), np.tile(past_len + np.arange(n_new), n_seq)])
    half = RET_HEAD_DIM // 2
    inv = (ROPE_BASE ** (-np.arange(half, dtype=np.float32) / half)).astype(np.float32)
    ang = pos.astype(np.float32)[:, None] * inv[None, :]
    cos, sin = jnp.asarray(np.cos(ang), F32), jnp.asarray(np.sin(ang), F32)
    log_gamma = jnp.log1p(-jnp.exp2(-5.0 - jnp.arange(ret_heads, dtype=F32)))

    h = _rmsnorm_stacked(xp, xs, norm_mix, BF16)
    rot = [(cos, half), (sin, half)]
    bf = lambda n: [(sds((n_tok, n), BF16), None)]
    (rq,) = _matmul([h], w_in, 0, d_ret, _ep_rotary(1.0), rot, bf(d_ret), "in_proj_ret_q")
    (rk,) = _matmul([h], w_in, d_ret, d_ret, _ep_rotary(RET_HEAD_DIM ** -0.5), rot, bf(d_ret), "in_proj_ret_k")
    (rv,) = _matmul([h], w_in, 2 * d_ret, d_ret, _ep_store(1.0), [], bf(d_ret), "in_proj_ret_v")
    (rg,) = _matmul([h], w_in, 3 * d_ret, d_ret, _ep_store(1.0), [], [(sds((n_tok, d_ret), F32), None)],
                    "in_proj_ret_gate")
    (sq,) = _matmul([h], w_in, 4 * d_ret, d_sb, _ep_store(SB_HEAD_DIM ** -0.5 * LOG2_E), [], bf(d_sb),
                    "in_proj_sb_q")
    split = [(sds((n_prompt, d_sb), F32), PROMPT), (sds((n_tok - n_prompt, d_sb), F32), SAMPLE),
             (sds((n_tok, d_sb), BF16), None)]
    sk_p, sk_s, sk_b = _matmul([h], w_in, 4 * d_ret + d_sb, d_sb, _ep_store_split, [], split, "in_proj_sb_k",
                               n_prompt=n_prompt)
    sv_p, sv_s, sv_b = _matmul([h], w_in, 4 * d_ret + 2 * d_sb, d_sb, _ep_store_split, [], split,
                               "in_proj_sb_v", n_prompt=n_prompt)

    rows_p = _tile(n_prompt, RET_BLOCK)
    zeros_state = jnp.zeros((1, ret_heads, RET_HEAD_DIM, RET_HEAD_DIM), F32)
    yr, state_p = _retention(log_gamma, rq, rk, rv, rg, ret_gn, zeros_state, 0, 1, n_prompt // rows_p, rows_p)
    yr, state_s = _retention(log_gamma, rq, rk, rv, rg, ret_gn, state_ret, n_prompt, n_seq, 1, n_new,
                             out_buf=yr)
    sb = _sb_prompt(sq, sk_b, sv_b, n_prompt)
    sb = _sb_sample(sq, sk_b, sv_b, cache_sb_k.reshape(n_seq, past_len, d_sb),
                    cache_sb_v.reshape(n_seq, past_len, d_sb), n_prompt, n_new, sb)
    res = [(sds((n_tok, d), F32), None)]
    (x,) = _matmul([yr, sb], w_out, 0, d, _ep_residual_split, [(xp, PROMPT), (xs, SAMPLE)], res, "out_proj",
                   n_prompt=n_prompt)

    n_mem = mem_prompt.shape[0]
    m = _rmsnorm(mem_prompt, norm_mem, BF16)
    mem_out = [(sds((n_mem, d), F32), None)]
    (mk,) = _matmul([m], w_ck, 0, d, _ep_store(1.0), [], mem_out, "mem_k")
    (mv,) = _matmul([m], w_cv, 0, d, _ep_store(1.0), [], mem_out, "mem_v")
    h = _rmsnorm(x, norm_cross, BF16)
    (cq,) = _matmul([h], w_cq, 0, d, _ep_store(1.0), [], bf(d), "cross_q")
    co = _cross_attend(cq, mk.reshape(1, n_mem, d), mv.reshape(1, n_mem, d), 0, n_prompt)
    co = _cross_attend(cq, cache_mem_k.reshape(n_seq, n_mem, d), cache_mem_v.reshape(n_seq, n_mem, d),
                       n_prompt, n_new, out_buf=co)
    (x,) = _matmul([co], w_co, 0, d, _ep_residual, [(x, None)], res, "cross_out")

    hn, ids, wts = _norm_and_route(x, norm_ffn, wg, bg, we, be)
    row_tok, blk_cnt, dest, blk_exp, blk_first, n_used = _route_layout(ids[:, :TOP_K], n_tok)
    x_rows = _dispatch_rows(hn, row_tok, blk_cnt, n_used)
    y_rows = _expert_ffn(x_rows, blk_exp, blk_first, n_used, w1, w3, w2)
    return x, wts, y_rows, dest, (sk_p, sv_p, sk_s, sv_s, state_p, state_s, mk, mv)


def kernel(x_prompt, x_sample, cache_sb_k, cache_sb_v, state_ret, cache_mem_k, cache_mem_v, mem_prompt, norm_mix, w_in, ret_gn, w_out, norm_cross, norm_mem, w_cq, w_ck, w_cv, w_co, norm_ffn, router_group_w, router_group_b, router_expert_w, router_expert_b, expert_w1, expert_w3, expert_w2, norm_final):
    batch, n_prompt, d = x_prompt.shape
    n_seq, n_new, _ = x_sample.shape
    depth, _, past_len, sb_heads, _ = cache_sb_k.shape
    n_mem = mem_prompt.shape[1]
    assert batch == 1 and depth == 1
    ret_heads = (d // 2) // RET_HEAD_DIM
    x, wts, y_rows, dest, (sk_p, sv_p, sk_s, sv_s, state_p, state_s, mk, mv) = _layer(
        x_prompt.reshape(n_prompt, d), x_sample.reshape(n_seq * n_new, d), n_seq, n_new, past_len,
        cache_sb_k, cache_sb_v, state_ret[0], cache_mem_k, cache_mem_v, mem_prompt[0], norm_mix[0], w_in,
        ret_gn[0], w_out, norm_cross[0], norm_mem[0], w_cq, w_ck, w_cv, w_co, norm_ffn[0], router_group_w[0],
        router_group_b[0], router_expert_w[0], router_expert_b[0], expert_w1[0], expert_w3[0], expert_w2[0])
    y_prompt, y_sample = _combine_and_norm(x, wts, y_rows, dest, norm_final, n_prompt)
    y_prompt = y_prompt.reshape(1, n_prompt, d)
    y_sample = y_sample.reshape(n_seq, n_new, d)
    sb_shape_p = (1, 1, n_prompt, sb_heads, SB_HEAD_DIM)
    sb_shape_s = (1, n_seq, n_new, sb_heads, SB_HEAD_DIM)
    mem_shape = (1, 1, n_mem, MEM_HEADS, d // MEM_HEADS)
    return (y_prompt, y_sample,
            sk_p.reshape(sb_shape_p), sv_p.reshape(sb_shape_p),
            state_p.reshape(1, 1, ret_heads, RET_HEAD_DIM, RET_HEAD_DIM),
            mk.reshape(mem_shape), mv.reshape(mem_shape),
            sk_s.reshape(sb_shape_s), sv_s.reshape(sb_shape_s),
            state_s.reshape(1, n_seq, ret_heads, RET_HEAD_DIM, RET_HEAD_DIM))
```

```python
import functools

import jax
import jax.numpy as jnp
import numpy as np
from jax import lax
from jax.experimental import pallas as pl
from jax.experimental.pallas import tpu as pltpu

BF16 = jnp.bfloat16
F32 = jnp.float32

EPS = 1e-6
CHUNK = 64
RET_HEAD_DIM = 256
SB_HEAD_DIM = 128
MEM_HEADS = 4
N_GROUPS = 4
EXPERTS_PER_GROUP = 8
N_EXPERTS = N_GROUPS * EXPERTS_PER_GROUP
TOP_K = 2
ROPE_BASE = 10000.0

LANES = 128
F32_SUBLANES = 8
VMEM_LIMIT_BYTES = 56 * 1024 * 1024
MATMUL_VMEM_BUDGET = 52 * 1024 * 1024
ROW_TILE = 512
COL_TILE = 1024
EXPERT_HIDDEN_TILE = 512
RET_BLOCK = 256
RET_HEADS_PER_STEP = 4
SB_BLOCK = 256
SB_HEADS_PER_STEP = 4
LOG2_E = 1.4426950408889634
SB_DEAD_LOG2 = -64.0
MOE_ROWS = 512
EXPERT_OUT_TILE = 2048

PROMPT = "prompt"
SAMPLE = "sample"

_NT = (((1,), (1,)), ((), ()))
_TN = (((0,), (0,)), ((), ()))


def _tile(n, pref):
    t = min(pref, n)
    while n % t:
        t //= 2
    return t


def _params(*sem):
    return pltpu.CompilerParams(dimension_semantics=sem, vmem_limit_bytes=VMEM_LIMIT_BYTES)


def _sigmoid(x):
    return 1.0 / (1.0 + jnp.exp(-x))


def _rmsnorm_rows(x, g):
    ms = jnp.mean(x * x, axis=-1, keepdims=True)
    return x * lax.rsqrt(ms + EPS) * g


def _rmsnorm_body(x_ref, g_ref, o_ref):
    o_ref[...] = _rmsnorm_rows(x_ref[...], g_ref[...]).astype(o_ref.dtype)


def _rmsnorm(x, g, out_dtype):
    m, d = x.shape
    tm = _tile(m, 256)
    return pl.pallas_call(
        _rmsnorm_body,
        out_shape=jax.ShapeDtypeStruct((m, d), out_dtype),
        grid=(m // tm,),
        in_specs=[pl.BlockSpec((tm, d), lambda i: (i, 0)), pl.BlockSpec((1, d), lambda i: (0, 0))],
        out_specs=pl.BlockSpec((tm, d), lambda i: (i, 0)),
        compiler_params=_params("parallel"),
        name="rmsnorm",
    )(x, g.reshape(1, d))


def _rmsnorm_stacked_body(xp_ref, xs_ref, g_ref, o_ref, *, prompt_tiles):
    is_prompt = pl.program_id(0) < prompt_tiles

    @pl.when(is_prompt)
    def _():
        o_ref[...] = _rmsnorm_rows(xp_ref[...], g_ref[...]).astype(o_ref.dtype)

    @pl.when(jnp.logical_not(is_prompt))
    def _():
        o_ref[...] = _rmsnorm_rows(xs_ref[...], g_ref[...]).astype(o_ref.dtype)


def _rmsnorm_stacked(xp, xs, g, out_dtype):
    (n_p, d), n_s = xp.shape, xs.shape[0]
    tm = _tile(n_s, _tile(n_p, 256))
    pt = n_p // tm
    return pl.pallas_call(
        functools.partial(_rmsnorm_stacked_body, prompt_tiles=pt),
        out_shape=jax.ShapeDtypeStruct((n_p + n_s, d), out_dtype),
        grid=((n_p + n_s) // tm,),
        in_specs=[pl.BlockSpec((tm, d), lambda i: (jnp.minimum(i, pt - 1), 0)),
                  pl.BlockSpec((tm, d), lambda i: (jnp.maximum(i - pt, 0), 0)),
                  pl.BlockSpec((1, d), lambda i: (0, 0))],
        out_specs=pl.BlockSpec((tm, d), lambda i: (i, 0)),
        compiler_params=_params("arbitrary"),
        name="rmsnorm_stacked",
    )(xp, xs, g.reshape(1, d))


def _matmul_body(*refs, n_lhs, n_extra, n_out, epilogue):
    x_refs = refs[:n_lhs]
    w_refs = refs[n_lhs:2 * n_lhs]
    extra = refs[2 * n_lhs:2 * n_lhs + n_extra]
    outs = refs[2 * n_lhs + n_extra:2 * n_lhs + n_extra + n_out]
    wb = refs[2 * n_lhs + n_extra + n_out:]

    @pl.when(pl.program_id(1) == 0)
    def _():
        for p in range(n_lhs):
            wb[p][...] = w_refs[p][...].astype(BF16)

    acc = jnp.dot(x_refs[0][...], wb[0][...], preferred_element_type=F32)
    for p in range(1, n_lhs):
        acc += jnp.dot(x_refs[p][...], wb[p][...], preferred_element_type=F32)
    epilogue(acc, extra, outs)


def _matmul(xs, w, col_off, n_cols, epilogue, extras, outs, name, n_prompt=None):
    m = xs[0].shape[0]
    tm = _tile(m, ROW_TILE) if n_prompt is None else _tile(m - n_prompt, _tile(n_prompt, ROW_TILE))
    k_total = sum(x.shape[1] for x in xs)

    def vmem_bytes(tn):
        tiles = sum((tn if kind is None or isinstance(kind, str) else kind) * jnp.dtype(a.dtype).itemsize
                    for a, kind in list(extras) + list(outs))
        return k_total * tn * (2 * 4 + 2) + 2 * tm * k_total * 2 + 2 * tm * tiles

    tn = _tile(n_cols, COL_TILE)
    while vmem_bytes(tn) > MATMUL_VMEM_BUDGET and tn > LANES:
        tn //= 2
    assert col_off % tn == 0
    jo = col_off // tn
    pt = None if n_prompt is None else n_prompt // tm
    in_specs, args = [], []
    for x in xs:
        in_specs.append(pl.BlockSpec((tm, x.shape[1]), lambda j, i: (i, 0)))
        args.append(x)
    row = 0
    for x in xs:
        kp = x.shape[1]
        assert row % kp == 0
        in_specs.append(pl.BlockSpec((None, kp, tn), lambda j, i, r=row // kp: (0, r, j + jo)))
        args.append(w)
        row += kp
    assert row == w.shape[1]

    def spec(kind):
        if kind is None:
            return pl.BlockSpec((tm, tn), lambda j, i: (i, j))
        if kind == PROMPT:
            return pl.BlockSpec((tm, tn), lambda j, i: (jnp.minimum(i, pt - 1), j))
        if kind == SAMPLE:
            return pl.BlockSpec((tm, tn), lambda j, i: (jnp.maximum(i - pt, 0), j))
        return pl.BlockSpec((tm, kind), lambda j, i: (i, 0))

    for a, kind in extras:
        in_specs.append(spec(kind))
        args.append(a)
    body = functools.partial(_matmul_body, n_lhs=len(xs), n_extra=len(extras), n_out=len(outs),
                             epilogue=epilogue if n_prompt is None else functools.partial(epilogue, pt))
    res = pl.pallas_call(
        body,
        out_shape=[o for o, _ in outs],
        grid=(n_cols // tn, m // tm),
        in_specs=in_specs,
        out_specs=[spec(kind) for _, kind in outs],
        scratch_shapes=[pltpu.VMEM((x.shape[1], tn), BF16) for x in xs],
        compiler_params=_params("arbitrary", "arbitrary"),
        name=name,
    )(*args)
    return res


def _ep_store(scale):
    def ep(acc, extra, outs):
        for o in outs:
            o[...] = (acc * scale).astype(o.dtype) if scale != 1.0 else acc.astype(o.dtype)
    return ep


def _ep_store_split(prompt_tiles, acc, extra, outs):
    is_prompt = pl.program_id(1) < prompt_tiles

    @pl.when(is_prompt)
    def _():
        outs[0][...] = acc

    @pl.when(jnp.logical_not(is_prompt))
    def _():
        outs[1][...] = acc

    outs[2][...] = acc.astype(outs[2].dtype)


def _ep_residual(acc, extra, outs):
    outs[0][...] = extra[0][...] + acc


def _ep_residual_split(prompt_tiles, acc, extra, outs):
    res = jnp.where(pl.program_id(1) < prompt_tiles, extra[0][...], extra[1][...])
    outs[0][...] = res + acc


def _ep_rotary(scale):
    half = RET_HEAD_DIM // 2

    def ep(acc, extra, outs):
        cos = extra[0][...]
        sin = extra[1][...]
        for h in range(acc.shape[1] // RET_HEAD_DIM):
            lo = h * RET_HEAD_DIM
            x1 = acc[:, lo:lo + half]
            x2 = acc[:, lo + half:lo + RET_HEAD_DIM]
            outs[0][:, lo:lo + half] = ((x1 * cos - x2 * sin) * scale).astype(outs[0].dtype)
            outs[0][:, lo + half:lo + RET_HEAD_DIM] = ((x1 * sin + x2 * cos) * scale).astype(outs[0].dtype)
    return ep


def _retention_body(lg_ref, q_ref, k_ref, v_ref, g_ref, gn_ref, s0_ref, *rest, rows, n_heads):
    o_ref, sout_ref, s_scr, decay_scr = rest[-4:]
    heads = range(n_heads)
    cols = lambda h: slice(h * RET_HEAD_DIM, (h + 1) * RET_HEAD_DIM)
    lgs = [lg_ref[pl.program_id(1) * n_heads + h] for h in heads]

    @pl.when(pl.program_id(2) == 0)
    def _():
        s_scr[...] = s0_ref[...]
        ii = lax.broadcasted_iota(jnp.int32, (rows, rows), 0)
        jj = lax.broadcasted_iota(jnp.int32, (rows, rows), 1)
        shift = CHUNK.bit_length() - 1
        dist = jnp.abs(ii - jj).astype(F32)
        for h in heads:
            decay_scr[h] = jnp.where((jj >> shift) <= (ii >> shift), jnp.exp(lgs[h] * dist), 0.0)

    idx = lax.broadcasted_iota(jnp.int32, (rows, 1), 0).astype(F32)
    qs = [q_ref[:, cols(h)] for h in heads]
    ks = [k_ref[:, cols(h)] for h in heads]
    vs = [v_ref[:, cols(h)] for h in heads]
    scores = [lax.dot_general(qs[h], ks[h], _NT, preferred_element_type=F32) * decay_scr[h] for h in heads]
    outs = [jnp.dot(scores[h].astype(BF16), vs[h], preferred_element_type=F32) for h in heads]
    states = [s_scr[h] for h in heads]
    outs = [outs[h] + jnp.exp(lgs[h] * (idx + 1.0))
            * jnp.dot(qs[h], states[h].astype(BF16), preferred_element_type=F32) for h in heads]
    kds = [(ks[h].astype(F32) * jnp.exp(lgs[h] * (rows - 1.0 - idx))).astype(BF16) for h in heads]
    for h in heads:
        s_new = (jnp.exp(lgs[h] * rows) * states[h]
                 + lax.dot_general(kds[h], vs[h], _TN, preferred_element_type=F32))
        s_scr[h] = s_new
        sout_ref[h] = s_new
    for h in heads:
        mu = jnp.mean(outs[h], axis=-1, keepdims=True)
        cen = outs[h] - mu
        var = jnp.mean(cen * cen, axis=-1, keepdims=True)
        yr = cen * lax.rsqrt(var + EPS) * gn_ref[h]
        g = g_ref[:, cols(h)]
        o_ref[:, cols(h)] = (g * _sigmoid(g) * yr).astype(o_ref.dtype)


def _into(out_buf):
    if out_buf is None:
        return [], []
    return [out_buf], [pl.BlockSpec(memory_space=pl.ANY)]


def _retention(log_gamma, q, k, v, gate, gn, s0, row_off, n_batch, n_steps, rows, out_buf=None):
    heads = q.shape[1] // RET_HEAD_DIM
    nh = min(RET_HEADS_PER_STEP, heads)
    assert row_off % rows == 0 and rows % CHUNK == 0 and heads % nh == 0
    ro = row_off // rows
    tok = lambda b, h, c, lg: (ro + b * n_steps + c, h)
    blk = pl.BlockSpec((rows, nh * RET_HEAD_DIM), tok)
    st = pl.BlockSpec((None, nh, RET_HEAD_DIM, RET_HEAD_DIM), lambda b, h, c, lg: (b, h, 0, 0))
    args = [log_gamma, q, k, v, gate, gn.reshape(heads, 1, RET_HEAD_DIM), s0]
    extra_args, extra_specs = _into(out_buf)
    grid_spec = pltpu.PrefetchScalarGridSpec(
        num_scalar_prefetch=1,
        grid=(n_batch, heads // nh, n_steps),
        in_specs=[blk, blk, blk, blk,
                  pl.BlockSpec((nh, 1, RET_HEAD_DIM), lambda b, h, c, lg: (h, 0, 0)), st] + extra_specs,
        out_specs=[blk, st],
        scratch_shapes=[pltpu.VMEM((nh, RET_HEAD_DIM, RET_HEAD_DIM), F32), pltpu.VMEM((nh, rows, rows), F32)],
    )
    return pl.pallas_call(
        functools.partial(_retention_body, rows=rows, n_heads=nh),
        out_shape=[jax.ShapeDtypeStruct((q.shape[0], heads * RET_HEAD_DIM), BF16),
                   jax.ShapeDtypeStruct((n_batch, heads, RET_HEAD_DIM, RET_HEAD_DIM), F32)],
        grid_spec=grid_spec,
        input_output_aliases={len(args): 0} if extra_args else {},
        compiler_params=_params("arbitrary", "arbitrary", "arbitrary"),
        name="retention",
    )(*args, *extra_args)


def _neg_suffix(n):
    j = lax.broadcasted_iota(jnp.int32, (2 * n, n), 0) & (n - 1)
    s = lax.broadcasted_iota(jnp.int32, (2 * n, n), 1)
    return jnp.where(j >= s, -1.0, 0.0).astype(BF16)


def _lanes_to(x, n):
    return x[:, :n] if n <= LANES else jnp.concatenate([x] * (n // LANES), axis=1)


def _suffix_sums(fail, neg_ones):
    tk = fail.shape[1]
    hi, lo = _split_bf16(fail)
    if tk % LANES == 0:
        return jnp.dot(jnp.concatenate([hi, lo], axis=1), neg_ones, preferred_element_type=F32)
    return (jnp.dot(hi, neg_ones[:tk], preferred_element_type=F32)
            + jnp.dot(lo, neg_ones[:tk], preferred_element_type=F32))


def _sb_tiles(qs, kbs, vbs, neg_ones, carry, acc, masked):
    heads = range(len(qs))
    tk = kbs[0].shape[0]
    zs = [lax.dot_general(qs[h], kbs[h], _NT, preferred_element_type=F32) for h in heads]
    fails = [jnp.maximum(z, 0.0) + jnp.log2(1.0 + jnp.exp2(-jnp.abs(z))) for z in zs]
    if masked:
        t = lax.broadcasted_iota(jnp.int32, zs[0].shape, 0)
        s = lax.broadcasted_iota(jnp.int32, zs[0].shape, 1)
        valid = s < t
        fails = [jnp.where(valid, f, 0.0) for f in fails]
    suffixes = [_suffix_sums(f, neg_ones) for f in fails]
    probs = [jnp.exp2(zs[h] + suffixes[h] + _lanes_to(carry[h], tk)) for h in heads]
    if masked:
        probs = [jnp.where(valid, p, 0.0) for p in probs]
    for h in heads:
        acc[h] += jnp.dot(probs[h].astype(BF16), vbs[h], preferred_element_type=F32)
        carry[h] += jnp.broadcast_to(suffixes[h][:, :1], carry.shape[1:])


def _sb_body(q_ref, kd_ref, vd_ref, kp_ref, vp_ref, *rest, tk, n_heads, past_per_step, past_static):
    o_ref, carry, acc = rest[-3:]
    tq = q_ref.shape[0]
    n_past = past_static if past_static is not None else pl.program_id(1) * past_per_step
    cols = lambda h: slice(h * SB_HEAD_DIM, (h + 1) * SB_HEAD_DIM)
    heads = range(n_heads)
    past = (lambda ref, h, rows: ref[rows, h, :]) if len(kp_ref.shape) == 3 else (
        lambda ref, h, rows: ref[rows, cols(h)])
    carry[...] = jnp.zeros(carry.shape, F32)
    acc[...] = jnp.zeros(acc.shape, F32)
    queries = lambda: [q_ref[:, cols(h)] for h in heads]
    _sb_tiles(queries(), [kd_ref[:, cols(h)].astype(BF16) for h in heads],
              [vd_ref[:, cols(h)].astype(BF16) for h in heads], _neg_suffix(tq), carry, acc, True)
    neg_ones = _neg_suffix(tk)

    alive = lambda: jnp.max(carry[...]) > SB_DEAD_LOG2

    def step(state):
        it = state[0]
        rows = pl.ds(pl.multiple_of((n_past - 1 - it) * tk, tk), tk)
        _sb_tiles(queries(), [past(kp_ref, h, rows).astype(BF16) for h in heads],
                  [past(vp_ref, h, rows).astype(BF16) for h in heads], neg_ones, carry, acc, False)
        return it + 1, alive()

    lax.while_loop(lambda state: (state[0] < n_past) & state[1], step, (jnp.int32(0), alive()))
    for h in range(n_heads):
        o_ref[:, cols(h)] = acc[h].astype(o_ref.dtype)


def _sb_scratch(n_heads, tq):
    return [pltpu.VMEM((n_heads, tq, LANES), F32), pltpu.VMEM((n_heads, tq, SB_HEAD_DIM), F32)]


def _sb_prompt(q, k, v, n_tok):
    heads = q.shape[1] // SB_HEAD_DIM
    tq = _tile(n_tok, SB_BLOCK)
    nh = SB_HEADS_PER_STEP
    width = nh * SB_HEAD_DIM
    blk = pl.BlockSpec((tq, width), lambda h, i: (i, h))
    full = pl.BlockSpec((n_tok, width), lambda h, i: (0, h))
    return pl.pallas_call(
        functools.partial(_sb_body, tk=tq, n_heads=nh, past_per_step=1, past_static=None),
        out_shape=jax.ShapeDtypeStruct((q.shape[0], heads * SB_HEAD_DIM), BF16),
        grid=(heads // nh, n_tok // tq),
        in_specs=[blk, blk, blk, full, full],
        out_specs=blk,
        scratch_shapes=_sb_scratch(nh, tq),
        compiler_params=_params("parallel", "arbitrary"),
        name="stick_breaking_prompt",
    )(q, k, v, k, v)


def _sb_sample(q, k, v, cache_k, cache_v, row_off, n_new, out_buf):
    heads = q.shape[1] // SB_HEAD_DIM
    _, n_batch, past, _, _ = cache_k.shape
    tk = _tile(past, SB_BLOCK)
    assert row_off % n_new == 0
    ro = row_off // n_new
    nh = min(F32_SUBLANES, heads)
    assert heads % nh == 0 and (nh == F32_SUBLANES or nh == heads)
    blk = pl.BlockSpec((n_new, nh * SB_HEAD_DIM), lambda b, h: (ro + b, h))
    full = pl.BlockSpec((None, None, past, nh, SB_HEAD_DIM), lambda b, h: (0, b, 0, h, 0))
    extra_args, extra_specs = _into(out_buf)
    return pl.pallas_call(
        functools.partial(_sb_body, tk=tk, n_heads=nh, past_per_step=0, past_static=past // tk),
        out_shape=jax.ShapeDtypeStruct(out_buf.shape, out_buf.dtype),
        grid=(n_batch, heads // nh),
        in_specs=[blk, blk, blk, full, full] + extra_specs,
        out_specs=blk,
        scratch_shapes=_sb_scratch(nh, n_new),
        input_output_aliases={5: 0},
        compiler_params=_params("parallel", "arbitrary"),
        name="stick_breaking_sample",
    )(q, k, v, cache_k, cache_v, *extra_args)


def _cross_head(q, k, v, scale):
    s = lax.dot_general(q, k.astype(BF16), _NT, preferred_element_type=F32) * scale
    e = jnp.exp(s - jnp.max(s, axis=-1, keepdims=True))
    p = e / jnp.sum(e, axis=-1, keepdims=True)
    return jnp.dot(p.astype(BF16), v.astype(BF16), preferred_element_type=F32)


def _cross_body(q_ref, k_ref, v_ref, *rest, scale):
    o_ref = rest[-1]
    o_ref[...] = _cross_head(q_ref[...], k_ref[...], v_ref[...], scale).astype(o_ref.dtype)


def _cross_cache_body(q_ref, k_ref, v_ref, *rest, scale):
    o_ref = rest[-1]
    dh = k_ref.shape[2]
    for h in range(k_ref.shape[1]):
        cols = slice(h * dh, (h + 1) * dh)
        o_ref[:, cols] = _cross_head(q_ref[:, cols], k_ref[:, h, :], v_ref[:, h, :], scale).astype(o_ref.dtype)


def _cross_attend(q, mem_k, mem_v, row_off, rows_per_seq, out_buf=None):
    d = q.shape[1]
    dh = d // MEM_HEADS
    tm = _tile(rows_per_seq, ROW_TILE)
    steps = rows_per_seq // tm
    assert row_off % tm == 0
    ro = row_off // tm
    extra_args, extra_specs = _into(out_buf)
    if mem_k.ndim == 3:
        n_seq, n_mem, _ = mem_k.shape
        body, grid = _cross_body, (n_seq * steps, MEM_HEADS)
        mem = pl.BlockSpec((None, n_mem, dh), lambda i, h: (i // steps, 0, h))
        blk = pl.BlockSpec((tm, dh), lambda i, h: (ro + i, h))
    else:
        _, n_seq, n_mem, _, _ = mem_k.shape
        body, grid = _cross_cache_body, (n_seq * steps,)
        mem = pl.BlockSpec((None, None, n_mem, MEM_HEADS, dh), lambda i: (0, i // steps, 0, 0, 0))
        blk = pl.BlockSpec((tm, d), lambda i: (ro + i, 0))
    return pl.pallas_call(
        functools.partial(body, scale=dh ** -0.5),
        out_shape=jax.ShapeDtypeStruct((q.shape[0], d), BF16),
        grid=grid,
        in_specs=[blk, mem, mem] + extra_specs,
        out_specs=blk,
        input_output_aliases={3: 0} if extra_args else {},
        compiler_params=_params(*("parallel", "arbitrary")[:len(grid)]),
        name="cross_attention",
    )(q, mem_k, mem_v, *extra_args)


def _split_bf16(x):
    hi = x.astype(BF16)
    return hi, (x - hi.astype(F32)).astype(BF16)


def _router_body(x_ref, g_ref, wr_ref, br_ref, h_ref, ids_ref, wts_ref):
    h = _rmsnorm_rows(x_ref[...], g_ref[...])
    h_ref[...] = _pack_bf16_pairs(h)
    h_hi, h_lo = _split_bf16(h)
    w_hi, w_lo = _split_bf16(wr_ref[...])
    logits = (jnp.dot(h_hi, w_hi, preferred_element_type=F32)
              + jnp.dot(h_hi, w_lo, preferred_element_type=F32)
              + jnp.dot(h_lo, w_hi, preferred_element_type=F32)) + br_ref[...]
    lane = lax.broadcasted_iota(jnp.int32, logits.shape, 1).astype(F32)
    big = float(LANES)
    neg = -jnp.inf

    def first_max(vals):
        top = jnp.max(vals, axis=-1, keepdims=True)
        return top, jnp.min(jnp.where(vals == top, lane, big), axis=-1, keepdims=True)

    is_group = lane < N_GROUPS
    g_max, g_top = first_max(jnp.where(is_group, logits, neg))
    g_w = 1.0 / jnp.sum(jnp.where(is_group, jnp.exp(logits - g_max), 0.0), axis=-1, keepdims=True)
    first = N_GROUPS + EXPERTS_PER_GROUP * g_top
    cand = jnp.where((lane >= first) & (lane < first + EXPERTS_PER_GROUP), logits, neg)
    v1, i1 = first_max(cand)
    v2, i2 = first_max(jnp.where(lane == i1, neg, cand))
    t = jnp.exp(v2 - v1)
    w1 = g_w / (1.0 + t)
    w2 = g_w * t / (1.0 + t)
    ids = jnp.where(lane == 0.0, i1 - N_GROUPS, jnp.where(lane == 1.0, i2 - N_GROUPS, 0.0))
    ids_ref[...] = ids.astype(jnp.int32)
    wts_ref[...] = jnp.where(lane == 0.0, w1, jnp.where(lane == 1.0, w2, 0.0))


def _norm_and_route(x, g, wg, bg, we, be):
    m, d = x.shape
    tm = _tile(m, 256)
    pad = LANES - N_GROUPS - N_EXPERTS
    wr = jnp.concatenate([wg, we, jnp.zeros((d, pad), F32)], axis=1)
    br = jnp.concatenate([bg, be, jnp.zeros((pad,), F32)]).reshape(1, LANES)
    row = lambda c: pl.BlockSpec((tm, c), lambda i: (i, 0))
    return pl.pallas_call(
        _router_body,
        out_shape=[jax.ShapeDtypeStruct((m, d // 2), jnp.int32),
                   jax.ShapeDtypeStruct((m, LANES), jnp.int32),
                   jax.ShapeDtypeStruct((m, LANES), F32)],
        grid=(m // tm,),
        in_specs=[row(d), pl.BlockSpec((1, d), lambda i: (0, 0)),
                  pl.BlockSpec((d, LANES), lambda i: (0, 0)), pl.BlockSpec((1, LANES), lambda i: (0, 0))],
        out_specs=[row(d // 2), row(LANES), row(LANES)],
        compiler_params=_params("parallel"),
        name="norm_route",
    )(x, g.reshape(1, d), wr, br)


GATHER_UNROLL = 8


def _start_row_gather(idx_ref, first, stride, src_ref, buf, sem, n_rows):
    def one(r, _):
        t = idx_ref[first + r * stride]
        pltpu.make_async_copy(src_ref.at[pl.ds(t, 1)], buf.at[pl.ds(r, 1)], sem).start()
        return 0

    def group(g, _):
        for u in range(GATHER_UNROLL):
            one(g * GATHER_UNROLL + u, 0)
        return 0

    groups = n_rows // GATHER_UNROLL
    lax.fori_loop(0, groups, group, 0)
    lax.fori_loop(groups * GATHER_UNROLL, n_rows, one, 0)


def _wait_row_gather(src_ref, buf, sem, n_rows):
    pltpu.make_async_copy(src_ref.at[pl.ds(0, n_rows)], buf.at[pl.ds(0, n_rows)], sem).wait()


def _pack_bf16_pairs(x):
    half = x.shape[1] // 2
    hi = lax.bitcast_convert_type(x[:, :half].astype(BF16).astype(F32), jnp.int32)
    lo = lax.bitcast_convert_type(x[:, half:].astype(BF16).astype(F32), jnp.int32)
    return hi | lax.shift_right_logical(lo, jnp.int32(16))


def _unpack_bf16_pairs(p):
    hi = lax.bitcast_convert_type(p & jnp.int32(-65536), F32)
    lo = lax.bitcast_convert_type(lax.shift_left(p, jnp.int32(16)), F32)
    return hi, lo


def _dispatch_body(idx_ref, used_ref, cnt_ref, src_ref, o_ref, buf, sem):
    b = pl.program_id(0)
    rows = buf.shape[1]

    copied = lambda blk: pl.multiple_of(pl.cdiv(cnt_ref[blk], GATHER_UNROLL) * GATHER_UNROLL, GATHER_UNROLL)

    def start(blk):
        _start_row_gather(idx_ref, blk * rows, 1, src_ref, buf.at[blk % 2], sem.at[blk % 2], copied(blk))

    @pl.when(b == 0)
    def _():
        buf[...] = jnp.zeros(buf.shape, buf.dtype)
        start(0)

    @pl.when(b + 1 < used_ref[0])
    def _():
        start(b + 1)

    @pl.when(b < used_ref[0])
    def _():
        slot = b % 2
        cnt = cnt_ref[b]
        _wait_row_gather(src_ref, buf.at[slot], sem.at[slot], copied(b))
        half = buf.shape[2]
        row = lax.broadcasted_iota(jnp.int32, (rows, 1), 0)
        hi, lo = _unpack_bf16_pairs(jnp.where(row < cnt, buf[slot], 0))
        o_ref[:, :half] = hi.astype(o_ref.dtype)
        o_ref[:, half:] = lo.astype(o_ref.dtype)


def _dispatch_rows(src, row_tok, blk_cnt, n_used):
    n_rows = row_tok.shape[0]
    half = src.shape[1]
    grid_spec = pltpu.PrefetchScalarGridSpec(
        num_scalar_prefetch=3,
        grid=(n_rows // MOE_ROWS,),
        in_specs=[pl.BlockSpec(memory_space=pl.ANY)],
        out_specs=pl.BlockSpec((MOE_ROWS, 2 * half), lambda b, idx, used, cnt: (b, 0)),
        scratch_shapes=[pltpu.VMEM((2, MOE_ROWS, half), src.dtype), pltpu.SemaphoreType.DMA((2,))],
    )
    return pl.pallas_call(
        _dispatch_body,
        out_shape=jax.ShapeDtypeStruct((n_rows, 2 * half), BF16),
        grid_spec=grid_spec,
        compiler_params=_params("arbitrary"),
        name="dispatch_rows",
    )(row_tok, n_used, blk_cnt, src)


def _expert_up_body(be_ref, first_ref, used_ref, x_ref, w1_ref, w3_ref, h_ref, w1b, w3b):
    b = pl.program_id(1)

    @pl.when(first_ref[b] == 1)
    def _():
        w1b[...] = w1_ref[...].astype(BF16)
        w3b[...] = w3_ref[...].astype(BF16)

    @pl.when(b < used_ref[0])
    def _():
        x = x_ref[...]
        a = jnp.dot(x, w1b[...], preferred_element_type=F32)
        g = jnp.dot(x, w3b[...], preferred_element_type=F32)
        h_ref[...] = (a * _sigmoid(a) * g).astype(h_ref.dtype)


def _expert_down_body(be_ref, first_ref, used_ref, h_ref, w2_ref, y_ref, w2b):
    b = pl.program_id(1)

    @pl.when(first_ref[b] == 1)
    def _():
        w2b[...] = w2_ref[...].astype(BF16)

    @pl.when(b < used_ref[0])
    def _():
        y_ref[...] = _pack_bf16_pairs(jnp.dot(h_ref[...], w2b[...], preferred_element_type=F32))


def _expert_ffn(xs, blk_exp, blk_first, n_used, w1, w3, w2):
    n_rows, d = xs.shape
    f = w1.shape[2]
    n_blocks = n_rows // MOE_ROWS
    tf = _tile(f, EXPERT_HIDDEN_TILE)
    tn = _tile(d, EXPERT_OUT_TILE)
    last = lambda b, used: jnp.minimum(b, used[0] - 1)
    up_spec = pltpu.PrefetchScalarGridSpec(
        num_scalar_prefetch=3,
        grid=(f // tf, n_blocks),
        in_specs=[pl.BlockSpec((MOE_ROWS, d), lambda j, b, be, fi, us: (last(b, us), 0)),
                  pl.BlockSpec((None, d, tf), lambda j, b, be, fi, us: (be[b], 0, j)),
                  pl.BlockSpec((None, d, tf), lambda j, b, be, fi, us: (be[b], 0, j))],
        out_specs=pl.BlockSpec((MOE_ROWS, tf), lambda j, b, be, fi, us: (b, j)),
        scratch_shapes=[pltpu.VMEM((d, tf), BF16), pltpu.VMEM((d, tf), BF16)],
    )
    hidden = pl.pallas_call(
        _expert_up_body,
        out_shape=jax.ShapeDtypeStruct((n_rows, f), BF16),
        grid_spec=up_spec,
        compiler_params=_params("arbitrary", "arbitrary"),
        name="expert_up",
    )(blk_exp, blk_first, n_used, xs, w1, w3)
    down_spec = pltpu.PrefetchScalarGridSpec(
        num_scalar_prefetch=3,
        grid=(d // tn, n_blocks),
        in_specs=[pl.BlockSpec((MOE_ROWS, f), lambda j, b, be, fi, us: (last(b, us), 0)),
                  pl.BlockSpec((None, f, tn), lambda j, b, be, fi, us: (be[b], 0, j))],
        out_specs=pl.BlockSpec((MOE_ROWS, tn // 2), lambda j, b, be, fi, us: (b, j)),
        scratch_shapes=[pltpu.VMEM((f, tn), BF16)],
    )
    return pl.pallas_call(
        _expert_down_body,
        out_shape=jax.ShapeDtypeStruct((n_rows, d // 2), jnp.int32),
        grid_spec=down_spec,
        compiler_params=_params("arbitrary", "arbitrary"),
        name="expert_down",
    )(blk_exp, blk_first, n_used, hidden, w2)


def _route_layout(ids, n_tok):
    n_assign = n_tok * TOP_K
    flat_e = ids.reshape(n_assign)
    flat_t = jnp.repeat(jnp.arange(n_tok, dtype=jnp.int32), TOP_K)
    onehot = (flat_e[:, None] == jnp.arange(N_EXPERTS, dtype=jnp.int32)[None, :]).astype(jnp.int32)
    running = jnp.cumsum(onehot, axis=0)
    rank = jnp.take_along_axis(running, flat_e[:, None], axis=1)[:, 0] - 1
    counts = running[-1]
    padded = (counts + MOE_ROWS - 1) // MOE_ROWS * MOE_ROWS
    pad_end = jnp.cumsum(padded)
    dest = (pad_end - padded)[flat_e] + rank
    n_blocks = -(-n_assign // MOE_ROWS) + N_EXPERTS
    n_rows = n_blocks * MOE_ROWS
    row_tok = jnp.zeros((n_rows,), jnp.int32).at[dest].set(flat_t)
    blk = jnp.arange(n_blocks, dtype=jnp.int32)
    n_used = (pad_end[-1] // MOE_ROWS).astype(jnp.int32)
    blk_exp = jnp.searchsorted(pad_end, jnp.minimum(blk, n_used - 1) * MOE_ROWS, side='right').astype(jnp.int32)
    blk_exp = jnp.minimum(blk_exp, N_EXPERTS - 1)
    prev = jnp.concatenate([jnp.full((1,), -1, jnp.int32), blk_exp[:-1]])
    blk_first = ((blk_exp != prev) & (blk < n_used)).astype(jnp.int32)
    into_range = blk * MOE_ROWS - (pad_end - padded)[blk_exp]
    blk_cnt = jnp.where(blk < n_used, jnp.clip(counts[blk_exp] - into_range, 0, MOE_ROWS), 0).astype(jnp.int32)
    return row_tok, blk_cnt, dest.astype(jnp.int32), blk_exp, blk_first, n_used.reshape(1)


def _final_body(dest_ref, x_ref, w_ref, y_ref, g_ref, op_ref, os_ref, buf, sem, *, prompt_tiles, pack_tile):
    i = pl.program_id(0)
    tm = buf.shape[2]

    def start(blk):
        for k in range(TOP_K):
            _start_row_gather(dest_ref, blk * tm * TOP_K + k, TOP_K, y_ref, buf.at[blk % 2, k], sem.at[blk % 2, k],
                              tm)

    @pl.when(i == 0)
    def _():
        start(0)

    @pl.when(i + 1 < pl.num_programs(0))
    def _():
        start(i + 1)

    slot = i % 2
    for k in range(TOP_K):
        _wait_row_gather(y_ref, buf.at[slot, k], sem.at[slot, k], tm)
    w = w_ref[...]
    pieces = []
    for j in range(buf.shape[3] // pack_tile):
        words = slice(j * pack_tile, (j + 1) * pack_tile)
        hi0, lo0 = _unpack_bf16_pairs(buf[slot, 0, :, words])
        hi1, lo1 = _unpack_bf16_pairs(buf[slot, 1, :, words])
        pieces += [hi0 * w[:, 0:1] + hi1 * w[:, 1:2], lo0 * w[:, 0:1] + lo1 * w[:, 1:2]]
    y = _rmsnorm_rows(x_ref[...] + jnp.concatenate(pieces, axis=1), g_ref[...])

    @pl.when(i < prompt_tiles)
    def _():
        op_ref[...] = y

    @pl.when(i >= prompt_tiles)
    def _():
        os_ref[...] = y


def _combine_and_norm(x, wts, y_rows, dest, g, n_prompt):
    m, d = x.shape
    tm = _tile(m - n_prompt, _tile(n_prompt, 256))
    assert n_prompt % tm == 0
    pt = n_prompt // tm
    grid_spec = pltpu.PrefetchScalarGridSpec(
        num_scalar_prefetch=1,
        grid=(m // tm,),
        in_specs=[pl.BlockSpec((tm, d), lambda i, dest: (i, 0)),
                  pl.BlockSpec((tm, LANES), lambda i, dest: (i, 0)),
                  pl.BlockSpec(memory_space=pl.ANY),
                  pl.BlockSpec((1, d), lambda i, dest: (0, 0))],
        out_specs=[pl.BlockSpec((tm, d), lambda i, dest: (jnp.minimum(i, pt - 1), 0)),
                   pl.BlockSpec((tm, d), lambda i, dest: (jnp.maximum(i - pt, 0), 0))],
        scratch_shapes=[pltpu.VMEM((2, TOP_K, tm, d // 2), y_rows.dtype), pltpu.SemaphoreType.DMA((2, TOP_K))],
    )
    return pl.pallas_call(
        functools.partial(_final_body, prompt_tiles=pt, pack_tile=_tile(d, EXPERT_OUT_TILE) // 2),
        out_shape=[jax.ShapeDtypeStruct((n_prompt, d), F32), jax.ShapeDtypeStruct((m - n_prompt, d), F32)],
        grid_spec=grid_spec,
        compiler_params=_params("arbitrary"),
        name="combine_norm",
    )(dest, x, wts, y_rows, g.reshape(1, d))


def _layer(xp, xs, n_seq, n_new, past_len, cache_sb_k, cache_sb_v, state_ret, cache_mem_k, cache_mem_v,
           mem_prompt, norm_mix, w_in, ret_gn, w_out, norm_cross, norm_mem, w_cq, w_ck, w_cv, w_co, norm_ffn,
           wg, bg, we, be, w1, w3, w2):
    n_prompt, d = xp.shape
    n_tok = n_prompt + xs.shape[0]
    d_ret = d // 2
    d_sb = d - d_ret
    ret_heads = d_ret // RET_HEAD_DIM
    sds = jax.ShapeDtypeStruct

    pos = np.concatenate([np.arange(n_prompt), np.tile(past_len + np.arange(n_new), n_seq)])
    half = RET_HEAD_DIM // 2
    inv = (ROPE_BASE ** (-np.arange(half, dtype=np.float32) / half)).astype(np.float32)
    ang = pos.astype(np.float32)[:, None] * inv[None, :]
    cos, sin = jnp.asarray(np.cos(ang), F32), jnp.asarray(np.sin(ang), F32)
    log_gamma = jnp.log1p(-jnp.exp2(-5.0 - jnp.arange(ret_heads, dtype=F32)))

    h = _rmsnorm_stacked(xp, xs, norm_mix, BF16)
    rot = [(cos, half), (sin, half)]
    bf = lambda n: [(sds((n_tok, n), BF16), None)]
    (rq,) = _matmul([h], w_in, 0, d_ret, _ep_rotary(1.0), rot, bf(d_ret), "in_proj_ret_q")
    (rk,) = _matmul([h], w_in, d_ret, d_ret, _ep_rotary(RET_HEAD_DIM ** -0.5), rot, bf(d_ret), "in_proj_ret_k")
    (rv,) = _matmul([h], w_in, 2 * d_ret, d_ret, _ep_store(1.0), [], bf(d_ret), "in_proj_ret_v")
    (rg,) = _matmul([h], w_in, 3 * d_ret, d_ret, _ep_store(1.0), [], [(sds((n_tok, d_ret), F32), None)],
                    "in_proj_ret_gate")
    (sq,) = _matmul([h], w_in, 4 * d_ret, d_sb, _ep_store(SB_HEAD_DIM ** -0.5 * LOG2_E), [], bf(d_sb),
                    "in_proj_sb_q")
    split = [(sds((n_prompt, d_sb), F32), PROMPT), (sds((n_tok - n_prompt, d_sb), F32), SAMPLE),
             (sds((n_tok, d_sb), BF16), None)]
    sk_p, sk_s, sk_b = _matmul([h], w_in, 4 * d_ret + d_sb, d_sb, _ep_store_split, [], split, "in_proj_sb_k",
                               n_prompt=n_prompt)
    sv_p, sv_s, sv_b = _matmul([h], w_in, 4 * d_ret + 2 * d_sb, d_sb, _ep_store_split, [], split,
                               "in_proj_sb_v", n_prompt=n_prompt)

    rows_p = _tile(n_prompt, RET_BLOCK)
    zeros_state = jnp.zeros((1, ret_heads, RET_HEAD_DIM, RET_HEAD_DIM), F32)
    yr, state_p = _retention(log_gamma, rq, rk, rv, rg, ret_gn, zeros_state, 0, 1, n_prompt // rows_p, rows_p)
    yr, state_s = _retention(log_gamma, rq, rk, rv, rg, ret_gn, state_ret, n_prompt, n_seq, 1, n_new,
                             out_buf=yr)
    sb = _sb_prompt(sq, sk_b, sv_b, n_prompt)
    sb = _sb_sample(sq, sk_b, sv_b, cache_sb_k, cache_sb_v, n_prompt, n_new, sb)
    res = [(sds((n_tok, d), F32), None)]
    (x,) = _matmul([yr, sb], w_out, 0, d, _ep_residual_split, [(xp, PROMPT), (xs, SAMPLE)], res, "out_proj",
                   n_prompt=n_prompt)

    n_mem = mem_prompt.shape[0]
    m = _rmsnorm(mem_prompt, norm_mem, BF16)
    mem_out = [(sds((n_mem, d), F32), None)]
    (mk,) = _matmul([m], w_ck, 0, d, _ep_store(1.0), [], mem_out, "mem_k")
    (mv,) = _matmul([m], w_cv, 0, d, _ep_store(1.0), [], mem_out, "mem_v")
    h = _rmsnorm(x, norm_cross, BF16)
    (cq,) = _matmul([h], w_cq, 0, d, _ep_store(1.0), [], bf(d), "cross_q")
    co = _cross_attend(cq, mk.reshape(1, n_mem, d), mv.reshape(1, n_mem, d), 0, n_prompt)
    co = _cross_attend(cq, cache_mem_k, cache_mem_v, n_prompt, n_new, out_buf=co)
    (x,) = _matmul([co], w_co, 0, d, _ep_residual, [(x, None)], res, "cross_out")

    hn, ids, wts = _norm_and_route(x, norm_ffn, wg, bg, we, be)
    row_tok, blk_cnt, dest, blk_exp, blk_first, n_used = _route_layout(ids[:, :TOP_K], n_tok)
    x_rows = _dispatch_rows(hn, row_tok, blk_cnt, n_used)
    y_rows = _expert_ffn(x_rows, blk_exp, blk_first, n_used, w1, w3, w2)
    return x, wts, y_rows, dest, (sk_p, sv_p, sk_s, sv_s, state_p, state_s, mk, mv)


def kernel(x_prompt, x_sample, cache_sb_k, cache_sb_v, state_ret, cache_mem_k, cache_mem_v, mem_prompt, norm_mix, w_in, ret_gn, w_out, norm_cross, norm_mem, w_cq, w_ck, w_cv, w_co, norm_ffn, router_group_w, router_group_b, router_expert_w, router_expert_b, expert_w1, expert_w3, expert_w2, norm_final):
    batch, n_prompt, d = x_prompt.shape
    n_seq, n_new, _ = x_sample.shape
    depth, _, past_len, sb_heads, _ = cache_sb_k.shape
    n_mem = mem_prompt.shape[1]
    assert batch == 1 and depth == 1
    ret_heads = (d // 2) // RET_HEAD_DIM
    x, wts, y_rows, dest, (sk_p, sv_p, sk_s, sv_s, state_p, state_s, mk, mv) = _layer(
        x_prompt.reshape(n_prompt, d), x_sample.reshape(n_seq * n_new, d), n_seq, n_new, past_len,
        cache_sb_k, cache_sb_v, state_ret[0], cache_mem_k, cache_mem_v, mem_prompt[0], norm_mix[0], w_in,
        ret_gn[0], w_out, norm_cross[0], norm_mem[0], w_cq, w_ck, w_cv, w_co, norm_ffn[0], router_group_w[0],
        router_group_b[0], router_expert_w[0], router_expert_b[0], expert_w1[0], expert_w3[0], expert_w2[0])
    y_prompt, y_sample = _combine_and_norm(x, wts, y_rows, dest, norm_final, n_prompt)
    y_prompt = y_prompt.reshape(1, n_prompt, d)
    y_sample = y_sample.reshape(n_seq, n_new, d)
    sb_shape_p = (1, 1, n_prompt, sb_heads, SB_HEAD_DIM)
    sb_shape_s = (1, n_seq, n_new, sb_heads, SB_HEAD_DIM)
    mem_shape = (1, 1, n_mem, MEM_HEADS, d // MEM_HEADS)
    return (y_prompt, y_sample,
            sk_p.reshape(sb_shape_p), sv_p.reshape(sb_shape_p),
            state_p.reshape(1, 1, ret_heads, RET_HEAD_DIM, RET_HEAD_DIM),
            mk.reshape(mem_shape), mv.reshape(mem_shape),
            sk_s.reshape(sb_shape_s), sv_s.reshape(sb_shape_s),
            state_s.reshape(1, n_seq, ret_heads, RET_HEAD_DIM, RET_HEAD_DIM))
```

```python
import functools

import jax
import jax.numpy as jnp
import numpy as np
from jax import lax
from jax.experimental import pallas as pl
from jax.experimental.pallas import tpu as pltpu

BF16 = jnp.bfloat16
F32 = jnp.float32

EPS = 1e-6
CHUNK = 64
RET_HEAD_DIM = 256
SB_HEAD_DIM = 128
MEM_HEADS = 4
N_GROUPS = 4
EXPERTS_PER_GROUP = 8
N_EXPERTS = N_GROUPS * EXPERTS_PER_GROUP
TOP_K = 2
ROPE_BASE = 10000.0

LANES = 128
F32_SUBLANES = 8
VMEM_LIMIT_BYTES = 60 * 1024 * 1024
MATMUL_VMEM_BUDGET = 56 * 1024 * 1024
ROW_TILE = 512
COL_TILE = 1024
EXPERT_HIDDEN_TILE = 512
RET_BLOCK = 256
RET_HEADS_PER_STEP = 4
SB_BLOCK = 256
SB_HEADS_PER_STEP = 4
LOG2_E = 1.4426950408889634
SB_DEAD_LOG2 = -64.0
MOE_ROWS = 512
EXPERT_OUT_TILE = 2048

PROMPT = "prompt"
SAMPLE = "sample"

_NT = (((1,), (1,)), ((), ()))
_TN = (((0,), (0,)), ((), ()))


def _tile(n, pref):
    t = min(pref, n)
    while n % t:
        t //= 2
    return t


def _params(*sem):
    return pltpu.CompilerParams(dimension_semantics=sem, vmem_limit_bytes=VMEM_LIMIT_BYTES)


def _sigmoid(x):
    return 1.0 / (1.0 + jnp.exp(-x))


def _rmsnorm_rows(x, g):
    ms = jnp.mean(x * x, axis=-1, keepdims=True)
    return x * lax.rsqrt(ms + EPS) * g


def _rmsnorm_body(x_ref, g_ref, o_ref):
    o_ref[...] = _rmsnorm_rows(x_ref[...], g_ref[...]).astype(o_ref.dtype)


def _rmsnorm(x, g, out_dtype):
    m, d = x.shape
    tm = _tile(m, 256)
    return pl.pallas_call(
        _rmsnorm_body,
        out_shape=jax.ShapeDtypeStruct((m, d), out_dtype),
        grid=(m // tm,),
        in_specs=[pl.BlockSpec((tm, d), lambda i: (i, 0)), pl.BlockSpec((1, d), lambda i: (0, 0))],
        out_specs=pl.BlockSpec((tm, d), lambda i: (i, 0)),
        compiler_params=_params("parallel"),
        name="rmsnorm",
    )(x, g.reshape(1, d))


def _rmsnorm_stacked_body(xp_ref, xs_ref, g_ref, o_ref, *, prompt_tiles):
    is_prompt = pl.program_id(0) < prompt_tiles

    @pl.when(is_prompt)
    def _():
        o_ref[...] = _rmsnorm_rows(xp_ref[...], g_ref[...]).astype(o_ref.dtype)

    @pl.when(jnp.logical_not(is_prompt))
    def _():
        o_ref[...] = _rmsnorm_rows(xs_ref[...], g_ref[...]).astype(o_ref.dtype)


def _rmsnorm_stacked(xp, xs, g, out_dtype):
    (n_p, d), n_s = xp.shape, xs.shape[0]
    tm = _tile(n_s, _tile(n_p, 256))
    pt = n_p // tm
    return pl.pallas_call(
        functools.partial(_rmsnorm_stacked_body, prompt_tiles=pt),
        out_shape=jax.ShapeDtypeStruct((n_p + n_s, d), out_dtype),
        grid=((n_p + n_s) // tm,),
        in_specs=[pl.BlockSpec((tm, d), lambda i: (jnp.minimum(i, pt - 1), 0)),
                  pl.BlockSpec((tm, d), lambda i: (jnp.maximum(i - pt, 0), 0)),
                  pl.BlockSpec((1, d), lambda i: (0, 0))],
        out_specs=pl.BlockSpec((tm, d), lambda i: (i, 0)),
        compiler_params=_params("arbitrary"),
        name="rmsnorm_stacked",
    )(xp, xs, g.reshape(1, d))


def _matmul_body(*refs, n_lhs, n_extra, n_out, epilogue):
    x_refs = refs[:n_lhs]
    w_refs = refs[n_lhs:2 * n_lhs]
    extra = refs[2 * n_lhs:2 * n_lhs + n_extra]
    outs = refs[2 * n_lhs + n_extra:2 * n_lhs + n_extra + n_out]
    wb = refs[2 * n_lhs + n_extra + n_out:]

    @pl.when(pl.program_id(1) == 0)
    def _():
        for p in range(n_lhs):
            wb[p][...] = w_refs[p][...].astype(BF16)

    acc = jnp.dot(x_refs[0][...], wb[0][...], preferred_element_type=F32)
    for p in range(1, n_lhs):
        acc += jnp.dot(x_refs[p][...], wb[p][...], preferred_element_type=F32)
    epilogue(acc, extra, outs)


def _matmul(xs, w, col_off, n_cols, epilogue, extras, outs, name, n_prompt=None):
    m = xs[0].shape[0]
    tm = _tile(m, ROW_TILE) if n_prompt is None else _tile(m - n_prompt, _tile(n_prompt, ROW_TILE))
    k_total = sum(x.shape[1] for x in xs)

    def vmem_bytes(tn):
        tiles = sum((tn if kind is None or isinstance(kind, str) else kind) * jnp.dtype(a.dtype).itemsize
                    for a, kind in list(extras) + list(outs))
        return k_total * tn * (2 * 4 + 2) + 2 * tm * k_total * 2 + 2 * tm * tiles

    tn = _tile(n_cols, COL_TILE)
    while vmem_bytes(tn) > MATMUL_VMEM_BUDGET and tn > LANES:
        tn //= 2
    assert col_off % tn == 0
    jo = col_off // tn
    pt = None if n_prompt is None else n_prompt // tm
    in_specs, args = [], []
    for x in xs:
        in_specs.append(pl.BlockSpec((tm, x.shape[1]), lambda j, i: (i, 0)))
        args.append(x)
    row = 0
    for x in xs:
        kp = x.shape[1]
        assert row % kp == 0
        in_specs.append(pl.BlockSpec((None, kp, tn), lambda j, i, r=row // kp: (0, r, j + jo)))
        args.append(w)
        row += kp
    assert row == w.shape[1]

    def spec(kind):
        if kind is None:
            return pl.BlockSpec((tm, tn), lambda j, i: (i, j))
        if kind == PROMPT:
            return pl.BlockSpec((tm, tn), lambda j, i: (jnp.minimum(i, pt - 1), j))
        if kind == SAMPLE:
            return pl.BlockSpec((tm, tn), lambda j, i: (jnp.maximum(i - pt, 0), j))
        return pl.BlockSpec((tm, kind), lambda j, i: (i, 0))

    for a, kind in extras:
        in_specs.append(spec(kind))
        args.append(a)
    body = functools.partial(_matmul_body, n_lhs=len(xs), n_extra=len(extras), n_out=len(outs),
                             epilogue=epilogue if n_prompt is None else functools.partial(epilogue, pt))
    res = pl.pallas_call(
        body,
        out_shape=[o for o, _ in outs],
        grid=(n_cols // tn, m // tm),
        in_specs=in_specs,
        out_specs=[spec(kind) for _, kind in outs],
        scratch_shapes=[pltpu.VMEM((x.shape[1], tn), BF16) for x in xs],
        compiler_params=_params("arbitrary", "arbitrary"),
        name=name,
    )(*args)
    return res


def _ep_store(scale):
    def ep(acc, extra, outs):
        for o in outs:
            o[...] = (acc * scale).astype(o.dtype) if scale != 1.0 else acc.astype(o.dtype)
    return ep


def _ep_store_split(prompt_tiles, acc, extra, outs):
    is_prompt = pl.program_id(1) < prompt_tiles

    @pl.when(is_prompt)
    def _():
        outs[0][...] = acc

    @pl.when(jnp.logical_not(is_prompt))
    def _():
        outs[1][...] = acc

    outs[2][...] = acc.astype(outs[2].dtype)


def _ep_residual(acc, extra, outs):
    outs[0][...] = extra[0][...] + acc


def _ep_residual_split(prompt_tiles, acc, extra, outs):
    res = jnp.where(pl.program_id(1) < prompt_tiles, extra[0][...], extra[1][...])
    outs[0][...] = res + acc


def _ep_rotary(scale):
    half = RET_HEAD_DIM // 2

    def ep(acc, extra, outs):
        cos = extra[0][...]
        sin = extra[1][...]
        for h in range(acc.shape[1] // RET_HEAD_DIM):
            lo = h * RET_HEAD_DIM
            x1 = acc[:, lo:lo + half]
            x2 = acc[:, lo + half:lo + RET_HEAD_DIM]
            outs[0][:, lo:lo + half] = ((x1 * cos - x2 * sin) * scale).astype(outs[0].dtype)
            outs[0][:, lo + half:lo + RET_HEAD_DIM] = ((x1 * sin + x2 * cos) * scale).astype(outs[0].dtype)
    return ep


def _retention_body(lg_ref, q_ref, k_ref, v_ref, g_ref, gn_ref, s0_ref, *rest, rows, n_heads):
    o_ref, sout_ref, s_scr, decay_scr = rest[-4:]
    heads = range(n_heads)
    cols = lambda h: slice(h * RET_HEAD_DIM, (h + 1) * RET_HEAD_DIM)
    lgs = [lg_ref[pl.program_id(1) * n_heads + h] for h in heads]

    @pl.when(pl.program_id(2) == 0)
    def _():
        s_scr[...] = s0_ref[...]
        ii = lax.broadcasted_iota(jnp.int32, (rows, rows), 0)
        jj = lax.broadcasted_iota(jnp.int32, (rows, rows), 1)
        shift = CHUNK.bit_length() - 1
        dist = jnp.abs(ii - jj).astype(F32)
        for h in heads:
            decay_scr[h] = jnp.where((jj >> shift) <= (ii >> shift), jnp.exp(lgs[h] * dist), 0.0)

    idx = lax.broadcasted_iota(jnp.int32, (rows, 1), 0).astype(F32)
    qs = [q_ref[:, cols(h)] for h in heads]
    ks = [k_ref[:, cols(h)] for h in heads]
    vs = [v_ref[:, cols(h)] for h in heads]
    scores = [lax.dot_general(qs[h], ks[h], _NT, preferred_element_type=F32) * decay_scr[h] for h in heads]
    outs = [jnp.dot(scores[h].astype(BF16), vs[h], preferred_element_type=F32) for h in heads]
    states = [s_scr[h] for h in heads]
    outs = [outs[h] + jnp.exp(lgs[h] * (idx + 1.0))
            * jnp.dot(qs[h], states[h].astype(BF16), preferred_element_type=F32) for h in heads]
    kds = [(ks[h].astype(F32) * jnp.exp(lgs[h] * (rows - 1.0 - idx))).astype(BF16) for h in heads]
    for h in heads:
        s_new = (jnp.exp(lgs[h] * rows) * states[h]
                 + lax.dot_general(kds[h], vs[h], _TN, preferred_element_type=F32))
        s_scr[h] = s_new
        sout_ref[h] = s_new
    for h in heads:
        mu = jnp.mean(outs[h], axis=-1, keepdims=True)
        cen = outs[h] - mu
        var = jnp.mean(cen * cen, axis=-1, keepdims=True)
        yr = cen * lax.rsqrt(var + EPS) * gn_ref[h]
        g = g_ref[:, cols(h)]
        o_ref[:, cols(h)] = (g * _sigmoid(g) * yr).astype(o_ref.dtype)


def _into(out_buf):
    if out_buf is None:
        return [], []
    return [out_buf], [pl.BlockSpec(memory_space=pl.ANY)]


def _retention(log_gamma, q, k, v, gate, gn, s0, row_off, n_batch, n_steps, rows, out_buf=None):
    heads = q.shape[1] // RET_HEAD_DIM
    nh = min(RET_HEADS_PER_STEP, heads)
    assert row_off % rows == 0 and rows % CHUNK == 0 and heads % nh == 0
    ro = row_off // rows
    tok = lambda b, h, c, lg: (ro + b * n_steps + c, h)
    blk = pl.BlockSpec((rows, nh * RET_HEAD_DIM), tok)
    st = pl.BlockSpec((None, nh, RET_HEAD_DIM, RET_HEAD_DIM), lambda b, h, c, lg: (b, h, 0, 0))
    args = [log_gamma, q, k, v, gate, gn.reshape(heads, 1, RET_HEAD_DIM), s0]
    extra_args, extra_specs = _into(out_buf)
    grid_spec = pltpu.PrefetchScalarGridSpec(
        num_scalar_prefetch=1,
        grid=(n_batch, heads // nh, n_steps),
        in_specs=[blk, blk, blk, blk,
                  pl.BlockSpec((nh, 1, RET_HEAD_DIM), lambda b, h, c, lg: (h, 0, 0)), st] + extra_specs,
        out_specs=[blk, st],
        scratch_shapes=[pltpu.VMEM((nh, RET_HEAD_DIM, RET_HEAD_DIM), F32), pltpu.VMEM((nh, rows, rows), F32)],
    )
    return pl.pallas_call(
        functools.partial(_retention_body, rows=rows, n_heads=nh),
        out_shape=[jax.ShapeDtypeStruct((q.shape[0], heads * RET_HEAD_DIM), BF16),
                   jax.ShapeDtypeStruct((n_batch, heads, RET_HEAD_DIM, RET_HEAD_DIM), F32)],
        grid_spec=grid_spec,
        input_output_aliases={len(args): 0} if extra_args else {},
        compiler_params=_params("arbitrary", "arbitrary", "arbitrary"),
        name="retention",
    )(*args, *extra_args)


def _neg_suffix(n):
    j = lax.broadcasted_iota(jnp.int32, (2 * n, n), 0) & (n - 1)
    s = lax.broadcasted_iota(jnp.int32, (2 * n, n), 1)
    return jnp.where(j >= s, -1.0, 0.0).astype(BF16)


def _lanes_to(x, n):
    return x[:, :n] if n <= LANES else jnp.concatenate([x] * (n // LANES), axis=1)


def _suffix_sums(fail, neg_ones):
    tk = fail.shape[1]
    hi, lo = _split_bf16(fail)
    if tk % LANES == 0:
        return jnp.dot(jnp.concatenate([hi, lo], axis=1), neg_ones, preferred_element_type=F32)
    return (jnp.dot(hi, neg_ones[:tk], preferred_element_type=F32)
            + jnp.dot(lo, neg_ones[:tk], preferred_element_type=F32))


def _sb_tiles(qs, kbs, vbs, neg_ones, carry, acc, masked):
    heads = range(len(qs))
    tk = kbs[0].shape[0]
    zs = [lax.dot_general(qs[h], kbs[h], _NT, preferred_element_type=F32) for h in heads]
    fails = [jnp.maximum(z, 0.0) + jnp.log2(1.0 + jnp.exp2(-jnp.abs(z))) for z in zs]
    if masked:
        t = lax.broadcasted_iota(jnp.int32, zs[0].shape, 0)
        s = lax.broadcasted_iota(jnp.int32, zs[0].shape, 1)
        valid = s < t
        fails = [jnp.where(valid, f, 0.0) for f in fails]
    suffixes = [_suffix_sums(f, neg_ones) for f in fails]
    probs = [jnp.exp2(zs[h] + suffixes[h] + _lanes_to(carry[h], tk)) for h in heads]
    if masked:
        probs = [jnp.where(valid, p, 0.0) for p in probs]
    for h in heads:
        acc[h] += jnp.dot(probs[h].astype(BF16), vbs[h], preferred_element_type=F32)
        carry[h] += jnp.broadcast_to(suffixes[h][:, :1], carry.shape[1:])


def _sb_body(q_ref, kd_ref, vd_ref, kp_ref, vp_ref, *rest, tk, n_heads, past_per_step, past_static):
    o_ref, carry, acc = rest[-3:]
    tq = q_ref.shape[0]
    n_past = past_static if past_static is not None else pl.program_id(1) * past_per_step
    cols = lambda h: slice(h * SB_HEAD_DIM, (h + 1) * SB_HEAD_DIM)
    heads = range(n_heads)
    past = (lambda ref, h, rows: ref[rows, h, :]) if len(kp_ref.shape) == 3 else (
        lambda ref, h, rows: ref[rows, cols(h)])
    carry[...] = jnp.zeros(carry.shape, F32)
    acc[...] = jnp.zeros(acc.shape, F32)
    queries = lambda: [q_ref[:, cols(h)] for h in heads]
    _sb_tiles(queries(), [kd_ref[:, cols(h)].astype(BF16) for h in heads],
              [vd_ref[:, cols(h)].astype(BF16) for h in heads], _neg_suffix(tq), carry, acc, True)
    neg_ones = _neg_suffix(tk)

    alive = lambda: jnp.max(carry[...]) > SB_DEAD_LOG2

    def step(state):
        it = state[0]
        rows = pl.ds(pl.multiple_of((n_past - 1 - it) * tk, tk), tk)
        _sb_tiles(queries(), [past(kp_ref, h, rows).astype(BF16) for h in heads],
                  [past(vp_ref, h, rows).astype(BF16) for h in heads], neg_ones, carry, acc, False)
        return it + 1, alive()

    lax.while_loop(lambda state: (state[0] < n_past) & state[1], step, (jnp.int32(0), alive()))
    for h in range(n_heads):
        o_ref[:, cols(h)] = acc[h].astype(o_ref.dtype)


def _sb_scratch(n_heads, tq):
    return [pltpu.VMEM((n_heads, tq, LANES), F32), pltpu.VMEM((n_heads, tq, SB_HEAD_DIM), F32)]


def _sb_prompt(q, k, v, n_tok):
    heads = q.shape[1] // SB_HEAD_DIM
    tq = _tile(n_tok, SB_BLOCK)
    nh = SB_HEADS_PER_STEP
    width = nh * SB_HEAD_DIM
    blk = pl.BlockSpec((tq, width), lambda h, i: (i, h))
    full = pl.BlockSpec((n_tok, width), lambda h, i: (0, h))
    return pl.pallas_call(
        functools.partial(_sb_body, tk=tq, n_heads=nh, past_per_step=1, past_static=None),
        out_shape=jax.ShapeDtypeStruct((q.shape[0], heads * SB_HEAD_DIM), BF16),
        grid=(heads // nh, n_tok // tq),
        in_specs=[blk, blk, blk, full, full],
        out_specs=blk,
        scratch_shapes=_sb_scratch(nh, tq),
        compiler_params=_params("parallel", "arbitrary"),
        name="stick_breaking_prompt",
    )(q, k, v, k, v)


def _sb_sample(q, k, v, cache_k, cache_v, row_off, n_new, out_buf):
    heads = q.shape[1] // SB_HEAD_DIM
    _, n_batch, past, _, _ = cache_k.shape
    tk = _tile(past, SB_BLOCK)
    assert row_off % n_new == 0
    ro = row_off // n_new
    nh = min(F32_SUBLANES, heads)
    assert heads % nh == 0 and (nh == F32_SUBLANES or nh == heads)
    blk = pl.BlockSpec((n_new, nh * SB_HEAD_DIM), lambda b, h: (ro + b, h))
    full = pl.BlockSpec((None, None, past, nh, SB_HEAD_DIM), lambda b, h: (0, b, 0, h, 0))
    extra_args, extra_specs = _into(out_buf)
    return pl.pallas_call(
        functools.partial(_sb_body, tk=tk, n_heads=nh, past_per_step=0, past_static=past // tk),
        out_shape=jax.ShapeDtypeStruct(out_buf.shape, out_buf.dtype),
        grid=(n_batch, heads // nh),
        in_specs=[blk, blk, blk, full, full] + extra_specs,
        out_specs=blk,
        scratch_shapes=_sb_scratch(nh, n_new),
        input_output_aliases={5: 0},
        compiler_params=_params("parallel", "arbitrary"),
        name="stick_breaking_sample",
    )(q, k, v, cache_k, cache_v, *extra_args)


def _cross_body(q_ref, k_ref, v_ref, *rest, scale, steps):
    o_ref, kb, vb = rest[-3:]
    dh = q_ref.shape[1] // MEM_HEADS
    heads = range(MEM_HEADS)
    cols = lambda h: slice(h * dh, (h + 1) * dh)
    head_of = (lambda ref, h: ref[:, h, :]) if len(k_ref.shape) == 3 else (lambda ref, h: ref[:, cols(h)])

    @pl.when(pl.program_id(0) % steps == 0)
    def _():
        for h in heads:
            kb[:, cols(h)] = head_of(k_ref, h).astype(BF16)
            vb[:, cols(h)] = head_of(v_ref, h).astype(BF16)

    scores = [lax.dot_general(q_ref[:, cols(h)], kb[:, cols(h)], _NT, preferred_element_type=F32) * scale
              for h in heads]
    exps = [jnp.exp(s - jnp.max(s, axis=-1, keepdims=True)) for s in scores]
    probs = [e / jnp.sum(e, axis=-1, keepdims=True) for e in exps]
    for h in heads:
        o_ref[:, cols(h)] = jnp.dot(probs[h].astype(BF16), vb[:, cols(h)],
                                    preferred_element_type=F32).astype(o_ref.dtype)


def _cross_attend(q, mem_k, mem_v, row_off, rows_per_seq, out_buf=None):
    d = q.shape[1]
    dh = d // MEM_HEADS
    tm = _tile(rows_per_seq, ROW_TILE)
    steps = rows_per_seq // tm
    assert row_off % tm == 0
    ro = row_off // tm
    extra_args, extra_specs = _into(out_buf)
    if mem_k.ndim == 3:
        n_seq, n_mem, _ = mem_k.shape
        mem = pl.BlockSpec((None, n_mem, d), lambda i: (i // steps, 0, 0))
    else:
        _, n_seq, n_mem, _, _ = mem_k.shape
        mem = pl.BlockSpec((None, None, n_mem, MEM_HEADS, dh), lambda i: (0, i // steps, 0, 0, 0))
    blk = pl.BlockSpec((tm, d), lambda i: (ro + i, 0))
    return pl.pallas_call(
        functools.partial(_cross_body, scale=dh ** -0.5, steps=steps),
        out_shape=jax.ShapeDtypeStruct((q.shape[0], d), BF16),
        grid=(n_seq * steps,),
        in_specs=[blk, mem, mem] + extra_specs,
        out_specs=blk,
        scratch_shapes=[pltpu.VMEM((n_mem, d), BF16), pltpu.VMEM((n_mem, d), BF16)],
        input_output_aliases={3: 0} if extra_args else {},
        compiler_params=_params("arbitrary"),
        name="cross_attention",
    )(q, mem_k, mem_v, *extra_args)


def _split_bf16(x):
    hi = x.astype(BF16)
    return hi, (x - hi.astype(F32)).astype(BF16)


def _router_body(x_ref, g_ref, wr_ref, br_ref, h_ref, ids_ref, wts_ref):
    h = _rmsnorm_rows(x_ref[...], g_ref[...])
    h_ref[...] = _pack_bf16_pairs(h)
    h_hi, h_lo = _split_bf16(h)
    w_hi, w_lo = _split_bf16(wr_ref[...])
    logits = (jnp.dot(h_hi, w_hi, preferred_element_type=F32)
              + jnp.dot(h_hi, w_lo, preferred_element_type=F32)
              + jnp.dot(h_lo, w_hi, preferred_element_type=F32)) + br_ref[...]
    lane = lax.broadcasted_iota(jnp.int32, logits.shape, 1).astype(F32)
    big = float(LANES)
    neg = -jnp.inf

    def first_max(vals):
        top = jnp.max(vals, axis=-1, keepdims=True)
        return top, jnp.min(jnp.where(vals == top, lane, big), axis=-1, keepdims=True)

    is_group = lane < N_GROUPS
    g_max, g_top = first_max(jnp.where(is_group, logits, neg))
    g_w = 1.0 / jnp.sum(jnp.where(is_group, jnp.exp(logits - g_max), 0.0), axis=-1, keepdims=True)
    first = N_GROUPS + EXPERTS_PER_GROUP * g_top
    cand = jnp.where((lane >= first) & (lane < first + EXPERTS_PER_GROUP), logits, neg)
    v1, i1 = first_max(cand)
    v2, i2 = first_max(jnp.where(lane == i1, neg, cand))
    t = jnp.exp(v2 - v1)
    w1 = g_w / (1.0 + t)
    w2 = g_w * t / (1.0 + t)
    ids = jnp.where(lane == 0.0, i1 - N_GROUPS, jnp.where(lane == 1.0, i2 - N_GROUPS, 0.0))
    ids_ref[...] = ids.astype(jnp.int32)
    wts_ref[...] = jnp.where(lane == 0.0, w1, jnp.where(lane == 1.0, w2, 0.0))


def _norm_and_route(x, g, wg, bg, we, be):
    m, d = x.shape
    tm = _tile(m, 256)
    pad = LANES - N_GROUPS - N_EXPERTS
    wr = jnp.concatenate([wg, we, jnp.zeros((d, pad), F32)], axis=1)
    br = jnp.concatenate([bg, be, jnp.zeros((pad,), F32)]).reshape(1, LANES)
    row = lambda c: pl.BlockSpec((tm, c), lambda i: (i, 0))
    return pl.pallas_call(
        _router_body,
        out_shape=[jax.ShapeDtypeStruct((m, d // 2), jnp.int32),
                   jax.ShapeDtypeStruct((m, LANES), jnp.int32),
                   jax.ShapeDtypeStruct((m, LANES), F32)],
        grid=(m // tm,),
        in_specs=[row(d), pl.BlockSpec((1, d), lambda i: (0, 0)),
                  pl.BlockSpec((d, LANES), lambda i: (0, 0)), pl.BlockSpec((1, LANES), lambda i: (0, 0))],
        out_specs=[row(d // 2), row(LANES), row(LANES)],
        compiler_params=_params("parallel"),
        name="norm_route",
    )(x, g.reshape(1, d), wr, br)


GATHER_UNROLL = 8


def _start_row_gather(idx_ref, first, stride, src_ref, buf, sem, n_rows):
    def one(r, _):
        t = idx_ref[first + r * stride]
        pltpu.make_async_copy(src_ref.at[pl.ds(t, 1)], buf.at[pl.ds(r, 1)], sem).start()
        return 0

    def group(g, _):
        for u in range(GATHER_UNROLL):
            one(g * GATHER_UNROLL + u, 0)
        return 0

    groups = n_rows // GATHER_UNROLL
    lax.fori_loop(0, groups, group, 0)
    lax.fori_loop(groups * GATHER_UNROLL, n_rows, one, 0)


def _wait_row_gather(src_ref, buf, sem, n_rows):
    pltpu.make_async_copy(src_ref.at[pl.ds(0, n_rows)], buf.at[pl.ds(0, n_rows)], sem).wait()


def _pack_bf16_pairs(x):
    half = x.shape[1] // 2
    hi = lax.bitcast_convert_type(x[:, :half].astype(BF16).astype(F32), jnp.int32)
    lo = lax.bitcast_convert_type(x[:, half:].astype(BF16).astype(F32), jnp.int32)
    return hi | lax.shift_right_logical(lo, jnp.int32(16))


def _unpack_bf16_pairs(p):
    hi = lax.bitcast_convert_type(p & jnp.int32(-65536), F32)
    lo = lax.bitcast_convert_type(lax.shift_left(p, jnp.int32(16)), F32)
    return hi, lo


def _dispatch_body(idx_ref, used_ref, cnt_ref, src_ref, o_ref, buf, sem):
    b = pl.program_id(0)
    rows = buf.shape[1]

    copied = lambda blk: pl.multiple_of(pl.cdiv(cnt_ref[blk], GATHER_UNROLL) * GATHER_UNROLL, GATHER_UNROLL)

    def start(blk):
        _start_row_gather(idx_ref, blk * rows, 1, src_ref, buf.at[blk % 2], sem.at[blk % 2], copied(blk))

    @pl.when(b == 0)
    def _():
        buf[...] = jnp.zeros(buf.shape, buf.dtype)
        start(0)

    @pl.when(b + 1 < used_ref[0])
    def _():
        start(b + 1)

    @pl.when(b < used_ref[0])
    def _():
        slot = b % 2
        cnt = cnt_ref[b]
        _wait_row_gather(src_ref, buf.at[slot], sem.at[slot], copied(b))
        half = buf.shape[2]
        row = lax.broadcasted_iota(jnp.int32, (rows, 1), 0)
        hi, lo = _unpack_bf16_pairs(jnp.where(row < cnt, buf[slot], 0))
        o_ref[:, :half] = hi.astype(o_ref.dtype)
        o_ref[:, half:] = lo.astype(o_ref.dtype)


def _dispatch_rows(src, row_tok, blk_cnt, n_used):
    n_rows = row_tok.shape[0]
    half = src.shape[1]
    grid_spec = pltpu.PrefetchScalarGridSpec(
        num_scalar_prefetch=3,
        grid=(n_rows // MOE_ROWS,),
        in_specs=[pl.BlockSpec(memory_space=pl.ANY)],
        out_specs=pl.BlockSpec((MOE_ROWS, 2 * half), lambda b, idx, used, cnt: (b, 0)),
        scratch_shapes=[pltpu.VMEM((2, MOE_ROWS, half), src.dtype), pltpu.SemaphoreType.DMA((2,))],
    )
    return pl.pallas_call(
        _dispatch_body,
        out_shape=jax.ShapeDtypeStruct((n_rows, 2 * half), BF16),
        grid_spec=grid_spec,
        compiler_params=_params("arbitrary"),
        name="dispatch_rows",
    )(row_tok, n_used, blk_cnt, src)


def _expert_up_body(be_ref, first_ref, used_ref, x_ref, w1_ref, w3_ref, h_ref, w1b, w3b):
    b = pl.program_id(1)

    @pl.when(first_ref[b] == 1)
    def _():
        w1b[...] = w1_ref[...].astype(BF16)
        w3b[...] = w3_ref[...].astype(BF16)

    @pl.when(b < used_ref[0])
    def _():
        x = x_ref[...]
        a = jnp.dot(x, w1b[...], preferred_element_type=F32)
        g = jnp.dot(x, w3b[...], preferred_element_type=F32)
        h_ref[...] = (a * _sigmoid(a) * g).astype(h_ref.dtype)


def _expert_down_body(be_ref, first_ref, used_ref, h_ref, w2_ref, y_ref, w2b):
    b = pl.program_id(1)

    @pl.when(first_ref[b] == 1)
    def _():
        w2b[...] = w2_ref[...].astype(BF16)

    @pl.when(b < used_ref[0])
    def _():
        y_ref[...] = _pack_bf16_pairs(jnp.dot(h_ref[...], w2b[...], preferred_element_type=F32))


def _expert_ffn(xs, blk_exp, blk_first, n_used, w1, w3, w2):
    n_rows, d = xs.shape
    f = w1.shape[2]
    n_blocks = n_rows // MOE_ROWS
    tf = _tile(f, EXPERT_HIDDEN_TILE)
    tn = _tile(d, EXPERT_OUT_TILE)
    last = lambda b, used: jnp.minimum(b, used[0] - 1)
    up_spec = pltpu.PrefetchScalarGridSpec(
        num_scalar_prefetch=3,
        grid=(f // tf, n_blocks),
        in_specs=[pl.BlockSpec((MOE_ROWS, d), lambda j, b, be, fi, us: (last(b, us), 0)),
                  pl.BlockSpec((None, d, tf), lambda j, b, be, fi, us: (be[b], 0, j)),
                  pl.BlockSpec((None, d, tf), lambda j, b, be, fi, us: (be[b], 0, j))],
        out_specs=pl.BlockSpec((MOE_ROWS, tf), lambda j, b, be, fi, us: (b, j)),
        scratch_shapes=[pltpu.VMEM((d, tf), BF16), pltpu.VMEM((d, tf), BF16)],
    )
    hidden = pl.pallas_call(
        _expert_up_body,
        out_shape=jax.ShapeDtypeStruct((n_rows, f), BF16),
        grid_spec=up_spec,
        compiler_params=_params("arbitrary", "arbitrary"),
        name="expert_up",
    )(blk_exp, blk_first, n_used, xs, w1, w3)
    down_spec = pltpu.PrefetchScalarGridSpec(
        num_scalar_prefetch=3,
        grid=(d // tn, n_blocks),
        in_specs=[pl.BlockSpec((MOE_ROWS, f), lambda j, b, be, fi, us: (last(b, us), 0)),
                  pl.BlockSpec((None, f, tn), lambda j, b, be, fi, us: (be[b], 0, j))],
        out_specs=pl.BlockSpec((MOE_ROWS, tn // 2), lambda j, b, be, fi, us: (b, j)),
        scratch_shapes=[pltpu.VMEM((f, tn), BF16)],
    )
    return pl.pallas_call(
        _expert_down_body,
        out_shape=jax.ShapeDtypeStruct((n_rows, d // 2), jnp.int32),
        grid_spec=down_spec,
        compiler_params=_params("arbitrary", "arbitrary"),
        name="expert_down",
    )(blk_exp, blk_first, n_used, hidden, w2)


def _route_layout(ids, n_tok):
    n_assign = n_tok * TOP_K
    flat_e = ids.reshape(n_assign)
    flat_t = jnp.repeat(jnp.arange(n_tok, dtype=jnp.int32), TOP_K)
    onehot = (flat_e[:, None] == jnp.arange(N_EXPERTS, dtype=jnp.int32)[None, :]).astype(jnp.int32)
    running = jnp.cumsum(onehot, axis=0)
    rank = jnp.take_along_axis(running, flat_e[:, None], axis=1)[:, 0] - 1
    counts = running[-1]
    padded = (counts + MOE_ROWS - 1) // MOE_ROWS * MOE_ROWS
    pad_end = jnp.cumsum(padded)
    dest = (pad_end - padded)[flat_e] + rank
    n_blocks = -(-n_assign // MOE_ROWS) + N_EXPERTS
    n_rows = n_blocks * MOE_ROWS
    row_tok = jnp.zeros((n_rows,), jnp.int32).at[dest].set(flat_t)
    blk = jnp.arange(n_blocks, dtype=jnp.int32)
    n_used = (pad_end[-1] // MOE_ROWS).astype(jnp.int32)
    blk_exp = jnp.searchsorted(pad_end, jnp.minimum(blk, n_used - 1) * MOE_ROWS, side='right').astype(jnp.int32)
    blk_exp = jnp.minimum(blk_exp, N_EXPERTS - 1)
    prev = jnp.concatenate([jnp.full((1,), -1, jnp.int32), blk_exp[:-1]])
    blk_first = ((blk_exp != prev) & (blk < n_used)).astype(jnp.int32)
    into_range = blk * MOE_ROWS - (pad_end - padded)[blk_exp]
    blk_cnt = jnp.where(blk < n_used, jnp.clip(counts[blk_exp] - into_range, 0, MOE_ROWS), 0).astype(jnp.int32)
    return row_tok, blk_cnt, dest.astype(jnp.int32), blk_exp, blk_first, n_used.reshape(1)


def _final_body(dest_ref, x_ref, w_ref, y_ref, g_ref, op_ref, os_ref, buf, sem, *, prompt_tiles, pack_tile):
    i = pl.program_id(0)
    tm = buf.shape[2]

    def start(blk):
        for k in range(TOP_K):
            _start_row_gather(dest_ref, blk * tm * TOP_K + k, TOP_K, y_ref, buf.at[blk % 2, k], sem.at[blk % 2, k],
                              tm)

    @pl.when(i == 0)
    def _():
        start(0)

    @pl.when(i + 1 < pl.num_programs(0))
    def _():
        start(i + 1)

    slot = i % 2
    for k in range(TOP_K):
        _wait_row_gather(y_ref, buf.at[slot, k], sem.at[slot, k], tm)
    w = w_ref[...]
    pieces = []
    for j in range(buf.shape[3] // pack_tile):
        words = slice(j * pack_tile, (j + 1) * pack_tile)
        hi0, lo0 = _unpack_bf16_pairs(buf[slot, 0, :, words])
        hi1, lo1 = _unpack_bf16_pairs(buf[slot, 1, :, words])
        pieces += [hi0 * w[:, 0:1] + hi1 * w[:, 1:2], lo0 * w[:, 0:1] + lo1 * w[:, 1:2]]
    y = _rmsnorm_rows(x_ref[...] + jnp.concatenate(pieces, axis=1), g_ref[...])

    @pl.when(i < prompt_tiles)
    def _():
        op_ref[...] = y

    @pl.when(i >= prompt_tiles)
    def _():
        os_ref[...] = y


def _combine_and_norm(x, wts, y_rows, dest, g, n_prompt):
    m, d = x.shape
    tm = _tile(m - n_prompt, _tile(n_prompt, 256))
    assert n_prompt % tm == 0
    pt = n_prompt // tm
    grid_spec = pltpu.PrefetchScalarGridSpec(
        num_scalar_prefetch=1,
        grid=(m // tm,),
        in_specs=[pl.BlockSpec((tm, d), lambda i, dest: (i, 0)),
                  pl.BlockSpec((tm, LANES), lambda i, dest: (i, 0)),
                  pl.BlockSpec(memory_space=pl.ANY),
                  pl.BlockSpec((1, d), lambda i, dest: (0, 0))],
        out_specs=[pl.BlockSpec((tm, d), lambda i, dest: (jnp.minimum(i, pt - 1), 0)),
                   pl.BlockSpec((tm, d), lambda i, dest: (jnp.maximum(i - pt, 0), 0))],
        scratch_shapes=[pltpu.VMEM((2, TOP_K, tm, d // 2), y_rows.dtype), pltpu.SemaphoreType.DMA((2, TOP_K))],
    )
    return pl.pallas_call(
        functools.partial(_final_body, prompt_tiles=pt, pack_tile=_tile(d, EXPERT_OUT_TILE) // 2),
        out_shape=[jax.ShapeDtypeStruct((n_prompt, d), F32), jax.ShapeDtypeStruct((m - n_prompt, d), F32)],
        grid_spec=grid_spec,
        compiler_params=_params("arbitrary"),
        name="combine_norm",
    )(dest, x, wts, y_rows, g.reshape(1, d))


def _layer(xp, xs, n_seq, n_new, past_len, cache_sb_k, cache_sb_v, state_ret, cache_mem_k, cache_mem_v,
           mem_prompt, norm_mix, w_in, ret_gn, w_out, norm_cross, norm_mem, w_cq, w_ck, w_cv, w_co, norm_ffn,
           wg, bg, we, be, w1, w3, w2):
    n_prompt, d = xp.shape
    n_tok = n_prompt + xs.shape[0]
    d_ret = d // 2
    d_sb = d - d_ret
    ret_heads = d_ret // RET_HEAD_DIM
    sds = jax.ShapeDtypeStruct

    pos = np.concatenate([np.arange(n_prompt), np.tile(past_len + np.arange(n_new), n_seq)])
    half = RET_HEAD_DIM // 2
    inv = (ROPE_BASE ** (-np.arange(half, dtype=np.float32) / half)).astype(np.float32)
    ang = pos.astype(np.float32)[:, None] * inv[None, :]
    cos, sin = jnp.asarray(np.cos(ang), F32), jnp.asarray(np.sin(ang), F32)
    log_gamma = jnp.log1p(-jnp.exp2(-5.0 - jnp.arange(ret_heads, dtype=F32)))

    h = _rmsnorm_stacked(xp, xs, norm_mix, BF16)
    rot = [(cos, half), (sin, half)]
    bf = lambda n: [(sds((n_tok, n), BF16), None)]
    (rq,) = _matmul([h], w_in, 0, d_ret, _ep_rotary(1.0), rot, bf(d_ret), "in_proj_ret_q")
    (rk,) = _matmul([h], w_in, d_ret, d_ret, _ep_rotary(RET_HEAD_DIM ** -0.5), rot, bf(d_ret), "in_proj_ret_k")
    (rv,) = _matmul([h], w_in, 2 * d_ret, d_ret, _ep_store(1.0), [], bf(d_ret), "in_proj_ret_v")
    (rg,) = _matmul([h], w_in, 3 * d_ret, d_ret, _ep_store(1.0), [], [(sds((n_tok, d_ret), F32), None)],
                    "in_proj_ret_gate")
    (sq,) = _matmul([h], w_in, 4 * d_ret, d_sb, _ep_store(SB_HEAD_DIM ** -0.5 * LOG2_E), [], bf(d_sb),
                    "in_proj_sb_q")
    split = [(sds((n_prompt, d_sb), F32), PROMPT), (sds((n_tok - n_prompt, d_sb), F32), SAMPLE),
             (sds((n_tok, d_sb), BF16), None)]
    sk_p, sk_s, sk_b = _matmul([h], w_in, 4 * d_ret + d_sb, d_sb, _ep_store_split, [], split, "in_proj_sb_k",
                               n_prompt=n_prompt)
    sv_p, sv_s, sv_b = _matmul([h], w_in, 4 * d_ret + 2 * d_sb, d_sb, _ep_store_split, [], split,
                               "in_proj_sb_v", n_prompt=n_prompt)

    rows_p = _tile(n_prompt, RET_BLOCK)
    zeros_state = jnp.zeros((1, ret_heads, RET_HEAD_DIM, RET_HEAD_DIM), F32)
    yr, state_p = _retention(log_gamma, rq, rk, rv, rg, ret_gn, zeros_state, 0, 1, n_prompt // rows_p, rows_p)
    yr, state_s = _retention(log_gamma, rq, rk, rv, rg, ret_gn, state_ret, n_prompt, n_seq, 1, n_new,
                             out_buf=yr)
    sb = _sb_prompt(sq, sk_b, sv_b, n_prompt)
    sb = _sb_sample(sq, sk_b, sv_b, cache_sb_k, cache_sb_v, n_prompt, n_new, sb)
    res = [(sds((n_tok, d), F32), None)]
    (x,) = _matmul([yr, sb], w_out, 0, d, _ep_residual_split, [(xp, PROMPT), (xs, SAMPLE)], res, "out_proj",
                   n_prompt=n_prompt)

    n_mem = mem_prompt.shape[0]
    m = _rmsnorm(mem_prompt, norm_mem, BF16)
    mem_out = [(sds((n_mem, d), F32), None)]
    (mk,) = _matmul([m], w_ck, 0, d, _ep_store(1.0), [], mem_out, "mem_k")
    (mv,) = _matmul([m], w_cv, 0, d, _ep_store(1.0), [], mem_out, "mem_v")
    h = _rmsnorm(x, norm_cross, BF16)
    (cq,) = _matmul([h], w_cq, 0, d, _ep_store(1.0), [], bf(d), "cross_q")
    co = _cross_attend(cq, mk.reshape(1, n_mem, d), mv.reshape(1, n_mem, d), 0, n_prompt)
    co = _cross_attend(cq, cache_mem_k, cache_mem_v, n_prompt, n_new, out_buf=co)
    (x,) = _matmul([co], w_co, 0, d, _ep_residual, [(x, None)], res, "cross_out")

    hn, ids, wts = _norm_and_route(x, norm_ffn, wg, bg, we, be)
    row_tok, blk_cnt, dest, blk_exp, blk_first, n_used = _route_layout(ids[:, :TOP_K], n_tok)
    x_rows = _dispatch_rows(hn, row_tok, blk_cnt, n_used)
    y_rows = _expert_ffn(x_rows, blk_exp, blk_first, n_used, w1, w3, w2)
    return x, wts, y_rows, dest, (sk_p, sv_p, sk_s, sv_s, state_p, state_s, mk, mv)


def kernel(x_prompt, x_sample, cache_sb_k, cache_sb_v, state_ret, cache_mem_k, cache_mem_v, mem_prompt, norm_mix, w_in, ret_gn, w_out, norm_cross, norm_mem, w_cq, w_ck, w_cv, w_co, norm_ffn, router_group_w, router_group_b, router_expert_w, router_expert_b, expert_w1, expert_w3, expert_w2, norm_final):
    batch, n_prompt, d = x_prompt.shape
    n_seq, n_new, _ = x_sample.shape
    depth, _, past_len, sb_heads, _ = cache_sb_k.shape
    n_mem = mem_prompt.shape[1]
    assert batch == 1 and depth == 1
    ret_heads = (d // 2) // RET_HEAD_DIM
    x, wts, y_rows, dest, (sk_p, sv_p, sk_s, sv_s, state_p, state_s, mk, mv) = _layer(
        x_prompt.reshape(n_prompt, d), x_sample.reshape(n_seq * n_new, d), n_seq, n_new, past_len,
        cache_sb_k, cache_sb_v, state_ret[0], cache_mem_k, cache_mem_v, mem_prompt[0], norm_mix[0], w_in,
        ret_gn[0], w_out, norm_cross[0], norm_mem[0], w_cq, w_ck, w_cv, w_co, norm_ffn[0], router_group_w[0],
        router_group_b[0], router_expert_w[0], router_expert_b[0], expert_w1[0], expert_w3[0], expert_w2[0])
    y_prompt, y_sample = _combine_and_norm(x, wts, y_rows, dest, norm_final, n_prompt)
    y_prompt = y_prompt.reshape(1, n_prompt, d)
    y_sample = y_sample.reshape(n_seq, n_new, d)
    sb_shape_p = (1, 1, n_prompt, sb_heads, SB_HEAD_DIM)
    sb_shape_s = (1, n_seq, n_new, sb_heads, SB_HEAD_DIM)
    mem_shape = (1, 1, n_mem, MEM_HEADS, d // MEM_HEADS)
    return (y_prompt, y_sample,
            sk_p.reshape(sb_shape_p), sv_p.reshape(sb_shape_p),
            state_p.reshape(1, 1, ret_heads, RET_HEAD_DIM, RET_HEAD_DIM),
            mk.reshape(mem_shape), mv.reshape(mem_shape),
            sk_s.reshape(sb_shape_s), sv_s.reshape(sb_shape_s),
            state_s.reshape(1, n_seq, ret_heads, RET_HEAD_DIM, RET_HEAD_DIM))
```

```python
import functools

import jax
import jax.numpy as jnp
import numpy as np
from jax import lax
from jax.experimental import pallas as pl
from jax.experimental.pallas import tpu as pltpu

BF16 = jnp.bfloat16
F32 = jnp.float32

EPS = 1e-6
CHUNK = 64
RET_HEAD_DIM = 256
SB_HEAD_DIM = 128
MEM_HEADS = 4
N_GROUPS = 4
EXPERTS_PER_GROUP = 8
N_EXPERTS = N_GROUPS * EXPERTS_PER_GROUP
TOP_K = 2
ROPE_BASE = 10000.0

LANES = 128
F32_SUBLANES = 8
VMEM_LIMIT_BYTES = 60 * 1024 * 1024
MATMUL_VMEM_BUDGET = 56 * 1024 * 1024
ROW_TILE = 512
COL_TILE = 1024
EXPERT_HIDDEN_TILE = 512
RET_BLOCK = 256
RET_HEADS_PER_STEP = 4
SB_BLOCK = 256
SB_HEADS_PER_STEP = 4
LOG2_E = 1.4426950408889634
SB_DEAD_LOG2 = -64.0
MOE_ROWS = 512
EXPERT_OUT_TILE = 2048

PROMPT = "prompt"
SAMPLE = "sample"

_NT = (((1,), (1,)), ((), ()))
_TN = (((0,), (0,)), ((), ()))


def _tile(n, pref):
    t = min(pref, n)
    while n % t:
        t //= 2
    return t


def _params(*sem):
    return pltpu.CompilerParams(dimension_semantics=sem, vmem_limit_bytes=VMEM_LIMIT_BYTES)


def _sigmoid(x):
    return 1.0 / (1.0 + jnp.exp(-x))


def _rmsnorm_rows(x, g):
    ms = jnp.mean(x * x, axis=-1, keepdims=True)
    return x * lax.rsqrt(ms + EPS) * g


def _rmsnorm_body(x_ref, g_ref, o_ref):
    o_ref[...] = _rmsnorm_rows(x_ref[...], g_ref[...]).astype(o_ref.dtype)


def _rmsnorm(x, g, out_dtype):
    m, d = x.shape
    tm = _tile(m, 256)
    return pl.pallas_call(
        _rmsnorm_body,
        out_shape=jax.ShapeDtypeStruct((m, d), out_dtype),
        grid=(m // tm,),
        in_specs=[pl.BlockSpec((tm, d), lambda i: (i, 0)), pl.BlockSpec((1, d), lambda i: (0, 0))],
        out_specs=pl.BlockSpec((tm, d), lambda i: (i, 0)),
        compiler_params=_params("parallel"),
        name="rmsnorm",
    )(x, g.reshape(1, d))


def _rmsnorm_stacked_body(xp_ref, xs_ref, g_ref, o_ref, *, prompt_tiles):
    is_prompt = pl.program_id(0) < prompt_tiles

    @pl.when(is_prompt)
    def _():
        o_ref[...] = _rmsnorm_rows(xp_ref[...], g_ref[...]).astype(o_ref.dtype)

    @pl.when(jnp.logical_not(is_prompt))
    def _():
        o_ref[...] = _rmsnorm_rows(xs_ref[...], g_ref[...]).astype(o_ref.dtype)


def _rmsnorm_stacked(xp, xs, g, out_dtype):
    (n_p, d), n_s = xp.shape, xs.shape[0]
    tm = _tile(n_s, _tile(n_p, 256))
    pt = n_p // tm
    return pl.pallas_call(
        functools.partial(_rmsnorm_stacked_body, prompt_tiles=pt),
        out_shape=jax.ShapeDtypeStruct((n_p + n_s, d), out_dtype),
        grid=((n_p + n_s) // tm,),
        in_specs=[pl.BlockSpec((tm, d), lambda i: (jnp.minimum(i, pt - 1), 0)),
                  pl.BlockSpec((tm, d), lambda i: (jnp.maximum(i - pt, 0), 0)),
                  pl.BlockSpec((1, d), lambda i: (0, 0))],
        out_specs=pl.BlockSpec((tm, d), lambda i: (i, 0)),
        compiler_params=_params("arbitrary"),
        name="rmsnorm_stacked",
    )(xp, xs, g.reshape(1, d))


def _matmul_body(*refs, n_lhs, n_extra, n_out, epilogue):
    x_refs = refs[:n_lhs]
    w_refs = refs[n_lhs:2 * n_lhs]
    extra = refs[2 * n_lhs:2 * n_lhs + n_extra]
    outs = refs[2 * n_lhs + n_extra:2 * n_lhs + n_extra + n_out]
    wb = refs[2 * n_lhs + n_extra + n_out:]

    @pl.when(pl.program_id(1) == 0)
    def _():
        for p in range(n_lhs):
            wb[p][...] = w_refs[p][...].astype(BF16)

    acc = jnp.dot(x_refs[0][...], wb[0][...], preferred_element_type=F32)
    for p in range(1, n_lhs):
        acc += jnp.dot(x_refs[p][...], wb[p][...], preferred_element_type=F32)
    epilogue(acc, extra, outs)


def _matmul(xs, w, col_off, n_cols, epilogue, extras, outs, name, n_prompt=None):
    m = xs[0].shape[0]
    tm = _tile(m, ROW_TILE) if n_prompt is None else _tile(m - n_prompt, _tile(n_prompt, ROW_TILE))
    k_total = sum(x.shape[1] for x in xs)

    def vmem_bytes(tn):
        tiles = sum((tn if kind is None or isinstance(kind, str) else kind) * jnp.dtype(a.dtype).itemsize
                    for a, kind in list(extras) + list(outs))
        return k_total * tn * (2 * 4 + 2) + 2 * tm * k_total * 2 + 2 * tm * tiles

    tn = _tile(n_cols, COL_TILE)
    while vmem_bytes(tn) > MATMUL_VMEM_BUDGET and tn > LANES:
        tn //= 2
    assert col_off % tn == 0
    jo = col_off // tn
    pt = None if n_prompt is None else n_prompt // tm
    in_specs, args = [], []
    for x in xs:
        in_specs.append(pl.BlockSpec((tm, x.shape[1]), lambda j, i: (i, 0)))
        args.append(x)
    row = 0
    for x in xs:
        kp = x.shape[1]
        assert row % kp == 0
        in_specs.append(pl.BlockSpec((None, kp, tn), lambda j, i, r=row // kp: (0, r, j + jo)))
        args.append(w)
        row += kp
    assert row == w.shape[1]

    def spec(kind):
        if kind is None:
            return pl.BlockSpec((tm, tn), lambda j, i: (i, j))
        if kind == PROMPT:
            return pl.BlockSpec((tm, tn), lambda j, i: (jnp.minimum(i, pt - 1), j))
        if kind == SAMPLE:
            return pl.BlockSpec((tm, tn), lambda j, i: (jnp.maximum(i - pt, 0), j))
        return pl.BlockSpec((tm, kind), lambda j, i: (i, 0))

    for a, kind in extras:
        in_specs.append(spec(kind))
        args.append(a)
    body = functools.partial(_matmul_body, n_lhs=len(xs), n_extra=len(extras), n_out=len(outs),
                             epilogue=epilogue if n_prompt is None else functools.partial(epilogue, pt))
    res = pl.pallas_call(
        body,
        out_shape=[o for o, _ in outs],
        grid=(n_cols // tn, m // tm),
        in_specs=in_specs,
        out_specs=[spec(kind) for _, kind in outs],
        scratch_shapes=[pltpu.VMEM((x.shape[1], tn), BF16) for x in xs],
        compiler_params=_params("arbitrary", "arbitrary"),
        name=name,
    )(*args)
    return res


def _ep_store(scale):
    def ep(acc, extra, outs):
        for o in outs:
            o[...] = (acc * scale).astype(o.dtype) if scale != 1.0 else acc.astype(o.dtype)
    return ep


def _ep_store_split(prompt_tiles, acc, extra, outs):
    is_prompt = pl.program_id(1) < prompt_tiles

    @pl.when(is_prompt)
    def _():
        outs[0][...] = acc

    @pl.when(jnp.logical_not(is_prompt))
    def _():
        outs[1][...] = acc

    outs[2][...] = acc.astype(outs[2].dtype)


def _ep_residual(acc, extra, outs):
    outs[0][...] = extra[0][...] + acc


def _ep_residual_split(prompt_tiles, acc, extra, outs):
    res = jnp.where(pl.program_id(1) < prompt_tiles, extra[0][...], extra[1][...])
    outs[0][...] = res + acc


def _ep_rotary(scale):
    half = RET_HEAD_DIM // 2

    def ep(acc, extra, outs):
        cos = extra[0][...]
        sin = extra[1][...]
        for h in range(acc.shape[1] // RET_HEAD_DIM):
            lo = h * RET_HEAD_DIM
            x1 = acc[:, lo:lo + half]
            x2 = acc[:, lo + half:lo + RET_HEAD_DIM]
            outs[0][:, lo:lo + half] = ((x1 * cos - x2 * sin) * scale).astype(outs[0].dtype)
            outs[0][:, lo + half:lo + RET_HEAD_DIM] = ((x1 * sin + x2 * cos) * scale).astype(outs[0].dtype)
    return ep


def _retention_body(lg_ref, q_ref, k_ref, v_ref, g_ref, gn_ref, s0_ref, *rest, rows, n_heads):
    o_ref, sout_ref, s_scr, decay_scr = rest[-4:]
    heads = range(n_heads)
    cols = lambda h: slice(h * RET_HEAD_DIM, (h + 1) * RET_HEAD_DIM)
    lgs = [lg_ref[pl.program_id(1) * n_heads + h] for h in heads]

    @pl.when(pl.program_id(2) == 0)
    def _():
        s_scr[...] = s0_ref[...]
        ii = lax.broadcasted_iota(jnp.int32, (rows, rows), 0)
        jj = lax.broadcasted_iota(jnp.int32, (rows, rows), 1)
        shift = CHUNK.bit_length() - 1
        dist = jnp.abs(ii - jj).astype(F32)
        for h in heads:
            decay_scr[h] = jnp.where((jj >> shift) <= (ii >> shift), jnp.exp(lgs[h] * dist), 0.0)

    idx = lax.broadcasted_iota(jnp.int32, (rows, 1), 0).astype(F32)
    qs = [q_ref[:, cols(h)] for h in heads]
    ks = [k_ref[:, cols(h)] for h in heads]
    vs = [v_ref[:, cols(h)] for h in heads]
    scores = [lax.dot_general(qs[h], ks[h], _NT, preferred_element_type=F32) * decay_scr[h] for h in heads]
    outs = [jnp.dot(scores[h].astype(BF16), vs[h], preferred_element_type=F32) for h in heads]
    states = [s_scr[h] for h in heads]
    outs = [outs[h] + jnp.exp(lgs[h] * (idx + 1.0))
            * jnp.dot(qs[h], states[h].astype(BF16), preferred_element_type=F32) for h in heads]
    kds = [(ks[h].astype(F32) * jnp.exp(lgs[h] * (rows - 1.0 - idx))).astype(BF16) for h in heads]
    for h in heads:
        s_new = (jnp.exp(lgs[h] * rows) * states[h]
                 + lax.dot_general(kds[h], vs[h], _TN, preferred_element_type=F32))
        s_scr[h] = s_new
        sout_ref[h] = s_new
    for h in heads:
        mu = jnp.mean(outs[h], axis=-1, keepdims=True)
        cen = outs[h] - mu
        var = jnp.mean(cen * cen, axis=-1, keepdims=True)
        yr = cen * lax.rsqrt(var + EPS) * gn_ref[h]
        g = g_ref[:, cols(h)]
        o_ref[:, cols(h)] = (g * _sigmoid(g) * yr).astype(o_ref.dtype)


def _into(out_buf):
    if out_buf is None:
        return [], []
    return [out_buf], [pl.BlockSpec(memory_space=pl.ANY)]


def _retention(log_gamma, q, k, v, gate, gn, s0, row_off, n_batch, n_steps, rows, out_buf=None):
    heads = q.shape[1] // RET_HEAD_DIM
    nh = min(RET_HEADS_PER_STEP, heads)
    assert row_off % rows == 0 and rows % CHUNK == 0 and heads % nh == 0
    ro = row_off // rows
    tok = lambda b, h, c, lg: (ro + b * n_steps + c, h)
    blk = pl.BlockSpec((rows, nh * RET_HEAD_DIM), tok)
    st = pl.BlockSpec((None, nh, RET_HEAD_DIM, RET_HEAD_DIM), lambda b, h, c, lg: (b, h, 0, 0))
    args = [log_gamma, q, k, v, gate, gn.reshape(heads, 1, RET_HEAD_DIM), s0]
    extra_args, extra_specs = _into(out_buf)
    grid_spec = pltpu.PrefetchScalarGridSpec(
        num_scalar_prefetch=1,
        grid=(n_batch, heads // nh, n_steps),
        in_specs=[blk, blk, blk, blk,
                  pl.BlockSpec((nh, 1, RET_HEAD_DIM), lambda b, h, c, lg: (h, 0, 0)), st] + extra_specs,
        out_specs=[blk, st],
        scratch_shapes=[pltpu.VMEM((nh, RET_HEAD_DIM, RET_HEAD_DIM), F32), pltpu.VMEM((nh, rows, rows), F32)],
    )
    return pl.pallas_call(
        functools.partial(_retention_body, rows=rows, n_heads=nh),
        out_shape=[jax.ShapeDtypeStruct((q.shape[0], heads * RET_HEAD_DIM), BF16),
                   jax.ShapeDtypeStruct((n_batch, heads, RET_HEAD_DIM, RET_HEAD_DIM), F32)],
        grid_spec=grid_spec,
        input_output_aliases={len(args): 0} if extra_args else {},
        compiler_params=_params("arbitrary", "arbitrary", "arbitrary"),
        name="retention",
    )(*args, *extra_args)


def _neg_suffix(n):
    j = lax.broadcasted_iota(jnp.int32, (2 * n, n), 0) & (n - 1)
    s = lax.broadcasted_iota(jnp.int32, (2 * n, n), 1)
    return jnp.where(j >= s, -1.0, 0.0).astype(BF16)


def _lanes_to(x, n):
    return x[:, :n] if n <= LANES else jnp.concatenate([x] * (n // LANES), axis=1)


def _suffix_sums(fail, neg_ones):
    tk = fail.shape[1]
    hi, lo = _split_bf16(fail)
    if tk % LANES == 0:
        return jnp.dot(jnp.concatenate([hi, lo], axis=1), neg_ones, preferred_element_type=F32)
    return (jnp.dot(hi, neg_ones[:tk], preferred_element_type=F32)
            + jnp.dot(lo, neg_ones[:tk], preferred_element_type=F32))


def _sb_tiles(qs, kbs, vbs, neg_ones, carry, acc, masked):
    heads = range(len(qs))
    tk = kbs[0].shape[0]
    zs = [lax.dot_general(qs[h], kbs[h], _NT, preferred_element_type=F32) for h in heads]
    fails = [jnp.maximum(z, 0.0) + jnp.log2(1.0 + jnp.exp2(-jnp.abs(z))) for z in zs]
    if masked:
        t = lax.broadcasted_iota(jnp.int32, zs[0].shape, 0)
        s = lax.broadcasted_iota(jnp.int32, zs[0].shape, 1)
        valid = s < t
        fails = [jnp.where(valid, f, 0.0) for f in fails]
    suffixes = [_suffix_sums(f, neg_ones) for f in fails]
    probs = [jnp.exp2(zs[h] + suffixes[h] + _lanes_to(carry[h], tk)) for h in heads]
    if masked:
        probs = [jnp.where(valid, p, 0.0) for p in probs]
    for h in heads:
        acc[h] += jnp.dot(probs[h].astype(BF16), vbs[h], preferred_element_type=F32)
        carry[h] += jnp.broadcast_to(suffixes[h][:, :1], carry.shape[1:])


def _sb_body(q_ref, kd_ref, vd_ref, kp_ref, vp_ref, *rest, tk, n_heads, past_per_step, past_static):
    o_ref, carry, acc = rest[-3:]
    tq = q_ref.shape[0]
    n_past = past_static if past_static is not None else pl.program_id(1) * past_per_step
    cols = lambda h: slice(h * SB_HEAD_DIM, (h + 1) * SB_HEAD_DIM)
    heads = range(n_heads)
    past = (lambda ref, h, rows: ref[rows, h, :]) if len(kp_ref.shape) == 3 else (
        lambda ref, h, rows: ref[rows, cols(h)])
    carry[...] = jnp.zeros(carry.shape, F32)
    acc[...] = jnp.zeros(acc.shape, F32)
    queries = lambda: [q_ref[:, cols(h)] for h in heads]
    _sb_tiles(queries(), [kd_ref[:, cols(h)].astype(BF16) for h in heads],
              [vd_ref[:, cols(h)].astype(BF16) for h in heads], _neg_suffix(tq), carry, acc, True)
    neg_ones = _neg_suffix(tk)

    alive = lambda: jnp.max(carry[...]) > SB_DEAD_LOG2

    def step(state):
        it = state[0]
        rows = pl.ds(pl.multiple_of((n_past - 1 - it) * tk, tk), tk)
        _sb_tiles(queries(), [past(kp_ref, h, rows).astype(BF16) for h in heads],
                  [past(vp_ref, h, rows).astype(BF16) for h in heads], neg_ones, carry, acc, False)
        return it + 1, alive()

    lax.while_loop(lambda state: (state[0] < n_past) & state[1], step, (jnp.int32(0), alive()))
    for h in range(n_heads):
        o_ref[:, cols(h)] = acc[h].astype(o_ref.dtype)


def _sb_scratch(n_heads, tq):
    return [pltpu.VMEM((n_heads, tq, LANES), F32), pltpu.VMEM((n_heads, tq, SB_HEAD_DIM), F32)]


def _sb_prompt(q, k, v, n_tok):
    heads = q.shape[1] // SB_HEAD_DIM
    tq = _tile(n_tok, SB_BLOCK)
    nh = SB_HEADS_PER_STEP
    width = nh * SB_HEAD_DIM
    blk = pl.BlockSpec((tq, width), lambda h, i: (i, h))
    full = pl.BlockSpec((n_tok, width), lambda h, i: (0, h))
    return pl.pallas_call(
        functools.partial(_sb_body, tk=tq, n_heads=nh, past_per_step=1, past_static=None),
        out_shape=jax.ShapeDtypeStruct((q.shape[0], heads * SB_HEAD_DIM), BF16),
        grid=(heads // nh, n_tok // tq),
        in_specs=[blk, blk, blk, full, full],
        out_specs=blk,
        scratch_shapes=_sb_scratch(nh, tq),
        compiler_params=_params("parallel", "arbitrary"),
        name="stick_breaking_prompt",
    )(q, k, v, k, v)


def _sb_sample(q, k, v, cache_k, cache_v, row_off, n_new, out_buf):
    heads = q.shape[1] // SB_HEAD_DIM
    _, n_batch, past, _, _ = cache_k.shape
    tk = _tile(past, SB_BLOCK)
    assert row_off % n_new == 0
    ro = row_off // n_new
    nh = min(F32_SUBLANES, heads)
    assert heads % nh == 0 and (nh == F32_SUBLANES or nh == heads)
    blk = pl.BlockSpec((n_new, nh * SB_HEAD_DIM), lambda b, h: (ro + b, h))
    full = pl.BlockSpec((None, None, past, nh, SB_HEAD_DIM), lambda b, h: (0, b, 0, h, 0))
    extra_args, extra_specs = _into(out_buf)
    return pl.pallas_call(
        functools.partial(_sb_body, tk=tk, n_heads=nh, past_per_step=0, past_static=past // tk),
        out_shape=jax.ShapeDtypeStruct(out_buf.shape, out_buf.dtype),
        grid=(n_batch, heads // nh),
        in_specs=[blk, blk, blk, full, full] + extra_specs,
        out_specs=blk,
        scratch_shapes=_sb_scratch(nh, n_new),
        input_output_aliases={5: 0},
        compiler_params=_params("parallel", "arbitrary"),
        name="stick_breaking_sample",
    )(q, k, v, cache_k, cache_v, *extra_args)


def _cross_body(q_ref, k_ref, v_ref, *rest, scale, steps):
    o_ref, kb, vb = rest[-3:]
    dh = q_ref.shape[1] // MEM_HEADS
    heads = range(MEM_HEADS)
    cols = lambda h: slice(h * dh, (h + 1) * dh)
    head_of = (lambda ref, h: ref[:, h, :]) if len(k_ref.shape) == 3 else (lambda ref, h: ref[:, cols(h)])

    @pl.when(pl.program_id(0) % steps == 0)
    def _():
        for h in heads:
            kb[:, cols(h)] = head_of(k_ref, h).astype(BF16)
            vb[:, cols(h)] = head_of(v_ref, h).astype(BF16)

    scores = [lax.dot_general(q_ref[:, cols(h)], kb[:, cols(h)], _NT, preferred_element_type=F32) * scale
              for h in heads]
    exps = [jnp.exp(s - jnp.max(s, axis=-1, keepdims=True)) for s in scores]
    probs = [e / jnp.sum(e, axis=-1, keepdims=True) for e in exps]
    for h in heads:
        o_ref[:, cols(h)] = jnp.dot(probs[h].astype(BF16), vb[:, cols(h)],
                                    preferred_element_type=F32).astype(o_ref.dtype)


def _cross_attend(q, mem_k, mem_v, row_off, rows_per_seq, out_buf=None):
    d = q.shape[1]
    dh = d // MEM_HEADS
    tm = _tile(rows_per_seq, ROW_TILE)
    steps = rows_per_seq // tm
    assert row_off % tm == 0
    ro = row_off // tm
    extra_args, extra_specs = _into(out_buf)
    if mem_k.ndim == 3:
        n_seq, n_mem, _ = mem_k.shape
        mem = pl.BlockSpec((None, n_mem, d), lambda i: (i // steps, 0, 0))
    else:
        _, n_seq, n_mem, _, _ = mem_k.shape
        mem = pl.BlockSpec((None, None, n_mem, MEM_HEADS, dh), lambda i: (0, i // steps, 0, 0, 0))
    blk = pl.BlockSpec((tm, d), lambda i: (ro + i, 0))
    return pl.pallas_call(
        functools.partial(_cross_body, scale=dh ** -0.5, steps=steps),
        out_shape=jax.ShapeDtypeStruct((q.shape[0], d), BF16),
        grid=(n_seq * steps,),
        in_specs=[blk, mem, mem] + extra_specs,
        out_specs=blk,
        scratch_shapes=[pltpu.VMEM((n_mem, d), BF16), pltpu.VMEM((n_mem, d), BF16)],
        input_output_aliases={3: 0} if extra_args else {},
        compiler_params=_params("arbitrary"),
        name="cross_attention",
    )(q, mem_k, mem_v, *extra_args)


def _split_bf16(x):
    hi = x.astype(BF16)
    return hi, (x - hi.astype(F32)).astype(BF16)


def _router_body(x_ref, g_ref, wr_ref, br_ref, h_ref, ids_ref, wts_ref):
    h = _rmsnorm_rows(x_ref[...], g_ref[...])
    h_ref[...] = _pack_bf16_pairs(h)
    h_hi, h_lo = _split_bf16(h)
    w_hi, w_lo = _split_bf16(wr_ref[...])
    logits = (jnp.dot(h_hi, w_hi, preferred_element_type=F32)
              + jnp.dot(h_hi, w_lo, preferred_element_type=F32)
              + jnp.dot(h_lo, w_hi, preferred_element_type=F32)) + br_ref[...]
    lane = lax.broadcasted_iota(jnp.int32, logits.shape, 1).astype(F32)
    big = float(LANES)
    neg = -jnp.inf

    def first_max(vals):
        top = jnp.max(vals, axis=-1, keepdims=True)
        return top, jnp.min(jnp.where(vals == top, lane, big), axis=-1, keepdims=True)

    is_group = lane < N_GROUPS
    g_max, g_top = first_max(jnp.where(is_group, logits, neg))
    g_w = 1.0 / jnp.sum(jnp.where(is_group, jnp.exp(logits - g_max), 0.0), axis=-1, keepdims=True)
    first = N_GROUPS + EXPERTS_PER_GROUP * g_top
    cand = jnp.where((lane >= first) & (lane < first + EXPERTS_PER_GROUP), logits, neg)
    v1, i1 = first_max(cand)
    v2, i2 = first_max(jnp.where(lane == i1, neg, cand))
    t = jnp.exp(v2 - v1)
    w1 = g_w / (1.0 + t)
    w2 = g_w * t / (1.0 + t)
    ids = jnp.where(lane == 0.0, i1 - N_GROUPS, jnp.where(lane == 1.0, i2 - N_GROUPS, 0.0))
    ids_ref[...] = ids.astype(jnp.int32)
    wts_ref[...] = jnp.where(lane == 0.0, w1, jnp.where(lane == 1.0, w2, 0.0))


def _norm_and_route(x, g, wg, bg, we, be):
    m, d = x.shape
    tm = _tile(m, 256)
    pad = LANES - N_GROUPS - N_EXPERTS
    wr = jnp.concatenate([wg, we, jnp.zeros((d, pad), F32)], axis=1)
    br = jnp.concatenate([bg, be, jnp.zeros((pad,), F32)]).reshape(1, LANES)
    row = lambda c: pl.BlockSpec((tm, c), lambda i: (i, 0))
    return pl.pallas_call(
        _router_body,
        out_shape=[jax.ShapeDtypeStruct((m, d // 2), jnp.int32),
                   jax.ShapeDtypeStruct((m, LANES), jnp.int32),
                   jax.ShapeDtypeStruct((m, LANES), F32)],
        grid=(m // tm,),
        in_specs=[row(d), pl.BlockSpec((1, d), lambda i: (0, 0)),
                  pl.BlockSpec((d, LANES), lambda i: (0, 0)), pl.BlockSpec((1, LANES), lambda i: (0, 0))],
        out_specs=[row(d // 2), row(LANES), row(LANES)],
        compiler_params=_params("parallel"),
        name="norm_route",
    )(x, g.reshape(1, d), wr, br)


GATHER_UNROLL = 8


def _start_row_gather(idx_ref, first, stride, src_ref, buf, sem, n_rows):
    def one(r, _):
        t = idx_ref[first + r * stride]
        pltpu.make_async_copy(src_ref.at[pl.ds(t, 1)], buf.at[pl.ds(r, 1)], sem).start()
        return 0

    def group(g, _):
        for u in range(GATHER_UNROLL):
            one(g * GATHER_UNROLL + u, 0)
        return 0

    if isinstance(n_rows, int):
        for r in range(n_rows):
            one(r, 0)
        return
    groups = n_rows // GATHER_UNROLL
    lax.fori_loop(0, groups, group, 0)
    lax.fori_loop(groups * GATHER_UNROLL, n_rows, one, 0)


def _wait_row_gather(src_ref, buf, sem, n_rows):
    pltpu.make_async_copy(src_ref.at[pl.ds(0, n_rows)], buf.at[pl.ds(0, n_rows)], sem).wait()


def _pack_bf16_pairs(x):
    half = x.shape[1] // 2
    hi = lax.bitcast_convert_type(x[:, :half].astype(BF16).astype(F32), jnp.int32)
    lo = lax.bitcast_convert_type(x[:, half:].astype(BF16).astype(F32), jnp.int32)
    return hi | lax.shift_right_logical(lo, jnp.int32(16))


def _unpack_bf16_pairs(p):
    hi = lax.bitcast_convert_type(p & jnp.int32(-65536), F32)
    lo = lax.bitcast_convert_type(lax.shift_left(p, jnp.int32(16)), F32)
    return hi, lo


def _dispatch_body(idx_ref, used_ref, cnt_ref, src_ref, o_ref, buf, sem):
    b = pl.program_id(0)
    rows = buf.shape[1]

    copied = lambda blk: pl.multiple_of(pl.cdiv(cnt_ref[blk], GATHER_UNROLL) * GATHER_UNROLL, GATHER_UNROLL)

    def start(blk):
        _start_row_gather(idx_ref, blk * rows, 1, src_ref, buf.at[blk % 2], sem.at[blk % 2], copied(blk))

    @pl.when(b == 0)
    def _():
        buf[...] = jnp.zeros(buf.shape, buf.dtype)
        start(0)

    @pl.when(b + 1 < used_ref[0])
    def _():
        start(b + 1)

    @pl.when(b < used_ref[0])
    def _():
        slot = b % 2
        cnt = cnt_ref[b]
        _wait_row_gather(src_ref, buf.at[slot], sem.at[slot], copied(b))
        half = buf.shape[2]
        row = lax.broadcasted_iota(jnp.int32, (rows, 1), 0)
        hi, lo = _unpack_bf16_pairs(jnp.where(row < cnt, buf[slot], 0))
        o_ref[:, :half] = hi.astype(o_ref.dtype)
        o_ref[:, half:] = lo.astype(o_ref.dtype)


def _dispatch_rows(src, row_tok, blk_cnt, n_used):
    n_rows = row_tok.shape[0]
    half = src.shape[1]
    grid_spec = pltpu.PrefetchScalarGridSpec(
        num_scalar_prefetch=3,
        grid=(n_rows // MOE_ROWS,),
        in_specs=[pl.BlockSpec(memory_space=pl.ANY)],
        out_specs=pl.BlockSpec((MOE_ROWS, 2 * half), lambda b, idx, used, cnt: (b, 0)),
        scratch_shapes=[pltpu.VMEM((2, MOE_ROWS, half), src.dtype), pltpu.SemaphoreType.DMA((2,))],
    )
    return pl.pallas_call(
        _dispatch_body,
        out_shape=jax.ShapeDtypeStruct((n_rows, 2 * half), BF16),
        grid_spec=grid_spec,
        compiler_params=_params("arbitrary"),
        name="dispatch_rows",
    )(row_tok, n_used, blk_cnt, src)


def _expert_up_body(be_ref, first_ref, used_ref, x_ref, w1_ref, w3_ref, h_ref, w1b, w3b):
    b = pl.program_id(1)

    @pl.when(first_ref[b] == 1)
    def _():
        w1b[...] = w1_ref[...].astype(BF16)
        w3b[...] = w3_ref[...].astype(BF16)

    @pl.when(b < used_ref[0])
    def _():
        x = x_ref[...]
        a = jnp.dot(x, w1b[...], preferred_element_type=F32)
        g = jnp.dot(x, w3b[...], preferred_element_type=F32)
        h_ref[...] = (a * _sigmoid(a) * g).astype(h_ref.dtype)


def _expert_down_body(be_ref, first_ref, used_ref, h_ref, w2_ref, y_ref, w2b):
    b = pl.program_id(1)

    @pl.when(first_ref[b] == 1)
    def _():
        w2b[...] = w2_ref[...].astype(BF16)

    @pl.when(b < used_ref[0])
    def _():
        y_ref[...] = _pack_bf16_pairs(jnp.dot(h_ref[...], w2b[...], preferred_element_type=F32))


def _expert_ffn(xs, blk_exp, blk_first, n_used, w1, w3, w2):
    n_rows, d = xs.shape
    f = w1.shape[2]
    n_blocks = n_rows // MOE_ROWS
    tf = _tile(f, EXPERT_HIDDEN_TILE)
    tn = _tile(d, EXPERT_OUT_TILE)
    last = lambda b, used: jnp.minimum(b, used[0] - 1)
    up_spec = pltpu.PrefetchScalarGridSpec(
        num_scalar_prefetch=3,
        grid=(f // tf, n_blocks),
        in_specs=[pl.BlockSpec((MOE_ROWS, d), lambda j, b, be, fi, us: (last(b, us), 0)),
                  pl.BlockSpec((None, d, tf), lambda j, b, be, fi, us: (be[b], 0, j)),
                  pl.BlockSpec((None, d, tf), lambda j, b, be, fi, us: (be[b], 0, j))],
        out_specs=pl.BlockSpec((MOE_ROWS, tf), lambda j, b, be, fi, us: (b, j)),
        scratch_shapes=[pltpu.VMEM((d, tf), BF16), pltpu.VMEM((d, tf), BF16)],
    )
    hidden = pl.pallas_call(
        _expert_up_body,
        out_shape=jax.ShapeDtypeStruct((n_rows, f), BF16),
        grid_spec=up_spec,
        compiler_params=_params("arbitrary", "arbitrary"),
        name="expert_up",
    )(blk_exp, blk_first, n_used, xs, w1, w3)
    down_spec = pltpu.PrefetchScalarGridSpec(
        num_scalar_prefetch=3,
        grid=(d // tn, n_blocks),
        in_specs=[pl.BlockSpec((MOE_ROWS, f), lambda j, b, be, fi, us: (last(b, us), 0)),
                  pl.BlockSpec((None, f, tn), lambda j, b, be, fi, us: (be[b], 0, j))],
        out_specs=pl.BlockSpec((MOE_ROWS, tn // 2), lambda j, b, be, fi, us: (b, j)),
        scratch_shapes=[pltpu.VMEM((f, tn), BF16)],
    )
    return pl.pallas_call(
        _expert_down_body,
        out_shape=jax.ShapeDtypeStruct((n_rows, d // 2), jnp.int32),
        grid_spec=down_spec,
        compiler_params=_params("arbitrary", "arbitrary"),
        name="expert_down",
    )(blk_exp, blk_first, n_used, hidden, w2)


def _route_layout(ids, n_tok):
    n_assign = n_tok * TOP_K
    flat_e = ids.reshape(n_assign)
    flat_t = jnp.repeat(jnp.arange(n_tok, dtype=jnp.int32), TOP_K)
    onehot = (flat_e[:, None] == jnp.arange(N_EXPERTS, dtype=jnp.int32)[None, :]).astype(jnp.int32)
    running = jnp.cumsum(onehot, axis=0)
    rank = jnp.take_along_axis(running, flat_e[:, None], axis=1)[:, 0] - 1
    counts = running[-1]
    padded = (counts + MOE_ROWS - 1) // MOE_ROWS * MOE_ROWS
    pad_end = jnp.cumsum(padded)
    dest = (pad_end - padded)[flat_e] + rank
    n_blocks = -(-n_assign // MOE_ROWS) + N_EXPERTS
    n_rows = n_blocks * MOE_ROWS
    row_tok = jnp.zeros((n_rows,), jnp.int32).at[dest].set(flat_t)
    blk = jnp.arange(n_blocks, dtype=jnp.int32)
    n_used = (pad_end[-1] // MOE_ROWS).astype(jnp.int32)
    blk_exp = jnp.searchsorted(pad_end, jnp.minimum(blk, n_used - 1) * MOE_ROWS, side='right').astype(jnp.int32)
    blk_exp = jnp.minimum(blk_exp, N_EXPERTS - 1)
    prev = jnp.concatenate([jnp.full((1,), -1, jnp.int32), blk_exp[:-1]])
    blk_first = ((blk_exp != prev) & (blk < n_used)).astype(jnp.int32)
    into_range = blk * MOE_ROWS - (pad_end - padded)[blk_exp]
    blk_cnt = jnp.where(blk < n_used, jnp.clip(counts[blk_exp] - into_range, 0, MOE_ROWS), 0).astype(jnp.int32)
    return row_tok, blk_cnt, dest.astype(jnp.int32), blk_exp, blk_first, n_used.reshape(1)


def _final_body(dest_ref, x_ref, w_ref, y_ref, g_ref, op_ref, os_ref, buf, sem, *, prompt_tiles, pack_tile):
    i = pl.program_id(0)
    tm = buf.shape[2]

    def start(blk):
        for k in range(TOP_K):
            _start_row_gather(dest_ref, blk * tm * TOP_K + k, TOP_K, y_ref, buf.at[blk % 2, k], sem.at[blk % 2, k],
                              tm)

    @pl.when(i == 0)
    def _():
        start(0)

    @pl.when(i + 1 < pl.num_programs(0))
    def _():
        start(i + 1)

    slot = i % 2
    for k in range(TOP_K):
        _wait_row_gather(y_ref, buf.at[slot, k], sem.at[slot, k], tm)
    w = w_ref[...]
    pieces = []
    for j in range(buf.shape[3] // pack_tile):
        words = slice(j * pack_tile, (j + 1) * pack_tile)
        hi0, lo0 = _unpack_bf16_pairs(buf[slot, 0, :, words])
        hi1, lo1 = _unpack_bf16_pairs(buf[slot, 1, :, words])
        pieces += [hi0 * w[:, 0:1] + hi1 * w[:, 1:2], lo0 * w[:, 0:1] + lo1 * w[:, 1:2]]
    y = _rmsnorm_rows(x_ref[...] + jnp.concatenate(pieces, axis=1), g_ref[...])

    @pl.when(i < prompt_tiles)
    def _():
        op_ref[...] = y

    @pl.when(i >= prompt_tiles)
    def _():
        os_ref[...] = y


def _combine_and_norm(x, wts, y_rows, dest, g, n_prompt):
    m, d = x.shape
    tm = _tile(m - n_prompt, _tile(n_prompt, 256))
    assert n_prompt % tm == 0
    pt = n_prompt // tm
    grid_spec = pltpu.PrefetchScalarGridSpec(
        num_scalar_prefetch=1,
        grid=(m // tm,),
        in_specs=[pl.BlockSpec((tm, d), lambda i, dest: (i, 0)),
                  pl.BlockSpec((tm, LANES), lambda i, dest: (i, 0)),
                  pl.BlockSpec(memory_space=pl.ANY),
                  pl.BlockSpec((1, d), lambda i, dest: (0, 0))],
        out_specs=[pl.BlockSpec((tm, d), lambda i, dest: (jnp.minimum(i, pt - 1), 0)),
                   pl.BlockSpec((tm, d), lambda i, dest: (jnp.maximum(i - pt, 0), 0))],
        scratch_shapes=[pltpu.VMEM((2, TOP_K, tm, d // 2), y_rows.dtype), pltpu.SemaphoreType.DMA((2, TOP_K))],
    )
    return pl.pallas_call(
        functools.partial(_final_body, prompt_tiles=pt, pack_tile=_tile(d, EXPERT_OUT_TILE) // 2),
        out_shape=[jax.ShapeDtypeStruct((n_prompt, d), F32), jax.ShapeDtypeStruct((m - n_prompt, d), F32)],
        grid_spec=grid_spec,
        compiler_params=_params("arbitrary"),
        name="combine_norm",
    )(dest, x, wts, y_rows, g.reshape(1, d))


def _layer(xp, xs, n_seq, n_new, past_len, cache_sb_k, cache_sb_v, state_ret, cache_mem_k, cache_mem_v,
           mem_prompt, norm_mix, w_in, ret_gn, w_out, norm_cross, norm_mem, w_cq, w_ck, w_cv, w_co, norm_ffn,
           wg, bg, we, be, w1, w3, w2):
    n_prompt, d = xp.shape
    n_tok = n_prompt + xs.shape[0]
    d_ret = d // 2
    d_sb = d - d_ret
    ret_heads = d_ret // RET_HEAD_DIM
    sds = jax.ShapeDtypeStruct

    pos = np.concatenate([np.arange(n_prompt), np.tile(past_len + np.arange(n_new), n_seq)])
    half = RET_HEAD_DIM // 2
    inv = (ROPE_BASE ** (-np.arange(half, dtype=np.float32) / half)).astype(np.float32)
    ang = pos.astype(np.float32)[:, None] * inv[None, :]
    cos, sin = jnp.asarray(np.cos(ang), F32), jnp.asarray(np.sin(ang), F32)
    log_gamma = jnp.log1p(-jnp.exp2(-5.0 - jnp.arange(ret_heads, dtype=F32)))

    h = _rmsnorm_stacked(xp, xs, norm_mix, BF16)
    rot = [(cos, half), (sin, half)]
    bf = lambda n: [(sds((n_tok, n), BF16), None)]
    (rq,) = _matmul([h], w_in, 0, d_ret, _ep_rotary(1.0), rot, bf(d_ret), "in_proj_ret_q")
    (rk,) = _matmul([h], w_in, d_ret, d_ret, _ep_rotary(RET_HEAD_DIM ** -0.5), rot, bf(d_ret), "in_proj_ret_k")
    (rv,) = _matmul([h], w_in, 2 * d_ret, d_ret, _ep_store(1.0), [], bf(d_ret), "in_proj_ret_v")
    (rg,) = _matmul([h], w_in, 3 * d_ret, d_ret, _ep_store(1.0), [], [(sds((n_tok, d_ret), F32), None)],
                    "in_proj_ret_gate")
    (sq,) = _matmul([h], w_in, 4 * d_ret, d_sb, _ep_store(SB_HEAD_DIM ** -0.5 * LOG2_E), [], bf(d_sb),
                    "in_proj_sb_q")
    split = [(sds((n_prompt, d_sb), F32), PROMPT), (sds((n_tok - n_prompt, d_sb), F32), SAMPLE),
             (sds((n_tok, d_sb), BF16), None)]
    sk_p, sk_s, sk_b = _matmul([h], w_in, 4 * d_ret + d_sb, d_sb, _ep_store_split, [], split, "in_proj_sb_k",
                               n_prompt=n_prompt)
    sv_p, sv_s, sv_b = _matmul([h], w_in, 4 * d_ret + 2 * d_sb, d_sb, _ep_store_split, [], split,
                               "in_proj_sb_v", n_prompt=n_prompt)

    rows_p = _tile(n_prompt, RET_BLOCK)
    zeros_state = jnp.zeros((1, ret_heads, RET_HEAD_DIM, RET_HEAD_DIM), F32)
    yr, state_p = _retention(log_gamma, rq, rk, rv, rg, ret_gn, zeros_state, 0, 1, n_prompt // rows_p, rows_p)
    yr, state_s = _retention(log_gamma, rq, rk, rv, rg, ret_gn, state_ret, n_prompt, n_seq, 1, n_new,
                             out_buf=yr)
    sb = _sb_prompt(sq, sk_b, sv_b, n_prompt)
    sb = _sb_sample(sq, sk_b, sv_b, cache_sb_k, cache_sb_v, n_prompt, n_new, sb)
    res = [(sds((n_tok, d), F32), None)]
    (x,) = _matmul([yr, sb], w_out, 0, d, _ep_residual_split, [(xp, PROMPT), (xs, SAMPLE)], res, "out_proj",
                   n_prompt=n_prompt)

    n_mem = mem_prompt.shape[0]
    m = _rmsnorm(mem_prompt, norm_mem, BF16)
    mem_out = [(sds((n_mem, d), F32), None)]
    (mk,) = _matmul([m], w_ck, 0, d, _ep_store(1.0), [], mem_out, "mem_k")
    (mv,) = _matmul([m], w_cv, 0, d, _ep_store(1.0), [], mem_out, "mem_v")
    h = _rmsnorm(x, norm_cross, BF16)
    (cq,) = _matmul([h], w_cq, 0, d, _ep_store(1.0), [], bf(d), "cross_q")
    co = _cross_attend(cq, mk.reshape(1, n_mem, d), mv.reshape(1, n_mem, d), 0, n_prompt)
    co = _cross_attend(cq, cache_mem_k, cache_mem_v, n_prompt, n_new, out_buf=co)
    (x,) = _matmul([co], w_co, 0, d, _ep_residual, [(x, None)], res, "cross_out")

    hn, ids, wts = _norm_and_route(x, norm_ffn, wg, bg, we, be)
    row_tok, blk_cnt, dest, blk_exp, blk_first, n_used = _route_layout(ids[:, :TOP_K], n_tok)
    x_rows = _dispatch_rows(hn, row_tok, blk_cnt, n_used)
    y_rows = _expert_ffn(x_rows, blk_exp, blk_first, n_used, w1, w3, w2)
    return x, wts, y_rows, dest, (sk_p, sv_p, sk_s, sv_s, state_p, state_s, mk, mv)


def kernel(x_prompt, x_sample, cache_sb_k, cache_sb_v, state_ret, cache_mem_k, cache_mem_v, mem_prompt, norm_mix, w_in, ret_gn, w_out, norm_cross, norm_mem, w_cq, w_ck, w_cv, w_co, norm_ffn, router_group_w, router_group_b, router_expert_w, router_expert_b, expert_w1, expert_w3, expert_w2, norm_final):
    batch, n_prompt, d = x_prompt.shape
    n_seq, n_new, _ = x_sample.shape
    depth, _, past_len, sb_heads, _ = cache_sb_k.shape
    n_mem = mem_prompt.shape[1]
    assert batch == 1 and depth == 1
    ret_heads = (d // 2) // RET_HEAD_DIM
    x, wts, y_rows, dest, (sk_p, sv_p, sk_s, sv_s, state_p, state_s, mk, mv) = _layer(
        x_prompt.reshape(n_prompt, d), x_sample.reshape(n_seq * n_new, d), n_seq, n_new, past_len,
        cache_sb_k, cache_sb_v, state_ret[0], cache_mem_k, cache_mem_v, mem_prompt[0], norm_mix[0], w_in,
        ret_gn[0], w_out, norm_cross[0], norm_mem[0], w_cq, w_ck, w_cv, w_co, norm_ffn[0], router_group_w[0],
        router_group_b[0], router_expert_w[0], router_expert_b[0], expert_w1[0], expert_w3[0], expert_w2[0])
    y_prompt, y_sample = _combine_and_norm(x, wts, y_rows, dest, norm_final, n_prompt)
    y_prompt = y_prompt.reshape(1, n_prompt, d)
    y_sample = y_sample.reshape(n_seq, n_new, d)
    sb_shape_p = (1, 1, n_prompt, sb_heads, SB_HEAD_DIM)
    sb_shape_s = (1, n_seq, n_new, sb_heads, SB_HEAD_DIM)
    mem_shape = (1, 1, n_mem, MEM_HEADS, d // MEM_HEADS)
    return (y_prompt, y_sample,
            sk_p.reshape(sb_shape_p), sv_p.reshape(sb_shape_p),
            state_p.reshape(1, 1, ret_heads, RET_HEAD_DIM, RET_HEAD_DIM),
            mk.reshape(mem_shape), mv.reshape(mem_shape),
            sk_s.reshape(sb_shape_s), sv_s.reshape(sb_shape_s),
            state_s.reshape(1, n_seq, ret_heads, RET_HEAD_DIM, RET_HEAD_DIM))
```

```python
import functools

import jax
import jax.numpy as jnp
import numpy as np
from jax import lax
from jax.experimental import pallas as pl
from jax.experimental.pallas import tpu as pltpu

BF16 = jnp.bfloat16
F32 = jnp.float32

EPS = 1e-6
CHUNK = 64
RET_HEAD_DIM = 256
SB_HEAD_DIM = 128
MEM_HEADS = 4
N_GROUPS = 4
EXPERTS_PER_GROUP = 8
N_EXPERTS = N_GROUPS * EXPERTS_PER_GROUP
TOP_K = 2
ROPE_BASE = 10000.0

LANES = 128
F32_SUBLANES = 8
VMEM_LIMIT_BYTES = 60 * 1024 * 1024
MATMUL_VMEM_BUDGET = 56 * 1024 * 1024
ROW_TILE = 512
COL_TILE = 1024
EXPERT_HIDDEN_TILE = 512
RET_BLOCK = 256
RET_HEADS_PER_STEP = 4
SB_BLOCK = 256
SB_HEADS_PER_STEP = 4
LOG2_E = 1.4426950408889634
SB_DEAD_LOG2 = -64.0
MOE_ROWS = 512
EXPERT_OUT_TILE = 4096

PROMPT = "prompt"
SAMPLE = "sample"

_NT = (((1,), (1,)), ((), ()))
_TN = (((0,), (0,)), ((), ()))


def _tile(n, pref):
    t = min(pref, n)
    while n % t:
        t //= 2
    return t


def _params(*sem):
    return pltpu.CompilerParams(dimension_semantics=sem, vmem_limit_bytes=VMEM_LIMIT_BYTES)


def _sigmoid(x):
    return 1.0 / (1.0 + jnp.exp(-x))


def _rmsnorm_rows(x, g):
    ms = jnp.mean(x * x, axis=-1, keepdims=True)
    return x * lax.rsqrt(ms + EPS) * g


def _rmsnorm_body(x_ref, g_ref, o_ref):
    o_ref[...] = _rmsnorm_rows(x_ref[...], g_ref[...]).astype(o_ref.dtype)


def _rmsnorm(x, g, out_dtype):
    m, d = x.shape
    tm = _tile(m, 256)
    return pl.pallas_call(
        _rmsnorm_body,
        out_shape=jax.ShapeDtypeStruct((m, d), out_dtype),
        grid=(m // tm,),
        in_specs=[pl.BlockSpec((tm, d), lambda i: (i, 0)), pl.BlockSpec((1, d), lambda i: (0, 0))],
        out_specs=pl.BlockSpec((tm, d), lambda i: (i, 0)),
        compiler_params=_params("parallel"),
        name="rmsnorm",
    )(x, g.reshape(1, d))


def _rmsnorm_stacked_body(xp_ref, xs_ref, g_ref, o_ref, *, prompt_tiles):
    is_prompt = pl.program_id(0) < prompt_tiles

    @pl.when(is_prompt)
    def _():
        o_ref[...] = _rmsnorm_rows(xp_ref[...], g_ref[...]).astype(o_ref.dtype)

    @pl.when(jnp.logical_not(is_prompt))
    def _():
        o_ref[...] = _rmsnorm_rows(xs_ref[...], g_ref[...]).astype(o_ref.dtype)


def _rmsnorm_stacked(xp, xs, g, out_dtype):
    (n_p, d), n_s = xp.shape, xs.shape[0]
    tm = _tile(n_s, _tile(n_p, 256))
    pt = n_p // tm
    return pl.pallas_call(
        functools.partial(_rmsnorm_stacked_body, prompt_tiles=pt),
        out_shape=jax.ShapeDtypeStruct((n_p + n_s, d), out_dtype),
        grid=((n_p + n_s) // tm,),
        in_specs=[pl.BlockSpec((tm, d), lambda i: (jnp.minimum(i, pt - 1), 0)),
                  pl.BlockSpec((tm, d), lambda i: (jnp.maximum(i - pt, 0), 0)),
                  pl.BlockSpec((1, d), lambda i: (0, 0))],
        out_specs=pl.BlockSpec((tm, d), lambda i: (i, 0)),
        compiler_params=_params("arbitrary"),
        name="rmsnorm_stacked",
    )(xp, xs, g.reshape(1, d))


def _matmul_body(*refs, n_lhs, n_extra, n_out, epilogue):
    x_refs = refs[:n_lhs]
    w_refs = refs[n_lhs:2 * n_lhs]
    extra = refs[2 * n_lhs:2 * n_lhs + n_extra]
    outs = refs[2 * n_lhs + n_extra:2 * n_lhs + n_extra + n_out]
    wb = refs[2 * n_lhs + n_extra + n_out:]

    @pl.when(pl.program_id(1) == 0)
    def _():
        for p in range(n_lhs):
            wb[p][...] = w_refs[p][...].astype(BF16)

    acc = jnp.dot(x_refs[0][...], wb[0][...], preferred_element_type=F32)
    for p in range(1, n_lhs):
        acc += jnp.dot(x_refs[p][...], wb[p][...], preferred_element_type=F32)
    epilogue(acc, extra, outs)


def _matmul(xs, w, col_off, n_cols, epilogue, extras, outs, name, n_prompt=None):
    m = xs[0].shape[0]
    tm = _tile(m, ROW_TILE) if n_prompt is None else _tile(m - n_prompt, _tile(n_prompt, ROW_TILE))
    k_total = sum(x.shape[1] for x in xs)

    def vmem_bytes(tn):
        tiles = sum((tn if kind is None or isinstance(kind, str) else kind) * jnp.dtype(a.dtype).itemsize
                    for a, kind in list(extras) + list(outs))
        return k_total * tn * (2 * 4 + 2) + 2 * tm * k_total * 2 + 2 * tm * tiles

    tn = _tile(n_cols, COL_TILE)
    while vmem_bytes(tn) > MATMUL_VMEM_BUDGET and tn > LANES:
        tn //= 2
    assert col_off % tn == 0
    jo = col_off // tn
    pt = None if n_prompt is None else n_prompt // tm
    in_specs, args = [], []
    for x in xs:
        in_specs.append(pl.BlockSpec((tm, x.shape[1]), lambda j, i: (i, 0)))
        args.append(x)
    row = 0
    for x in xs:
        kp = x.shape[1]
        assert row % kp == 0
        in_specs.append(pl.BlockSpec((None, kp, tn), lambda j, i, r=row // kp: (0, r, j + jo)))
        args.append(w)
        row += kp
    assert row == w.shape[1]

    def spec(kind):
        if kind is None:
            return pl.BlockSpec((tm, tn), lambda j, i: (i, j))
        if kind == PROMPT:
            return pl.BlockSpec((tm, tn), lambda j, i: (jnp.minimum(i, pt - 1), j))
        if kind == SAMPLE:
            return pl.BlockSpec((tm, tn), lambda j, i: (jnp.maximum(i - pt, 0), j))
        return pl.BlockSpec((tm, kind), lambda j, i: (i, 0))

    for a, kind in extras:
        in_specs.append(spec(kind))
        args.append(a)
    body = functools.partial(_matmul_body, n_lhs=len(xs), n_extra=len(extras), n_out=len(outs),
                             epilogue=epilogue if n_prompt is None else functools.partial(epilogue, pt))
    res = pl.pallas_call(
        body,
        out_shape=[o for o, _ in outs],
        grid=(n_cols // tn, m // tm),
        in_specs=in_specs,
        out_specs=[spec(kind) for _, kind in outs],
        scratch_shapes=[pltpu.VMEM((x.shape[1], tn), BF16) for x in xs],
        compiler_params=_params("arbitrary", "arbitrary"),
        name=name,
    )(*args)
    return res


def _ep_store(scale):
    def ep(acc, extra, outs):
        for o in outs:
            o[...] = (acc * scale).astype(o.dtype) if scale != 1.0 else acc.astype(o.dtype)
    return ep


def _ep_store_split(prompt_tiles, acc, extra, outs):
    is_prompt = pl.program_id(1) < prompt_tiles

    @pl.when(is_prompt)
    def _():
        outs[0][...] = acc

    @pl.when(jnp.logical_not(is_prompt))
    def _():
        outs[1][...] = acc

    outs[2][...] = acc.astype(outs[2].dtype)


def _ep_residual(acc, extra, outs):
    outs[0][...] = extra[0][...] + acc


def _ep_residual_split(prompt_tiles, acc, extra, outs):
    res = jnp.where(pl.program_id(1) < prompt_tiles, extra[0][...], extra[1][...])
    outs[0][...] = res + acc


def _ep_rotary(scale):
    half = RET_HEAD_DIM // 2

    def ep(acc, extra, outs):
        cos = extra[0][...]
        sin = extra[1][...]
        for h in range(acc.shape[1] // RET_HEAD_DIM):
            lo = h * RET_HEAD_DIM
            x1 = acc[:, lo:lo + half]
            x2 = acc[:, lo + half:lo + RET_HEAD_DIM]
            outs[0][:, lo:lo + half] = ((x1 * cos - x2 * sin) * scale).astype(outs[0].dtype)
            outs[0][:, lo + half:lo + RET_HEAD_DIM] = ((x1 * sin + x2 * cos) * scale).astype(outs[0].dtype)
    return ep


def _retention_body(lg_ref, q_ref, k_ref, v_ref, g_ref, gn_ref, s0_ref, *rest, rows, n_heads):
    o_ref, sout_ref, s_scr, decay_scr = rest[-4:]
    heads = range(n_heads)
    cols = lambda h: slice(h * RET_HEAD_DIM, (h + 1) * RET_HEAD_DIM)
    lgs = [lg_ref[pl.program_id(1) * n_heads + h] for h in heads]

    @pl.when(pl.program_id(2) == 0)
    def _():
        s_scr[...] = s0_ref[...]
        ii = lax.broadcasted_iota(jnp.int32, (rows, rows), 0)
        jj = lax.broadcasted_iota(jnp.int32, (rows, rows), 1)
        shift = CHUNK.bit_length() - 1
        dist = jnp.abs(ii - jj).astype(F32)
        for h in heads:
            decay_scr[h] = jnp.where((jj >> shift) <= (ii >> shift), jnp.exp(lgs[h] * dist), 0.0)

    idx = lax.broadcasted_iota(jnp.int32, (rows, 1), 0).astype(F32)
    qs = [q_ref[:, cols(h)] for h in heads]
    ks = [k_ref[:, cols(h)] for h in heads]
    vs = [v_ref[:, cols(h)] for h in heads]
    scores = [lax.dot_general(qs[h], ks[h], _NT, preferred_element_type=F32) * decay_scr[h] for h in heads]
    outs = [jnp.dot(scores[h].astype(BF16), vs[h], preferred_element_type=F32) for h in heads]
    states = [s_scr[h] for h in heads]
    outs = [outs[h] + jnp.exp(lgs[h] * (idx + 1.0))
            * jnp.dot(qs[h], states[h].astype(BF16), preferred_element_type=F32) for h in heads]
    kds = [(ks[h].astype(F32) * jnp.exp(lgs[h] * (rows - 1.0 - idx))).astype(BF16) for h in heads]
    for h in heads:
        s_new = (jnp.exp(lgs[h] * rows) * states[h]
                 + lax.dot_general(kds[h], vs[h], _TN, preferred_element_type=F32))
        s_scr[h] = s_new
        sout_ref[h] = s_new
    for h in heads:
        mu = jnp.mean(outs[h], axis=-1, keepdims=True)
        cen = outs[h] - mu
        var = jnp.mean(cen * cen, axis=-1, keepdims=True)
        yr = cen * lax.rsqrt(var + EPS) * gn_ref[h]
        g = g_ref[:, cols(h)]
        o_ref[:, cols(h)] = (g * _sigmoid(g) * yr).astype(o_ref.dtype)


def _into(out_buf):
    if out_buf is None:
        return [], []
    return [out_buf], [pl.BlockSpec(memory_space=pl.ANY)]


def _retention(log_gamma, q, k, v, gate, gn, s0, row_off, n_batch, n_steps, rows, out_buf=None):
    heads = q.shape[1] // RET_HEAD_DIM
    nh = min(RET_HEADS_PER_STEP, heads)
    assert row_off % rows == 0 and rows % CHUNK == 0 and heads % nh == 0
    ro = row_off // rows
    tok = lambda b, h, c, lg: (ro + b * n_steps + c, h)
    blk = pl.BlockSpec((rows, nh * RET_HEAD_DIM), tok)
    st = pl.BlockSpec((None, nh, RET_HEAD_DIM, RET_HEAD_DIM), lambda b, h, c, lg: (b, h, 0, 0))
    args = [log_gamma, q, k, v, gate, gn.reshape(heads, 1, RET_HEAD_DIM), s0]
    extra_args, extra_specs = _into(out_buf)
    grid_spec = pltpu.PrefetchScalarGridSpec(
        num_scalar_prefetch=1,
        grid=(n_batch, heads // nh, n_steps),
        in_specs=[blk, blk, blk, blk,
                  pl.BlockSpec((nh, 1, RET_HEAD_DIM), lambda b, h, c, lg: (h, 0, 0)), st] + extra_specs,
        out_specs=[blk, st],
        scratch_shapes=[pltpu.VMEM((nh, RET_HEAD_DIM, RET_HEAD_DIM), F32), pltpu.VMEM((nh, rows, rows), F32)],
    )
    return pl.pallas_call(
        functools.partial(_retention_body, rows=rows, n_heads=nh),
        out_shape=[jax.ShapeDtypeStruct((q.shape[0], heads * RET_HEAD_DIM), BF16),
                   jax.ShapeDtypeStruct((n_batch, heads, RET_HEAD_DIM, RET_HEAD_DIM), F32)],
        grid_spec=grid_spec,
        input_output_aliases={len(args): 0} if extra_args else {},
        compiler_params=_params("arbitrary", "arbitrary", "arbitrary"),
        name="retention",
    )(*args, *extra_args)


def _neg_suffix(n):
    j = lax.broadcasted_iota(jnp.int32, (2 * n, n), 0) & (n - 1)
    s = lax.broadcasted_iota(jnp.int32, (2 * n, n), 1)
    return jnp.where(j >= s, -1.0, 0.0).astype(BF16)


def _lanes_to(x, n):
    return x[:, :n] if n <= LANES else jnp.concatenate([x] * (n // LANES), axis=1)


def _suffix_sums(fail, neg_ones):
    tk = fail.shape[1]
    hi, lo = _split_bf16(fail)
    if tk % LANES == 0:
        return jnp.dot(jnp.concatenate([hi, lo], axis=1), neg_ones, preferred_element_type=F32)
    return (jnp.dot(hi, neg_ones[:tk], preferred_element_type=F32)
            + jnp.dot(lo, neg_ones[:tk], preferred_element_type=F32))


def _sb_tiles(qs, kbs, vbs, neg_ones, carry, acc, masked):
    heads = range(len(qs))
    tk = kbs[0].shape[0]
    zs = [lax.dot_general(qs[h], kbs[h], _NT, preferred_element_type=F32) for h in heads]
    fails = [jnp.maximum(z, 0.0) + jnp.log2(1.0 + jnp.exp2(-jnp.abs(z))) for z in zs]
    if masked:
        t = lax.broadcasted_iota(jnp.int32, zs[0].shape, 0)
        s = lax.broadcasted_iota(jnp.int32, zs[0].shape, 1)
        valid = s < t
        fails = [jnp.where(valid, f, 0.0) for f in fails]
    suffixes = [_suffix_sums(f, neg_ones) for f in fails]
    probs = [jnp.exp2(zs[h] + suffixes[h] + _lanes_to(carry[h], tk)) for h in heads]
    if masked:
        probs = [jnp.where(valid, p, 0.0) for p in probs]
    for h in heads:
        acc[h] += jnp.dot(probs[h].astype(BF16), vbs[h], preferred_element_type=F32)
        carry[h] += jnp.broadcast_to(suffixes[h][:, :1], carry.shape[1:])


def _sb_body(q_ref, kd_ref, vd_ref, kp_ref, vp_ref, *rest, tk, n_heads, past_per_step, past_static):
    o_ref, carry, acc = rest[-3:]
    tq = q_ref.shape[0]
    n_past = past_static if past_static is not None else pl.program_id(1) * past_per_step
    cols = lambda h: slice(h * SB_HEAD_DIM, (h + 1) * SB_HEAD_DIM)
    heads = range(n_heads)
    past = (lambda ref, h, rows: ref[rows, h, :]) if len(kp_ref.shape) == 3 else (
        lambda ref, h, rows: ref[rows, cols(h)])
    carry[...] = jnp.zeros(carry.shape, F32)
    acc[...] = jnp.zeros(acc.shape, F32)
    queries = lambda: [q_ref[:, cols(h)] for h in heads]
    _sb_tiles(queries(), [kd_ref[:, cols(h)].astype(BF16) for h in heads],
              [vd_ref[:, cols(h)].astype(BF16) for h in heads], _neg_suffix(tq), carry, acc, True)
    neg_ones = _neg_suffix(tk)

    alive = lambda: jnp.max(carry[...]) > SB_DEAD_LOG2

    def step(state):
        it = state[0]
        rows = pl.ds(pl.multiple_of((n_past - 1 - it) * tk, tk), tk)
        _sb_tiles(queries(), [past(kp_ref, h, rows).astype(BF16) for h in heads],
                  [past(vp_ref, h, rows).astype(BF16) for h in heads], neg_ones, carry, acc, False)
        return it + 1, alive()

    lax.while_loop(lambda state: (state[0] < n_past) & state[1], step, (jnp.int32(0), alive()))
    for h in range(n_heads):
        o_ref[:, cols(h)] = acc[h].astype(o_ref.dtype)


def _sb_scratch(n_heads, tq):
    return [pltpu.VMEM((n_heads, tq, LANES), F32), pltpu.VMEM((n_heads, tq, SB_HEAD_DIM), F32)]


def _sb_prompt(q, k, v, n_tok):
    heads = q.shape[1] // SB_HEAD_DIM
    tq = _tile(n_tok, SB_BLOCK)
    nh = SB_HEADS_PER_STEP
    width = nh * SB_HEAD_DIM
    blk = pl.BlockSpec((tq, width), lambda h, i: (i, h))
    full = pl.BlockSpec((n_tok, width), lambda h, i: (0, h))
    return pl.pallas_call(
        functools.partial(_sb_body, tk=tq, n_heads=nh, past_per_step=1, past_static=None),
        out_shape=jax.ShapeDtypeStruct((q.shape[0], heads * SB_HEAD_DIM), BF16),
        grid=(heads // nh, n_tok // tq),
        in_specs=[blk, blk, blk, full, full],
        out_specs=blk,
        scratch_shapes=_sb_scratch(nh, tq),
        compiler_params=_params("parallel", "arbitrary"),
        name="stick_breaking_prompt",
    )(q, k, v, k, v)


def _sb_sample(q, k, v, cache_k, cache_v, row_off, n_new, out_buf):
    heads = q.shape[1] // SB_HEAD_DIM
    _, n_batch, past, _, _ = cache_k.shape
    tk = _tile(past, SB_BLOCK)
    assert row_off % n_new == 0
    ro = row_off // n_new
    nh = min(F32_SUBLANES, heads)
    assert heads % nh == 0 and (nh == F32_SUBLANES or nh == heads)
    blk = pl.BlockSpec((n_new, nh * SB_HEAD_DIM), lambda b, h: (ro + b, h))
    full = pl.BlockSpec((None, None, past, nh, SB_HEAD_DIM), lambda b, h: (0, b, 0, h, 0))
    extra_args, extra_specs = _into(out_buf)
    return pl.pallas_call(
        functools.partial(_sb_body, tk=tk, n_heads=nh, past_per_step=0, past_static=past // tk),
        out_shape=jax.ShapeDtypeStruct(out_buf.shape, out_buf.dtype),
        grid=(n_batch, heads // nh),
        in_specs=[blk, blk, blk, full, full] + extra_specs,
        out_specs=blk,
        scratch_shapes=_sb_scratch(nh, n_new),
        input_output_aliases={5: 0},
        compiler_params=_params("parallel", "arbitrary"),
        name="stick_breaking_sample",
    )(q, k, v, cache_k, cache_v, *extra_args)


def _cross_body(q_ref, k_ref, v_ref, *rest, scale, steps):
    o_ref, kb, vb = rest[-3:]
    dh = q_ref.shape[1] // MEM_HEADS
    heads = range(MEM_HEADS)
    cols = lambda h: slice(h * dh, (h + 1) * dh)
    head_of = (lambda ref, h: ref[:, h, :]) if len(k_ref.shape) == 3 else (lambda ref, h: ref[:, cols(h)])

    @pl.when(pl.program_id(0) % steps == 0)
    def _():
        for h in heads:
            kb[:, cols(h)] = head_of(k_ref, h).astype(BF16)
            vb[:, cols(h)] = head_of(v_ref, h).astype(BF16)

    scores = [lax.dot_general(q_ref[:, cols(h)], kb[:, cols(h)], _NT, preferred_element_type=F32) * scale
              for h in heads]
    exps = [jnp.exp(s - jnp.max(s, axis=-1, keepdims=True)) for s in scores]
    probs = [e / jnp.sum(e, axis=-1, keepdims=True) for e in exps]
    for h in heads:
        o_ref[:, cols(h)] = jnp.dot(probs[h].astype(BF16), vb[:, cols(h)],
                                    preferred_element_type=F32).astype(o_ref.dtype)


def _cross_attend(q, mem_k, mem_v, row_off, rows_per_seq, out_buf=None):
    d = q.shape[1]
    dh = d // MEM_HEADS
    tm = _tile(rows_per_seq, ROW_TILE)
    steps = rows_per_seq // tm
    assert row_off % tm == 0
    ro = row_off // tm
    extra_args, extra_specs = _into(out_buf)
    if mem_k.ndim == 3:
        n_seq, n_mem, _ = mem_k.shape
        mem = pl.BlockSpec((None, n_mem, d), lambda i: (i // steps, 0, 0))
    else:
        _, n_seq, n_mem, _, _ = mem_k.shape
        mem = pl.BlockSpec((None, None, n_mem, MEM_HEADS, dh), lambda i: (0, i // steps, 0, 0, 0))
    blk = pl.BlockSpec((tm, d), lambda i: (ro + i, 0))
    return pl.pallas_call(
        functools.partial(_cross_body, scale=dh ** -0.5, steps=steps),
        out_shape=jax.ShapeDtypeStruct((q.shape[0], d), BF16),
        grid=(n_seq * steps,),
        in_specs=[blk, mem, mem] + extra_specs,
        out_specs=blk,
        scratch_shapes=[pltpu.VMEM((n_mem, d), BF16), pltpu.VMEM((n_mem, d), BF16)],
        input_output_aliases={3: 0} if extra_args else {},
        compiler_params=_params("arbitrary"),
        name="cross_attention",
    )(q, mem_k, mem_v, *extra_args)


def _split_bf16(x):
    hi = x.astype(BF16)
    return hi, (x - hi.astype(F32)).astype(BF16)


def _router_body(x_ref, g_ref, wr_ref, br_ref, h_ref, ids_ref, wts_ref):
    h = _rmsnorm_rows(x_ref[...], g_ref[...])
    h_ref[...] = _pack_bf16_pairs(h)
    h_hi, h_lo = _split_bf16(h)
    w_hi, w_lo = _split_bf16(wr_ref[...])
    logits = (jnp.dot(h_hi, w_hi, preferred_element_type=F32)
              + jnp.dot(h_hi, w_lo, preferred_element_type=F32)
              + jnp.dot(h_lo, w_hi, preferred_element_type=F32)) + br_ref[...]
    lane = lax.broadcasted_iota(jnp.int32, logits.shape, 1).astype(F32)
    big = float(LANES)
    neg = -jnp.inf

    def first_max(vals):
        top = jnp.max(vals, axis=-1, keepdims=True)
        return top, jnp.min(jnp.where(vals == top, lane, big), axis=-1, keepdims=True)

    is_group = lane < N_GROUPS
    g_max, g_top = first_max(jnp.where(is_group, logits, neg))
    g_w = 1.0 / jnp.sum(jnp.where(is_group, jnp.exp(logits - g_max), 0.0), axis=-1, keepdims=True)
    first = N_GROUPS + EXPERTS_PER_GROUP * g_top
    cand = jnp.where((lane >= first) & (lane < first + EXPERTS_PER_GROUP), logits, neg)
    v1, i1 = first_max(cand)
    v2, i2 = first_max(jnp.where(lane == i1, neg, cand))
    t = jnp.exp(v2 - v1)
    w1 = g_w / (1.0 + t)
    w2 = g_w * t / (1.0 + t)
    ids = jnp.where(lane == 0.0, i1 - N_GROUPS, jnp.where(lane == 1.0, i2 - N_GROUPS, 0.0))
    ids_ref[...] = ids.astype(jnp.int32)
    wts_ref[...] = jnp.where(lane == 0.0, w1, jnp.where(lane == 1.0, w2, 0.0))


def _norm_and_route(x, g, wg, bg, we, be):
    m, d = x.shape
    tm = _tile(m, 256)
    pad = LANES - N_GROUPS - N_EXPERTS
    wr = jnp.concatenate([wg, we, jnp.zeros((d, pad), F32)], axis=1)
    br = jnp.concatenate([bg, be, jnp.zeros((pad,), F32)]).reshape(1, LANES)
    row = lambda c: pl.BlockSpec((tm, c), lambda i: (i, 0))
    return pl.pallas_call(
        _router_body,
        out_shape=[jax.ShapeDtypeStruct((m, d // 2), jnp.int32),
                   jax.ShapeDtypeStruct((m, LANES), jnp.int32),
                   jax.ShapeDtypeStruct((m, LANES), F32)],
        grid=(m // tm,),
        in_specs=[row(d), pl.BlockSpec((1, d), lambda i: (0, 0)),
                  pl.BlockSpec((d, LANES), lambda i: (0, 0)), pl.BlockSpec((1, LANES), lambda i: (0, 0))],
        out_specs=[row(d // 2), row(LANES), row(LANES)],
        compiler_params=_params("parallel"),
        name="norm_route",
    )(x, g.reshape(1, d), wr, br)


GATHER_UNROLL = 8


def _start_row_gather(idx_ref, first, stride, src_ref, buf, sem, n_rows):
    def one(r, _):
        t = idx_ref[first + r * stride]
        pltpu.make_async_copy(src_ref.at[pl.ds(t, 1)], buf.at[pl.ds(r, 1)], sem).start()
        return 0

    def group(g, _):
        for u in range(GATHER_UNROLL):
            one(g * GATHER_UNROLL + u, 0)
        return 0

    if isinstance(n_rows, int):
        for r in range(n_rows):
            one(r, 0)
        return
    groups = n_rows // GATHER_UNROLL
    lax.fori_loop(0, groups, group, 0)
    lax.fori_loop(groups * GATHER_UNROLL, n_rows, one, 0)


def _wait_row_gather(src_ref, buf, sem, n_rows):
    pltpu.make_async_copy(src_ref.at[pl.ds(0, n_rows)], buf.at[pl.ds(0, n_rows)], sem).wait()


def _pack_bf16_pairs(x):
    half = x.shape[1] // 2
    hi = lax.bitcast_convert_type(x[:, :half].astype(BF16).astype(F32), jnp.int32)
    lo = lax.bitcast_convert_type(x[:, half:].astype(BF16).astype(F32), jnp.int32)
    return hi | lax.shift_right_logical(lo, jnp.int32(16))


def _unpack_bf16_pairs(p):
    hi = lax.bitcast_convert_type(p & jnp.int32(-65536), F32)
    lo = lax.bitcast_convert_type(lax.shift_left(p, jnp.int32(16)), F32)
    return hi, lo


def _dispatch_body(idx_ref, used_ref, cnt_ref, src_ref, o_ref, buf, sem):
    b = pl.program_id(0)
    rows = buf.shape[1]

    copied = lambda blk: pl.multiple_of(pl.cdiv(cnt_ref[blk], GATHER_UNROLL) * GATHER_UNROLL, GATHER_UNROLL)

    def start(blk):
        _start_row_gather(idx_ref, blk * rows, 1, src_ref, buf.at[blk % 2], sem.at[blk % 2], copied(blk))

    @pl.when(b == 0)
    def _():
        buf[...] = jnp.zeros(buf.shape, buf.dtype)
        start(0)

    @pl.when(b + 1 < used_ref[0])
    def _():
        start(b + 1)

    @pl.when(b < used_ref[0])
    def _():
        slot = b % 2
        cnt = cnt_ref[b]
        _wait_row_gather(src_ref, buf.at[slot], sem.at[slot], copied(b))
        half = buf.shape[2]
        row = lax.broadcasted_iota(jnp.int32, (rows, 1), 0)
        hi, lo = _unpack_bf16_pairs(jnp.where(row < cnt, buf[slot], 0))
        o_ref[:, :half] = hi.astype(o_ref.dtype)
        o_ref[:, half:] = lo.astype(o_ref.dtype)


def _dispatch_rows(src, row_tok, blk_cnt, n_used):
    n_rows = row_tok.shape[0]
    half = src.shape[1]
    grid_spec = pltpu.PrefetchScalarGridSpec(
        num_scalar_prefetch=3,
        grid=(n_rows // MOE_ROWS,),
        in_specs=[pl.BlockSpec(memory_space=pl.ANY)],
        out_specs=pl.BlockSpec((MOE_ROWS, 2 * half), lambda b, idx, used, cnt: (b, 0)),
        scratch_shapes=[pltpu.VMEM((2, MOE_ROWS, half), src.dtype), pltpu.SemaphoreType.DMA((2,))],
    )
    return pl.pallas_call(
        _dispatch_body,
        out_shape=jax.ShapeDtypeStruct((n_rows, 2 * half), BF16),
        grid_spec=grid_spec,
        compiler_params=_params("arbitrary"),
        name="dispatch_rows",
    )(row_tok, n_used, blk_cnt, src)


def _expert_up_body(be_ref, used_ref, x_ref, w1_ref, w3_ref, h_ref):
    @pl.when(pl.program_id(1) < used_ref[0])
    def _():
        x = x_ref[...]
        a = jnp.dot(x, w1_ref[...].astype(BF16), preferred_element_type=F32)
        g = jnp.dot(x, w3_ref[...].astype(BF16), preferred_element_type=F32)
        h_ref[...] = (a * _sigmoid(a) * g).astype(h_ref.dtype)


def _expert_down_body(be_ref, used_ref, h_ref, w2_ref, y_ref):
    @pl.when(pl.program_id(1) < used_ref[0])
    def _():
        y_ref[...] = _pack_bf16_pairs(jnp.dot(h_ref[...], w2_ref[...].astype(BF16),
                                              preferred_element_type=F32))


def _expert_ffn(xs, blk_exp, n_used, w1, w3, w2):
    n_rows, d = xs.shape
    f = w1.shape[2]
    n_blocks = n_rows // MOE_ROWS
    tf = _tile(f, EXPERT_HIDDEN_TILE)
    tn = _tile(d, EXPERT_OUT_TILE)
    last = lambda b, used: jnp.minimum(b, used[0] - 1)
    up_spec = pltpu.PrefetchScalarGridSpec(
        num_scalar_prefetch=2,
        grid=(f // tf, n_blocks),
        in_specs=[pl.BlockSpec((MOE_ROWS, d), lambda j, b, be, us: (last(b, us), 0)),
                  pl.BlockSpec((None, d, tf), lambda j, b, be, us: (be[b], 0, j)),
                  pl.BlockSpec((None, d, tf), lambda j, b, be, us: (be[b], 0, j))],
        out_specs=pl.BlockSpec((MOE_ROWS, tf), lambda j, b, be, us: (b, j)),
    )
    hidden = pl.pallas_call(
        _expert_up_body,
        out_shape=jax.ShapeDtypeStruct((n_rows, f), BF16),
        grid_spec=up_spec,
        compiler_params=_params("arbitrary", "arbitrary"),
        name="expert_up",
    )(blk_exp, n_used, xs, w1, w3)
    down_spec = pltpu.PrefetchScalarGridSpec(
        num_scalar_prefetch=2,
        grid=(d // tn, n_blocks),
        in_specs=[pl.BlockSpec((MOE_ROWS, f), lambda j, b, be, us: (last(b, us), 0)),
                  pl.BlockSpec((None, f, tn), lambda j, b, be, us: (be[b], 0, j))],
        out_specs=pl.BlockSpec((MOE_ROWS, tn // 2), lambda j, b, be, us: (b, j)),
    )
    return pl.pallas_call(
        _expert_down_body,
        out_shape=jax.ShapeDtypeStruct((n_rows, d // 2), jnp.int32),
        grid_spec=down_spec,
        compiler_params=_params("arbitrary", "arbitrary"),
        name="expert_down",
    )(blk_exp, n_used, hidden, w2)


def _route_layout(ids, n_tok):
    n_assign = n_tok * TOP_K
    flat_e = ids.reshape(n_assign)
    flat_t = jnp.repeat(jnp.arange(n_tok, dtype=jnp.int32), TOP_K)
    onehot = (flat_e[:, None] == jnp.arange(N_EXPERTS, dtype=jnp.int32)[None, :]).astype(jnp.int32)
    running = jnp.cumsum(onehot, axis=0)
    rank = jnp.take_along_axis(running, flat_e[:, None], axis=1)[:, 0] - 1
    counts = running[-1]
    padded = (counts + MOE_ROWS - 1) // MOE_ROWS * MOE_ROWS
    pad_end = jnp.cumsum(padded)
    dest = (pad_end - padded)[flat_e] + rank
    n_blocks = -(-n_assign // MOE_ROWS) + N_EXPERTS
    n_rows = n_blocks * MOE_ROWS
    row_tok = jnp.zeros((n_rows,), jnp.int32).at[dest].set(flat_t)
    blk = jnp.arange(n_blocks, dtype=jnp.int32)
    n_used = (pad_end[-1] // MOE_ROWS).astype(jnp.int32)
    blk_exp = jnp.searchsorted(pad_end, jnp.minimum(blk, n_used - 1) * MOE_ROWS, side='right').astype(jnp.int32)
    blk_exp = jnp.minimum(blk_exp, N_EXPERTS - 1)
    into_range = blk * MOE_ROWS - (pad_end - padded)[blk_exp]
    blk_cnt = jnp.where(blk < n_used, jnp.clip(counts[blk_exp] - into_range, 0, MOE_ROWS), 0).astype(jnp.int32)
    return row_tok, blk_cnt, dest.astype(jnp.int32), blk_exp, n_used.reshape(1)


def _final_body(dest_ref, x_ref, w_ref, y_ref, g_ref, op_ref, os_ref, buf, sem, *, prompt_tiles, pack_tile):
    i = pl.program_id(0)
    tm = buf.shape[2]

    def start(blk):
        for k in range(TOP_K):
            _start_row_gather(dest_ref, blk * tm * TOP_K + k, TOP_K, y_ref, buf.at[blk % 2, k], sem.at[blk % 2, k],
                              tm)

    @pl.when(i == 0)
    def _():
        start(0)

    @pl.when(i + 1 < pl.num_programs(0))
    def _():
        start(i + 1)

    slot = i % 2
    for k in range(TOP_K):
        _wait_row_gather(y_ref, buf.at[slot, k], sem.at[slot, k], tm)
    w = w_ref[...]
    pieces = []
    for j in range(buf.shape[3] // pack_tile):
        words = slice(j * pack_tile, (j + 1) * pack_tile)
        hi0, lo0 = _unpack_bf16_pairs(buf[slot, 0, :, words])
        hi1, lo1 = _unpack_bf16_pairs(buf[slot, 1, :, words])
        pieces += [hi0 * w[:, 0:1] + hi1 * w[:, 1:2], lo0 * w[:, 0:1] + lo1 * w[:, 1:2]]
    y = _rmsnorm_rows(x_ref[...] + jnp.concatenate(pieces, axis=1), g_ref[...])

    @pl.when(i < prompt_tiles)
    def _():
        op_ref[...] = y

    @pl.when(i >= prompt_tiles)
    def _():
        os_ref[...] = y


def _combine_and_norm(x, wts, y_rows, dest, g, n_prompt):
    m, d = x.shape
    tm = _tile(m - n_prompt, _tile(n_prompt, 256))
    assert n_prompt % tm == 0
    pt = n_prompt // tm
    grid_spec = pltpu.PrefetchScalarGridSpec(
        num_scalar_prefetch=1,
        grid=(m // tm,),
        in_specs=[pl.BlockSpec((tm, d), lambda i, dest: (i, 0)),
                  pl.BlockSpec((tm, LANES), lambda i, dest: (i, 0)),
                  pl.BlockSpec(memory_space=pl.ANY),
                  pl.BlockSpec((1, d), lambda i, dest: (0, 0))],
        out_specs=[pl.BlockSpec((tm, d), lambda i, dest: (jnp.minimum(i, pt - 1), 0)),
                   pl.BlockSpec((tm, d), lambda i, dest: (jnp.maximum(i - pt, 0), 0))],
        scratch_shapes=[pltpu.VMEM((2, TOP_K, tm, d // 2), y_rows.dtype), pltpu.SemaphoreType.DMA((2, TOP_K))],
    )
    return pl.pallas_call(
        functools.partial(_final_body, prompt_tiles=pt, pack_tile=_tile(d, EXPERT_OUT_TILE) // 2),
        out_shape=[jax.ShapeDtypeStruct((n_prompt, d), F32), jax.ShapeDtypeStruct((m - n_prompt, d), F32)],
        grid_spec=grid_spec,
        compiler_params=_params("arbitrary"),
        name="combine_norm",
    )(dest, x, wts, y_rows, g.reshape(1, d))


def _layer(xp, xs, n_seq, n_new, past_len, cache_sb_k, cache_sb_v, state_ret, cache_mem_k, cache_mem_v,
           mem_prompt, norm_mix, w_in, ret_gn, w_out, norm_cross, norm_mem, w_cq, w_ck, w_cv, w_co, norm_ffn,
           wg, bg, we, be, w1, w3, w2):
    n_prompt, d = xp.shape
    n_tok = n_prompt + xs.shape[0]
    d_ret = d // 2
    d_sb = d - d_ret
    ret_heads = d_ret // RET_HEAD_DIM
    sds = jax.ShapeDtypeStruct

    pos = np.concatenate([np.arange(n_prompt), np.tile(past_len + np.arange(n_new), n_seq)])
    half = RET_HEAD_DIM // 2
    inv = (ROPE_BASE ** (-np.arange(half, dtype=np.float32) / half)).astype(np.float32)
    ang = pos.astype(np.float32)[:, None] * inv[None, :]
    cos, sin = jnp.asarray(np.cos(ang), F32), jnp.asarray(np.sin(ang), F32)
    log_gamma = jnp.log1p(-jnp.exp2(-5.0 - jnp.arange(ret_heads, dtype=F32)))

    h = _rmsnorm_stacked(xp, xs, norm_mix, BF16)
    rot = [(cos, half), (sin, half)]
    bf = lambda n: [(sds((n_tok, n), BF16), None)]
    (rq,) = _matmul([h], w_in, 0, d_ret, _ep_rotary(1.0), rot, bf(d_ret), "in_proj_ret_q")
    (rk,) = _matmul([h], w_in, d_ret, d_ret, _ep_rotary(RET_HEAD_DIM ** -0.5), rot, bf(d_ret), "in_proj_ret_k")
    (rv,) = _matmul([h], w_in, 2 * d_ret, d_ret, _ep_store(1.0), [], bf(d_ret), "in_proj_ret_v")
    (rg,) = _matmul([h], w_in, 3 * d_ret, d_ret, _ep_store(1.0), [], [(sds((n_tok, d_ret), F32), None)],
                    "in_proj_ret_gate")
    (sq,) = _matmul([h], w_in, 4 * d_ret, d_sb, _ep_store(SB_HEAD_DIM ** -0.5 * LOG2_E), [], bf(d_sb),
                    "in_proj_sb_q")
    split = [(sds((n_prompt, d_sb), F32), PROMPT), (sds((n_tok - n_prompt, d_sb), F32), SAMPLE),
             (sds((n_tok, d_sb), BF16), None)]
    sk_p, sk_s, sk_b = _matmul([h], w_in, 4 * d_ret + d_sb, d_sb, _ep_store_split, [], split, "in_proj_sb_k",
                               n_prompt=n_prompt)
    sv_p, sv_s, sv_b = _matmul([h], w_in, 4 * d_ret + 2 * d_sb, d_sb, _ep_store_split, [], split,
                               "in_proj_sb_v", n_prompt=n_prompt)

    rows_p = _tile(n_prompt, RET_BLOCK)
    zeros_state = jnp.zeros((1, ret_heads, RET_HEAD_DIM, RET_HEAD_DIM), F32)
    yr, state_p = _retention(log_gamma, rq, rk, rv, rg, ret_gn, zeros_state, 0, 1, n_prompt // rows_p, rows_p)
    yr, state_s = _retention(log_gamma, rq, rk, rv, rg, ret_gn, state_ret, n_prompt, n_seq, 1, n_new,
                             out_buf=yr)
    sb = _sb_prompt(sq, sk_b, sv_b, n_prompt)
    sb = _sb_sample(sq, sk_b, sv_b, cache_sb_k, cache_sb_v, n_prompt, n_new, sb)
    res = [(sds((n_tok, d), F32), None)]
    (x,) = _matmul([yr, sb], w_out, 0, d, _ep_residual_split, [(xp, PROMPT), (xs, SAMPLE)], res, "out_proj",
                   n_prompt=n_prompt)

    n_mem = mem_prompt.shape[0]
    m = _rmsnorm(mem_prompt, norm_mem, BF16)
    mem_out = [(sds((n_mem, d), F32), None)]
    (mk,) = _matmul([m], w_ck, 0, d, _ep_store(1.0), [], mem_out, "mem_k")
    (mv,) = _matmul([m], w_cv, 0, d, _ep_store(1.0), [], mem_out, "mem_v")
    h = _rmsnorm(x, norm_cross, BF16)
    (cq,) = _matmul([h], w_cq, 0, d, _ep_store(1.0), [], bf(d), "cross_q")
    co = _cross_attend(cq, mk.reshape(1, n_mem, d), mv.reshape(1, n_mem, d), 0, n_prompt)
    co = _cross_attend(cq, cache_mem_k, cache_mem_v, n_prompt, n_new, out_buf=co)
    (x,) = _matmul([co], w_co, 0, d, _ep_residual, [(x, None)], res, "cross_out")

    hn, ids, wts = _norm_and_route(x, norm_ffn, wg, bg, we, be)
    row_tok, blk_cnt, dest, blk_exp, n_used = _route_layout(ids[:, :TOP_K], n_tok)
    x_rows = _dispatch_rows(hn, row_tok, blk_cnt, n_used)
    y_rows = _expert_ffn(x_rows, blk_exp, n_used, w1, w3, w2)
    return x, wts, y_rows, dest, (sk_p, sv_p, sk_s, sv_s, state_p, state_s, mk, mv)


def kernel(x_prompt, x_sample, cache_sb_k, cache_sb_v, state_ret, cache_mem_k, cache_mem_v, mem_prompt, norm_mix, w_in, ret_gn, w_out, norm_cross, norm_mem, w_cq, w_ck, w_cv, w_co, norm_ffn, router_group_w, router_group_b, router_expert_w, router_expert_b, expert_w1, expert_w3, expert_w2, norm_final):
    batch, n_prompt, d = x_prompt.shape
    n_seq, n_new, _ = x_sample.shape
    depth, _, past_len, sb_heads, _ = cache_sb_k.shape
    n_mem = mem_prompt.shape[1]
    assert batch == 1 and depth == 1
    ret_heads = (d // 2) // RET_HEAD_DIM
    x, wts, y_rows, dest, (sk_p, sv_p, sk_s, sv_s, state_p, state_s, mk, mv) = _layer(
        x_prompt.reshape(n_prompt, d), x_sample.reshape(n_seq * n_new, d), n_seq, n_new, past_len,
        cache_sb_k, cache_sb_v, state_ret[0], cache_mem_k, cache_mem_v, mem_prompt[0], norm_mix[0], w_in,
        ret_gn[0], w_out, norm_cross[0], norm_mem[0], w_cq, w_ck, w_cv, w_co, norm_ffn[0], router_group_w[0],
        router_group_b[0], router_expert_w[0], router_expert_b[0], expert_w1[0], expert_w3[0], expert_w2[0])
    y_prompt, y_sample = _combine_and_norm(x, wts, y_rows, dest, norm_final, n_prompt)
    y_prompt = y_prompt.reshape(1, n_prompt, d)
    y_sample = y_sample.reshape(n_seq, n_new, d)
    sb_shape_p = (1, 1, n_prompt, sb_heads, SB_HEAD_DIM)
    sb_shape_s = (1, n_seq, n_new, sb_heads, SB_HEAD_DIM)
    mem_shape = (1, 1, n_mem, MEM_HEADS, d // MEM_HEADS)
    return (y_prompt, y_sample,
            sk_p.reshape(sb_shape_p), sv_p.reshape(sb_shape_p),
            state_p.reshape(1, 1, ret_heads, RET_HEAD_DIM, RET_HEAD_DIM),
            mk.reshape(mem_shape), mv.reshape(mem_shape),
            sk_s.reshape(sb_shape_s), sv_s.reshape(sb_shape_s),
            state_s.reshape(1, n_seq, ret_heads, RET_HEAD_DIM, RET_HEAD_DIM))
```

```python
import functools

import jax
import jax.numpy as jnp
import numpy as np
from jax import lax
from jax.experimental import pallas as pl
from jax.experimental.pallas import tpu as pltpu

BF16 = jnp.bfloat16
F32 = jnp.float32

EPS = 1e-6
CHUNK = 64
RET_HEAD_DIM = 256
SB_HEAD_DIM = 128
MEM_HEADS = 4
N_GROUPS = 4
EXPERTS_PER_GROUP = 8
N_EXPERTS = N_GROUPS * EXPERTS_PER_GROUP
TOP_K = 2
ROPE_BASE = 10000.0

LANES = 128
F32_SUBLANES = 8
VMEM_LIMIT_BYTES = 60 * 1024 * 1024
MATMUL_VMEM_BUDGET = 56 * 1024 * 1024
ROW_TILE = 512
NORM_ROWS = 512
COL_TILE = 1024
EXPERT_HIDDEN_TILE = 512
RET_BLOCK = 256
RET_HEADS_PER_STEP = 4
SB_BLOCK = 256
SB_HEADS_PER_STEP = 4
LOG2_E = 1.4426950408889634
SB_DEAD_LOG2 = -64.0
MOE_ROWS = 512
EXPERT_OUT_TILE = 4096

PROMPT = "prompt"
SAMPLE = "sample"

_NT = (((1,), (1,)), ((), ()))
_TN = (((0,), (0,)), ((), ()))


def _tile(n, pref):
    t = min(pref, n)
    while n % t:
        t //= 2
    return t


def _params(*sem):
    return pltpu.CompilerParams(dimension_semantics=sem, vmem_limit_bytes=VMEM_LIMIT_BYTES)


def _sigmoid(x):
    return 1.0 / (1.0 + jnp.exp(-x))


def _rmsnorm_rows(x, g):
    ms = jnp.mean(x * x, axis=-1, keepdims=True)
    return x * lax.rsqrt(ms + EPS) * g


def _rmsnorm_body(x_ref, g_ref, o_ref):
    o_ref[...] = _rmsnorm_rows(x_ref[...], g_ref[...]).astype(o_ref.dtype)


def _rmsnorm(x, g, out_dtype):
    m, d = x.shape
    tm = _tile(m, NORM_ROWS)
    return pl.pallas_call(
        _rmsnorm_body,
        out_shape=jax.ShapeDtypeStruct((m, d), out_dtype),
        grid=(m // tm,),
        in_specs=[pl.BlockSpec((tm, d), lambda i: (i, 0)), pl.BlockSpec((1, d), lambda i: (0, 0))],
        out_specs=pl.BlockSpec((tm, d), lambda i: (i, 0)),
        compiler_params=_params("parallel"),
        name="rmsnorm",
    )(x, g.reshape(1, d))


def _rmsnorm_stacked_body(xp_ref, xs_ref, g_ref, o_ref, *, prompt_tiles):
    is_prompt = pl.program_id(0) < prompt_tiles

    @pl.when(is_prompt)
    def _():
        o_ref[...] = _rmsnorm_rows(xp_ref[...], g_ref[...]).astype(o_ref.dtype)

    @pl.when(jnp.logical_not(is_prompt))
    def _():
        o_ref[...] = _rmsnorm_rows(xs_ref[...], g_ref[...]).astype(o_ref.dtype)


def _rmsnorm_stacked(xp, xs, g, out_dtype):
    (n_p, d), n_s = xp.shape, xs.shape[0]
    tm = _tile(n_s, _tile(n_p, NORM_ROWS))
    pt = n_p // tm
    return pl.pallas_call(
        functools.partial(_rmsnorm_stacked_body, prompt_tiles=pt),
        out_shape=jax.ShapeDtypeStruct((n_p + n_s, d), out_dtype),
        grid=((n_p + n_s) // tm,),
        in_specs=[pl.BlockSpec((tm, d), lambda i: (jnp.minimum(i, pt - 1), 0)),
                  pl.BlockSpec((tm, d), lambda i: (jnp.maximum(i - pt, 0), 0)),
                  pl.BlockSpec((1, d), lambda i: (0, 0))],
        out_specs=pl.BlockSpec((tm, d), lambda i: (i, 0)),
        compiler_params=_params("arbitrary"),
        name="rmsnorm_stacked",
    )(xp, xs, g.reshape(1, d))


def _matmul_body(*refs, n_lhs, n_extra, n_out, epilogue):
    x_refs = refs[:n_lhs]
    w_refs = refs[n_lhs:2 * n_lhs]
    extra = refs[2 * n_lhs:2 * n_lhs + n_extra]
    outs = refs[2 * n_lhs + n_extra:2 * n_lhs + n_extra + n_out]
    wb = refs[2 * n_lhs + n_extra + n_out:]

    @pl.when(pl.program_id(1) == 0)
    def _():
        for p in range(n_lhs):
            wb[p][...] = w_refs[p][...].astype(BF16)

    acc = jnp.dot(x_refs[0][...], wb[0][...], preferred_element_type=F32)
    for p in range(1, n_lhs):
        acc += jnp.dot(x_refs[p][...], wb[p][...], preferred_element_type=F32)
    epilogue(acc, extra, outs)


def _matmul(xs, w, col_off, n_cols, epilogue, extras, outs, name, n_prompt=None):
    m = xs[0].shape[0]
    tm = _tile(m, ROW_TILE) if n_prompt is None else _tile(m - n_prompt, _tile(n_prompt, ROW_TILE))
    k_total = sum(x.shape[1] for x in xs)

    def vmem_bytes(tn):
        tiles = sum((tn if kind is None or isinstance(kind, str) else kind) * jnp.dtype(a.dtype).itemsize
                    for a, kind in list(extras) + list(outs))
        return k_total * tn * (2 * 4 + 2) + 2 * tm * k_total * 2 + 2 * tm * tiles

    tn = _tile(n_cols, COL_TILE)
    while vmem_bytes(tn) > MATMUL_VMEM_BUDGET and tn > LANES:
        tn //= 2
    assert col_off % tn == 0
    jo = col_off // tn
    pt = None if n_prompt is None else n_prompt // tm
    in_specs, args = [], []
    for x in xs:
        in_specs.append(pl.BlockSpec((tm, x.shape[1]), lambda j, i: (i, 0)))
        args.append(x)
    row = 0
    for x in xs:
        kp = x.shape[1]
        assert row % kp == 0
        in_specs.append(pl.BlockSpec((None, kp, tn), lambda j, i, r=row // kp: (0, r, j + jo)))
        args.append(w)
        row += kp
    assert row == w.shape[1]

    def spec(kind):
        if kind is None:
            return pl.BlockSpec((tm, tn), lambda j, i: (i, j))
        if kind == PROMPT:
            return pl.BlockSpec((tm, tn), lambda j, i: (jnp.minimum(i, pt - 1), j))
        if kind == SAMPLE:
            return pl.BlockSpec((tm, tn), lambda j, i: (jnp.maximum(i - pt, 0), j))
        return pl.BlockSpec((tm, kind), lambda j, i: (i, 0))

    for a, kind in extras:
        in_specs.append(spec(kind))
        args.append(a)
    body = functools.partial(_matmul_body, n_lhs=len(xs), n_extra=len(extras), n_out=len(outs),
                             epilogue=epilogue if n_prompt is None else functools.partial(epilogue, pt))
    res = pl.pallas_call(
        body,
        out_shape=[o for o, _ in outs],
        grid=(n_cols // tn, m // tm),
        in_specs=in_specs,
        out_specs=[spec(kind) for _, kind in outs],
        scratch_shapes=[pltpu.VMEM((x.shape[1], tn), BF16) for x in xs],
        compiler_params=_params("arbitrary", "arbitrary"),
        name=name,
    )(*args)
    return res


def _ep_store(scale):
    def ep(acc, extra, outs):
        for o in outs:
            o[...] = (acc * scale).astype(o.dtype) if scale != 1.0 else acc.astype(o.dtype)
    return ep


def _ep_store_split(prompt_tiles, acc, extra, outs):
    is_prompt = pl.program_id(1) < prompt_tiles

    @pl.when(is_prompt)
    def _():
        outs[0][...] = acc

    @pl.when(jnp.logical_not(is_prompt))
    def _():
        outs[1][...] = acc

    outs[2][...] = acc.astype(outs[2].dtype)


def _ep_residual(acc, extra, outs):
    outs[0][...] = extra[0][...] + acc


def _ep_residual_split(prompt_tiles, acc, extra, outs):
    res = jnp.where(pl.program_id(1) < prompt_tiles, extra[0][...], extra[1][...])
    outs[0][...] = res + acc


def _ep_rotary(scale):
    half = RET_HEAD_DIM // 2

    def ep(acc, extra, outs):
        cos = extra[0][...]
        sin = extra[1][...]
        for h in range(acc.shape[1] // RET_HEAD_DIM):
            lo = h * RET_HEAD_DIM
            x1 = acc[:, lo:lo + half]
            x2 = acc[:, lo + half:lo + RET_HEAD_DIM]
            outs[0][:, lo:lo + half] = ((x1 * cos - x2 * sin) * scale).astype(outs[0].dtype)
            outs[0][:, lo + half:lo + RET_HEAD_DIM] = ((x1 * sin + x2 * cos) * scale).astype(outs[0].dtype)
    return ep


def _retention_body(lg_ref, q_ref, k_ref, v_ref, g_ref, gn_ref, s0_ref, *rest, rows, n_heads):
    o_ref, sout_ref, s_scr, decay_scr = rest[-4:]
    heads = range(n_heads)
    cols = lambda h: slice(h * RET_HEAD_DIM, (h + 1) * RET_HEAD_DIM)
    lgs = [lg_ref[pl.program_id(1) * n_heads + h] for h in heads]

    @pl.when(pl.program_id(2) == 0)
    def _():
        s_scr[...] = s0_ref[...]
        ii = lax.broadcasted_iota(jnp.int32, (rows, rows), 0)
        jj = lax.broadcasted_iota(jnp.int32, (rows, rows), 1)
        shift = CHUNK.bit_length() - 1
        dist = jnp.abs(ii - jj).astype(F32)
        for h in heads:
            decay_scr[h] = jnp.where((jj >> shift) <= (ii >> shift), jnp.exp(lgs[h] * dist), 0.0)

    idx = lax.broadcasted_iota(jnp.int32, (rows, 1), 0).astype(F32)
    qs = [q_ref[:, cols(h)] for h in heads]
    ks = [k_ref[:, cols(h)] for h in heads]
    vs = [v_ref[:, cols(h)] for h in heads]
    scores = [lax.dot_general(qs[h], ks[h], _NT, preferred_element_type=F32) * decay_scr[h] for h in heads]
    outs = [jnp.dot(scores[h].astype(BF16), vs[h], preferred_element_type=F32) for h in heads]
    states = [s_scr[h] for h in heads]
    outs = [outs[h] + jnp.exp(lgs[h] * (idx + 1.0))
            * jnp.dot(qs[h], states[h].astype(BF16), preferred_element_type=F32) for h in heads]
    kds = [(ks[h].astype(F32) * jnp.exp(lgs[h] * (rows - 1.0 - idx))).astype(BF16) for h in heads]
    for h in heads:
        s_new = (jnp.exp(lgs[h] * rows) * states[h]
                 + lax.dot_general(kds[h], vs[h], _TN, preferred_element_type=F32))
        s_scr[h] = s_new
        sout_ref[h] = s_new
    for h in heads:
        mu = jnp.mean(outs[h], axis=-1, keepdims=True)
        cen = outs[h] - mu
        var = jnp.mean(cen * cen, axis=-1, keepdims=True)
        yr = cen * lax.rsqrt(var + EPS) * gn_ref[h]
        g = g_ref[:, cols(h)]
        o_ref[:, cols(h)] = (g * _sigmoid(g) * yr).astype(o_ref.dtype)


def _into(out_buf):
    if out_buf is None:
        return [], []
    return [out_buf], [pl.BlockSpec(memory_space=pl.ANY)]


def _retention(log_gamma, q, k, v, gate, gn, s0, row_off, n_batch, n_steps, rows, out_buf=None):
    heads = q.shape[1] // RET_HEAD_DIM
    nh = min(RET_HEADS_PER_STEP, heads)
    assert row_off % rows == 0 and rows % CHUNK == 0 and heads % nh == 0
    ro = row_off // rows
    tok = lambda b, h, c, lg: (ro + b * n_steps + c, h)
    blk = pl.BlockSpec((rows, nh * RET_HEAD_DIM), tok)
    st = pl.BlockSpec((None, nh, RET_HEAD_DIM, RET_HEAD_DIM), lambda b, h, c, lg: (b, h, 0, 0))
    args = [log_gamma, q, k, v, gate, gn.reshape(heads, 1, RET_HEAD_DIM), s0]
    extra_args, extra_specs = _into(out_buf)
    grid_spec = pltpu.PrefetchScalarGridSpec(
        num_scalar_prefetch=1,
        grid=(n_batch, heads // nh, n_steps),
        in_specs=[blk, blk, blk, blk,
                  pl.BlockSpec((nh, 1, RET_HEAD_DIM), lambda b, h, c, lg: (h, 0, 0)), st] + extra_specs,
        out_specs=[blk, st],
        scratch_shapes=[pltpu.VMEM((nh, RET_HEAD_DIM, RET_HEAD_DIM), F32), pltpu.VMEM((nh, rows, rows), F32)],
    )
    return pl.pallas_call(
        functools.partial(_retention_body, rows=rows, n_heads=nh),
        out_shape=[jax.ShapeDtypeStruct((q.shape[0], heads * RET_HEAD_DIM), BF16),
                   jax.ShapeDtypeStruct((n_batch, heads, RET_HEAD_DIM, RET_HEAD_DIM), F32)],
        grid_spec=grid_spec,
        input_output_aliases={len(args): 0} if extra_args else {},
        compiler_params=_params("arbitrary", "arbitrary", "arbitrary"),
        name="retention",
    )(*args, *extra_args)


def _neg_suffix(n):
    j = lax.broadcasted_iota(jnp.int32, (2 * n, n), 0) & (n - 1)
    s = lax.broadcasted_iota(jnp.int32, (2 * n, n), 1)
    return jnp.where(j >= s, -1.0, 0.0).astype(BF16)


def _lanes_to(x, n):
    return x[:, :n] if n <= LANES else jnp.concatenate([x] * (n // LANES), axis=1)


def _suffix_sums(fail, neg_ones):
    tk = fail.shape[1]
    hi, lo = _split_bf16(fail)
    if tk % LANES == 0:
        return jnp.dot(jnp.concatenate([hi, lo], axis=1), neg_ones, preferred_element_type=F32)
    return (jnp.dot(hi, neg_ones[:tk], preferred_element_type=F32)
            + jnp.dot(lo, neg_ones[:tk], preferred_element_type=F32))


def _sb_tiles(qs, kbs, vbs, neg_ones, carry, acc, masked):
    heads = range(len(qs))
    tk = kbs[0].shape[0]
    zs = [lax.dot_general(qs[h], kbs[h], _NT, preferred_element_type=F32) for h in heads]
    fails = [jnp.maximum(z, 0.0) + jnp.log2(1.0 + jnp.exp2(-jnp.abs(z))) for z in zs]
    if masked:
        t = lax.broadcasted_iota(jnp.int32, zs[0].shape, 0)
        s = lax.broadcasted_iota(jnp.int32, zs[0].shape, 1)
        valid = s < t
        fails = [jnp.where(valid, f, 0.0) for f in fails]
    suffixes = [_suffix_sums(f, neg_ones) for f in fails]
    probs = [jnp.exp2(zs[h] + suffixes[h] + _lanes_to(carry[h], tk)) for h in heads]
    if masked:
        probs = [jnp.where(valid, p, 0.0) for p in probs]
    for h in heads:
        acc[h] += jnp.dot(probs[h].astype(BF16), vbs[h], preferred_element_type=F32)
        carry[h] += jnp.broadcast_to(suffixes[h][:, :1], carry.shape[1:])


def _sb_body(q_ref, kd_ref, vd_ref, kp_ref, vp_ref, *rest, tk, n_heads, past_per_step, past_static):
    o_ref, carry, acc = rest[-3:]
    tq = q_ref.shape[0]
    n_past = past_static if past_static is not None else pl.program_id(1) * past_per_step
    cols = lambda h: slice(h * SB_HEAD_DIM, (h + 1) * SB_HEAD_DIM)
    heads = range(n_heads)
    past = (lambda ref, h, rows: ref[rows, h, :]) if len(kp_ref.shape) == 3 else (
        lambda ref, h, rows: ref[rows, cols(h)])
    carry[...] = jnp.zeros(carry.shape, F32)
    acc[...] = jnp.zeros(acc.shape, F32)
    queries = lambda: [q_ref[:, cols(h)] for h in heads]
    _sb_tiles(queries(), [kd_ref[:, cols(h)].astype(BF16) for h in heads],
              [vd_ref[:, cols(h)].astype(BF16) for h in heads], _neg_suffix(tq), carry, acc, True)
    neg_ones = _neg_suffix(tk)

    alive = lambda: jnp.max(carry[...]) > SB_DEAD_LOG2

    def step(state):
        it = state[0]
        rows = pl.ds(pl.multiple_of((n_past - 1 - it) * tk, tk), tk)
        _sb_tiles(queries(), [past(kp_ref, h, rows).astype(BF16) for h in heads],
                  [past(vp_ref, h, rows).astype(BF16) for h in heads], neg_ones, carry, acc, False)
        return it + 1, alive()

    lax.while_loop(lambda state: (state[0] < n_past) & state[1], step, (jnp.int32(0), alive()))
    for h in range(n_heads):
        o_ref[:, cols(h)] = acc[h].astype(o_ref.dtype)


def _sb_scratch(n_heads, tq):
    return [pltpu.VMEM((n_heads, tq, LANES), F32), pltpu.VMEM((n_heads, tq, SB_HEAD_DIM), F32)]


def _sb_prompt(q, k, v, n_tok):
    heads = q.shape[1] // SB_HEAD_DIM
    tq = _tile(n_tok, SB_BLOCK)
    nh = SB_HEADS_PER_STEP
    width = nh * SB_HEAD_DIM
    blk = pl.BlockSpec((tq, width), lambda h, i: (i, h))
    full = pl.BlockSpec((n_tok, width), lambda h, i: (0, h))
    return pl.pallas_call(
        functools.partial(_sb_body, tk=tq, n_heads=nh, past_per_step=1, past_static=None),
        out_shape=jax.ShapeDtypeStruct((q.shape[0], heads * SB_HEAD_DIM), BF16),
        grid=(heads // nh, n_tok // tq),
        in_specs=[blk, blk, blk, full, full],
        out_specs=blk,
        scratch_shapes=_sb_scratch(nh, tq),
        compiler_params=_params("parallel", "arbitrary"),
        name="stick_breaking_prompt",
    )(q, k, v, k, v)


def _sb_sample(q, k, v, cache_k, cache_v, row_off, n_new, out_buf):
    heads = q.shape[1] // SB_HEAD_DIM
    _, n_batch, past, _, _ = cache_k.shape
    tk = _tile(past, SB_BLOCK)
    assert row_off % n_new == 0
    ro = row_off // n_new
    nh = min(F32_SUBLANES, heads)
    assert heads % nh == 0 and (nh == F32_SUBLANES or nh == heads)
    blk = pl.BlockSpec((n_new, nh * SB_HEAD_DIM), lambda b, h: (ro + b, h))
    full = pl.BlockSpec((None, None, past, nh, SB_HEAD_DIM), lambda b, h: (0, b, 0, h, 0))
    extra_args, extra_specs = _into(out_buf)
    return pl.pallas_call(
        functools.partial(_sb_body, tk=tk, n_heads=nh, past_per_step=0, past_static=past // tk),
        out_shape=jax.ShapeDtypeStruct(out_buf.shape, out_buf.dtype),
        grid=(n_batch, heads // nh),
        in_specs=[blk, blk, blk, full, full] + extra_specs,
        out_specs=blk,
        scratch_shapes=_sb_scratch(nh, n_new),
        input_output_aliases={5: 0},
        compiler_params=_params("parallel", "arbitrary"),
        name="stick_breaking_sample",
    )(q, k, v, cache_k, cache_v, *extra_args)


def _cross_body(q_ref, k_ref, v_ref, *rest, scale, steps):
    o_ref, kb, vb = rest[-3:]
    dh = q_ref.shape[1] // MEM_HEADS
    heads = range(MEM_HEADS)
    cols = lambda h: slice(h * dh, (h + 1) * dh)
    head_of = (lambda ref, h: ref[:, h, :]) if len(k_ref.shape) == 3 else (lambda ref, h: ref[:, cols(h)])

    @pl.when(pl.program_id(0) % steps == 0)
    def _():
        for h in heads:
            kb[:, cols(h)] = head_of(k_ref, h).astype(BF16)
            vb[:, cols(h)] = head_of(v_ref, h).astype(BF16)

    scores = [lax.dot_general(q_ref[:, cols(h)], kb[:, cols(h)], _NT, preferred_element_type=F32) * scale
              for h in heads]
    exps = [jnp.exp(s - jnp.max(s, axis=-1, keepdims=True)) for s in scores]
    probs = [e / jnp.sum(e, axis=-1, keepdims=True) for e in exps]
    for h in heads:
        o_ref[:, cols(h)] = jnp.dot(probs[h].astype(BF16), vb[:, cols(h)],
                                    preferred_element_type=F32).astype(o_ref.dtype)


def _cross_attend(q, mem_k, mem_v, row_off, rows_per_seq, out_buf=None):
    d = q.shape[1]
    dh = d // MEM_HEADS
    tm = _tile(rows_per_seq, ROW_TILE)
    steps = rows_per_seq // tm
    assert row_off % tm == 0
    ro = row_off // tm
    extra_args, extra_specs = _into(out_buf)
    if mem_k.ndim == 3:
        n_seq, n_mem, _ = mem_k.shape
        mem = pl.BlockSpec((None, n_mem, d), lambda i: (i // steps, 0, 0))
    else:
        _, n_seq, n_mem, _, _ = mem_k.shape
        mem = pl.BlockSpec((None, None, n_mem, MEM_HEADS, dh), lambda i: (0, i // steps, 0, 0, 0))
    blk = pl.BlockSpec((tm, d), lambda i: (ro + i, 0))
    return pl.pallas_call(
        functools.partial(_cross_body, scale=dh ** -0.5, steps=steps),
        out_shape=jax.ShapeDtypeStruct((q.shape[0], d), BF16),
        grid=(n_seq * steps,),
        in_specs=[blk, mem, mem] + extra_specs,
        out_specs=blk,
        scratch_shapes=[pltpu.VMEM((n_mem, d), BF16), pltpu.VMEM((n_mem, d), BF16)],
        input_output_aliases={3: 0} if extra_args else {},
        compiler_params=_params("arbitrary"),
        name="cross_attention",
    )(q, mem_k, mem_v, *extra_args)


def _split_bf16(x):
    hi = x.astype(BF16)
    return hi, (x - hi.astype(F32)).astype(BF16)


def _router_body(x_ref, g_ref, wr_ref, br_ref, h_ref, ids_ref, wts_ref):
    h = _rmsnorm_rows(x_ref[...], g_ref[...])
    h_ref[...] = _pack_bf16_pairs(h)
    h_hi, h_lo = _split_bf16(h)
    w_hi, w_lo = _split_bf16(wr_ref[...])
    logits = (jnp.dot(h_hi, w_hi, preferred_element_type=F32)
              + jnp.dot(h_hi, w_lo, preferred_element_type=F32)
              + jnp.dot(h_lo, w_hi, preferred_element_type=F32)) + br_ref[...]
    lane = lax.broadcasted_iota(jnp.int32, logits.shape, 1).astype(F32)
    big = float(LANES)
    neg = -jnp.inf

    def first_max(vals):
        top = jnp.max(vals, axis=-1, keepdims=True)
        return top, jnp.min(jnp.where(vals == top, lane, big), axis=-1, keepdims=True)

    is_group = lane < N_GROUPS
    g_max, g_top = first_max(jnp.where(is_group, logits, neg))
    g_w = 1.0 / jnp.sum(jnp.where(is_group, jnp.exp(logits - g_max), 0.0), axis=-1, keepdims=True)
    first = N_GROUPS + EXPERTS_PER_GROUP * g_top
    cand = jnp.where((lane >= first) & (lane < first + EXPERTS_PER_GROUP), logits, neg)
    v1, i1 = first_max(cand)
    v2, i2 = first_max(jnp.where(lane == i1, neg, cand))
    t = jnp.exp(v2 - v1)
    w1 = g_w / (1.0 + t)
    w2 = g_w * t / (1.0 + t)
    ids = jnp.where(lane == 0.0, i1 - N_GROUPS, jnp.where(lane == 1.0, i2 - N_GROUPS, 0.0))
    ids_ref[...] = ids.astype(jnp.int32)
    wts_ref[...] = jnp.where(lane == 0.0, w1, jnp.where(lane == 1.0, w2, 0.0))


def _norm_and_route(x, g, wg, bg, we, be):
    m, d = x.shape
    tm = _tile(m, NORM_ROWS)
    pad = LANES - N_GROUPS - N_EXPERTS
    wr = jnp.concatenate([wg, we, jnp.zeros((d, pad), F32)], axis=1)
    br = jnp.concatenate([bg, be, jnp.zeros((pad,), F32)]).reshape(1, LANES)
    row = lambda c: pl.BlockSpec((tm, c), lambda i: (i, 0))
    return pl.pallas_call(
        _router_body,
        out_shape=[jax.ShapeDtypeStruct((m, d // 2), jnp.int32),
                   jax.ShapeDtypeStruct((m, LANES), jnp.int32),
                   jax.ShapeDtypeStruct((m, LANES), F32)],
        grid=(m // tm,),
        in_specs=[row(d), pl.BlockSpec((1, d), lambda i: (0, 0)),
                  pl.BlockSpec((d, LANES), lambda i: (0, 0)), pl.BlockSpec((1, LANES), lambda i: (0, 0))],
        out_specs=[row(d // 2), row(LANES), row(LANES)],
        compiler_params=_params("parallel"),
        name="norm_route",
    )(x, g.reshape(1, d), wr, br)


GATHER_UNROLL = 32


def _start_row_gather(idx_ref, first, stride, src_ref, buf, sem, n_rows):
    def one(r, _):
        t = idx_ref[first + r * stride]
        pltpu.make_async_copy(src_ref.at[pl.ds(t, 1)], buf.at[pl.ds(r, 1)], sem).start()
        return 0

    def group(g, _):
        for u in range(GATHER_UNROLL):
            one(g * GATHER_UNROLL + u, 0)
        return 0

    if isinstance(n_rows, int):
        for r in range(n_rows):
            one(r, 0)
        return
    groups = n_rows // GATHER_UNROLL
    lax.fori_loop(0, groups, group, 0)
    lax.fori_loop(groups * GATHER_UNROLL, n_rows, one, 0)


def _wait_row_gather(src_ref, buf, sem, n_rows):
    pltpu.make_async_copy(src_ref.at[pl.ds(0, n_rows)], buf.at[pl.ds(0, n_rows)], sem).wait()


def _pack_bf16_pairs(x):
    half = x.shape[1] // 2
    hi = lax.bitcast_convert_type(x[:, :half].astype(BF16).astype(F32), jnp.int32)
    lo = lax.bitcast_convert_type(x[:, half:].astype(BF16).astype(F32), jnp.int32)
    return hi | lax.shift_right_logical(lo, jnp.int32(16))


def _unpack_bf16_pairs(p):
    hi = lax.bitcast_convert_type(p & jnp.int32(-65536), F32)
    lo = lax.bitcast_convert_type(lax.shift_left(p, jnp.int32(16)), F32)
    return hi, lo


def _dispatch_body(idx_ref, used_ref, cnt_ref, src_ref, o_ref, buf, sem):
    b = pl.program_id(0)
    rows = buf.shape[1]

    copied = lambda blk: pl.multiple_of(pl.cdiv(cnt_ref[blk], GATHER_UNROLL) * GATHER_UNROLL, GATHER_UNROLL)

    def start(blk):
        _start_row_gather(idx_ref, blk * rows, 1, src_ref, buf.at[blk % 2], sem.at[blk % 2], copied(blk))

    @pl.when(b == 0)
    def _():
        buf[...] = jnp.zeros(buf.shape, buf.dtype)
        start(0)

    @pl.when(b + 1 < used_ref[0])
    def _():
        start(b + 1)

    @pl.when(b < used_ref[0])
    def _():
        slot = b % 2
        cnt = cnt_ref[b]
        _wait_row_gather(src_ref, buf.at[slot], sem.at[slot], copied(b))
        half = buf.shape[2]
        row = lax.broadcasted_iota(jnp.int32, (rows, 1), 0)
        hi, lo = _unpack_bf16_pairs(jnp.where(row < cnt, buf[slot], 0))
        o_ref[:, :half] = hi.astype(o_ref.dtype)
        o_ref[:, half:] = lo.astype(o_ref.dtype)


def _dispatch_rows(src, row_tok, blk_cnt, n_used):
    n_rows = row_tok.shape[0]
    half = src.shape[1]
    grid_spec = pltpu.PrefetchScalarGridSpec(
        num_scalar_prefetch=3,
        grid=(n_rows // MOE_ROWS,),
        in_specs=[pl.BlockSpec(memory_space=pl.ANY)],
        out_specs=pl.BlockSpec((MOE_ROWS, 2 * half), lambda b, idx, used, cnt: (b, 0)),
        scratch_shapes=[pltpu.VMEM((2, MOE_ROWS, half), src.dtype), pltpu.SemaphoreType.DMA((2,))],
    )
    return pl.pallas_call(
        _dispatch_body,
        out_shape=jax.ShapeDtypeStruct((n_rows, 2 * half), BF16),
        grid_spec=grid_spec,
        compiler_params=_params("arbitrary"),
        name="dispatch_rows",
    )(row_tok, n_used, blk_cnt, src)


def _expert_up_body(be_ref, used_ref, x_ref, w1_ref, w3_ref, h_ref):
    @pl.when(pl.program_id(1) < used_ref[0])
    def _():
        x = x_ref[...]
        a = jnp.dot(x, w1_ref[...].astype(BF16), preferred_element_type=F32)
        g = jnp.dot(x, w3_ref[...].astype(BF16), preferred_element_type=F32)
        h_ref[...] = (a * _sigmoid(a) * g).astype(h_ref.dtype)


def _expert_down_body(be_ref, used_ref, h_ref, w2_ref, y_ref):
    @pl.when(pl.program_id(1) < used_ref[0])
    def _():
        y_ref[...] = _pack_bf16_pairs(jnp.dot(h_ref[...], w2_ref[...].astype(BF16),
                                              preferred_element_type=F32))


def _expert_ffn(xs, blk_exp, n_used, w1, w3, w2):
    n_rows, d = xs.shape
    f = w1.shape[2]
    n_blocks = n_rows // MOE_ROWS
    tf = _tile(f, EXPERT_HIDDEN_TILE)
    tn = _tile(d, EXPERT_OUT_TILE)
    last = lambda b, used: jnp.minimum(b, used[0] - 1)
    up_spec = pltpu.PrefetchScalarGridSpec(
        num_scalar_prefetch=2,
        grid=(f // tf, n_blocks),
        in_specs=[pl.BlockSpec((MOE_ROWS, d), lambda j, b, be, us: (last(b, us), 0)),
                  pl.BlockSpec((None, d, tf), lambda j, b, be, us: (be[b], 0, j)),
                  pl.BlockSpec((None, d, tf), lambda j, b, be, us: (be[b], 0, j))],
        out_specs=pl.BlockSpec((MOE_ROWS, tf), lambda j, b, be, us: (b, j)),
    )
    hidden = pl.pallas_call(
        _expert_up_body,
        out_shape=jax.ShapeDtypeStruct((n_rows, f), BF16),
        grid_spec=up_spec,
        compiler_params=_params("arbitrary", "arbitrary"),
        name="expert_up",
    )(blk_exp, n_used, xs, w1, w3)
    down_spec = pltpu.PrefetchScalarGridSpec(
        num_scalar_prefetch=2,
        grid=(d // tn, n_blocks),
        in_specs=[pl.BlockSpec((MOE_ROWS, f), lambda j, b, be, us: (last(b, us), 0)),
                  pl.BlockSpec((None, f, tn), lambda j, b, be, us: (be[b], 0, j))],
        out_specs=pl.BlockSpec((MOE_ROWS, tn // 2), lambda j, b, be, us: (b, j)),
    )
    return pl.pallas_call(
        _expert_down_body,
        out_shape=jax.ShapeDtypeStruct((n_rows, d // 2), jnp.int32),
        grid_spec=down_spec,
        compiler_params=_params("arbitrary", "arbitrary"),
        name="expert_down",
    )(blk_exp, n_used, hidden, w2)


def _route_layout(ids, n_tok):
    n_assign = n_tok * TOP_K
    flat_e = ids.reshape(n_assign)
    flat_t = jnp.repeat(jnp.arange(n_tok, dtype=jnp.int32), TOP_K)
    onehot = (flat_e[:, None] == jnp.arange(N_EXPERTS, dtype=jnp.int32)[None, :]).astype(jnp.int32)
    running = jnp.cumsum(onehot, axis=0)
    rank = jnp.take_along_axis(running, flat_e[:, None], axis=1)[:, 0] - 1
    counts = running[-1]
    padded = (counts + MOE_ROWS - 1) // MOE_ROWS * MOE_ROWS
    pad_end = jnp.cumsum(padded)
    dest = (pad_end - padded)[flat_e] + rank
    n_blocks = -(-n_assign // MOE_ROWS) + N_EXPERTS
    n_rows = n_blocks * MOE_ROWS
    row_tok = jnp.zeros((n_rows,), jnp.int32).at[dest].set(flat_t)
    blk = jnp.arange(n_blocks, dtype=jnp.int32)
    n_used = (pad_end[-1] // MOE_ROWS).astype(jnp.int32)
    blk_exp = jnp.searchsorted(pad_end, jnp.minimum(blk, n_used - 1) * MOE_ROWS, side='right').astype(jnp.int32)
    blk_exp = jnp.minimum(blk_exp, N_EXPERTS - 1)
    into_range = blk * MOE_ROWS - (pad_end - padded)[blk_exp]
    blk_cnt = jnp.where(blk < n_used, jnp.clip(counts[blk_exp] - into_range, 0, MOE_ROWS), 0).astype(jnp.int32)
    return row_tok, blk_cnt, dest.astype(jnp.int32), blk_exp, n_used.reshape(1)


def _final_body(dest_ref, x_ref, w_ref, y_ref, g_ref, op_ref, os_ref, buf, sem, *, prompt_tiles, pack_tile):
    i = pl.program_id(0)
    tm = buf.shape[2]

    def start(blk):
        for k in range(TOP_K):
            _start_row_gather(dest_ref, blk * tm * TOP_K + k, TOP_K, y_ref, buf.at[blk % 2, k], sem.at[blk % 2, k],
                              tm)

    @pl.when(i == 0)
    def _():
        start(0)

    @pl.when(i + 1 < pl.num_programs(0))
    def _():
        start(i + 1)

    slot = i % 2
    for k in range(TOP_K):
        _wait_row_gather(y_ref, buf.at[slot, k], sem.at[slot, k], tm)
    w = w_ref[...]
    pieces = []
    for j in range(buf.shape[3] // pack_tile):
        words = slice(j * pack_tile, (j + 1) * pack_tile)
        hi0, lo0 = _unpack_bf16_pairs(buf[slot, 0, :, words])
        hi1, lo1 = _unpack_bf16_pairs(buf[slot, 1, :, words])
        pieces += [hi0 * w[:, 0:1] + hi1 * w[:, 1:2], lo0 * w[:, 0:1] + lo1 * w[:, 1:2]]
    y = _rmsnorm_rows(x_ref[...] + jnp.concatenate(pieces, axis=1), g_ref[...])

    @pl.when(i < prompt_tiles)
    def _():
        op_ref[...] = y

    @pl.when(i >= prompt_tiles)
    def _():
        os_ref[...] = y


def _combine_and_norm(x, wts, y_rows, dest, g, n_prompt):
    m, d = x.shape
    tm = _tile(m - n_prompt, _tile(n_prompt, 256))
    assert n_prompt % tm == 0
    pt = n_prompt // tm
    grid_spec = pltpu.PrefetchScalarGridSpec(
        num_scalar_prefetch=1,
        grid=(m // tm,),
        in_specs=[pl.BlockSpec((tm, d), lambda i, dest: (i, 0)),
                  pl.BlockSpec((tm, LANES), lambda i, dest: (i, 0)),
                  pl.BlockSpec(memory_space=pl.ANY),
                  pl.BlockSpec((1, d), lambda i, dest: (0, 0))],
        out_specs=[pl.BlockSpec((tm, d), lambda i, dest: (jnp.minimum(i, pt - 1), 0)),
                   pl.BlockSpec((tm, d), lambda i, dest: (jnp.maximum(i - pt, 0), 0))],
        scratch_shapes=[pltpu.VMEM((2, TOP_K, tm, d // 2), y_rows.dtype), pltpu.SemaphoreType.DMA((2, TOP_K))],
    )
    return pl.pallas_call(
        functools.partial(_final_body, prompt_tiles=pt, pack_tile=_tile(d, EXPERT_OUT_TILE) // 2),
        out_shape=[jax.ShapeDtypeStruct((n_prompt, d), F32), jax.ShapeDtypeStruct((m - n_prompt, d), F32)],
        grid_spec=grid_spec,
        compiler_params=_params("arbitrary"),
        name="combine_norm",
    )(dest, x, wts, y_rows, g.reshape(1, d))


def _layer(xp, xs, n_seq, n_new, past_len, cache_sb_k, cache_sb_v, state_ret, cache_mem_k, cache_mem_v,
           mem_prompt, norm_mix, w_in, ret_gn, w_out, norm_cross, norm_mem, w_cq, w_ck, w_cv, w_co, norm_ffn,
           wg, bg, we, be, w1, w3, w2):
    n_prompt, d = xp.shape
    n_tok = n_prompt + xs.shape[0]
    d_ret = d // 2
    d_sb = d - d_ret
    ret_heads = d_ret // RET_HEAD_DIM
    sds = jax.ShapeDtypeStruct

    pos = np.concatenate([np.arange(n_prompt), np.tile(past_len + np.arange(n_new), n_seq)])
    half = RET_HEAD_DIM // 2
    inv = (ROPE_BASE ** (-np.arange(half, dtype=np.float32) / half)).astype(np.float32)
    ang = pos.astype(np.float32)[:, None] * inv[None, :]
    cos, sin = jnp.asarray(np.cos(ang), F32), jnp.asarray(np.sin(ang), F32)
    log_gamma = jnp.log1p(-jnp.exp2(-5.0 - jnp.arange(ret_heads, dtype=F32)))

    h = _rmsnorm_stacked(xp, xs, norm_mix, BF16)
    rot = [(cos, half), (sin, half)]
    bf = lambda n: [(sds((n_tok, n), BF16), None)]
    (rq,) = _matmul([h], w_in, 0, d_ret, _ep_rotary(1.0), rot, bf(d_ret), "in_proj_ret_q")
    (rk,) = _matmul([h], w_in, d_ret, d_ret, _ep_rotary(RET_HEAD_DIM ** -0.5), rot, bf(d_ret), "in_proj_ret_k")
    (rv,) = _matmul([h], w_in, 2 * d_ret, d_ret, _ep_store(1.0), [], bf(d_ret), "in_proj_ret_v")
    (rg,) = _matmul([h], w_in, 3 * d_ret, d_ret, _ep_store(1.0), [], [(sds((n_tok, d_ret), F32), None)],
                    "in_proj_ret_gate")
    (sq,) = _matmul([h], w_in, 4 * d_ret, d_sb, _ep_store(SB_HEAD_DIM ** -0.5 * LOG2_E), [], bf(d_sb),
                    "in_proj_sb_q")
    split = [(sds((n_prompt, d_sb), F32), PROMPT), (sds((n_tok - n_prompt, d_sb), F32), SAMPLE),
             (sds((n_tok, d_sb), BF16), None)]
    sk_p, sk_s, sk_b = _matmul([h], w_in, 4 * d_ret + d_sb, d_sb, _ep_store_split, [], split, "in_proj_sb_k",
                               n_prompt=n_prompt)
    sv_p, sv_s, sv_b = _matmul([h], w_in, 4 * d_ret + 2 * d_sb, d_sb, _ep_store_split, [], split,
                               "in_proj_sb_v", n_prompt=n_prompt)

    rows_p = _tile(n_prompt, RET_BLOCK)
    zeros_state = jnp.zeros((1, ret_heads, RET_HEAD_DIM, RET_HEAD_DIM), F32)
    yr, state_p = _retention(log_gamma, rq, rk, rv, rg, ret_gn, zeros_state, 0, 1, n_prompt // rows_p, rows_p)
    yr, state_s = _retention(log_gamma, rq, rk, rv, rg, ret_gn, state_ret, n_prompt, n_seq, 1, n_new,
                             out_buf=yr)
    sb = _sb_prompt(sq, sk_b, sv_b, n_prompt)
    sb = _sb_sample(sq, sk_b, sv_b, cache_sb_k, cache_sb_v, n_prompt, n_new, sb)
    res = [(sds((n_tok, d), F32), None)]
    (x,) = _matmul([yr, sb], w_out, 0, d, _ep_residual_split, [(xp, PROMPT), (xs, SAMPLE)], res, "out_proj",
                   n_prompt=n_prompt)

    n_mem = mem_prompt.shape[0]
    m = _rmsnorm(mem_prompt, norm_mem, BF16)
    mem_out = [(sds((n_mem, d), F32), None)]
    (mk,) = _matmul([m], w_ck, 0, d, _ep_store(1.0), [], mem_out, "mem_k")
    (mv,) = _matmul([m], w_cv, 0, d, _ep_store(1.0), [], mem_out, "mem_v")
    h = _rmsnorm(x, norm_cross, BF16)
    (cq,) = _matmul([h], w_cq, 0, d, _ep_store(1.0), [], bf(d), "cross_q")
    co = _cross_attend(cq, mk.reshape(1, n_mem, d), mv.reshape(1, n_mem, d), 0, n_prompt)
    co = _cross_attend(cq, cache_mem_k, cache_mem_v, n_prompt, n_new, out_buf=co)
    (x,) = _matmul([co], w_co, 0, d, _ep_residual, [(x, None)], res, "cross_out")

    hn, ids, wts = _norm_and_route(x, norm_ffn, wg, bg, we, be)
    row_tok, blk_cnt, dest, blk_exp, n_used = _route_layout(ids[:, :TOP_K], n_tok)
    x_rows = _dispatch_rows(hn, row_tok, blk_cnt, n_used)
    y_rows = _expert_ffn(x_rows, blk_exp, n_used, w1, w3, w2)
    return x, wts, y_rows, dest, (sk_p, sv_p, sk_s, sv_s, state_p, state_s, mk, mv)


def kernel(x_prompt, x_sample, cache_sb_k, cache_sb_v, state_ret, cache_mem_k, cache_mem_v, mem_prompt, norm_mix, w_in, ret_gn, w_out, norm_cross, norm_mem, w_cq, w_ck, w_cv, w_co, norm_ffn, router_group_w, router_group_b, router_expert_w, router_expert_b, expert_w1, expert_w3, expert_w2, norm_final):
    batch, n_prompt, d = x_prompt.shape
    n_seq, n_new, _ = x_sample.shape
    depth, _, past_len, sb_heads, _ = cache_sb_k.shape
    n_mem = mem_prompt.shape[1]
    assert batch == 1 and depth == 1
    ret_heads = (d // 2) // RET_HEAD_DIM
    x, wts, y_rows, dest, (sk_p, sv_p, sk_s, sv_s, state_p, state_s, mk, mv) = _layer(
        x_prompt.reshape(n_prompt, d), x_sample.reshape(n_seq * n_new, d), n_seq, n_new, past_len,
        cache_sb_k, cache_sb_v, state_ret[0], cache_mem_k, cache_mem_v, mem_prompt[0], norm_mix[0], w_in,
        ret_gn[0], w_out, norm_cross[0], norm_mem[0], w_cq, w_ck, w_cv, w_co, norm_ffn[0], router_group_w[0],
        router_group_b[0], router_expert_w[0], router_expert_b[0], expert_w1[0], expert_w3[0], expert_w2[0])
    y_prompt, y_sample = _combine_and_norm(x, wts, y_rows, dest, norm_final, n_prompt)
    y_prompt = y_prompt.reshape(1, n_prompt, d)
    y_sample = y_sample.reshape(n_seq, n_new, d)
    sb_shape_p = (1, 1, n_prompt, sb_heads, SB_HEAD_DIM)
    sb_shape_s = (1, n_seq, n_new, sb_heads, SB_HEAD_DIM)
    mem_shape = (1, 1, n_mem, MEM_HEADS, d // MEM_HEADS)
    return (y_prompt, y_sample,
            sk_p.reshape(sb_shape_p), sv_p.reshape(sb_shape_p),
            state_p.reshape(1, 1, ret_heads, RET_HEAD_DIM, RET_HEAD_DIM),
            mk.reshape(mem_shape), mv.reshape(mem_shape),
            sk_s.reshape(sb_shape_s), sv_s.reshape(sb_shape_s),
            state_s.reshape(1, n_seq, ret_heads, RET_HEAD_DIM, RET_HEAD_DIM))
```

```python
import functools

import jax
import jax.numpy as jnp
import numpy as np
from jax import lax
from jax.experimental import pallas as pl
from jax.experimental.pallas import tpu as pltpu

BF16 = jnp.bfloat16
F32 = jnp.float32

EPS = 1e-6
CHUNK = 64
RET_HEAD_DIM = 256
SB_HEAD_DIM = 128
MEM_HEADS = 4
N_GROUPS = 4
EXPERTS_PER_GROUP = 8
N_EXPERTS = N_GROUPS * EXPERTS_PER_GROUP
TOP_K = 2
ROPE_BASE = 10000.0

LANES = 128
F32_SUBLANES = 8
VMEM_LIMIT_BYTES = 60 * 1024 * 1024
MATMUL_VMEM_BUDGET = 56 * 1024 * 1024
ROW_TILE = 512
NORM_ROWS = 512
COL_TILE = 1024
EXPERT_HIDDEN_TILE = 512
RET_BLOCK = 256
RET_HEADS_PER_STEP = 4
SB_BLOCK = 256
SB_HEADS_PER_STEP = 4
LOG2_E = 1.4426950408889634
SB_DEAD_LOG2 = -64.0
MOE_ROWS = 256
EXPERT_OUT_TILE = 4096

PROMPT = "prompt"
SAMPLE = "sample"

_NT = (((1,), (1,)), ((), ()))
_TN = (((0,), (0,)), ((), ()))


def _tile(n, pref):
    t = min(pref, n)
    while n % t:
        t //= 2
    return t


def _params(*sem):
    return pltpu.CompilerParams(dimension_semantics=sem, vmem_limit_bytes=VMEM_LIMIT_BYTES)


def _sigmoid(x):
    return 1.0 / (1.0 + jnp.exp(-x))


def _rmsnorm_rows(x, g):
    ms = jnp.mean(x * x, axis=-1, keepdims=True)
    return x * lax.rsqrt(ms + EPS) * g


def _rmsnorm_body(x_ref, g_ref, o_ref):
    o_ref[...] = _rmsnorm_rows(x_ref[...], g_ref[...]).astype(o_ref.dtype)


def _rmsnorm(x, g, out_dtype):
    m, d = x.shape
    tm = _tile(m, NORM_ROWS)
    return pl.pallas_call(
        _rmsnorm_body,
        out_shape=jax.ShapeDtypeStruct((m, d), out_dtype),
        grid=(m // tm,),
        in_specs=[pl.BlockSpec((tm, d), lambda i: (i, 0)), pl.BlockSpec((1, d), lambda i: (0, 0))],
        out_specs=pl.BlockSpec((tm, d), lambda i: (i, 0)),
        compiler_params=_params("parallel"),
        name="rmsnorm",
    )(x, g.reshape(1, d))


def _rmsnorm_stacked_body(xp_ref, xs_ref, g_ref, o_ref, *, prompt_tiles):
    is_prompt = pl.program_id(0) < prompt_tiles

    @pl.when(is_prompt)
    def _():
        o_ref[...] = _rmsnorm_rows(xp_ref[...], g_ref[...]).astype(o_ref.dtype)

    @pl.when(jnp.logical_not(is_prompt))
    def _():
        o_ref[...] = _rmsnorm_rows(xs_ref[...], g_ref[...]).astype(o_ref.dtype)


def _rmsnorm_stacked(xp, xs, g, out_dtype):
    (n_p, d), n_s = xp.shape, xs.shape[0]
    tm = _tile(n_s, _tile(n_p, NORM_ROWS))
    pt = n_p // tm
    return pl.pallas_call(
        functools.partial(_rmsnorm_stacked_body, prompt_tiles=pt),
        out_shape=jax.ShapeDtypeStruct((n_p + n_s, d), out_dtype),
        grid=((n_p + n_s) // tm,),
        in_specs=[pl.BlockSpec((tm, d), lambda i: (jnp.minimum(i, pt - 1), 0)),
                  pl.BlockSpec((tm, d), lambda i: (jnp.maximum(i - pt, 0), 0)),
                  pl.BlockSpec((1, d), lambda i: (0, 0))],
        out_specs=pl.BlockSpec((tm, d), lambda i: (i, 0)),
        compiler_params=_params("arbitrary"),
        name="rmsnorm_stacked",
    )(xp, xs, g.reshape(1, d))


def _matmul_body(*refs, n_lhs, n_extra, n_out, epilogue):
    x_refs = refs[:n_lhs]
    w_refs = refs[n_lhs:2 * n_lhs]
    extra = refs[2 * n_lhs:2 * n_lhs + n_extra]
    outs = refs[2 * n_lhs + n_extra:2 * n_lhs + n_extra + n_out]
    wb = refs[2 * n_lhs + n_extra + n_out:]

    @pl.when(pl.program_id(1) == 0)
    def _():
        for p in range(n_lhs):
            wb[p][...] = w_refs[p][...].astype(BF16)

    acc = jnp.dot(x_refs[0][...], wb[0][...], preferred_element_type=F32)
    for p in range(1, n_lhs):
        acc += jnp.dot(x_refs[p][...], wb[p][...], preferred_element_type=F32)
    epilogue(acc, extra, outs)


def _matmul(xs, w, col_off, n_cols, epilogue, extras, outs, name, n_prompt=None):
    m = xs[0].shape[0]
    tm = _tile(m, ROW_TILE) if n_prompt is None else _tile(m - n_prompt, _tile(n_prompt, ROW_TILE))
    k_total = sum(x.shape[1] for x in xs)

    def vmem_bytes(tn):
        tiles = sum((tn if kind is None or isinstance(kind, str) else kind) * jnp.dtype(a.dtype).itemsize
                    for a, kind in list(extras) + list(outs))
        return k_total * tn * (2 * 4 + 2) + 2 * tm * k_total * 2 + 2 * tm * tiles

    tn = _tile(n_cols, COL_TILE)
    while vmem_bytes(tn) > MATMUL_VMEM_BUDGET and tn > LANES:
        tn //= 2
    assert col_off % tn == 0
    jo = col_off // tn
    pt = None if n_prompt is None else n_prompt // tm
    in_specs, args = [], []
    for x in xs:
        in_specs.append(pl.BlockSpec((tm, x.shape[1]), lambda j, i: (i, 0)))
        args.append(x)
    row = 0
    for x in xs:
        kp = x.shape[1]
        assert row % kp == 0
        in_specs.append(pl.BlockSpec((None, kp, tn), lambda j, i, r=row // kp: (0, r, j + jo)))
        args.append(w)
        row += kp
    assert row == w.shape[1]

    def spec(kind):
        if kind is None:
            return pl.BlockSpec((tm, tn), lambda j, i: (i, j))
        if kind == PROMPT:
            return pl.BlockSpec((tm, tn), lambda j, i: (jnp.minimum(i, pt - 1), j))
        if kind == SAMPLE:
            return pl.BlockSpec((tm, tn), lambda j, i: (jnp.maximum(i - pt, 0), j))
        return pl.BlockSpec((tm, kind), lambda j, i: (i, 0))

    for a, kind in extras:
        in_specs.append(spec(kind))
        args.append(a)
    body = functools.partial(_matmul_body, n_lhs=len(xs), n_extra=len(extras), n_out=len(outs),
                             epilogue=epilogue if n_prompt is None else functools.partial(epilogue, pt))
    res = pl.pallas_call(
        body,
        out_shape=[o for o, _ in outs],
        grid=(n_cols // tn, m // tm),
        in_specs=in_specs,
        out_specs=[spec(kind) for _, kind in outs],
        scratch_shapes=[pltpu.VMEM((x.shape[1], tn), BF16) for x in xs],
        compiler_params=_params("arbitrary", "arbitrary"),
        name=name,
    )(*args)
    return res


def _ep_store(scale):
    def ep(acc, extra, outs):
        for o in outs:
            o[...] = (acc * scale).astype(o.dtype) if scale != 1.0 else acc.astype(o.dtype)
    return ep


def _ep_store_split(prompt_tiles, acc, extra, outs):
    is_prompt = pl.program_id(1) < prompt_tiles

    @pl.when(is_prompt)
    def _():
        outs[0][...] = acc

    @pl.when(jnp.logical_not(is_prompt))
    def _():
        outs[1][...] = acc

    outs[2][...] = acc.astype(outs[2].dtype)


def _ep_residual(acc, extra, outs):
    outs[0][...] = extra[0][...] + acc


def _ep_residual_split(prompt_tiles, acc, extra, outs):
    res = jnp.where(pl.program_id(1) < prompt_tiles, extra[0][...], extra[1][...])
    outs[0][...] = res + acc


def _ep_rotary(scale):
    half = RET_HEAD_DIM // 2

    def ep(acc, extra, outs):
        cos = extra[0][...]
        sin = extra[1][...]
        for h in range(acc.shape[1] // RET_HEAD_DIM):
            lo = h * RET_HEAD_DIM
            x1 = acc[:, lo:lo + half]
            x2 = acc[:, lo + half:lo + RET_HEAD_DIM]
            outs[0][:, lo:lo + half] = ((x1 * cos - x2 * sin) * scale).astype(outs[0].dtype)
            outs[0][:, lo + half:lo + RET_HEAD_DIM] = ((x1 * sin + x2 * cos) * scale).astype(outs[0].dtype)
    return ep


def _retention_body(lg_ref, q_ref, k_ref, v_ref, g_ref, gn_ref, s0_ref, *rest, rows, n_heads):
    o_ref, sout_ref, s_scr, decay_scr = rest[-4:]
    heads = range(n_heads)
    cols = lambda h: slice(h * RET_HEAD_DIM, (h + 1) * RET_HEAD_DIM)
    lgs = [lg_ref[pl.program_id(1) * n_heads + h] for h in heads]

    @pl.when(pl.program_id(2) == 0)
    def _():
        s_scr[...] = s0_ref[...]
        ii = lax.broadcasted_iota(jnp.int32, (rows, rows), 0)
        jj = lax.broadcasted_iota(jnp.int32, (rows, rows), 1)
        shift = CHUNK.bit_length() - 1
        dist = jnp.abs(ii - jj).astype(F32)
        for h in heads:
            decay_scr[h] = jnp.where((jj >> shift) <= (ii >> shift), jnp.exp(lgs[h] * dist), 0.0)

    idx = lax.broadcasted_iota(jnp.int32, (rows, 1), 0).astype(F32)
    qs = [q_ref[:, cols(h)] for h in heads]
    ks = [k_ref[:, cols(h)] for h in heads]
    vs = [v_ref[:, cols(h)] for h in heads]
    scores = [lax.dot_general(qs[h], ks[h], _NT, preferred_element_type=F32) * decay_scr[h] for h in heads]
    outs = [jnp.dot(scores[h].astype(BF16), vs[h], preferred_element_type=F32) for h in heads]
    states = [s_scr[h] for h in heads]
    outs = [outs[h] + jnp.exp(lgs[h] * (idx + 1.0))
            * jnp.dot(qs[h], states[h].astype(BF16), preferred_element_type=F32) for h in heads]
    kds = [(ks[h].astype(F32) * jnp.exp(lgs[h] * (rows - 1.0 - idx))).astype(BF16) for h in heads]
    for h in heads:
        s_new = (jnp.exp(lgs[h] * rows) * states[h]
                 + lax.dot_general(kds[h], vs[h], _TN, preferred_element_type=F32))
        s_scr[h] = s_new
        sout_ref[h] = s_new
    for h in heads:
        mu = jnp.mean(outs[h], axis=-1, keepdims=True)
        cen = outs[h] - mu
        var = jnp.mean(cen * cen, axis=-1, keepdims=True)
        yr = cen * lax.rsqrt(var + EPS) * gn_ref[h]
        g = g_ref[:, cols(h)]
        o_ref[:, cols(h)] = (g * _sigmoid(g) * yr).astype(o_ref.dtype)


def _into(out_buf):
    if out_buf is None:
        return [], []
    return [out_buf], [pl.BlockSpec(memory_space=pl.ANY)]


def _retention(log_gamma, q, k, v, gate, gn, s0, row_off, n_batch, n_steps, rows, out_buf=None):
    heads = q.shape[1] // RET_HEAD_DIM
    nh = min(RET_HEADS_PER_STEP, heads)
    assert row_off % rows == 0 and rows % CHUNK == 0 and heads % nh == 0
    ro = row_off // rows
    tok = lambda b, h, c, lg: (ro + b * n_steps + c, h)
    blk = pl.BlockSpec((rows, nh * RET_HEAD_DIM), tok)
    st = pl.BlockSpec((None, nh, RET_HEAD_DIM, RET_HEAD_DIM), lambda b, h, c, lg: (b, h, 0, 0))
    args = [log_gamma, q, k, v, gate, gn.reshape(heads, 1, RET_HEAD_DIM), s0]
    extra_args, extra_specs = _into(out_buf)
    grid_spec = pltpu.PrefetchScalarGridSpec(
        num_scalar_prefetch=1,
        grid=(n_batch, heads // nh, n_steps),
        in_specs=[blk, blk, blk, blk,
                  pl.BlockSpec((nh, 1, RET_HEAD_DIM), lambda b, h, c, lg: (h, 0, 0)), st] + extra_specs,
        out_specs=[blk, st],
        scratch_shapes=[pltpu.VMEM((nh, RET_HEAD_DIM, RET_HEAD_DIM), F32), pltpu.VMEM((nh, rows, rows), F32)],
    )
    return pl.pallas_call(
        functools.partial(_retention_body, rows=rows, n_heads=nh),
        out_shape=[jax.ShapeDtypeStruct((q.shape[0], heads * RET_HEAD_DIM), BF16),
                   jax.ShapeDtypeStruct((n_batch, heads, RET_HEAD_DIM, RET_HEAD_DIM), F32)],
        grid_spec=grid_spec,
        input_output_aliases={len(args): 0} if extra_args else {},
        compiler_params=_params("arbitrary", "arbitrary", "arbitrary"),
        name="retention",
    )(*args, *extra_args)


def _neg_suffix(n):
    j = lax.broadcasted_iota(jnp.int32, (2 * n, n), 0) & (n - 1)
    s = lax.broadcasted_iota(jnp.int32, (2 * n, n), 1)
    return jnp.where(j >= s, -1.0, 0.0).astype(BF16)


def _lanes_to(x, n):
    return x[:, :n] if n <= LANES else jnp.concatenate([x] * (n // LANES), axis=1)


def _suffix_sums(fail, neg_ones):
    tk = fail.shape[1]
    hi, lo = _split_bf16(fail)
    if tk % LANES == 0:
        return jnp.dot(jnp.concatenate([hi, lo], axis=1), neg_ones, preferred_element_type=F32)
    return (jnp.dot(hi, neg_ones[:tk], preferred_element_type=F32)
            + jnp.dot(lo, neg_ones[:tk], preferred_element_type=F32))


def _sb_tiles(qs, kbs, vbs, neg_ones, carry, acc, masked):
    heads = range(len(qs))
    tk = kbs[0].shape[0]
    zs = [lax.dot_general(qs[h], kbs[h], _NT, preferred_element_type=F32) for h in heads]
    fails = [jnp.maximum(z, 0.0) + jnp.log2(1.0 + jnp.exp2(-jnp.abs(z))) for z in zs]
    if masked:
        t = lax.broadcasted_iota(jnp.int32, zs[0].shape, 0)
        s = lax.broadcasted_iota(jnp.int32, zs[0].shape, 1)
        valid = s < t
        fails = [jnp.where(valid, f, 0.0) for f in fails]
    suffixes = [_suffix_sums(f, neg_ones) for f in fails]
    probs = [jnp.exp2(zs[h] + suffixes[h] + _lanes_to(carry[h], tk)) for h in heads]
    if masked:
        probs = [jnp.where(valid, p, 0.0) for p in probs]
    for h in heads:
        acc[h] += jnp.dot(probs[h].astype(BF16), vbs[h], preferred_element_type=F32)
        carry[h] += jnp.broadcast_to(suffixes[h][:, :1], carry.shape[1:])


def _sb_body(q_ref, kd_ref, vd_ref, kp_ref, vp_ref, *rest, tk, n_heads, past_per_step, past_static):
    o_ref, carry, acc = rest[-3:]
    tq = q_ref.shape[0]
    n_past = past_static if past_static is not None else pl.program_id(1) * past_per_step
    cols = lambda h: slice(h * SB_HEAD_DIM, (h + 1) * SB_HEAD_DIM)
    heads = range(n_heads)
    past = (lambda ref, h, rows: ref[rows, h, :]) if len(kp_ref.shape) == 3 else (
        lambda ref, h, rows: ref[rows, cols(h)])
    carry[...] = jnp.zeros(carry.shape, F32)
    acc[...] = jnp.zeros(acc.shape, F32)
    queries = lambda: [q_ref[:, cols(h)] for h in heads]
    _sb_tiles(queries(), [kd_ref[:, cols(h)].astype(BF16) for h in heads],
              [vd_ref[:, cols(h)].astype(BF16) for h in heads], _neg_suffix(tq), carry, acc, True)
    neg_ones = _neg_suffix(tk)

    alive = lambda: jnp.max(carry[...]) > SB_DEAD_LOG2

    def step(state):
        it = state[0]
        rows = pl.ds(pl.multiple_of((n_past - 1 - it) * tk, tk), tk)
        _sb_tiles(queries(), [past(kp_ref, h, rows).astype(BF16) for h in heads],
                  [past(vp_ref, h, rows).astype(BF16) for h in heads], neg_ones, carry, acc, False)
        return it + 1, alive()

    lax.while_loop(lambda state: (state[0] < n_past) & state[1], step, (jnp.int32(0), alive()))
    for h in range(n_heads):
        o_ref[:, cols(h)] = acc[h].astype(o_ref.dtype)


def _sb_scratch(n_heads, tq):
    return [pltpu.VMEM((n_heads, tq, LANES), F32), pltpu.VMEM((n_heads, tq, SB_HEAD_DIM), F32)]


def _sb_prompt(q, k, v, n_tok):
    heads = q.shape[1] // SB_HEAD_DIM
    tq = _tile(n_tok, SB_BLOCK)
    nh = SB_HEADS_PER_STEP
    width = nh * SB_HEAD_DIM
    blk = pl.BlockSpec((tq, width), lambda h, i: (i, h))
    full = pl.BlockSpec((n_tok, width), lambda h, i: (0, h))
    return pl.pallas_call(
        functools.partial(_sb_body, tk=tq, n_heads=nh, past_per_step=1, past_static=None),
        out_shape=jax.ShapeDtypeStruct((q.shape[0], heads * SB_HEAD_DIM), BF16),
        grid=(heads // nh, n_tok // tq),
        in_specs=[blk, blk, blk, full, full],
        out_specs=blk,
        scratch_shapes=_sb_scratch(nh, tq),
        compiler_params=_params("parallel", "arbitrary"),
        name="stick_breaking_prompt",
    )(q, k, v, k, v)


def _sb_sample(q, k, v, cache_k, cache_v, row_off, n_new, out_buf):
    heads = q.shape[1] // SB_HEAD_DIM
    _, n_batch, past, _, _ = cache_k.shape
    tk = _tile(past, SB_BLOCK)
    assert row_off % n_new == 0
    ro = row_off // n_new
    nh = min(F32_SUBLANES, heads)
    assert heads % nh == 0 and (nh == F32_SUBLANES or nh == heads)
    blk = pl.BlockSpec((n_new, nh * SB_HEAD_DIM), lambda b, h: (ro + b, h))
    full = pl.BlockSpec((None, None, past, nh, SB_HEAD_DIM), lambda b, h: (0, b, 0, h, 0))
    extra_args, extra_specs = _into(out_buf)
    return pl.pallas_call(
        functools.partial(_sb_body, tk=tk, n_heads=nh, past_per_step=0, past_static=past // tk),
        out_shape=jax.ShapeDtypeStruct(out_buf.shape, out_buf.dtype),
        grid=(n_batch, heads // nh),
        in_specs=[blk, blk, blk, full, full] + extra_specs,
        out_specs=blk,
        scratch_shapes=_sb_scratch(nh, n_new),
        input_output_aliases={5: 0},
        compiler_params=_params("parallel", "arbitrary"),
        name="stick_breaking_sample",
    )(q, k, v, cache_k, cache_v, *extra_args)


def _cross_body(q_ref, k_ref, v_ref, *rest, scale, steps):
    o_ref, kb, vb = rest[-3:]
    dh = q_ref.shape[1] // MEM_HEADS
    heads = range(MEM_HEADS)
    cols = lambda h: slice(h * dh, (h + 1) * dh)
    head_of = (lambda ref, h: ref[:, h, :]) if len(k_ref.shape) == 3 else (lambda ref, h: ref[:, cols(h)])

    @pl.when(pl.program_id(0) % steps == 0)
    def _():
        for h in heads:
            kb[:, cols(h)] = head_of(k_ref, h).astype(BF16)
            vb[:, cols(h)] = head_of(v_ref, h).astype(BF16)

    scores = [lax.dot_general(q_ref[:, cols(h)], kb[:, cols(h)], _NT, preferred_element_type=F32) * scale
              for h in heads]
    exps = [jnp.exp(s - jnp.max(s, axis=-1, keepdims=True)) for s in scores]
    probs = [e / jnp.sum(e, axis=-1, keepdims=True) for e in exps]
    for h in heads:
        o_ref[:, cols(h)] = jnp.dot(probs[h].astype(BF16), vb[:, cols(h)],
                                    preferred_element_type=F32).astype(o_ref.dtype)


def _cross_attend(q, mem_k, mem_v, row_off, rows_per_seq, out_buf=None):
    d = q.shape[1]
    dh = d // MEM_HEADS
    tm = _tile(rows_per_seq, ROW_TILE)
    steps = rows_per_seq // tm
    assert row_off % tm == 0
    ro = row_off // tm
    extra_args, extra_specs = _into(out_buf)
    if mem_k.ndim == 3:
        n_seq, n_mem, _ = mem_k.shape
        mem = pl.BlockSpec((None, n_mem, d), lambda i: (i // steps, 0, 0))
    else:
        _, n_seq, n_mem, _, _ = mem_k.shape
        mem = pl.BlockSpec((None, None, n_mem, MEM_HEADS, dh), lambda i: (0, i // steps, 0, 0, 0))
    blk = pl.BlockSpec((tm, d), lambda i: (ro + i, 0))
    return pl.pallas_call(
        functools.partial(_cross_body, scale=dh ** -0.5, steps=steps),
        out_shape=jax.ShapeDtypeStruct((q.shape[0], d), BF16),
        grid=(n_seq * steps,),
        in_specs=[blk, mem, mem] + extra_specs,
        out_specs=blk,
        scratch_shapes=[pltpu.VMEM((n_mem, d), BF16), pltpu.VMEM((n_mem, d), BF16)],
        input_output_aliases={3: 0} if extra_args else {},
        compiler_params=_params("arbitrary"),
        name="cross_attention",
    )(q, mem_k, mem_v, *extra_args)


def _split_bf16(x):
    hi = x.astype(BF16)
    return hi, (x - hi.astype(F32)).astype(BF16)


def _router_body(x_ref, g_ref, wr_ref, br_ref, h_ref, ids_ref, wts_ref):
    h = _rmsnorm_rows(x_ref[...], g_ref[...])
    h_ref[...] = _pack_bf16_pairs(h)
    h_hi, h_lo = _split_bf16(h)
    w_hi, w_lo = _split_bf16(wr_ref[...])
    logits = (jnp.dot(h_hi, w_hi, preferred_element_type=F32)
              + jnp.dot(h_hi, w_lo, preferred_element_type=F32)
              + jnp.dot(h_lo, w_hi, preferred_element_type=F32)) + br_ref[...]
    lane = lax.broadcasted_iota(jnp.int32, logits.shape, 1).astype(F32)
    big = float(LANES)
    neg = -jnp.inf

    def first_max(vals):
        top = jnp.max(vals, axis=-1, keepdims=True)
        return top, jnp.min(jnp.where(vals == top, lane, big), axis=-1, keepdims=True)

    is_group = lane < N_GROUPS
    g_max, g_top = first_max(jnp.where(is_group, logits, neg))
    g_w = 1.0 / jnp.sum(jnp.where(is_group, jnp.exp(logits - g_max), 0.0), axis=-1, keepdims=True)
    first = N_GROUPS + EXPERTS_PER_GROUP * g_top
    cand = jnp.where((lane >= first) & (lane < first + EXPERTS_PER_GROUP), logits, neg)
    v1, i1 = first_max(cand)
    v2, i2 = first_max(jnp.where(lane == i1, neg, cand))
    t = jnp.exp(v2 - v1)
    w1 = g_w / (1.0 + t)
    w2 = g_w * t / (1.0 + t)
    ids = jnp.where(lane == 0.0, i1 - N_GROUPS, jnp.where(lane == 1.0, i2 - N_GROUPS, 0.0))
    ids_ref[...] = ids.astype(jnp.int32)
    wts_ref[...] = jnp.where(lane == 0.0, w1, jnp.where(lane == 1.0, w2, 0.0))


def _norm_and_route(x, g, wg, bg, we, be):
    m, d = x.shape
    tm = _tile(m, NORM_ROWS)
    pad = LANES - N_GROUPS - N_EXPERTS
    wr = jnp.concatenate([wg, we, jnp.zeros((d, pad), F32)], axis=1)
    br = jnp.concatenate([bg, be, jnp.zeros((pad,), F32)]).reshape(1, LANES)
    row = lambda c: pl.BlockSpec((tm, c), lambda i: (i, 0))
    return pl.pallas_call(
        _router_body,
        out_shape=[jax.ShapeDtypeStruct((m, d // 2), jnp.int32),
                   jax.ShapeDtypeStruct((m, LANES), jnp.int32),
                   jax.ShapeDtypeStruct((m, LANES), F32)],
        grid=(m // tm,),
        in_specs=[row(d), pl.BlockSpec((1, d), lambda i: (0, 0)),
                  pl.BlockSpec((d, LANES), lambda i: (0, 0)), pl.BlockSpec((1, LANES), lambda i: (0, 0))],
        out_specs=[row(d // 2), row(LANES), row(LANES)],
        compiler_params=_params("parallel"),
        name="norm_route",
    )(x, g.reshape(1, d), wr, br)


GATHER_UNROLL = 32


def _start_row_gather(idx_ref, first, stride, src_ref, buf, sem, n_rows):
    def one(r, _):
        t = idx_ref[first + r * stride]
        pltpu.make_async_copy(src_ref.at[pl.ds(t, 1)], buf.at[pl.ds(r, 1)], sem).start()
        return 0

    def group(g, _):
        for u in range(GATHER_UNROLL):
            one(g * GATHER_UNROLL + u, 0)
        return 0

    if isinstance(n_rows, int):
        for r in range(n_rows):
            one(r, 0)
        return
    groups = n_rows // GATHER_UNROLL
    lax.fori_loop(0, groups, group, 0)
    lax.fori_loop(groups * GATHER_UNROLL, n_rows, one, 0)


def _wait_row_gather(src_ref, buf, sem, n_rows):
    pltpu.make_async_copy(src_ref.at[pl.ds(0, n_rows)], buf.at[pl.ds(0, n_rows)], sem).wait()


def _pack_bf16_pairs(x):
    half = x.shape[1] // 2
    hi = lax.bitcast_convert_type(x[:, :half].astype(BF16).astype(F32), jnp.int32)
    lo = lax.bitcast_convert_type(x[:, half:].astype(BF16).astype(F32), jnp.int32)
    return hi | lax.shift_right_logical(lo, jnp.int32(16))


def _unpack_bf16_pairs(p):
    hi = lax.bitcast_convert_type(p & jnp.int32(-65536), F32)
    lo = lax.bitcast_convert_type(lax.shift_left(p, jnp.int32(16)), F32)
    return hi, lo


def _dispatch_body(idx_ref, used_ref, cnt_ref, src_ref, o_ref, buf, sem):
    b = pl.program_id(0)
    rows = buf.shape[1]

    copied = lambda blk: pl.multiple_of(pl.cdiv(cnt_ref[blk], GATHER_UNROLL) * GATHER_UNROLL, GATHER_UNROLL)

    def start(blk):
        _start_row_gather(idx_ref, blk * rows, 1, src_ref, buf.at[blk % 2], sem.at[blk % 2], copied(blk))

    @pl.when(b == 0)
    def _():
        buf[...] = jnp.zeros(buf.shape, buf.dtype)
        start(0)

    @pl.when(b + 1 < used_ref[0])
    def _():
        start(b + 1)

    @pl.when(b < used_ref[0])
    def _():
        slot = b % 2
        cnt = cnt_ref[b]
        _wait_row_gather(src_ref, buf.at[slot], sem.at[slot], copied(b))
        half = buf.shape[2]
        row = lax.broadcasted_iota(jnp.int32, (rows, 1), 0)
        hi, lo = _unpack_bf16_pairs(jnp.where(row < cnt, buf[slot], 0))
        o_ref[:, :half] = hi.astype(o_ref.dtype)
        o_ref[:, half:] = lo.astype(o_ref.dtype)


def _dispatch_rows(src, row_tok, blk_cnt, n_used):
    n_rows = row_tok.shape[0]
    half = src.shape[1]
    grid_spec = pltpu.PrefetchScalarGridSpec(
        num_scalar_prefetch=3,
        grid=(n_rows // MOE_ROWS,),
        in_specs=[pl.BlockSpec(memory_space=pl.ANY)],
        out_specs=pl.BlockSpec((MOE_ROWS, 2 * half), lambda b, idx, used, cnt: (b, 0)),
        scratch_shapes=[pltpu.VMEM((2, MOE_ROWS, half), src.dtype), pltpu.SemaphoreType.DMA((2,))],
    )
    return pl.pallas_call(
        _dispatch_body,
        out_shape=jax.ShapeDtypeStruct((n_rows, 2 * half), BF16),
        grid_spec=grid_spec,
        compiler_params=_params("arbitrary"),
        name="dispatch_rows",
    )(row_tok, n_used, blk_cnt, src)


def _expert_up_body(be_ref, used_ref, x_ref, w1_ref, w3_ref, h_ref):
    @pl.when(pl.program_id(1) < used_ref[0])
    def _():
        x = x_ref[...]
        a = jnp.dot(x, w1_ref[...].astype(BF16), preferred_element_type=F32)
        g = jnp.dot(x, w3_ref[...].astype(BF16), preferred_element_type=F32)
        h_ref[...] = (a * _sigmoid(a) * g).astype(h_ref.dtype)


def _expert_down_body(be_ref, used_ref, h_ref, w2_ref, y_ref):
    @pl.when(pl.program_id(1) < used_ref[0])
    def _():
        y_ref[...] = _pack_bf16_pairs(jnp.dot(h_ref[...], w2_ref[...].astype(BF16),
                                              preferred_element_type=F32))


def _expert_ffn(xs, blk_exp, n_used, w1, w3, w2):
    n_rows, d = xs.shape
    f = w1.shape[2]
    n_blocks = n_rows // MOE_ROWS
    tf = _tile(f, EXPERT_HIDDEN_TILE)
    tn = _tile(d, EXPERT_OUT_TILE)
    last = lambda b, used: jnp.minimum(b, used[0] - 1)
    up_spec = pltpu.PrefetchScalarGridSpec(
        num_scalar_prefetch=2,
        grid=(f // tf, n_blocks),
        in_specs=[pl.BlockSpec((MOE_ROWS, d), lambda j, b, be, us: (last(b, us), 0)),
                  pl.BlockSpec((None, d, tf), lambda j, b, be, us: (be[b], 0, j)),
                  pl.BlockSpec((None, d, tf), lambda j, b, be, us: (be[b], 0, j))],
        out_specs=pl.BlockSpec((MOE_ROWS, tf), lambda j, b, be, us: (b, j)),
    )
    hidden = pl.pallas_call(
        _expert_up_body,
        out_shape=jax.ShapeDtypeStruct((n_rows, f), BF16),
        grid_spec=up_spec,
        compiler_params=_params("arbitrary", "arbitrary"),
        name="expert_up",
    )(blk_exp, n_used, xs, w1, w3)
    down_spec = pltpu.PrefetchScalarGridSpec(
        num_scalar_prefetch=2,
        grid=(d // tn, n_blocks),
        in_specs=[pl.BlockSpec((MOE_ROWS, f), lambda j, b, be, us: (last(b, us), 0)),
                  pl.BlockSpec((None, f, tn), lambda j, b, be, us: (be[b], 0, j))],
        out_specs=pl.BlockSpec((MOE_ROWS, tn // 2), lambda j, b, be, us: (b, j)),
    )
    return pl.pallas_call(
        _expert_down_body,
        out_shape=jax.ShapeDtypeStruct((n_rows, d // 2), jnp.int32),
        grid_spec=down_spec,
        compiler_params=_params("arbitrary", "arbitrary"),
        name="expert_down",
    )(blk_exp, n_used, hidden, w2)


def _route_layout(ids, n_tok):
    n_assign = n_tok * TOP_K
    flat_e = ids.reshape(n_assign)
    flat_t = jnp.repeat(jnp.arange(n_tok, dtype=jnp.int32), TOP_K)
    onehot = (flat_e[:, None] == jnp.arange(N_EXPERTS, dtype=jnp.int32)[None, :]).astype(jnp.int32)
    running = jnp.cumsum(onehot, axis=0)
    rank = jnp.take_along_axis(running, flat_e[:, None], axis=1)[:, 0] - 1
    counts = running[-1]
    padded = (counts + MOE_ROWS - 1) // MOE_ROWS * MOE_ROWS
    pad_end = jnp.cumsum(padded)
    dest = (pad_end - padded)[flat_e] + rank
    n_blocks = -(-n_assign // MOE_ROWS) + N_EXPERTS
    n_rows = n_blocks * MOE_ROWS
    row_tok = jnp.zeros((n_rows,), jnp.int32).at[dest].set(flat_t)
    blk = jnp.arange(n_blocks, dtype=jnp.int32)
    n_used = (pad_end[-1] // MOE_ROWS).astype(jnp.int32)
    blk_exp = jnp.searchsorted(pad_end, jnp.minimum(blk, n_used - 1) * MOE_ROWS, side='right').astype(jnp.int32)
    blk_exp = jnp.minimum(blk_exp, N_EXPERTS - 1)
    into_range = blk * MOE_ROWS - (pad_end - padded)[blk_exp]
    blk_cnt = jnp.where(blk < n_used, jnp.clip(counts[blk_exp] - into_range, 0, MOE_ROWS), 0).astype(jnp.int32)
    return row_tok, blk_cnt, dest.astype(jnp.int32), blk_exp, n_used.reshape(1)


def _final_body(dest_ref, x_ref, w_ref, y_ref, g_ref, op_ref, os_ref, buf, sem, *, prompt_tiles, pack_tile):
    i = pl.program_id(0)
    tm = buf.shape[2]

    def start(blk):
        for k in range(TOP_K):
            _start_row_gather(dest_ref, blk * tm * TOP_K + k, TOP_K, y_ref, buf.at[blk % 2, k], sem.at[blk % 2, k],
                              tm)

    @pl.when(i == 0)
    def _():
        start(0)

    @pl.when(i + 1 < pl.num_programs(0))
    def _():
        start(i + 1)

    slot = i % 2
    for k in range(TOP_K):
        _wait_row_gather(y_ref, buf.at[slot, k], sem.at[slot, k], tm)
    w = w_ref[...]
    pieces = []
    for j in range(buf.shape[3] // pack_tile):
        words = slice(j * pack_tile, (j + 1) * pack_tile)
        hi0, lo0 = _unpack_bf16_pairs(buf[slot, 0, :, words])
        hi1, lo1 = _unpack_bf16_pairs(buf[slot, 1, :, words])
        pieces += [hi0 * w[:, 0:1] + hi1 * w[:, 1:2], lo0 * w[:, 0:1] + lo1 * w[:, 1:2]]
    y = _rmsnorm_rows(x_ref[...] + jnp.concatenate(pieces, axis=1), g_ref[...])

    @pl.when(i < prompt_tiles)
    def _():
        op_ref[...] = y

    @pl.when(i >= prompt_tiles)
    def _():
        os_ref[...] = y


def _combine_and_norm(x, wts, y_rows, dest, g, n_prompt):
    m, d = x.shape
    tm = _tile(m - n_prompt, _tile(n_prompt, 256))
    assert n_prompt % tm == 0
    pt = n_prompt // tm
    grid_spec = pltpu.PrefetchScalarGridSpec(
        num_scalar_prefetch=1,
        grid=(m // tm,),
        in_specs=[pl.BlockSpec((tm, d), lambda i, dest: (i, 0)),
                  pl.BlockSpec((tm, LANES), lambda i, dest: (i, 0)),
                  pl.BlockSpec(memory_space=pl.ANY),
                  pl.BlockSpec((1, d), lambda i, dest: (0, 0))],
        out_specs=[pl.BlockSpec((tm, d), lambda i, dest: (jnp.minimum(i, pt - 1), 0)),
                   pl.BlockSpec((tm, d), lambda i, dest: (jnp.maximum(i - pt, 0), 0))],
        scratch_shapes=[pltpu.VMEM((2, TOP_K, tm, d // 2), y_rows.dtype), pltpu.SemaphoreType.DMA((2, TOP_K))],
    )
    return pl.pallas_call(
        functools.partial(_final_body, prompt_tiles=pt, pack_tile=_tile(d, EXPERT_OUT_TILE) // 2),
        out_shape=[jax.ShapeDtypeStruct((n_prompt, d), F32), jax.ShapeDtypeStruct((m - n_prompt, d), F32)],
        grid_spec=grid_spec,
        compiler_params=_params("arbitrary"),
        name="combine_norm",
    )(dest, x, wts, y_rows, g.reshape(1, d))


def _layer(xp, xs, n_seq, n_new, past_len, cache_sb_k, cache_sb_v, state_ret, cache_mem_k, cache_mem_v,
           mem_prompt, norm_mix, w_in, ret_gn, w_out, norm_cross, norm_mem, w_cq, w_ck, w_cv, w_co, norm_ffn,
           wg, bg, we, be, w1, w3, w2):
    n_prompt, d = xp.shape
    n_tok = n_prompt + xs.shape[0]
    d_ret = d // 2
    d_sb = d - d_ret
    ret_heads = d_ret // RET_HEAD_DIM
    sds = jax.ShapeDtypeStruct

    pos = np.concatenate([np.arange(n_prompt), np.tile(past_len + np.arange(n_new), n_seq)])
    half = RET_HEAD_DIM // 2
    inv = (ROPE_BASE ** (-np.arange(half, dtype=np.float32) / half)).astype(np.float32)
    ang = pos.astype(np.float32)[:, None] * inv[None, :]
    cos, sin = jnp.asarray(np.cos(ang), F32), jnp.asarray(np.sin(ang), F32)
    log_gamma = jnp.log1p(-jnp.exp2(-5.0 - jnp.arange(ret_heads, dtype=F32)))

    h = _rmsnorm_stacked(xp, xs, norm_mix, BF16)
    rot = [(cos, half), (sin, half)]
    bf = lambda n: [(sds((n_tok, n), BF16), None)]
    (rq,) = _matmul([h], w_in, 0, d_ret, _ep_rotary(1.0), rot, bf(d_ret), "in_proj_ret_q")
    (rk,) = _matmul([h], w_in, d_ret, d_ret, _ep_rotary(RET_HEAD_DIM ** -0.5), rot, bf(d_ret), "in_proj_ret_k")
    (rv,) = _matmul([h], w_in, 2 * d_ret, d_ret, _ep_store(1.0), [], bf(d_ret), "in_proj_ret_v")
    (rg,) = _matmul([h], w_in, 3 * d_ret, d_ret, _ep_store(1.0), [], [(sds((n_tok, d_ret), F32), None)],
                    "in_proj_ret_gate")
    (sq,) = _matmul([h], w_in, 4 * d_ret, d_sb, _ep_store(SB_HEAD_DIM ** -0.5 * LOG2_E), [], bf(d_sb),
                    "in_proj_sb_q")
    split = [(sds((n_prompt, d_sb), F32), PROMPT), (sds((n_tok - n_prompt, d_sb), F32), SAMPLE),
             (sds((n_tok, d_sb), BF16), None)]
    sk_p, sk_s, sk_b = _matmul([h], w_in, 4 * d_ret + d_sb, d_sb, _ep_store_split, [], split, "in_proj_sb_k",
                               n_prompt=n_prompt)
    sv_p, sv_s, sv_b = _matmul([h], w_in, 4 * d_ret + 2 * d_sb, d_sb, _ep_store_split, [], split,
                               "in_proj_sb_v", n_prompt=n_prompt)

    rows_p = _tile(n_prompt, RET_BLOCK)
    zeros_state = jnp.zeros((1, ret_heads, RET_HEAD_DIM, RET_HEAD_DIM), F32)
    yr, state_p = _retention(log_gamma, rq, rk, rv, rg, ret_gn, zeros_state, 0, 1, n_prompt // rows_p, rows_p)
    yr, state_s = _retention(log_gamma, rq, rk, rv, rg, ret_gn, state_ret, n_prompt, n_seq, 1, n_new,
                             out_buf=yr)
    sb = _sb_prompt(sq, sk_b, sv_b, n_prompt)
    sb = _sb_sample(sq, sk_b, sv_b, cache_sb_k, cache_sb_v, n_prompt, n_new, sb)
    res = [(sds((n_tok, d), F32), None)]
    (x,) = _matmul([yr, sb], w_out, 0, d, _ep_residual_split, [(xp, PROMPT), (xs, SAMPLE)], res, "out_proj",
                   n_prompt=n_prompt)

    n_mem = mem_prompt.shape[0]
    m = _rmsnorm(mem_prompt, norm_mem, BF16)
    mem_out = [(sds((n_mem, d), F32), None)]
    (mk,) = _matmul([m], w_ck, 0, d, _ep_store(1.0), [], mem_out, "mem_k")
    (mv,) = _matmul([m], w_cv, 0, d, _ep_store(1.0), [], mem_out, "mem_v")
    h = _rmsnorm(x, norm_cross, BF16)
    (cq,) = _matmul([h], w_cq, 0, d, _ep_store(1.0), [], bf(d), "cross_q")
    co = _cross_attend(cq, mk.reshape(1, n_mem, d), mv.reshape(1, n_mem, d), 0, n_prompt)
    co = _cross_attend(cq, cache_mem_k, cache_mem_v, n_prompt, n_new, out_buf=co)
    (x,) = _matmul([co], w_co, 0, d, _ep_residual, [(x, None)], res, "cross_out")

    hn, ids, wts = _norm_and_route(x, norm_ffn, wg, bg, we, be)
    row_tok, blk_cnt, dest, blk_exp, n_used = _route_layout(ids[:, :TOP_K], n_tok)
    x_rows = _dispatch_rows(hn, row_tok, blk_cnt, n_used)
    y_rows = _expert_ffn(x_rows, blk_exp, n_used, w1, w3, w2)
    return x, wts, y_rows, dest, (sk_p, sv_p, sk_s, sv_s, state_p, state_s, mk, mv)


def kernel(x_prompt, x_sample, cache_sb_k, cache_sb_v, state_ret, cache_mem_k, cache_mem_v, mem_prompt, norm_mix, w_in, ret_gn, w_out, norm_cross, norm_mem, w_cq, w_ck, w_cv, w_co, norm_ffn, router_group_w, router_group_b, router_expert_w, router_expert_b, expert_w1, expert_w3, expert_w2, norm_final):
    batch, n_prompt, d = x_prompt.shape
    n_seq, n_new, _ = x_sample.shape
    depth, _, past_len, sb_heads, _ = cache_sb_k.shape
    n_mem = mem_prompt.shape[1]
    assert batch == 1 and depth == 1
    ret_heads = (d // 2) // RET_HEAD_DIM
    x, wts, y_rows, dest, (sk_p, sv_p, sk_s, sv_s, state_p, state_s, mk, mv) = _layer(
        x_prompt.reshape(n_prompt, d), x_sample.reshape(n_seq * n_new, d), n_seq, n_new, past_len,
        cache_sb_k, cache_sb_v, state_ret[0], cache_mem_k, cache_mem_v, mem_prompt[0], norm_mix[0], w_in,
        ret_gn[0], w_out, norm_cross[0], norm_mem[0], w_cq, w_ck, w_cv, w_co, norm_ffn[0], router_group_w[0],
        router_group_b[0], router_expert_w[0], router_expert_b[0], expert_w1[0], expert_w3[0], expert_w2[0])
    y_prompt, y_sample = _combine_and_norm(x, wts, y_rows, dest, norm_final, n_prompt)
    y_prompt = y_prompt.reshape(1, n_prompt, d)
    y_sample = y_sample.reshape(n_seq, n_new, d)
    sb_shape_p = (1, 1, n_prompt, sb_heads, SB_HEAD_DIM)
    sb_shape_s = (1, n_seq, n_new, sb_heads, SB_HEAD_DIM)
    mem_shape = (1, 1, n_mem, MEM_HEADS, d // MEM_HEADS)
    return (y_prompt, y_sample,
            sk_p.reshape(sb_shape_p), sv_p.reshape(sb_shape_p),
            state_p.reshape(1, 1, ret_heads, RET_HEAD_DIM, RET_HEAD_DIM),
            mk.reshape(mem_shape), mv.reshape(mem_shape),
            sk_s.reshape(sb_shape_s), sv_s.reshape(sb_shape_s),
            state_s.reshape(1, n_seq, ret_heads, RET_HEAD_DIM, RET_HEAD_DIM))
```

```python
import functools

import jax
import jax.numpy as jnp
import numpy as np
from jax import lax
from jax.experimental import pallas as pl
from jax.experimental.pallas import tpu as pltpu

BF16 = jnp.bfloat16
F32 = jnp.float32

EPS = 1e-6
CHUNK = 64
RET_HEAD_DIM = 256
SB_HEAD_DIM = 128
MEM_HEADS = 4
N_GROUPS = 4
EXPERTS_PER_GROUP = 8
N_EXPERTS = N_GROUPS * EXPERTS_PER_GROUP
TOP_K = 2
ROPE_BASE = 10000.0

LANES = 128
F32_SUBLANES = 8
VMEM_LIMIT_BYTES = 60 * 1024 * 1024
MATMUL_VMEM_BUDGET = 56 * 1024 * 1024
ROW_TILE = 512
NORM_ROWS = 512
COL_TILE = 1024
EXPERT_HIDDEN_TILE = 512
RET_BLOCK = 256
RET_HEADS_PER_STEP = 4
SB_BLOCK = 256
SB_HEADS_PER_STEP = 4
LOG2_E = 1.4426950408889634
SB_DEAD_LOG2 = -64.0
MOE_ROWS = 512
EXPERT_OUT_TILE = 4096

PROMPT = "prompt"
SAMPLE = "sample"

_NT = (((1,), (1,)), ((), ()))
_TN = (((0,), (0,)), ((), ()))


def _tile(n, pref):
    t = min(pref, n)
    while n % t:
        t //= 2
    return t


def _params(*sem):
    return pltpu.CompilerParams(dimension_semantics=sem, vmem_limit_bytes=VMEM_LIMIT_BYTES)


def _sigmoid(x):
    return 1.0 / (1.0 + jnp.exp(-x))


def _rmsnorm_rows(x, g):
    ms = jnp.mean(x * x, axis=-1, keepdims=True)
    return x * lax.rsqrt(ms + EPS) * g


def _rmsnorm_body(x_ref, g_ref, o_ref):
    o_ref[...] = _rmsnorm_rows(x_ref[...], g_ref[...]).astype(o_ref.dtype)


def _rmsnorm(x, g, out_dtype):
    m, d = x.shape
    tm = _tile(m, NORM_ROWS)
    return pl.pallas_call(
        _rmsnorm_body,
        out_shape=jax.ShapeDtypeStruct((m, d), out_dtype),
        grid=(m // tm,),
        in_specs=[pl.BlockSpec((tm, d), lambda i: (i, 0)), pl.BlockSpec((1, d), lambda i: (0, 0))],
        out_specs=pl.BlockSpec((tm, d), lambda i: (i, 0)),
        compiler_params=_params("parallel"),
        name="rmsnorm",
    )(x, g.reshape(1, d))


def _rmsnorm_stacked_body(xp_ref, xs_ref, g_ref, o_ref, *, prompt_tiles):
    is_prompt = pl.program_id(0) < prompt_tiles

    @pl.when(is_prompt)
    def _():
        o_ref[...] = _rmsnorm_rows(xp_ref[...], g_ref[...]).astype(o_ref.dtype)

    @pl.when(jnp.logical_not(is_prompt))
    def _():
        o_ref[...] = _rmsnorm_rows(xs_ref[...], g_ref[...]).astype(o_ref.dtype)


def _rmsnorm_stacked(xp, xs, g, out_dtype):
    (n_p, d), n_s = xp.shape, xs.shape[0]
    tm = _tile(n_s, _tile(n_p, NORM_ROWS))
    pt = n_p // tm
    return pl.pallas_call(
        functools.partial(_rmsnorm_stacked_body, prompt_tiles=pt),
        out_shape=jax.ShapeDtypeStruct((n_p + n_s, d), out_dtype),
        grid=((n_p + n_s) // tm,),
        in_specs=[pl.BlockSpec((tm, d), lambda i: (jnp.minimum(i, pt - 1), 0)),
                  pl.BlockSpec((tm, d), lambda i: (jnp.maximum(i - pt, 0), 0)),
                  pl.BlockSpec((1, d), lambda i: (0, 0))],
        out_specs=pl.BlockSpec((tm, d), lambda i: (i, 0)),
        compiler_params=_params("arbitrary"),
        name="rmsnorm_stacked",
    )(xp, xs, g.reshape(1, d))


def _matmul_body(*refs, n_lhs, n_extra, n_out, epilogue):
    x_refs = refs[:n_lhs]
    w_refs = refs[n_lhs:2 * n_lhs]
    extra = refs[2 * n_lhs:2 * n_lhs + n_extra]
    outs = refs[2 * n_lhs + n_extra:2 * n_lhs + n_extra + n_out]
    wb = refs[2 * n_lhs + n_extra + n_out:]

    @pl.when(pl.program_id(1) == 0)
    def _():
        for p in range(n_lhs):
            wb[p][...] = w_refs[p][...].astype(BF16)

    acc = jnp.dot(x_refs[0][...], wb[0][...], preferred_element_type=F32)
    for p in range(1, n_lhs):
        acc += jnp.dot(x_refs[p][...], wb[p][...], preferred_element_type=F32)
    epilogue(acc, extra, outs)


def _matmul(xs, w, col_off, n_cols, epilogue, extras, outs, name, n_prompt=None):
    m = xs[0].shape[0]
    tm = _tile(m, ROW_TILE) if n_prompt is None else _tile(m - n_prompt, _tile(n_prompt, ROW_TILE))
    k_total = sum(x.shape[1] for x in xs)

    def vmem_bytes(tn):
        tiles = sum((tn if kind is None or isinstance(kind, str) else kind) * jnp.dtype(a.dtype).itemsize
                    for a, kind in list(extras) + list(outs))
        return k_total * tn * (2 * 4 + 2) + 2 * tm * k_total * 2 + 2 * tm * tiles

    tn = _tile(n_cols, COL_TILE)
    while vmem_bytes(tn) > MATMUL_VMEM_BUDGET and tn > LANES:
        tn //= 2
    assert col_off % tn == 0
    jo = col_off // tn
    pt = None if n_prompt is None else n_prompt // tm
    in_specs, args = [], []
    for x in xs:
        in_specs.append(pl.BlockSpec((tm, x.shape[1]), lambda j, i: (i, 0)))
        args.append(x)
    row = 0
    for x in xs:
        kp = x.shape[1]
        assert row % kp == 0
        in_specs.append(pl.BlockSpec((None, kp, tn), lambda j, i, r=row // kp: (0, r, j + jo)))
        args.append(w)
        row += kp
    assert row == w.shape[1]

    def spec(kind):
        if kind is None:
            return pl.BlockSpec((tm, tn), lambda j, i: (i, j))
        if kind == PROMPT:
            return pl.BlockSpec((tm, tn), lambda j, i: (jnp.minimum(i, pt - 1), j))
        if kind == SAMPLE:
            return pl.BlockSpec((tm, tn), lambda j, i: (jnp.maximum(i - pt, 0), j))
        return pl.BlockSpec((tm, kind), lambda j, i: (i, 0))

    for a, kind in extras:
        in_specs.append(spec(kind))
        args.append(a)
    body = functools.partial(_matmul_body, n_lhs=len(xs), n_extra=len(extras), n_out=len(outs),
                             epilogue=epilogue if n_prompt is None else functools.partial(epilogue, pt))
    res = pl.pallas_call(
        body,
        out_shape=[o for o, _ in outs],
        grid=(n_cols // tn, m // tm),
        in_specs=in_specs,
        out_specs=[spec(kind) for _, kind in outs],
        scratch_shapes=[pltpu.VMEM((x.shape[1], tn), BF16) for x in xs],
        compiler_params=_params("arbitrary", "arbitrary"),
        name=name,
    )(*args)
    return res


def _ep_store(scale):
    def ep(acc, extra, outs):
        for o in outs:
            o[...] = (acc * scale).astype(o.dtype) if scale != 1.0 else acc.astype(o.dtype)
    return ep


def _ep_store_split(prompt_tiles, acc, extra, outs):
    is_prompt = pl.program_id(1) < prompt_tiles

    @pl.when(is_prompt)
    def _():
        outs[0][...] = acc

    @pl.when(jnp.logical_not(is_prompt))
    def _():
        outs[1][...] = acc

    outs[2][...] = acc.astype(outs[2].dtype)


def _ep_residual(acc, extra, outs):
    outs[0][...] = extra[0][...] + acc


def _ep_residual_split(prompt_tiles, acc, extra, outs):
    res = jnp.where(pl.program_id(1) < prompt_tiles, extra[0][...], extra[1][...])
    outs[0][...] = res + acc


def _ep_rotary(scale):
    half = RET_HEAD_DIM // 2

    def ep(acc, extra, outs):
        cos = extra[0][...]
        sin = extra[1][...]
        for h in range(acc.shape[1] // RET_HEAD_DIM):
            lo = h * RET_HEAD_DIM
            x1 = acc[:, lo:lo + half]
            x2 = acc[:, lo + half:lo + RET_HEAD_DIM]
            outs[0][:, lo:lo + half] = ((x1 * cos - x2 * sin) * scale).astype(outs[0].dtype)
            outs[0][:, lo + half:lo + RET_HEAD_DIM] = ((x1 * sin + x2 * cos) * scale).astype(outs[0].dtype)
    return ep


def _retention_body(lg_ref, q_ref, k_ref, v_ref, g_ref, gn_ref, s0_ref, *rest, rows, n_heads):
    o_ref, sout_ref, s_scr, decay_scr = rest[-4:]
    heads = range(n_heads)
    cols = lambda h: slice(h * RET_HEAD_DIM, (h + 1) * RET_HEAD_DIM)
    lgs = [lg_ref[pl.program_id(1) * n_heads + h] for h in heads]

    @pl.when(pl.program_id(2) == 0)
    def _():
        s_scr[...] = s0_ref[...]
        ii = lax.broadcasted_iota(jnp.int32, (rows, rows), 0)
        jj = lax.broadcasted_iota(jnp.int32, (rows, rows), 1)
        shift = CHUNK.bit_length() - 1
        dist = jnp.abs(ii - jj).astype(F32)
        for h in heads:
            decay_scr[h] = jnp.where((jj >> shift) <= (ii >> shift), jnp.exp(lgs[h] * dist), 0.0)

    idx = lax.broadcasted_iota(jnp.int32, (rows, 1), 0).astype(F32)
    qs = [q_ref[:, cols(h)] for h in heads]
    ks = [k_ref[:, cols(h)] for h in heads]
    vs = [v_ref[:, cols(h)] for h in heads]
    scores = [lax.dot_general(qs[h], ks[h], _NT, preferred_element_type=F32) * decay_scr[h] for h in heads]
    outs = [jnp.dot(scores[h].astype(BF16), vs[h], preferred_element_type=F32) for h in heads]
    states = [s_scr[h] for h in heads]
    outs = [outs[h] + jnp.exp(lgs[h] * (idx + 1.0))
            * jnp.dot(qs[h], states[h].astype(BF16), preferred_element_type=F32) for h in heads]
    kds = [(ks[h].astype(F32) * jnp.exp(lgs[h] * (rows - 1.0 - idx))).astype(BF16) for h in heads]
    for h in heads:
        s_new = (jnp.exp(lgs[h] * rows) * states[h]
                 + lax.dot_general(kds[h], vs[h], _TN, preferred_element_type=F32))
        s_scr[h] = s_new
        sout_ref[h] = s_new
    for h in heads:
        mu = jnp.mean(outs[h], axis=-1, keepdims=True)
        cen = outs[h] - mu
        var = jnp.mean(cen * cen, axis=-1, keepdims=True)
        yr = cen * lax.rsqrt(var + EPS) * gn_ref[h]
        g = g_ref[:, cols(h)]
        o_ref[:, cols(h)] = (g * _sigmoid(g) * yr).astype(o_ref.dtype)


def _into(out_buf):
    if out_buf is None:
        return [], []
    return [out_buf], [pl.BlockSpec(memory_space=pl.ANY)]


def _retention(log_gamma, q, k, v, gate, gn, s0, row_off, n_batch, n_steps, rows, out_buf=None):
    heads = q.shape[1] // RET_HEAD_DIM
    nh = min(RET_HEADS_PER_STEP, heads)
    assert row_off % rows == 0 and rows % CHUNK == 0 and heads % nh == 0
    ro = row_off // rows
    tok = lambda b, h, c, lg: (ro + b * n_steps + c, h)
    blk = pl.BlockSpec((rows, nh * RET_HEAD_DIM), tok)
    st = pl.BlockSpec((None, nh, RET_HEAD_DIM, RET_HEAD_DIM), lambda b, h, c, lg: (b, h, 0, 0))
    args = [log_gamma, q, k, v, gate, gn.reshape(heads, 1, RET_HEAD_DIM), s0]
    extra_args, extra_specs = _into(out_buf)
    grid_spec = pltpu.PrefetchScalarGridSpec(
        num_scalar_prefetch=1,
        grid=(n_batch, heads // nh, n_steps),
        in_specs=[blk, blk, blk, blk,
                  pl.BlockSpec((nh, 1, RET_HEAD_DIM), lambda b, h, c, lg: (h, 0, 0)), st] + extra_specs,
        out_specs=[blk, st],
        scratch_shapes=[pltpu.VMEM((nh, RET_HEAD_DIM, RET_HEAD_DIM), F32), pltpu.VMEM((nh, rows, rows), F32)],
    )
    return pl.pallas_call(
        functools.partial(_retention_body, rows=rows, n_heads=nh),
        out_shape=[jax.ShapeDtypeStruct((q.shape[0], heads * RET_HEAD_DIM), BF16),
                   jax.ShapeDtypeStruct((n_batch, heads, RET_HEAD_DIM, RET_HEAD_DIM), F32)],
        grid_spec=grid_spec,
        input_output_aliases={len(args): 0} if extra_args else {},
        compiler_params=_params("arbitrary", "arbitrary", "arbitrary"),
        name="retention",
    )(*args, *extra_args)


def _neg_suffix(n):
    j = lax.broadcasted_iota(jnp.int32, (2 * n, n), 0) & (n - 1)
    s = lax.broadcasted_iota(jnp.int32, (2 * n, n), 1)
    return jnp.where(j >= s, -1.0, 0.0).astype(BF16)


def _lanes_to(x, n):
    return x[:, :n] if n <= LANES else jnp.concatenate([x] * (n // LANES), axis=1)


def _suffix_sums(fail, neg_ones):
    tk = fail.shape[1]
    hi, lo = _split_bf16(fail)
    if tk % LANES == 0:
        return jnp.dot(jnp.concatenate([hi, lo], axis=1), neg_ones, preferred_element_type=F32)
    return (jnp.dot(hi, neg_ones[:tk], preferred_element_type=F32)
            + jnp.dot(lo, neg_ones[:tk], preferred_element_type=F32))


def _sb_tiles(qs, kbs, vbs, neg_ones, carry, acc, masked):
    heads = range(len(qs))
    tk = kbs[0].shape[0]
    zs = [lax.dot_general(qs[h], kbs[h], _NT, preferred_element_type=F32) for h in heads]
    fails = [jnp.maximum(z, 0.0) + jnp.log2(1.0 + jnp.exp2(-jnp.abs(z))) for z in zs]
    if masked:
        t = lax.broadcasted_iota(jnp.int32, zs[0].shape, 0)
        s = lax.broadcasted_iota(jnp.int32, zs[0].shape, 1)
        valid = s < t
        fails = [jnp.where(valid, f, 0.0) for f in fails]
    suffixes = [_suffix_sums(f, neg_ones) for f in fails]
    probs = [jnp.exp2(zs[h] + suffixes[h] + _lanes_to(carry[h], tk)) for h in heads]
    if masked:
        probs = [jnp.where(valid, p, 0.0) for p in probs]
    for h in heads:
        acc[h] += jnp.dot(probs[h].astype(BF16), vbs[h], preferred_element_type=F32)
        carry[h] += jnp.broadcast_to(suffixes[h][:, :1], carry.shape[1:])


def _sb_body(q_ref, kd_ref, vd_ref, kp_ref, vp_ref, *rest, tk, n_heads, past_per_step, past_static):
    o_ref, carry, acc = rest[-3:]
    tq = q_ref.shape[0]
    n_past = past_static if past_static is not None else pl.program_id(1) * past_per_step
    cols = lambda h: slice(h * SB_HEAD_DIM, (h + 1) * SB_HEAD_DIM)
    heads = range(n_heads)
    past = (lambda ref, h, rows: ref[rows, h, :]) if len(kp_ref.shape) == 3 else (
        lambda ref, h, rows: ref[rows, cols(h)])
    carry[...] = jnp.zeros(carry.shape, F32)
    acc[...] = jnp.zeros(acc.shape, F32)
    queries = lambda: [q_ref[:, cols(h)] for h in heads]
    _sb_tiles(queries(), [kd_ref[:, cols(h)].astype(BF16) for h in heads],
              [vd_ref[:, cols(h)].astype(BF16) for h in heads], _neg_suffix(tq), carry, acc, True)
    neg_ones = _neg_suffix(tk)

    alive = lambda: jnp.max(carry[...]) > SB_DEAD_LOG2

    def step(state):
        it = state[0]
        rows = pl.ds(pl.multiple_of((n_past - 1 - it) * tk, tk), tk)
        _sb_tiles(queries(), [past(kp_ref, h, rows).astype(BF16) for h in heads],
                  [past(vp_ref, h, rows).astype(BF16) for h in heads], neg_ones, carry, acc, False)
        return it + 1, alive()

    lax.while_loop(lambda state: (state[0] < n_past) & state[1], step, (jnp.int32(0), alive()))
    for h in range(n_heads):
        o_ref[:, cols(h)] = acc[h].astype(o_ref.dtype)


def _sb_scratch(n_heads, tq):
    return [pltpu.VMEM((n_heads, tq, LANES), F32), pltpu.VMEM((n_heads, tq, SB_HEAD_DIM), F32)]


def _sb_prompt(q, k, v, n_tok):
    heads = q.shape[1] // SB_HEAD_DIM
    tq = _tile(n_tok, SB_BLOCK)
    nh = SB_HEADS_PER_STEP
    width = nh * SB_HEAD_DIM
    blk = pl.BlockSpec((tq, width), lambda h, i: (i, h))
    full = pl.BlockSpec((n_tok, width), lambda h, i: (0, h))
    return pl.pallas_call(
        functools.partial(_sb_body, tk=tq, n_heads=nh, past_per_step=1, past_static=None),
        out_shape=jax.ShapeDtypeStruct((q.shape[0], heads * SB_HEAD_DIM), BF16),
        grid=(heads // nh, n_tok // tq),
        in_specs=[blk, blk, blk, full, full],
        out_specs=blk,
        scratch_shapes=_sb_scratch(nh, tq),
        compiler_params=_params("parallel", "arbitrary"),
        name="stick_breaking_prompt",
    )(q, k, v, k, v)


def _sb_sample(q, k, v, cache_k, cache_v, row_off, n_new, out_buf):
    heads = q.shape[1] // SB_HEAD_DIM
    _, n_batch, past, _, _ = cache_k.shape
    tk = _tile(past, SB_BLOCK)
    assert row_off % n_new == 0
    ro = row_off // n_new
    nh = min(F32_SUBLANES, heads)
    assert heads % nh == 0 and (nh == F32_SUBLANES or nh == heads)
    blk = pl.BlockSpec((n_new, nh * SB_HEAD_DIM), lambda b, h: (ro + b, h))
    full = pl.BlockSpec((None, None, past, nh, SB_HEAD_DIM), lambda b, h: (0, b, 0, h, 0))
    extra_args, extra_specs = _into(out_buf)
    return pl.pallas_call(
        functools.partial(_sb_body, tk=tk, n_heads=nh, past_per_step=0, past_static=past // tk),
        out_shape=jax.ShapeDtypeStruct(out_buf.shape, out_buf.dtype),
        grid=(n_batch, heads // nh),
        in_specs=[blk, blk, blk, full, full] + extra_specs,
        out_specs=blk,
        scratch_shapes=_sb_scratch(nh, n_new),
        input_output_aliases={5: 0},
        compiler_params=_params("parallel", "arbitrary"),
        name="stick_breaking_sample",
    )(q, k, v, cache_k, cache_v, *extra_args)


def _cross_body(q_ref, k_ref, v_ref, *rest, scale, steps):
    o_ref, kb, vb = rest[-3:]
    dh = q_ref.shape[1] // MEM_HEADS
    heads = range(MEM_HEADS)
    cols = lambda h: slice(h * dh, (h + 1) * dh)
    head_of = (lambda ref, h: ref[:, h, :]) if len(k_ref.shape) == 3 else (lambda ref, h: ref[:, cols(h)])

    @pl.when(pl.program_id(0) % steps == 0)
    def _():
        for h in heads:
            kb[:, cols(h)] = head_of(k_ref, h).astype(BF16)
            vb[:, cols(h)] = head_of(v_ref, h).astype(BF16)

    scores = [lax.dot_general(q_ref[:, cols(h)], kb[:, cols(h)], _NT, preferred_element_type=F32) * scale
              for h in heads]
    exps = [jnp.exp(s - jnp.max(s, axis=-1, keepdims=True)) for s in scores]
    probs = [e / jnp.sum(e, axis=-1, keepdims=True) for e in exps]
    for h in heads:
        o_ref[:, cols(h)] = jnp.dot(probs[h].astype(BF16), vb[:, cols(h)],
                                    preferred_element_type=F32).astype(o_ref.dtype)


def _cross_attend(q, mem_k, mem_v, row_off, rows_per_seq, out_buf=None):
    d = q.shape[1]
    dh = d // MEM_HEADS
    tm = _tile(rows_per_seq, ROW_TILE)
    steps = rows_per_seq // tm
    assert row_off % tm == 0
    ro = row_off // tm
    extra_args, extra_specs = _into(out_buf)
    if mem_k.ndim == 3:
        n_seq, n_mem, _ = mem_k.shape
        mem = pl.BlockSpec((None, n_mem, d), lambda i: (i // steps, 0, 0))
    else:
        _, n_seq, n_mem, _, _ = mem_k.shape
        mem = pl.BlockSpec((None, None, n_mem, MEM_HEADS, dh), lambda i: (0, i // steps, 0, 0, 0))
    blk = pl.BlockSpec((tm, d), lambda i: (ro + i, 0))
    return pl.pallas_call(
        functools.partial(_cross_body, scale=dh ** -0.5, steps=steps),
        out_shape=jax.ShapeDtypeStruct((q.shape[0], d), BF16),
        grid=(n_seq * steps,),
        in_specs=[blk, mem, mem] + extra_specs,
        out_specs=blk,
        scratch_shapes=[pltpu.VMEM((n_mem, d), BF16), pltpu.VMEM((n_mem, d), BF16)],
        input_output_aliases={3: 0} if extra_args else {},
        compiler_params=_params("arbitrary"),
        name="cross_attention",
    )(q, mem_k, mem_v, *extra_args)


def _split_bf16(x):
    hi = x.astype(BF16)
    return hi, (x - hi.astype(F32)).astype(BF16)


def _router_body(x_ref, g_ref, wr_ref, br_ref, h_ref, ids_ref, wts_ref):
    h = _rmsnorm_rows(x_ref[...], g_ref[...])
    h_ref[...] = _pack_bf16_pairs(h)
    h_hi, h_lo = _split_bf16(h)
    w_hi, w_lo = _split_bf16(wr_ref[...])
    logits = (jnp.dot(h_hi, w_hi, preferred_element_type=F32)
              + jnp.dot(h_hi, w_lo, preferred_element_type=F32)
              + jnp.dot(h_lo, w_hi, preferred_element_type=F32)) + br_ref[...]
    lane = lax.broadcasted_iota(jnp.int32, logits.shape, 1).astype(F32)
    big = float(LANES)
    neg = -jnp.inf

    def first_max(vals):
        top = jnp.max(vals, axis=-1, keepdims=True)
        return top, jnp.min(jnp.where(vals == top, lane, big), axis=-1, keepdims=True)

    is_group = lane < N_GROUPS
    g_max, g_top = first_max(jnp.where(is_group, logits, neg))
    g_w = 1.0 / jnp.sum(jnp.where(is_group, jnp.exp(logits - g_max), 0.0), axis=-1, keepdims=True)
    first = N_GROUPS + EXPERTS_PER_GROUP * g_top
    cand = jnp.where((lane >= first) & (lane < first + EXPERTS_PER_GROUP), logits, neg)
    v1, i1 = first_max(cand)
    v2, i2 = first_max(jnp.where(lane == i1, neg, cand))
    t = jnp.exp(v2 - v1)
    w1 = g_w / (1.0 + t)
    w2 = g_w * t / (1.0 + t)
    ids = jnp.where(lane == 0.0, i1 - N_GROUPS, jnp.where(lane == 1.0, i2 - N_GROUPS, 0.0))
    ids_ref[...] = ids.astype(jnp.int32)
    wts_ref[...] = jnp.where(lane == 0.0, w1, jnp.where(lane == 1.0, w2, 0.0))


def _norm_and_route(x, g, wg, bg, we, be):
    m, d = x.shape
    tm = _tile(m, NORM_ROWS)
    pad = LANES - N_GROUPS - N_EXPERTS
    wr = jnp.concatenate([wg, we, jnp.zeros((d, pad), F32)], axis=1)
    br = jnp.concatenate([bg, be, jnp.zeros((pad,), F32)]).reshape(1, LANES)
    row = lambda c: pl.BlockSpec((tm, c), lambda i: (i, 0))
    return pl.pallas_call(
        _router_body,
        out_shape=[jax.ShapeDtypeStruct((m, d // 2), jnp.int32),
                   jax.ShapeDtypeStruct((m, LANES), jnp.int32),
                   jax.ShapeDtypeStruct((m, LANES), F32)],
        grid=(m // tm,),
        in_specs=[row(d), pl.BlockSpec((1, d), lambda i: (0, 0)),
                  pl.BlockSpec((d, LANES), lambda i: (0, 0)), pl.BlockSpec((1, LANES), lambda i: (0, 0))],
        out_specs=[row(d // 2), row(LANES), row(LANES)],
        compiler_params=_params("parallel"),
        name="norm_route",
    )(x, g.reshape(1, d), wr, br)


GATHER_UNROLL = 32


def _start_row_gather(idx_ref, first, stride, src_ref, buf, sem, n_rows):
    def one(r, _, priority=0):
        t = idx_ref[first + r * stride]
        pltpu.make_async_copy(src_ref.at[pl.ds(t, 1)], buf.at[pl.ds(r, 1)], sem).start(priority=priority)
        return 0

    def group(g, _):
        for u in range(GATHER_UNROLL):
            one(g * GATHER_UNROLL + u, 0, priority=u % 2)
        return 0

    if isinstance(n_rows, int):
        for r in range(n_rows):
            one(r, 0, priority=r % 2)
        return
    groups = n_rows // GATHER_UNROLL
    lax.fori_loop(0, groups, group, 0)
    lax.fori_loop(groups * GATHER_UNROLL, n_rows, one, 0)


def _wait_row_gather(src_ref, buf, sem, n_rows):
    pltpu.make_async_copy(src_ref.at[pl.ds(0, n_rows)], buf.at[pl.ds(0, n_rows)], sem).wait()


def _pack_bf16_pairs(x):
    half = x.shape[1] // 2
    hi = lax.bitcast_convert_type(x[:, :half].astype(BF16).astype(F32), jnp.int32)
    lo = lax.bitcast_convert_type(x[:, half:].astype(BF16).astype(F32), jnp.int32)
    return hi | lax.shift_right_logical(lo, jnp.int32(16))


def _unpack_bf16_pairs(p):
    hi = lax.bitcast_convert_type(p & jnp.int32(-65536), F32)
    lo = lax.bitcast_convert_type(lax.shift_left(p, jnp.int32(16)), F32)
    return hi, lo


def _dispatch_body(idx_ref, used_ref, cnt_ref, src_ref, o_ref, buf, sem):
    b = pl.program_id(0)
    rows = buf.shape[1]

    copied = lambda blk: pl.multiple_of(pl.cdiv(cnt_ref[blk], GATHER_UNROLL) * GATHER_UNROLL, GATHER_UNROLL)

    def start(blk):
        _start_row_gather(idx_ref, blk * rows, 1, src_ref, buf.at[blk % 2], sem.at[blk % 2], copied(blk))

    @pl.when(b == 0)
    def _():
        buf[...] = jnp.zeros(buf.shape, buf.dtype)
        start(0)

    @pl.when(b + 1 < used_ref[0])
    def _():
        start(b + 1)

    @pl.when(b < used_ref[0])
    def _():
        slot = b % 2
        cnt = cnt_ref[b]
        _wait_row_gather(src_ref, buf.at[slot], sem.at[slot], copied(b))
        half = buf.shape[2]
        row = lax.broadcasted_iota(jnp.int32, (rows, 1), 0)
        hi, lo = _unpack_bf16_pairs(jnp.where(row < cnt, buf[slot], 0))
        o_ref[:, :half] = hi.astype(o_ref.dtype)
        o_ref[:, half:] = lo.astype(o_ref.dtype)


def _dispatch_rows(src, row_tok, blk_cnt, n_used):
    n_rows = row_tok.shape[0]
    half = src.shape[1]
    grid_spec = pltpu.PrefetchScalarGridSpec(
        num_scalar_prefetch=3,
        grid=(n_rows // MOE_ROWS,),
        in_specs=[pl.BlockSpec(memory_space=pl.ANY)],
        out_specs=pl.BlockSpec((MOE_ROWS, 2 * half), lambda b, idx, used, cnt: (b, 0)),
        scratch_shapes=[pltpu.VMEM((2, MOE_ROWS, half), src.dtype), pltpu.SemaphoreType.DMA((2,))],
    )
    return pl.pallas_call(
        _dispatch_body,
        out_shape=jax.ShapeDtypeStruct((n_rows, 2 * half), BF16),
        grid_spec=grid_spec,
        compiler_params=_params("arbitrary"),
        name="dispatch_rows",
    )(row_tok, n_used, blk_cnt, src)


def _expert_up_body(be_ref, used_ref, x_ref, w1_ref, w3_ref, h_ref):
    @pl.when(pl.program_id(1) < used_ref[0])
    def _():
        x = x_ref[...]
        a = jnp.dot(x, w1_ref[...].astype(BF16), preferred_element_type=F32)
        g = jnp.dot(x, w3_ref[...].astype(BF16), preferred_element_type=F32)
        h_ref[...] = (a * _sigmoid(a) * g).astype(h_ref.dtype)


def _expert_down_body(be_ref, used_ref, h_ref, w2_ref, y_ref):
    @pl.when(pl.program_id(1) < used_ref[0])
    def _():
        y_ref[...] = _pack_bf16_pairs(jnp.dot(h_ref[...], w2_ref[...].astype(BF16),
                                              preferred_element_type=F32))


def _expert_ffn(xs, blk_exp, n_used, w1, w3, w2):
    n_rows, d = xs.shape
    f = w1.shape[2]
    n_blocks = n_rows // MOE_ROWS
    tf = _tile(f, EXPERT_HIDDEN_TILE)
    tn = _tile(d, EXPERT_OUT_TILE)
    last = lambda b, used: jnp.minimum(b, used[0] - 1)
    up_spec = pltpu.PrefetchScalarGridSpec(
        num_scalar_prefetch=2,
        grid=(f // tf, n_blocks),
        in_specs=[pl.BlockSpec((MOE_ROWS, d), lambda j, b, be, us: (last(b, us), 0)),
                  pl.BlockSpec((None, d, tf), lambda j, b, be, us: (be[b], 0, j)),
                  pl.BlockSpec((None, d, tf), lambda j, b, be, us: (be[b], 0, j))],
        out_specs=pl.BlockSpec((MOE_ROWS, tf), lambda j, b, be, us: (b, j)),
    )
    hidden = pl.pallas_call(
        _expert_up_body,
        out_shape=jax.ShapeDtypeStruct((n_rows, f), BF16),
        grid_spec=up_spec,
        compiler_params=_params("arbitrary", "arbitrary"),
        name="expert_up",
    )(blk_exp, n_used, xs, w1, w3)
    down_spec = pltpu.PrefetchScalarGridSpec(
        num_scalar_prefetch=2,
        grid=(d // tn, n_blocks),
        in_specs=[pl.BlockSpec((MOE_ROWS, f), lambda j, b, be, us: (last(b, us), 0)),
                  pl.BlockSpec((None, f, tn), lambda j, b, be, us: (be[b], 0, j))],
        out_specs=pl.BlockSpec((MOE_ROWS, tn // 2), lambda j, b, be, us: (b, j)),
    )
    return pl.pallas_call(
        _expert_down_body,
        out_shape=jax.ShapeDtypeStruct((n_rows, d // 2), jnp.int32),
        grid_spec=down_spec,
        compiler_params=_params("arbitrary", "arbitrary"),
        name="expert_down",
    )(blk_exp, n_used, hidden, w2)


def _route_layout(ids, n_tok):
    n_assign = n_tok * TOP_K
    flat_e = ids.reshape(n_assign)
    flat_t = jnp.repeat(jnp.arange(n_tok, dtype=jnp.int32), TOP_K)
    onehot = (flat_e[:, None] == jnp.arange(N_EXPERTS, dtype=jnp.int32)[None, :]).astype(jnp.int32)
    running = jnp.cumsum(onehot, axis=0)
    rank = jnp.take_along_axis(running, flat_e[:, None], axis=1)[:, 0] - 1
    counts = running[-1]
    padded = (counts + MOE_ROWS - 1) // MOE_ROWS * MOE_ROWS
    pad_end = jnp.cumsum(padded)
    dest = (pad_end - padded)[flat_e] + rank
    n_blocks = -(-n_assign // MOE_ROWS) + N_EXPERTS
    n_rows = n_blocks * MOE_ROWS
    row_tok = jnp.zeros((n_rows,), jnp.int32).at[dest].set(flat_t)
    blk = jnp.arange(n_blocks, dtype=jnp.int32)
    n_used = (pad_end[-1] // MOE_ROWS).astype(jnp.int32)
    blk_exp = jnp.searchsorted(pad_end, jnp.minimum(blk, n_used - 1) * MOE_ROWS, side='right').astype(jnp.int32)
    blk_exp = jnp.minimum(blk_exp, N_EXPERTS - 1)
    into_range = blk * MOE_ROWS - (pad_end - padded)[blk_exp]
    blk_cnt = jnp.where(blk < n_used, jnp.clip(counts[blk_exp] - into_range, 0, MOE_ROWS), 0).astype(jnp.int32)
    return row_tok, blk_cnt, dest.astype(jnp.int32), blk_exp, n_used.reshape(1)


def _final_body(dest_ref, x_ref, w_ref, y_ref, g_ref, op_ref, os_ref, buf, sem, *, prompt_tiles, pack_tile):
    i = pl.program_id(0)
    tm = buf.shape[2]

    def start(blk):
        for k in range(TOP_K):
            _start_row_gather(dest_ref, blk * tm * TOP_K + k, TOP_K, y_ref, buf.at[blk % 2, k], sem.at[blk % 2, k],
                              tm)

    @pl.when(i == 0)
    def _():
        start(0)

    @pl.when(i + 1 < pl.num_programs(0))
    def _():
        start(i + 1)

    slot = i % 2
    for k in range(TOP_K):
        _wait_row_gather(y_ref, buf.at[slot, k], sem.at[slot, k], tm)
    w = w_ref[...]
    pieces = []
    for j in range(buf.shape[3] // pack_tile):
        words = slice(j * pack_tile, (j + 1) * pack_tile)
        hi0, lo0 = _unpack_bf16_pairs(buf[slot, 0, :, words])
        hi1, lo1 = _unpack_bf16_pairs(buf[slot, 1, :, words])
        pieces += [hi0 * w[:, 0:1] + hi1 * w[:, 1:2], lo0 * w[:, 0:1] + lo1 * w[:, 1:2]]
    y = _rmsnorm_rows(x_ref[...] + jnp.concatenate(pieces, axis=1), g_ref[...])

    @pl.when(i < prompt_tiles)
    def _():
        op_ref[...] = y

    @pl.when(i >= prompt_tiles)
    def _():
        os_ref[...] = y


def _combine_and_norm(x, wts, y_rows, dest, g, n_prompt):
    m, d = x.shape
    tm = _tile(m - n_prompt, _tile(n_prompt, 256))
    assert n_prompt % tm == 0
    pt = n_prompt // tm
    grid_spec = pltpu.PrefetchScalarGridSpec(
        num_scalar_prefetch=1,
        grid=(m // tm,),
        in_specs=[pl.BlockSpec((tm, d), lambda i, dest: (i, 0)),
                  pl.BlockSpec((tm, LANES), lambda i, dest: (i, 0)),
                  pl.BlockSpec(memory_space=pl.ANY),
                  pl.BlockSpec((1, d), lambda i, dest: (0, 0))],
        out_specs=[pl.BlockSpec((tm, d), lambda i, dest: (jnp.minimum(i, pt - 1), 0)),
                   pl.BlockSpec((tm, d), lambda i, dest: (jnp.maximum(i - pt, 0), 0))],
        scratch_shapes=[pltpu.VMEM((2, TOP_K, tm, d // 2), y_rows.dtype), pltpu.SemaphoreType.DMA((2, TOP_K))],
    )
    return pl.pallas_call(
        functools.partial(_final_body, prompt_tiles=pt, pack_tile=_tile(d, EXPERT_OUT_TILE) // 2),
        out_shape=[jax.ShapeDtypeStruct((n_prompt, d), F32), jax.ShapeDtypeStruct((m - n_prompt, d), F32)],
        grid_spec=grid_spec,
        compiler_params=_params("arbitrary"),
        name="combine_norm",
    )(dest, x, wts, y_rows, g.reshape(1, d))


def _layer(xp, xs, n_seq, n_new, past_len, cache_sb_k, cache_sb_v, state_ret, cache_mem_k, cache_mem_v,
           mem_prompt, norm_mix, w_in, ret_gn, w_out, norm_cross, norm_mem, w_cq, w_ck, w_cv, w_co, norm_ffn,
           wg, bg, we, be, w1, w3, w2):
    n_prompt, d = xp.shape
    n_tok = n_prompt + xs.shape[0]
    d_ret = d // 2
    d_sb = d - d_ret
    ret_heads = d_ret // RET_HEAD_DIM
    sds = jax.ShapeDtypeStruct

    pos = np.concatenate([np.arange(n_prompt), np.tile(past_len + np.arange(n_new), n_seq)])
    half = RET_HEAD_DIM // 2
    inv = (ROPE_BASE ** (-np.arange(half, dtype=np.float32) / half)).astype(np.float32)
    ang = pos.astype(np.float32)[:, None] * inv[None, :]
    cos, sin = jnp.asarray(np.cos(ang), F32), jnp.asarray(np.sin(ang), F32)
    log_gamma = jnp.log1p(-jnp.exp2(-5.0 - jnp.arange(ret_heads, dtype=F32)))

    h = _rmsnorm_stacked(xp, xs, norm_mix, BF16)
    rot = [(cos, half), (sin, half)]
    bf = lambda n: [(sds((n_tok, n), BF16), None)]
    (rq,) = _matmul([h], w_in, 0, d_ret, _ep_rotary(1.0), rot, bf(d_ret), "in_proj_ret_q")
    (rk,) = _matmul([h], w_in, d_ret, d_ret, _ep_rotary(RET_HEAD_DIM ** -0.5), rot, bf(d_ret), "in_proj_ret_k")
    (rv,) = _matmul([h], w_in, 2 * d_ret, d_ret, _ep_store(1.0), [], bf(d_ret), "in_proj_ret_v")
    (rg,) = _matmul([h], w_in, 3 * d_ret, d_ret, _ep_store(1.0), [], [(sds((n_tok, d_ret), F32), None)],
                    "in_proj_ret_gate")
    (sq,) = _matmul([h], w_in, 4 * d_ret, d_sb, _ep_store(SB_HEAD_DIM ** -0.5 * LOG2_E), [], bf(d_sb),
                    "in_proj_sb_q")
    split = [(sds((n_prompt, d_sb), F32), PROMPT), (sds((n_tok - n_prompt, d_sb), F32), SAMPLE),
             (sds((n_tok, d_sb), BF16), None)]
    sk_p, sk_s, sk_b = _matmul([h], w_in, 4 * d_ret + d_sb, d_sb, _ep_store_split, [], split, "in_proj_sb_k",
                               n_prompt=n_prompt)
    sv_p, sv_s, sv_b = _matmul([h], w_in, 4 * d_ret + 2 * d_sb, d_sb, _ep_store_split, [], split,
                               "in_proj_sb_v", n_prompt=n_prompt)

    rows_p = _tile(n_prompt, RET_BLOCK)
    zeros_state = jnp.zeros((1, ret_heads, RET_HEAD_DIM, RET_HEAD_DIM), F32)
    yr, state_p = _retention(log_gamma, rq, rk, rv, rg, ret_gn, zeros_state, 0, 1, n_prompt // rows_p, rows_p)
    yr, state_s = _retention(log_gamma, rq, rk, rv, rg, ret_gn, state_ret, n_prompt, n_seq, 1, n_new,
                             out_buf=yr)
    sb = _sb_prompt(sq, sk_b, sv_b, n_prompt)
    sb = _sb_sample(sq, sk_b, sv_b, cache_sb_k, cache_sb_v, n_prompt, n_new, sb)
    res = [(sds((n_tok, d), F32), None)]
    (x,) = _matmul([yr, sb], w_out, 0, d, _ep_residual_split, [(xp, PROMPT), (xs, SAMPLE)], res, "out_proj",
                   n_prompt=n_prompt)

    n_mem = mem_prompt.shape[0]
    m = _rmsnorm(mem_prompt, norm_mem, BF16)
    mem_out = [(sds((n_mem, d), F32), None)]
    (mk,) = _matmul([m], w_ck, 0, d, _ep_store(1.0), [], mem_out, "mem_k")
    (mv,) = _matmul([m], w_cv, 0, d, _ep_store(1.0), [], mem_out, "mem_v")
    h = _rmsnorm(x, norm_cross, BF16)
    (cq,) = _matmul([h], w_cq, 0, d, _ep_store(1.0), [], bf(d), "cross_q")
    co = _cross_attend(cq, mk.reshape(1, n_mem, d), mv.reshape(1, n_mem, d), 0, n_prompt)
    co = _cross_attend(cq, cache_mem_k, cache_mem_v, n_prompt, n_new, out_buf=co)
    (x,) = _matmul([co], w_co, 0, d, _ep_residual, [(x, None)], res, "cross_out")

    hn, ids, wts = _norm_and_route(x, norm_ffn, wg, bg, we, be)
    row_tok, blk_cnt, dest, blk_exp, n_used = _route_layout(ids[:, :TOP_K], n_tok)
    x_rows = _dispatch_rows(hn, row_tok, blk_cnt, n_used)
    y_rows = _expert_ffn(x_rows, blk_exp, n_used, w1, w3, w2)
    return x, wts, y_rows, dest, (sk_p, sv_p, sk_s, sv_s, state_p, state_s, mk, mv)


def kernel(x_prompt, x_sample, cache_sb_k, cache_sb_v, state_ret, cache_mem_k, cache_mem_v, mem_prompt, norm_mix, w_in, ret_gn, w_out, norm_cross, norm_mem, w_cq, w_ck, w_cv, w_co, norm_ffn, router_group_w, router_group_b, router_expert_w, router_expert_b, expert_w1, expert_w3, expert_w2, norm_final):
    batch, n_prompt, d = x_prompt.shape
    n_seq, n_new, _ = x_sample.shape
    depth, _, past_len, sb_heads, _ = cache_sb_k.shape
    n_mem = mem_prompt.shape[1]
    assert batch == 1 and depth == 1
    ret_heads = (d // 2) // RET_HEAD_DIM
    x, wts, y_rows, dest, (sk_p, sv_p, sk_s, sv_s, state_p, state_s, mk, mv) = _layer(
        x_prompt.reshape(n_prompt, d), x_sample.reshape(n_seq * n_new, d), n_seq, n_new, past_len,
        cache_sb_k, cache_sb_v, state_ret[0], cache_mem_k, cache_mem_v, mem_prompt[0], norm_mix[0], w_in,
        ret_gn[0], w_out, norm_cross[0], norm_mem[0], w_cq, w_ck, w_cv, w_co, norm_ffn[0], router_group_w[0],
        router_group_b[0], router_expert_w[0], router_expert_b[0], expert_w1[0], expert_w3[0], expert_w2[0])
    y_prompt, y_sample = _combine_and_norm(x, wts, y_rows, dest, norm_final, n_prompt)
    y_prompt = y_prompt.reshape(1, n_prompt, d)
    y_sample = y_sample.reshape(n_seq, n_new, d)
    sb_shape_p = (1, 1, n_prompt, sb_heads, SB_HEAD_DIM)
    sb_shape_s = (1, n_seq, n_new, sb_heads, SB_HEAD_DIM)
    mem_shape = (1, 1, n_mem, MEM_HEADS, d // MEM_HEADS)
    return (y_prompt, y_sample,
            sk_p.reshape(sb_shape_p), sv_p.reshape(sb_shape_p),
            state_p.reshape(1, 1, ret_heads, RET_HEAD_DIM, RET_HEAD_DIM),
            mk.reshape(mem_shape), mv.reshape(mem_shape),
            sk_s.reshape(sb_shape_s), sv_s.reshape(sb_shape_s),
            state_s.reshape(1, n_seq, ret_heads, RET_HEAD_DIM, RET_HEAD_DIM))
```
